```python
import jax
import jax.numpy as jnp
from jax import lax
import numpy as np

D_MODEL = 1024
BATCH = 2
SEQ = 8192
DEPTH = 4

N_MIXERS = 3
N_HEADS = 16
HEAD_DIM = D_MODEL // N_HEADS
ROPE_DIM = HEAD_DIM // 4
ROPE_THETA = 500000.0
NORM_EPS = 1e-6
Q_BLOCK = 128
NEG_INF = -1e30

NSA_KV_GROUPS = 4
NSA_HEADS_PER_GROUP = N_HEADS // NSA_KV_GROUPS
NSA_KV_WIDTH = NSA_KV_GROUPS * HEAD_DIM
NSA_CMP_LEN = 32
NSA_CMP_STRIDE = 16
NSA_SEL_LEN = 64
NSA_TOP_N = 16
NSA_WINDOW = 512
NSA_FORCED_BONUS = 1e3
NSA_IN_WIDTH = D_MODEL + 6 * NSA_KV_WIDTH + 3 * N_HEADS

MOBA_BLOCK = 256
MOBA_TOP_K = 3
MOBA_Q_CHUNK = 64

PEER_HEADS = 8
PEER_N_KEYS = 128
PEER_N_EXPERTS = PEER_N_KEYS ** 2
PEER_TOP_K = 16
PEER_D_QUERY = 128
PEER_HALF = PEER_D_QUERY // 2
PEER_TOKEN_CHUNK = 512

kernel_name = 'hybrid_sb_nsa_moba_peer_trunk'


def rmsnorm(x, g):
    xf = x.astype(jnp.float32)
    y = xf * lax.rsqrt(jnp.mean(xf * xf, axis=-1, keepdims=True) + NORM_EPS)
    return (y * g.astype(jnp.float32)).astype(x.dtype)


def rope_cos_sin(pos, dtype):
    inv = ROPE_THETA ** (-jnp.arange(0, ROPE_DIM, 2, dtype=jnp.float32) / ROPE_DIM)
    ang = pos.astype(jnp.float32)[:, None] * inv[None, :]
    return jnp.cos(ang).astype(dtype), jnp.sin(ang).astype(dtype)


def partial_rope(x, cos, sin):
    half = ROPE_DIM // 2
    x1, x2, rest = x[..., :half], x[..., half:ROPE_DIM], x[..., ROPE_DIM:]
    c, s = cos[:, None, :], sin[:, None, :]
    return jnp.concatenate([x1 * c - x2 * s, x1 * s + x2 * c, rest], axis=-1)


def masked_softmax(s, mask):
    p = jax.nn.softmax(jnp.where(mask, s.astype(jnp.float32), NEG_INF), axis=-1)
    return jnp.where(mask, p, 0.0)


def stick_breaking_mixer(h, w_in, w_out):
    B, S, _ = h.shape
    nb = S // Q_BLOCK
    scale = HEAD_DIM ** -0.5
    qkv = (h @ w_in).reshape(B, S, 3, N_HEADS, HEAD_DIM)
    q, k, v = qkv[:, :, 0], qkv[:, :, 1], qkv[:, :, 2]
    key_pos = jnp.arange(S)
    q_blocks = q.reshape(B, nb, Q_BLOCK, N_HEADS, HEAD_DIM).swapaxes(0, 1)

    def one_block(args):
        qb, bi = args
        q_pos = bi * Q_BLOCK + jnp.arange(Q_BLOCK)
        past = key_pos[None, :] < q_pos[:, None]
        z = jnp.einsum('bqhd,bshd->bhqs', qb, k).astype(jnp.float32) * scale
        log_1m_beta = jnp.where(past, jax.nn.log_sigmoid(-z), 0.0)
        between = lax.cumsum(log_1m_beta, axis=3, reverse=True) - log_1m_beta
        a = jnp.where(past, jnp.exp(jax.nn.log_sigmoid(z) + between), 0.0)
        return jnp.einsum('bhqs,bshd->bqhd', a.astype(v.dtype), v)

    o = lax.map(one_block, (q_blocks, jnp.arange(nb)))
    return o.swapaxes(0, 1).reshape(B, S, D_MODEL) @ w_out


def nsa_mixer(h, w_in, cmp_pos_k, cmp_pos_v, cmp_w_k, cmp_w_v, w_out):
    B, S, _ = h.shape
    G, HPG, Dh, KV = NSA_KV_GROUPS, NSA_HEADS_PER_GROUP, HEAD_DIM, NSA_KV_WIDTH
    nb = S // Q_BLOCK
    scale = Dh ** -0.5
    proj = h @ w_in
    q = proj[..., :D_MODEL].reshape(B, S, N_HEADS, Dh)
    kv = proj[..., D_MODEL:D_MODEL + 6 * KV].reshape(B, S, 6, G, Dh)
    k_cmp, v_cmp, k_slc, v_slc, k_win, v_win = [kv[:, :, i] for i in range(6)]
    gates = jax.nn.sigmoid(proj[..., D_MODEL + 6 * KV:]).reshape(B, S, G, HPG, 3)

    cos, sin = rope_cos_sin(jnp.arange(S), h.dtype)
    q = partial_rope(q, cos, sin)
    k_slc = partial_rope(k_slc, cos, sin)
    k_win = partial_rope(k_win, cos, sin)

    n_cmp = (S - NSA_CMP_LEN) // NSA_CMP_STRIDE + 1
    cmp_start = np.arange(n_cmp) * NSA_CMP_STRIDE
    cmp_end = cmp_start + NSA_CMP_LEN - 1
    cmp_idx = cmp_start[:, None] + np.arange(NSA_CMP_LEN)[None, :]
    wk = cmp_w_k.reshape(NSA_CMP_LEN, Dh, Dh)
    wv = cmp_w_v.reshape(NSA_CMP_LEN, Dh, Dh)
    kc = jnp.einsum('bnlgd,lde->bnge', k_cmp[:, cmp_idx] + cmp_pos_k[:, None, :], wk)
    vc = jnp.einsum('bnlgd,lde->bnge', v_cmp[:, cmp_idx] + cmp_pos_v[:, None, :], wv)
    cc, sc = rope_cos_sin(jnp.asarray(cmp_end), h.dtype)
    kc = partial_rope(kc, cc, sc)
    cmp_end_j = jnp.asarray(cmp_end)

    n_sel = S // NSA_SEL_LEN
    sel_start = np.arange(n_sel) * NSA_SEL_LEN
    overlap = np.clip(np.minimum(cmp_start[:, None] + NSA_CMP_LEN, sel_start[None, :] + NSA_SEL_LEN)
                      - np.maximum(cmp_start[:, None], sel_start[None, :]), 0, None)
    cmp_to_sel = jnp.asarray(overlap / NSA_CMP_LEN, dtype=jnp.float32)
    top_n = min(NSA_TOP_N, n_sel)
    k_sel_blocks = k_slc.reshape(B, n_sel, NSA_SEL_LEN, G, Dh).transpose(0, 3, 1, 2, 4)
    v_sel_blocks = v_slc.reshape(B, n_sel, NSA_SEL_LEN, G, Dh).transpose(0, 3, 1, 2, 4)
    b_ix = jnp.arange(B)[:, None, None, None]
    g_ix = jnp.arange(G)[None, :, None, None]
    sel_ids = jnp.arange(n_sel)

    pad = ((0, 0), (NSA_WINDOW, 0), (0, 0), (0, 0))
    k_win_pad = jnp.pad(k_win, pad)
    v_win_pad = jnp.pad(v_win, pad)
    win_len = NSA_WINDOW + Q_BLOCK

    q_blocks = q.reshape(B, nb, Q_BLOCK, G, HPG, Dh).swapaxes(0, 1)
    g_blocks = gates.reshape(B, nb, Q_BLOCK, G, HPG, 3).swapaxes(0, 1)

    def one_block(args):
        qb, gb, bi = args
        t0 = bi * Q_BLOCK
        q_pos = t0 + jnp.arange(Q_BLOCK)
        s_c = jnp.einsum('bqghd,bngd->bghqn', qb, kc) * scale
        p_c = masked_softmax(s_c, cmp_end_j[None, :] <= q_pos[:, None])
        o_c = jnp.einsum('bghqn,bngd->bqghd', p_c.astype(vc.dtype), vc)
        imp = jnp.einsum('bghqn,nj->bgqj', p_c, cmp_to_sel)
        own = q_pos // NSA_SEL_LEN
        vis_s = sel_ids[None, :] <= own[:, None]
        forced = (sel_ids[None, :] == 0) | (sel_ids[None, :] == own[:, None]) | (sel_ids[None, :] == own[:, None] - 1)
        imp = jnp.where(vis_s, imp + NSA_FORCED_BONUS * forced, NEG_INF)
        _, sel = lax.top_k(imp, top_n)
        kg = k_sel_blocks[b_ix, g_ix, sel]
        vg = v_sel_blocks[b_ix, g_ix, sel]
        s_s = jnp.einsum('bqghd,bgqnld->bghqnl', qb, kg) * scale
        key_pos = sel[..., None] * NSA_SEL_LEN + jnp.arange(NSA_SEL_LEN)
        vis = key_pos <= q_pos[None, None, :, None, None]
        p_s = masked_softmax(s_s.reshape(B, G, HPG, Q_BLOCK, -1), vis.reshape(B, G, 1, Q_BLOCK, -1))
        o_s = jnp.einsum('bghqk,bgqkd->bqghd', p_s.astype(vg.dtype),
                         vg.reshape(B, G, Q_BLOCK, top_n * NSA_SEL_LEN, Dh))
        kw = lax.dynamic_slice_in_dim(k_win_pad, t0, win_len, axis=1)
        vw = lax.dynamic_slice_in_dim(v_win_pad, t0, win_len, axis=1)
        kp = t0 - NSA_WINDOW + jnp.arange(win_len)
        vis_w = (kp[None, :] <= q_pos[:, None]) & (kp[None, :] > q_pos[:, None] - NSA_WINDOW) & (kp[None, :] >= 0)
        s_w = jnp.einsum('bqghd,bkgd->bghqk', qb, kw) * scale
        p_w = masked_softmax(s_w, vis_w)
        o_w = jnp.einsum('bghqk,bkgd->bqghd', p_w.astype(vw.dtype), vw)
        gb = gb.astype(o_c.dtype)
        o = gb[..., 0:1] * o_c + gb[..., 1:2] * o_s + gb[..., 2:3] * o_w
        return o.reshape(B, Q_BLOCK, D_MODEL)

    o = lax.map(one_block, (q_blocks, g_blocks, jnp.arange(nb)))
    return o.swapaxes(0, 1).reshape(B, S, D_MODEL) @ w_out


def moba_mixer(h, w_in, w_out):
    B, S, _ = h.shape
    H, Dh = N_HEADS, HEAD_DIM
    scale = Dh ** -0.5
    qkv = (h @ w_in).reshape(B, S, 3, H, Dh)
    cos, sin = rope_cos_sin(jnp.arange(S), h.dtype)
    q = partial_rope(qkv[:, :, 0], cos, sin)
    k = partial_rope(qkv[:, :, 1], cos, sin)
    v = qkv[:, :, 2]
    s_pad = -(-S // MOBA_BLOCK) * MOBA_BLOCK
    n_blk = s_pad // MOBA_BLOCK
    pad = ((0, 0), (0, s_pad - S), (0, 0), (0, 0))
    k_pad = jnp.pad(k, pad)
    v_pad = jnp.pad(v, pad)
    k_blocks = k_pad.reshape(B, n_blk, MOBA_BLOCK, H, Dh).transpose(0, 3, 1, 2, 4)
    v_blocks = v_pad.reshape(B, n_blk, MOBA_BLOCK, H, Dh).transpose(0, 3, 1, 2, 4)
    k_mean = jnp.mean(k_blocks.astype(jnp.float32), axis=3).astype(h.dtype)
    top_k = min(MOBA_TOP_K, n_blk)
    n_sel_keys = top_k * MOBA_BLOCK
    nq = S // MOBA_Q_CHUNK
    q_chunks = q.reshape(B, nq, MOBA_Q_CHUNK, H, Dh).swapaxes(0, 1)
    b_ix = jnp.arange(B)[:, None, None, None]
    h_ix = jnp.arange(H)[None, :, None, None]
    blk_ids = jnp.arange(n_blk)

    def one_chunk(args):
        qc, ci = args
        t0 = ci * MOBA_Q_CHUNK
        q_pos = t0 + jnp.arange(MOBA_Q_CHUNK)
        own = t0 // MOBA_BLOCK
        gate = jnp.einsum('bqhd,bhnd->bhqn', qc, k_mean).astype(jnp.float32)
        gate = jnp.where(blk_ids < own, gate, NEG_INF)
        _, sel = lax.top_k(gate, top_k)
        valid = sel < own
        kg = k_blocks[b_ix, h_ix, sel]
        vg = v_blocks[b_ix, h_ix, sel]
        s_sel = jnp.einsum('bqhd,bhqkld->bhqkl', qc, kg).reshape(B, H, MOBA_Q_CHUNK, n_sel_keys)
        m_sel = jnp.broadcast_to(valid[..., None], (B, H, MOBA_Q_CHUNK, top_k, MOBA_BLOCK)).reshape(
            B, H, MOBA_Q_CHUNK, n_sel_keys)
        k_own = lax.dynamic_slice_in_dim(k_pad, own * MOBA_BLOCK, MOBA_BLOCK, axis=1)
        v_own = lax.dynamic_slice_in_dim(v_pad, own * MOBA_BLOCK, MOBA_BLOCK, axis=1)
        own_pos = own * MOBA_BLOCK + jnp.arange(MOBA_BLOCK)
        s_own = jnp.einsum('bqhd,bkhd->bhqk', qc, k_own)
        m_own = jnp.broadcast_to(own_pos[None, :] <= q_pos[:, None], (B, H, MOBA_Q_CHUNK, MOBA_BLOCK))
        p = masked_softmax(jnp.concatenate([s_sel, s_own], axis=-1) * scale,
                           jnp.concatenate([m_sel, m_own], axis=-1)).astype(v.dtype)
        o = (jnp.einsum('bhqk,bhqkd->bqhd', p[..., :n_sel_keys],
                        vg.reshape(B, H, MOBA_Q_CHUNK, n_sel_keys, Dh))
             + jnp.einsum('bhqk,bkhd->bqhd', p[..., n_sel_keys:], v_own))
        return o.reshape(B, MOBA_Q_CHUNK, D_MODEL)

    o = lax.map(one_chunk, (q_chunks, jnp.arange(nq)))
    return o.swapaxes(0, 1).reshape(B, S, D_MODEL) @ w_out


def peer_mixer(h, w_q, sub_keys, u, v):
    B, S, D = h.shape
    T = B * S
    xt = h.reshape(T, D)
    pad = (-T) % PEER_TOKEN_CHUNK
    xt = jnp.pad(xt, ((0, pad), (0, 0)))
    n_chunk = xt.shape[0] // PEER_TOKEN_CHUNK

    def one_chunk(xc):
        q = (xc @ w_q).astype(jnp.float32).reshape(PEER_TOKEN_CHUNK, PEER_HEADS, 2, PEER_HALF)
        s1 = jnp.einsum('chd,kd->chk', q[:, :, 0], sub_keys[0])
        s2 = jnp.einsum('chd,kd->chk', q[:, :, 1], sub_keys[1])
        v1, i1 = lax.top_k(s1, PEER_TOP_K)
        v2, i2 = lax.top_k(s2, PEER_TOP_K)
        cand = (v1[..., :, None] + v2[..., None, :]).reshape(PEER_TOKEN_CHUNK, PEER_HEADS, -1)
        cand_idx = (i1[..., :, None] * PEER_N_KEYS + i2[..., None, :]).reshape(PEER_TOKEN_CHUNK, PEER_HEADS, -1)
        score, pos = lax.top_k(cand, PEER_TOP_K)
        expert = jnp.take_along_axis(cand_idx, pos, axis=-1)
        g = jax.nn.softmax(score, axis=-1)
        ue = u[expert]
        ve = v[expert]
        act = jax.nn.gelu(jnp.einsum('cd,chkd->chk', xc, ue).astype(jnp.float32), approximate=False)
        return jnp.einsum('chk,chkd->cd', (g * act).astype(ve.dtype), ve)

    y = lax.map(one_chunk, xt.reshape(n_chunk, PEER_TOKEN_CHUNK, D))
    return y.reshape(-1, D)[:T].reshape(B, S, D)


MIXERS = (stick_breaking_mixer, nsa_mixer, moba_mixer)


def setup_inputs(seed: int = 0) -> dict:
    key = jax.random.key(seed)
    ks = list(jax.random.split(key, 16 * DEPTH + 4))

    def normal(shape, scale):
        return jax.random.normal(ks.pop(), shape, jnp.float32) * scale

    def gain():
        return 1.0 + normal((D_MODEL,), 0.02)

    w_scale = D_MODEL ** -0.5
    inputs = {'x': normal((BATCH, SEQ, D_MODEL), 1.0)}
    for i in range(DEPTH):
        kind = i % N_MIXERS
        inputs[f'l{i}_norm_mix'] = gain()
        if kind == 0:
            inputs[f'l{i}_sb_w_in'] = normal((D_MODEL, 3 * D_MODEL), w_scale)
            inputs[f'l{i}_sb_w_out'] = normal((D_MODEL, D_MODEL), w_scale)
        elif kind == 1:
            inputs[f'l{i}_nsa_w_in'] = normal((D_MODEL, NSA_IN_WIDTH), w_scale)
            inputs[f'l{i}_nsa_cmp_pos_k'] = normal((NSA_CMP_LEN, HEAD_DIM), HEAD_DIM ** -0.5)
            inputs[f'l{i}_nsa_cmp_pos_v'] = normal((NSA_CMP_LEN, HEAD_DIM), HEAD_DIM ** -0.5)
            inputs[f'l{i}_nsa_cmp_w_k'] = normal((NSA_CMP_LEN * HEAD_DIM, HEAD_DIM), (NSA_CMP_LEN * HEAD_DIM) ** -0.5)
            inputs[f'l{i}_nsa_cmp_w_v'] = normal((NSA_CMP_LEN * HEAD_DIM, HEAD_DIM), (NSA_CMP_LEN * HEAD_DIM) ** -0.5)
            inputs[f'l{i}_nsa_w_out'] = normal((D_MODEL, D_MODEL), w_scale)
        else:
            inputs[f'l{i}_moba_w_in'] = normal((D_MODEL, 3 * D_MODEL), w_scale)
            inputs[f'l{i}_moba_w_out'] = normal((D_MODEL, D_MODEL), w_scale)
        inputs[f'l{i}_norm_ffn'] = gain()
        inputs[f'l{i}_peer_w_q'] = normal((D_MODEL, PEER_HEADS * PEER_D_QUERY), w_scale)
        inputs[f'l{i}_peer_sub_keys'] = normal((2, PEER_N_KEYS, PEER_HALF), PEER_HALF ** -0.5)
        inputs[f'l{i}_peer_u'] = normal((PEER_N_EXPERTS, D_MODEL), w_scale)
        inputs[f'l{i}_peer_v'] = normal((PEER_N_EXPERTS, D_MODEL), (PEER_HEADS * PEER_TOP_K) ** -0.5)
    inputs['final_norm'] = gain()
    return inputs


def reference(x,
              l0_norm_mix, l0_sb_w_in, l0_sb_w_out,
              l0_norm_ffn, l0_peer_w_q, l0_peer_sub_keys, l0_peer_u, l0_peer_v,
              l1_norm_mix, l1_nsa_w_in, l1_nsa_cmp_pos_k, l1_nsa_cmp_pos_v, l1_nsa_cmp_w_k, l1_nsa_cmp_w_v,
              l1_nsa_w_out,
              l1_norm_ffn, l1_peer_w_q, l1_peer_sub_keys, l1_peer_u, l1_peer_v,
              l2_norm_mix, l2_moba_w_in, l2_moba_w_out,
              l2_norm_ffn, l2_peer_w_q, l2_peer_sub_keys, l2_peer_u, l2_peer_v,
              l3_norm_mix, l3_sb_w_in, l3_sb_w_out,
              l3_norm_ffn, l3_peer_w_q, l3_peer_sub_keys, l3_peer_u, l3_peer_v,
              final_norm):
    norm_mix = (l0_norm_mix, l1_norm_mix, l2_norm_mix, l3_norm_mix)
    norm_ffn = (l0_norm_ffn, l1_norm_ffn, l2_norm_ffn, l3_norm_ffn)
    mixer_params = (
        (l0_sb_w_in, l0_sb_w_out),
        (l1_nsa_w_in, l1_nsa_cmp_pos_k, l1_nsa_cmp_pos_v, l1_nsa_cmp_w_k, l1_nsa_cmp_w_v, l1_nsa_w_out),
        (l2_moba_w_in, l2_moba_w_out),
        (l3_sb_w_in, l3_sb_w_out),
    )
    peer_params = (
        (l0_peer_w_q, l0_peer_sub_keys, l0_peer_u, l0_peer_v),
        (l1_peer_w_q, l1_peer_sub_keys, l1_peer_u, l1_peer_v),
        (l2_peer_w_q, l2_peer_sub_keys, l2_peer_u, l2_peer_v),
        (l3_peer_w_q, l3_peer_sub_keys, l3_peer_u, l3_peer_v),
    )
    for i in range(DEPTH):
        mixer = MIXERS[i % N_MIXERS]
        x = x + mixer(rmsnorm(x, norm_mix[i]), *mixer_params[i])
        x = x + peer_mixer(rmsnorm(x, norm_ffn[i]), *peer_params[i])
    return rmsnorm(x, final_norm)
```

```python
import functools

import numpy as np
import jax
import jax.numpy as jnp
from jax import lax
from jax.experimental import pallas as pl
from jax.experimental.pallas import tpu as pltpu

F32 = jnp.float32
BF16 = jnp.bfloat16

N_HEADS = 16
HEAD_DIM = 64
ROPE_DIM = 16
ROPE_HALF = ROPE_DIM // 2
ROPE_THETA = 500000.0
NORM_EPS = 1e-6
NEG = -1e30
LOW = -3e38
ATT_SCALE = HEAD_DIM ** -0.5

NSA_GROUPS = 4
NSA_HPG = N_HEADS // NSA_GROUPS
NSA_KV = NSA_GROUPS * HEAD_DIM
NSA_CMP_LEN = 32
NSA_CMP_STRIDE = 16
NSA_SEL_LEN = 64
NSA_TOP_N = 16
NSA_WINDOW = 512
NSA_BONUS = 1e3
NSA_SEL_PAD = 128

MOBA_BLOCK = 256
MOBA_TOP_K = 3
MOBA_BLK_PAD = 128

PEER_HEADS = 8
PEER_KEYS = 128
PEER_TOP_K = 16
PEER_HALF = 64

SB_EXIT = -110.0

LANE = 128
VMEM_LIMIT = 56 << 20


def _cparams(sem, vmem=VMEM_LIMIT):
    return pltpu.CompilerParams(dimension_semantics=sem, vmem_limit_bytes=vmem)


def _dot(a, b):
    return jnp.dot(a, b, preferred_element_type=F32)


def _dot_nt(a, b):
    return lax.dot_general(a, b, (((1,), (1,)), ((), ())), preferred_element_type=F32)


def _split_bf16(x):
    hi = x.astype(BF16)
    lo = (x - hi.astype(F32)).astype(BF16)
    return hi, lo


def _topk_mask(vals, k):
    lane = lax.broadcasted_iota(jnp.int32, vals.shape, 1)
    work = vals
    sel = jnp.zeros(vals.shape, F32)
    for _ in range(k):
        m = jnp.max(work, axis=-1, keepdims=True)
        first = jnp.min(jnp.where(work == m, lane, LANE), axis=-1, keepdims=True)
        hit = lane == first
        sel = jnp.where(hit, 1.0, sel)
        work = jnp.where(hit, LOW, work)
    return sel


def _norm_matmul_kernel(*refs, n_rope, tn):
    if n_rope:
        x_ref, g_ref, w_ref, c_ref, sa_ref, sb_ref, o_ref, xn_ref = refs
    else:
        x_ref, g_ref, w_ref, o_ref, xn_ref = refs
    j = pl.program_id(1)

    @pl.when(j == 0)
    def _():
        x = x_ref[...]
        y = x * lax.rsqrt(jnp.mean(x * x, axis=-1, keepdims=True) + NORM_EPS)
        xn_ref[...] = (y * g_ref[...]).astype(BF16)

    acc = _dot(xn_ref[...], w_ref[...])
    if n_rope:
        @pl.when(j < n_rope)
        def _():
            r = (acc * c_ref[...] + pltpu.roll(acc, ROPE_HALF, 1) * sb_ref[...]
                 + pltpu.roll(acc, tn - ROPE_HALF, 1) * sa_ref[...])
            o_ref[...] = r.astype(o_ref.dtype)

        @pl.when(j >= n_rope)
        def _():
            o_ref[...] = acc.astype(o_ref.dtype)
    else:
        o_ref[...] = acc.astype(o_ref.dtype)


def _rope_tables(pos, width):
    inv = ROPE_THETA ** (-jnp.arange(0, ROPE_DIM, 2, dtype=F32) / ROPE_DIM)
    ang = pos.astype(F32)[:, None] * inv[None, :]
    cos, sin = jnp.cos(ang), jnp.sin(ang)
    n = pos.shape[0]
    rest = HEAD_DIM - ROPE_DIM
    c = jnp.concatenate([cos, cos, jnp.ones((n, rest), F32)], axis=1)
    sa = jnp.concatenate([-sin, jnp.zeros((n, ROPE_HALF + rest), F32)], axis=1)
    sb = jnp.concatenate([jnp.zeros((n, ROPE_HALF), F32), sin, jnp.zeros((n, rest), F32)], axis=1)
    reps = width // HEAD_DIM
    return jnp.tile(c, (1, reps)), jnp.tile(sa, (1, reps)), jnp.tile(sb, (1, reps))


def _norm_matmul(x2, g, w, seq, *, tn, n_rope=0, out_dtype=BF16, tt=512):
    t, d = x2.shape
    n = w.shape[1]
    tt = min(tt, seq)
    grid = (t // tt, n // tn)
    in_specs = [pl.BlockSpec((tt, d), lambda i, j: (i, 0)),
                pl.BlockSpec((1, d), lambda i, j: (0, 0)),
                pl.BlockSpec((d, tn), lambda i, j: (0, j))]
    args = [x2, g.reshape(1, d).astype(F32), w.astype(BF16)]
    if n_rope:
        nper = seq // tt
        tabs = _rope_tables(jnp.arange(seq), tn)
        in_specs += [pl.BlockSpec((tt, tn), lambda i, j: (i % nper, 0))] * 3
        args += list(tabs)
    return pl.pallas_call(
        functools.partial(_norm_matmul_kernel, n_rope=n_rope, tn=tn),
        grid=grid, in_specs=in_specs,
        out_specs=pl.BlockSpec((tt, tn), lambda i, j: (i, j)),
        out_shape=jax.ShapeDtypeStruct((t, n), out_dtype),
        scratch_shapes=[pltpu.VMEM((tt, d), BF16)],
        compiler_params=_cparams(("parallel", "arbitrary")),
        name="norm_matmul",
    )(*args)


def _norm_matmul3_kernel(x_ref, g_ref, wh_ref, wl_ref, o_ref, xh_ref, xl_ref):
    @pl.when(pl.program_id(1) == 0)
    def _():
        x = x_ref[...]
        y = x * lax.rsqrt(jnp.mean(x * x, axis=-1, keepdims=True) + NORM_EPS) * g_ref[...]
        xh_ref[...], xl_ref[...] = _split_bf16(y)

    o_ref[...] = (_dot(xh_ref[...], wh_ref[...]) + _dot(xl_ref[...], wh_ref[...])
                  + _dot(xh_ref[...], wl_ref[...]))


def _norm_matmul3(x2, g, w, *, tn=512, tt=512):
    t, d = x2.shape
    n = w.shape[1]
    tt = min(tt, t)
    w_hi, w_lo = _split_bf16(w.astype(F32))
    w_spec = pl.BlockSpec((d, tn), lambda i, j: (0, j))
    return pl.pallas_call(
        _norm_matmul3_kernel,
        grid=(t // tt, n // tn),
        in_specs=[pl.BlockSpec((tt, d), lambda i, j: (i, 0)),
                  pl.BlockSpec((1, d), lambda i, j: (0, 0)), w_spec, w_spec],
        out_specs=pl.BlockSpec((tt, tn), lambda i, j: (i, j)),
        out_shape=jax.ShapeDtypeStruct((t, n), F32),
        scratch_shapes=[pltpu.VMEM((tt, d), BF16), pltpu.VMEM((tt, d), BF16)],
        compiler_params=_cparams(("parallel", "arbitrary")),
        name="norm_matmul3",
    )(x2, g.reshape(1, d).astype(F32), w_hi, w_lo)


def _proj_resid_kernel(a_ref, w_ref, r_ref, o_ref):
    o_ref[...] = r_ref[...] + _dot(a_ref[...], w_ref[...])


def _proj_resid(a, w, resid, *, tt=512, tn=512):
    t, d = a.shape
    n = w.shape[1]
    tt = min(tt, t)
    return pl.pallas_call(
        _proj_resid_kernel,
        grid=(t // tt, n // tn),
        in_specs=[pl.BlockSpec((tt, d), lambda i, j: (i, 0)),
                  pl.BlockSpec((d, tn), lambda i, j: (0, j)),
                  pl.BlockSpec((tt, tn), lambda i, j: (i, j))],
        out_specs=pl.BlockSpec((tt, tn), lambda i, j: (i, j)),
        out_shape=jax.ShapeDtypeStruct((t, n), F32),
        compiler_params=_cparams(("parallel", "parallel")),
        name="proj_resid",
    )(a, w.astype(BF16), resid)


def _rmsnorm_kernel(x_ref, g_ref, o_ref):
    x = x_ref[...]
    y = x * lax.rsqrt(jnp.mean(x * x, axis=-1, keepdims=True) + NORM_EPS)
    o_ref[...] = y * g_ref[...]


def _rmsnorm(x2, g, *, tt=512):
    t, d = x2.shape
    tt = min(tt, t)
    return pl.pallas_call(
        _rmsnorm_kernel,
        grid=(t // tt,),
        in_specs=[pl.BlockSpec((tt, d), lambda i: (i, 0)), pl.BlockSpec((1, d), lambda i: (0, 0))],
        out_specs=pl.BlockSpec((tt, d), lambda i: (i, 0)),
        out_shape=jax.ShapeDtypeStruct((t, d), F32),
        compiler_params=_cparams(("parallel",)),
        name="final_rmsnorm",
    )(x2, g.reshape(1, d).astype(F32))


SB_TILE = 256


def _sb_kernel(q_ref, k_ref, v_ref, tri_ref, o_ref):
    tile = SB_TILE
    qi = pl.program_id(2)
    tri = tri_ref[...]
    row = lax.broadcasted_iota(jnp.int32, (tile, tile), 0)
    col = lax.broadcasted_iota(jnp.int32, (tile, tile), 1)
    past = col < row

    for hh in range(2):
        lanes = slice(hh * HEAD_DIM, (hh + 1) * HEAD_DIM)
        q = q_ref[:, lanes] * jnp.asarray(ATT_SCALE, BF16)

        def step(start, r_sum, acc, diag):
            k = k_ref[pl.ds(start, tile), lanes]
            v = v_ref[pl.ds(start, tile), lanes]
            z = _dot_nt(q, k)
            sp = jnp.maximum(z, 0.0) + jnp.log1p(jnp.exp(-jnp.abs(z)))
            lb = jnp.where(past, -sp, 0.0) if diag else -sp
            hi, lo = _split_bf16(lb)
            suffix = _dot(hi, tri) + _dot(lo, tri)
            a = jnp.exp((z - sp) + suffix + r_sum)
            if diag:
                a = jnp.where(past, a, 0.0)
            acc = acc + _dot(a.astype(BF16), v)
            r_sum = r_sum + jnp.sum(lb, axis=-1, keepdims=True)
            return r_sum, acc

        r0 = jnp.zeros((tile, 1), F32)
        acc0 = jnp.zeros((tile, HEAD_DIM), F32)
        r1, acc1 = step(pl.multiple_of(qi * tile, tile), r0, acc0, True)

        def older(j, r_sum):
            return jnp.where(jnp.max(r_sum) > SB_EXIT, j - 1, -1)

        def cond(c):
            return c[0] >= 0

        def body(c):
            j, r_sum, acc = c
            r_sum, acc = step(pl.multiple_of(j * tile, tile), r_sum, acc, False)
            return older(j, r_sum), r_sum, acc

        _, _, acc = lax.while_loop(cond, body, (older(qi, r1), r1, acc1))
        o_ref[:, lanes] = acc.astype(o_ref.dtype)


def _sb_attention(qkv, batch, seq):
    tile = SB_TILE
    nq = seq // tile
    d_model = N_HEADS * HEAD_DIM
    ncol = d_model // LANE
    ii = np.arange(tile)
    tri = jnp.asarray(ii[:, None] > ii[None, :], dtype=BF16)
    return pl.pallas_call(
        _sb_kernel,
        grid=(batch, ncol, nq),
        in_specs=[pl.BlockSpec((tile, LANE), lambda b, h, i: (b * nq + i, h)),
                  pl.BlockSpec((seq, LANE), lambda b, h, i: (b, ncol + h)),
                  pl.BlockSpec((seq, LANE), lambda b, h, i: (b, 2 * ncol + h)),
                  pl.BlockSpec((tile, tile), lambda b, h, i: (0, 0))],
        out_specs=pl.BlockSpec((tile, LANE), lambda b, h, i: (b * nq + i, h)),
        out_shape=jax.ShapeDtypeStruct((batch * seq, d_model), BF16),
        compiler_params=_cparams(("parallel", "parallel", "arbitrary")),
        name="sb_attention",
    )(qkv, qkv, qkv, tri)


def _flash_step(q, k, v, mask, state):
    m, l, acc = state
    s = jnp.where(mask, _dot_nt(q, k), NEG)
    m_new = jnp.maximum(m, jnp.max(s, axis=-1, keepdims=True))
    p = jnp.where(mask, jnp.exp(s - m_new), 0.0)
    alpha = jnp.exp(m - m_new)
    l = alpha * l + jnp.sum(p, axis=-1, keepdims=True)
    acc = alpha * acc + _dot(p.astype(BF16), v)
    return m_new, l, acc


def _flash_init(rows):
    return (jnp.full((rows, 1), NEG, F32), jnp.zeros((rows, 1), F32), jnp.zeros((rows, HEAD_DIM), F32))


def _flash_out(state):
    _, l, acc = state
    return acc / jnp.where(l > 0.0, l, 1.0)


def _moba_kernel(q_ref, k_ref, v_ref, avg_ref, o_ref, kmean_ref):
    blk = MOBA_BLOCK
    qi = pl.program_id(2)

    @pl.when(qi == 0)
    def _():
        kmean_ref[...] = _dot(avg_ref[...], k_ref[...])

    row = lax.broadcasted_iota(jnp.int32, (blk, blk), 0)
    col = lax.broadcasted_iota(jnp.int32, (blk, blk), 1)
    causal = col <= row
    lane = lax.broadcasted_iota(jnp.int32, (blk, MOBA_BLK_PAD), 1)

    for hh in range(2):
        lanes = slice(hh * HEAD_DIM, (hh + 1) * HEAD_DIM)
        q_raw = q_ref[:, lanes]
        km_hi, km_lo = _split_bf16(kmean_ref[:, lanes])
        gate = _dot_nt(q_raw, km_hi) + _dot_nt(q_raw, km_lo)
        gate = jnp.where(lane < qi, gate, NEG)
        sel = jnp.where(lane < qi, _topk_mask(gate, MOBA_TOP_K), 0.0)
        q = q_raw * jnp.asarray(ATT_SCALE, BF16)

        start = pl.multiple_of(qi * blk, blk)
        state = _flash_step(q, k_ref[pl.ds(start, blk), lanes], v_ref[pl.ds(start, blk), lanes],
                            causal, _flash_init(blk))

        def body(n, st):
            picked = jnp.max(jnp.where(lane == n, sel, 0.0), axis=-1, keepdims=True) > 0.5
            mask = jnp.broadcast_to(picked, (blk, blk))
            s0 = pl.multiple_of(n * blk, blk)
            return _flash_step(q, k_ref[pl.ds(s0, blk), lanes], v_ref[pl.ds(s0, blk), lanes], mask, st)

        state = lax.fori_loop(0, qi, body, state)
        o_ref[:, lanes] = _flash_out(state).astype(o_ref.dtype)


def _moba_attention(qkv, batch, seq):
    blk = MOBA_BLOCK
    nq = seq // blk
    d_model = N_HEADS * HEAD_DIM
    ncol = d_model // LANE
    avg = np.zeros((MOBA_BLK_PAD, seq), np.float32)
    for n in range(nq):
        avg[n, n * blk:(n + 1) * blk] = 1.0 / blk
    avg = jnp.asarray(avg, dtype=BF16)
    return pl.pallas_call(
        _moba_kernel,
        grid=(batch, ncol, nq),
        in_specs=[pl.BlockSpec((blk, LANE), lambda b, h, i: (b * nq + i, h)),
                  pl.BlockSpec((seq, LANE), lambda b, h, i: (b, ncol + h)),
                  pl.BlockSpec((seq, LANE), lambda b, h, i: (b, 2 * ncol + h)),
                  pl.BlockSpec((MOBA_BLK_PAD, seq), lambda b, h, i: (0, 0))],
        out_specs=pl.BlockSpec((blk, LANE), lambda b, h, i: (b * nq + i, h)),
        out_shape=jax.ShapeDtypeStruct((batch * seq, d_model), BF16),
        scratch_shapes=[pltpu.VMEM((MOBA_BLK_PAD, LANE), F32)],
        compiler_params=_cparams(("parallel", "parallel", "arbitrary")),
        name="moba_attention",
    )(qkv, qkv, qkv, avg)


def _nsa_compress_kernel(kr_ref, vr_ref, pk_ref, pv_ref, wk_ref, wv_ref, c_ref, sa_ref, sb_ref, kc_ref, vc_ref):
    nc = kr_ref.shape[0]
    rowi = lax.broadcasted_iota(jnp.int32, (nc, 1), 0)

    def windows(x_ref, pos_ref, w_ref):
        x = x_ref[...].astype(F32)
        lo = _dot((x + pos_ref[0:1, :]).astype(BF16), w_ref[0])
        hi = _dot((x + pos_ref[1:2, :]).astype(BF16), w_ref[1])
        hi_next = jnp.where(rowi < nc - 1, pltpu.roll(hi, nc - 1, 0), 0.0)
        return lo + hi_next

    kk = windows(kr_ref, pk_ref, wk_ref)
    d = HEAD_DIM
    kc = kk[:, 0:d] * c_ref[...] + kk[:, d:2 * d] * sb_ref[...] + kk[:, 2 * d:3 * d] * sa_ref[...]
    kc_ref[...] = kc.astype(kc_ref.dtype)
    vv = windows(vr_ref, pv_ref, wv_ref)
    vc_ref[...] = vv[:, 0:d].astype(vc_ref.dtype)


def _nsa_compress(k_cmp, v_cmp, pos_k, pos_v, w_k, w_v, batch, seq):
    g, d, st = NSA_GROUPS, HEAD_DIM, NSA_CMP_STRIDE
    nc = seq // st

    def chunks(x):
        return x.reshape(batch, nc, st, g, d).transpose(0, 3, 1, 2, 4).reshape(batch, g, nc, st * d)

    def weights(w):
        w3 = w.reshape(NSA_CMP_LEN, d, d)
        cat = jnp.concatenate([w3, jnp.roll(w3, ROPE_HALF, axis=2), jnp.roll(w3, -ROPE_HALF, axis=2)], axis=2)
        return cat.reshape(2, st * d, 3 * d).astype(BF16)

    def positions(p):
        return p.reshape(2, st * d).astype(F32)

    cmp_end = jnp.arange(nc) * st + NSA_CMP_LEN - 1
    tabs = _rope_tables(cmp_end, d)
    blk4 = pl.BlockSpec((None, None, nc, st * d), lambda b, gi: (b, gi, 0, 0))
    out4 = pl.BlockSpec((None, None, nc, d), lambda b, gi: (b, gi, 0, 0))
    const2 = lambda shape: pl.BlockSpec(shape, lambda b, gi: (0,) * len(shape))
    return pl.pallas_call(
        _nsa_compress_kernel,
        grid=(batch, g),
        in_specs=[blk4, blk4, const2((2, st * d)), const2((2, st * d)),
                  const2((2, st * d, 3 * d)), const2((2, st * d, 3 * d)),
                  const2((nc, d)), const2((nc, d)), const2((nc, d))],
        out_specs=[out4, out4],
        out_shape=[jax.ShapeDtypeStruct((batch, g, nc, d), BF16)] * 2,
        compiler_params=_cparams(("parallel", "parallel")),
        name="nsa_compress",
    )(chunks(k_cmp), chunks(v_cmp), positions(pos_k), positions(pos_v), weights(w_k), weights(w_v), *tabs)


NSA_TQ = 128
NSA_TK = 128


def _nsa_kernel(q_ref, kc_ref, vc_ref, ks_ref, vs_ref, kw_ref, vw_ref, gate_ref, c2s_ref, exp_ref, o_ref):
    tq, tk, d = NSA_TQ, NSA_TK, HEAD_DIM
    qi = pl.program_id(2)
    t0 = qi * tq
    nc = kc_ref.shape[0]
    tpos = t0 + lax.broadcasted_iota(jnp.int32, (tq, 1), 0)
    gates = jax.nn.sigmoid(gate_ref[...])
    qs = [q_ref[:, h * d:(h + 1) * d] * jnp.asarray(ATT_SCALE, BF16) for h in range(NSA_HPG)]

    cmp_end = lax.broadcasted_iota(jnp.int32, (tq, nc), 1) * NSA_CMP_STRIDE + (NSA_CMP_LEN - 1)
    vis_c = cmp_end <= tpos
    kc = kc_ref[...]
    vc = vc_ref[...]
    psum = jnp.zeros((tq, nc), F32)
    outs = []
    for h in range(NSA_HPG):
        s = jnp.where(vis_c, _dot_nt(qs[h], kc), NEG)
        e = jnp.where(vis_c, jnp.exp(s - jnp.max(s, axis=-1, keepdims=True)), 0.0)
        l = jnp.sum(e, axis=-1, keepdims=True)
        p = e / jnp.where(l > 0.0, l, 1.0)
        psum = psum + p
        outs.append(gates[:, 3 * h:3 * h + 1] * _dot(p.astype(BF16), vc))
    p_hi, p_lo = _split_bf16(psum)
    imp = _dot(p_hi, c2s_ref[...]) + _dot(p_lo, c2s_ref[...])
    blk = lax.broadcasted_iota(jnp.int32, (tq, NSA_SEL_PAD), 1)
    own = tpos // NSA_SEL_LEN
    forced = (blk == 0) | (blk == own) | (blk == own - 1)
    imp = jnp.where(blk <= own, imp + jnp.where(forced, NSA_BONUS, 0.0), NEG)
    sel = _topk_mask(imp, NSA_TOP_N).astype(BF16)

    row = lax.broadcasted_iota(jnp.int32, (tq, tk), 0)
    col = lax.broadcasted_iota(jnp.int32, (tq, tk), 1)

    def sel_body(kb, states):
        start = pl.multiple_of(kb * tk, tk)
        picked = _dot(sel, exp_ref[kb]) > 0.5
        mask = picked & (start + col <= t0 + row)
        k = ks_ref[pl.ds(start, tk), :]
        v = vs_ref[pl.ds(start, tk), :]
        return tuple(_flash_step(qs[h], k, v, mask, states[h]) for h in range(NSA_HPG))

    init = tuple(_flash_init(tq) for _ in range(NSA_HPG))
    states = lax.fori_loop(0, qi + 1, sel_body, init)
    for h in range(NSA_HPG):
        outs[h] = outs[h] + gates[:, 3 * h + 1:3 * h + 2] * _flash_out(states[h])

    def win_body(kb, states):
        start = pl.multiple_of(kb * tk, tk)
        kpos = start + col
        mask = (kpos <= t0 + row) & (kpos > t0 + row - NSA_WINDOW)
        k = kw_ref[pl.ds(start, tk), :]
        v = vw_ref[pl.ds(start, tk), :]
        return tuple(_flash_step(qs[h], k, v, mask, states[h]) for h in range(NSA_HPG))

    first = jnp.maximum(qi - NSA_WINDOW // tk, 0)
    states = lax.fori_loop(first, qi + 1, win_body, init)
    for h in range(NSA_HPG):
        o = outs[h] + gates[:, 3 * h + 2:3 * h + 3] * _flash_out(states[h])
        o_ref[:, h * d:(h + 1) * d] = o.astype(o_ref.dtype)


def _nsa_attention(q_all, kc, vc, ks, vs, kw, vw, gates, batch, seq):
    tq, tk, d, g = NSA_TQ, NSA_TK, HEAD_DIM, NSA_GROUPS
    nq = seq // tq
    nc = seq // NSA_CMP_STRIDE
    n_sel = seq // NSA_SEL_LEN
    d_model = N_HEADS * HEAD_DIM
    cs = np.arange(nc) * NSA_CMP_STRIDE
    ss = np.arange(n_sel) * NSA_SEL_LEN
    ov = np.clip(np.minimum(cs[:, None] + NSA_CMP_LEN, ss[None, :] + NSA_SEL_LEN)
                 - np.maximum(cs[:, None], ss[None, :]), 0, None) / NSA_CMP_LEN
    c2s = np.zeros((nc, NSA_SEL_PAD), np.float32)
    c2s[:, :n_sel] = ov
    c2s[nc - 1, :] = 0.0
    key_blk = (np.arange(seq) // NSA_SEL_LEN).reshape(seq // tk, tk)
    expand = (np.arange(NSA_SEL_PAD)[None, :, None] == key_blk[:, None, :]).astype(np.float32)
    kv_spec = lambda n: pl.BlockSpec((None, None, n, d), lambda b, gi, i: (b, gi, 0, 0))
    return pl.pallas_call(
        _nsa_kernel,
        grid=(batch, g, nq),
        in_specs=[pl.BlockSpec((tq, NSA_HPG * d), lambda b, gi, i: (b * nq + i, gi)),
                  kv_spec(nc), kv_spec(nc), kv_spec(seq), kv_spec(seq), kv_spec(seq), kv_spec(seq),
                  pl.BlockSpec((tq, LANE), lambda b, gi, i: (b * nq + i, gi)),
                  pl.BlockSpec((nc, NSA_SEL_PAD), lambda b, gi, i: (0, 0)),
                  pl.BlockSpec((seq // tk, NSA_SEL_PAD, tk), lambda b, gi, i: (0, 0, 0))],
        out_specs=pl.BlockSpec((tq, NSA_HPG * d), lambda b, gi, i: (b * nq + i, gi)),
        out_shape=jax.ShapeDtypeStruct((batch * seq, d_model), BF16),
        compiler_params=_cparams(("parallel", "parallel", "arbitrary")),
        name="nsa_attention",
    )(q_all, kc, vc, ks, vs, kw, vw, gates, jnp.asarray(c2s, dtype=BF16), jnp.asarray(expand, dtype=BF16))


def _nsa_mixer(x2, g_norm, w_in, pos_k, pos_v, w_k, w_v, w_out, batch, seq):
    d_model, kv = N_HEADS * HEAD_DIM, NSA_KV
    sec = lambda i: w_in[:, d_model + i * kv: d_model + (i + 1) * kv]
    w_main = jnp.concatenate([w_in[:, :d_model], sec(2), sec(4), sec(0), sec(1), sec(3), sec(5)], axis=1)
    proj = _norm_matmul(x2, g_norm, w_main, seq, tn=512, n_rope=3)
    w_gate = w_in[:, d_model + 6 * kv:].reshape(d_model, NSA_GROUPS, 3 * NSA_HPG)
    w_gate = jnp.pad(w_gate, ((0, 0), (0, 0), (0, LANE - 3 * NSA_HPG))).reshape(d_model, NSA_GROUPS * LANE)
    gates = _norm_matmul(x2, g_norm, w_gate, seq, tn=NSA_GROUPS * LANE, out_dtype=F32)

    col = lambda i: proj[:, d_model + i * kv: d_model + (i + 1) * kv]
    heads = lambda x: x.reshape(batch, seq, NSA_GROUPS, HEAD_DIM).transpose(0, 2, 1, 3)
    kc, vc = _nsa_compress(col(2), col(3), pos_k, pos_v, w_k, w_v, batch, seq)
    o = _nsa_attention(proj, kc, vc, heads(col(0)), heads(col(4)), heads(col(1)), heads(col(5)),
                       gates, batch, seq)
    return _proj_resid(o, w_out, x2)


PEER_SEL_TT = 256
PEER_PAIRS = [(i, j) for i in range(PEER_TOP_K + 1) for j in range(PEER_TOP_K + 1)
              if (i + 1) * (j + 1) <= PEER_TOP_K + 1]


def _peer_select_kernel(q_ref, k1_ref, k2_ref, a1_ref, e1_ref, s2_ref, e2_ref):
    q = q_ref[...]
    q_hi, q_lo = _split_bf16(q)
    nh, nr = PEER_HEADS, PEER_TOP_K + 1

    def scores(k_ref, off):
        k_hi, k_lo = _split_bf16(k_ref[...])
        out = []
        for h in range(nh):
            lanes = slice(off + h * PEER_HALF, off + (h + 1) * PEER_HALF)
            out.append(_dot_nt(k_hi, q_hi[:, lanes]) + _dot_nt(k_hi, q_lo[:, lanes])
                       + _dot_nt(k_lo, q_hi[:, lanes]))
        return out

    def top_values(s):
        vals, work = [], s
        for _ in range(nr):
            m = jnp.max(work, axis=0, keepdims=True)
            vals.append(m)
            work = jnp.where(work == m, LOW, work)
        return vals

    s1 = scores(k1_ref, 0)
    s2 = scores(k2_ref, nh * PEER_HALF)
    t1 = [top_values(s) for s in s1]
    t2 = [top_values(s) for s in s2]
    r1 = [jnp.concatenate([t1[h][r] for h in range(nh)], axis=0) for r in range(nr)]
    r2 = [jnp.concatenate([t2[h][r] for h in range(nh)], axis=0) for r in range(nr)]
    cands = [r1[i] + r2[j] for (i, j) in PEER_PAIRS]
    work, tops = list(cands), []
    for _ in range(nr):
        m = functools.reduce(jnp.maximum, work)
        tops.append(m)
        work = [jnp.where(c == m, LOW, c) for c in work]
    thr = 0.5 * (tops[PEER_TOP_K - 1] + tops[PEER_TOP_K])
    cmax = tops[0]
    z = functools.reduce(lambda a, b: a + b, [jnp.where(c >= thr, jnp.exp(c - cmax), 0.0) for c in cands])
    inv_z = 1.0 / z
    for h in range(nh):
        a1_ref[h] = thr[h:h + 1, :] - s1[h]
        e1_ref[h] = jnp.exp(s1[h] - r1[0][h:h + 1, :])
        s2_ref[h] = s2[h]
        e2_ref[h] = jnp.exp(s2[h] - r2[0][h:h + 1, :]) * inv_z[h:h + 1, :]


def _peer_select(q, sub_keys):
    t = q.shape[0]
    tt = min(PEER_SEL_TT, t)
    out_spec = pl.BlockSpec((PEER_HEADS, PEER_KEYS, tt), lambda i: (0, 0, i))
    return pl.pallas_call(
        _peer_select_kernel,
        grid=(t // tt,),
        in_specs=[pl.BlockSpec((tt, q.shape[1]), lambda i: (i, 0)),
                  pl.BlockSpec((PEER_KEYS, PEER_HALF), lambda i: (0, 0)),
                  pl.BlockSpec((PEER_KEYS, PEER_HALF), lambda i: (0, 0))],
        out_specs=[out_spec] * 4,
        out_shape=[jax.ShapeDtypeStruct((PEER_HEADS, PEER_KEYS, t), F32)] * 4,
        compiler_params=_cparams(("parallel",)),
        name="peer_select",
    )(q, sub_keys[0].astype(F32), sub_keys[1].astype(F32))


PEER_TT = 512
PEER_NE = 256


def _gelu(x):
    return 0.5 * x * (1.0 + lax.erf(x * np.float32(np.sqrt(0.5))))


def _peer_main_kernel(x_ref, g_ref, u_ref, vt_ref, a1_ref, e1_ref, s2_ref, e2_ref, o_ref, xn_ref, acc_ref):
    e = pl.program_id(1)
    ne = u_ref.shape[0]
    tt = x_ref.shape[0]

    @pl.when(e == 0)
    def _():
        x = x_ref[...]
        y = x * lax.rsqrt(jnp.mean(x * x, axis=-1, keepdims=True) + NORM_EPS)
        xn_ref[...] = (y * g_ref[...]).astype(BF16)
        acc_ref[...] = jnp.zeros_like(acc_ref)

    act = _gelu(_dot_nt(u_ref[...], xn_ref[...]))
    parts = []
    for c in range(ne // PEER_KEYS):
        i1 = e * (ne // PEER_KEYS) + c
        gsum = jnp.zeros((PEER_KEYS, tt), F32)
        for h in range(PEER_HEADS):
            a1 = a1_ref[h, pl.ds(i1, 1), :]
            e1 = e1_ref[h, pl.ds(i1, 1), :]
            gsum = gsum + jnp.where(s2_ref[h] >= a1, e2_ref[h], 0.0) * e1
        parts.append((gsum * act[c * PEER_KEYS:(c + 1) * PEER_KEYS, :]).astype(BF16))
    w = jnp.concatenate(parts, axis=0)
    acc_ref[...] += _dot(vt_ref[...], w)

    @pl.when(e == pl.num_programs(1) - 1)
    def _():
        o_ref[...] = x_ref[...] + acc_ref[...].T


def _peer_mixer(x2, g_norm, w_q, sub_keys, u, v, seq):
    t, d = x2.shape
    w_q2 = w_q.reshape(d, PEER_HEADS, 2, PEER_HALF).transpose(0, 2, 1, 3).reshape(d, 2 * PEER_HEADS * PEER_HALF)
    q = _norm_matmul3(x2, g_norm, w_q2)
    a1, e1, s2, e2 = _peer_select(q, sub_keys)
    tt = min(PEER_TT, t)
    ne = PEER_NE
    n_exp = u.shape[0]
    tok_spec = pl.BlockSpec((PEER_HEADS, PEER_KEYS, tt), lambda i, e: (0, 0, i))
    return pl.pallas_call(
        _peer_main_kernel,
        grid=(t // tt, n_exp // ne),
        in_specs=[pl.BlockSpec((tt, d), lambda i, e: (i, 0)),
                  pl.BlockSpec((1, d), lambda i, e: (0, 0)),
                  pl.BlockSpec((ne, d), lambda i, e: (e, 0)),
                  pl.BlockSpec((d, ne), lambda i, e: (0, e)),
                  tok_spec, tok_spec, tok_spec, tok_spec],
        out_specs=pl.BlockSpec((tt, d), lambda i, e: (i, 0)),
        out_shape=jax.ShapeDtypeStruct((t, d), F32),
        scratch_shapes=[pltpu.VMEM((tt, d), BF16), pltpu.VMEM((d, tt), F32)],
        compiler_params=_cparams(("parallel", "arbitrary")),
        name="peer_main",
    )(x2, g_norm.reshape(1, d).astype(F32), u.astype(BF16), v.T.astype(BF16), a1, e1, s2, e2)


def _sb_mixer(x2, g_norm, w_in, w_out, batch, seq):
    qkv = _norm_matmul(x2, g_norm, w_in, seq, tn=512)
    return _proj_resid(_sb_attention(qkv, batch, seq), w_out, x2)


def _moba_mixer(x2, g_norm, w_in, w_out, batch, seq):
    d_model = N_HEADS * HEAD_DIM
    qkv = _norm_matmul(x2, g_norm, w_in, seq, tn=512, n_rope=2 * d_model // 512)
    return _proj_resid(_moba_attention(qkv, batch, seq), w_out, x2)


def kernel(x, l0_norm_mix, l0_sb_w_in, l0_sb_w_out, l0_norm_ffn, l0_peer_w_q, l0_peer_sub_keys, l0_peer_u, l0_peer_v, l1_norm_mix, l1_nsa_w_in, l1_nsa_cmp_pos_k, l1_nsa_cmp_pos_v, l1_nsa_cmp_w_k, l1_nsa_cmp_w_v, l1_nsa_w_out, l1_norm_ffn, l1_peer_w_q, l1_peer_sub_keys, l1_peer_u, l1_peer_v, l2_norm_mix, l2_moba_w_in, l2_moba_w_out, l2_norm_ffn, l2_peer_w_q, l2_peer_sub_keys, l2_peer_u, l2_peer_v, l3_norm_mix, l3_sb_w_in, l3_sb_w_out, l3_norm_ffn, l3_peer_w_q, l3_peer_sub_keys, l3_peer_u, l3_peer_v, final_norm):
    batch, seq, d = x.shape
    x2 = x.reshape(batch * seq, d)
    x2 = _sb_mixer(x2, l0_norm_mix, l0_sb_w_in, l0_sb_w_out, batch, seq)
    x2 = _peer_mixer(x2, l0_norm_ffn, l0_peer_w_q, l0_peer_sub_keys, l0_peer_u, l0_peer_v, seq)
    x2 = _nsa_mixer(x2, l1_norm_mix, l1_nsa_w_in, l1_nsa_cmp_pos_k, l1_nsa_cmp_pos_v, l1_nsa_cmp_w_k,
                    l1_nsa_cmp_w_v, l1_nsa_w_out, batch, seq)
    x2 = _peer_mixer(x2, l1_norm_ffn, l1_peer_w_q, l1_peer_sub_keys, l1_peer_u, l1_peer_v, seq)
    x2 = _moba_mixer(x2, l2_norm_mix, l2_moba_w_in, l2_moba_w_out, batch, seq)
    x2 = _peer_mixer(x2, l2_norm_ffn, l2_peer_w_q, l2_peer_sub_keys, l2_peer_u, l2_peer_v, seq)
    x2 = _sb_mixer(x2, l3_norm_mix, l3_sb_w_in, l3_sb_w_out, batch, seq)
    x2 = _peer_mixer(x2, l3_norm_ffn, l3_peer_w_q, l3_peer_sub_keys, l3_peer_u, l3_peer_v, seq)
    return _rmsnorm(x2, final_norm).reshape(batch, seq, d)
```

```python
import functools

import numpy as np
import jax
import jax.numpy as jnp
from jax import lax
from jax.experimental import pallas as pl
from jax.experimental.pallas import tpu as pltpu

F32 = jnp.float32
BF16 = jnp.bfloat16

N_HEADS = 16
HEAD_DIM = 64
ROPE_DIM = 16
ROPE_HALF = ROPE_DIM // 2
ROPE_THETA = 500000.0
NORM_EPS = 1e-6
NEG = -1e30
LOW = -3e38
ATT_SCALE = HEAD_DIM ** -0.5

NSA_GROUPS = 4
NSA_HPG = N_HEADS // NSA_GROUPS
NSA_KV = NSA_GROUPS * HEAD_DIM
NSA_CMP_LEN = 32
NSA_CMP_STRIDE = 16
NSA_SEL_LEN = 64
NSA_TOP_N = 16
NSA_WINDOW = 512
NSA_BONUS = 1e3
NSA_SEL_PAD = 128

MOBA_BLOCK = 256
MOBA_TOP_K = 3
MOBA_BLK_PAD = 128

PEER_HEADS = 8
PEER_KEYS = 128
PEER_TOP_K = 16
PEER_HALF = 64

SB_EXIT = -110.0

LANE = 128
VMEM_LIMIT = 56 << 20


def _cparams(sem, vmem=VMEM_LIMIT):
    return pltpu.CompilerParams(dimension_semantics=sem, vmem_limit_bytes=vmem)


def _dot(a, b):
    return jnp.dot(a, b, preferred_element_type=F32)


def _dot_nt(a, b):
    return lax.dot_general(a, b, (((1,), (1,)), ((), ())), preferred_element_type=F32)


def _split_bf16(x):
    hi = x.astype(BF16)
    lo = (x - hi.astype(F32)).astype(BF16)
    return hi, lo


def _topk_mask(vals, k):
    lane = lax.broadcasted_iota(jnp.int32, vals.shape, 1)
    work = vals
    sel = jnp.zeros(vals.shape, F32)
    for _ in range(k):
        m = jnp.max(work, axis=-1, keepdims=True)
        first = jnp.min(jnp.where(work == m, lane, LANE), axis=-1, keepdims=True)
        hit = lane == first
        sel = jnp.where(hit, 1.0, sel)
        work = jnp.where(hit, LOW, work)
    return sel


def _norm_matmul_kernel(*refs, n_rope, tn):
    if n_rope:
        x_ref, g_ref, w_ref, c_ref, sa_ref, sb_ref, o_ref, xn_ref = refs
    else:
        x_ref, g_ref, w_ref, o_ref, xn_ref = refs
    j = pl.program_id(1)

    @pl.when(j == 0)
    def _():
        x = x_ref[...]
        y = x * lax.rsqrt(jnp.mean(x * x, axis=-1, keepdims=True) + NORM_EPS)
        xn_ref[...] = (y * g_ref[...]).astype(BF16)

    acc = _dot(xn_ref[...], w_ref[...])
    if n_rope:
        @pl.when(j < n_rope)
        def _():
            r = (acc * c_ref[...] + pltpu.roll(acc, ROPE_HALF, 1) * sb_ref[...]
                 + pltpu.roll(acc, tn - ROPE_HALF, 1) * sa_ref[...])
            o_ref[...] = r.astype(o_ref.dtype)

        @pl.when(j >= n_rope)
        def _():
            o_ref[...] = acc.astype(o_ref.dtype)
    else:
        o_ref[...] = acc.astype(o_ref.dtype)


def _rope_tables(pos, width):
    inv = ROPE_THETA ** (-jnp.arange(0, ROPE_DIM, 2, dtype=F32) / ROPE_DIM)
    ang = pos.astype(F32)[:, None] * inv[None, :]
    cos, sin = jnp.cos(ang), jnp.sin(ang)
    n = pos.shape[0]
    rest = HEAD_DIM - ROPE_DIM
    c = jnp.concatenate([cos, cos, jnp.ones((n, rest), F32)], axis=1)
    sa = jnp.concatenate([-sin, jnp.zeros((n, ROPE_HALF + rest), F32)], axis=1)
    sb = jnp.concatenate([jnp.zeros((n, ROPE_HALF), F32), sin, jnp.zeros((n, rest), F32)], axis=1)
    reps = width // HEAD_DIM
    return jnp.tile(c, (1, reps)), jnp.tile(sa, (1, reps)), jnp.tile(sb, (1, reps))


def _norm_matmul(x2, g, w, seq, *, tn, n_rope=0, out_dtype=BF16, tt=512):
    t, d = x2.shape
    n = w.shape[1]
    tt = min(tt, seq)
    grid = (t // tt, n // tn)
    in_specs = [pl.BlockSpec((tt, d), lambda i, j: (i, 0)),
                pl.BlockSpec((1, d), lambda i, j: (0, 0)),
                pl.BlockSpec((d, tn), lambda i, j: (0, j))]
    args = [x2, g.reshape(1, d).astype(F32), w.astype(BF16)]
    if n_rope:
        nper = seq // tt
        tabs = _rope_tables(jnp.arange(seq), tn)
        in_specs += [pl.BlockSpec((tt, tn), lambda i, j: (i % nper, 0))] * 3
        args += list(tabs)
    return pl.pallas_call(
        functools.partial(_norm_matmul_kernel, n_rope=n_rope, tn=tn),
        grid=grid, in_specs=in_specs,
        out_specs=pl.BlockSpec((tt, tn), lambda i, j: (i, j)),
        out_shape=jax.ShapeDtypeStruct((t, n), out_dtype),
        scratch_shapes=[pltpu.VMEM((tt, d), BF16)],
        compiler_params=_cparams(("parallel", "arbitrary")),
        name="norm_matmul",
    )(*args)


def _norm_matmul3_kernel(x_ref, g_ref, wh_ref, wl_ref, o_ref, xh_ref, xl_ref):
    @pl.when(pl.program_id(1) == 0)
    def _():
        x = x_ref[...]
        y = x * lax.rsqrt(jnp.mean(x * x, axis=-1, keepdims=True) + NORM_EPS) * g_ref[...]
        xh_ref[...], xl_ref[...] = _split_bf16(y)

    o_ref[...] = (_dot(xh_ref[...], wh_ref[...]) + _dot(xl_ref[...], wh_ref[...])
                  + _dot(xh_ref[...], wl_ref[...]))


def _norm_matmul3(x2, g, w, *, tn=512, tt=512):
    t, d = x2.shape
    n = w.shape[1]
    tt = min(tt, t)
    w_hi, w_lo = _split_bf16(w.astype(F32))
    w_spec = pl.BlockSpec((d, tn), lambda i, j: (0, j))
    return pl.pallas_call(
        _norm_matmul3_kernel,
        grid=(t // tt, n // tn),
        in_specs=[pl.BlockSpec((tt, d), lambda i, j: (i, 0)),
                  pl.BlockSpec((1, d), lambda i, j: (0, 0)), w_spec, w_spec],
        out_specs=pl.BlockSpec((tt, tn), lambda i, j: (i, j)),
        out_shape=jax.ShapeDtypeStruct((t, n), F32),
        scratch_shapes=[pltpu.VMEM((tt, d), BF16), pltpu.VMEM((tt, d), BF16)],
        compiler_params=_cparams(("parallel", "arbitrary")),
        name="norm_matmul3",
    )(x2, g.reshape(1, d).astype(F32), w_hi, w_lo)


def _proj_resid_kernel(a_ref, w_ref, r_ref, o_ref):
    o_ref[...] = r_ref[...] + _dot(a_ref[...], w_ref[...])


def _proj_resid(a, w, resid, *, tt=512, tn=512):
    t, d = a.shape
    n = w.shape[1]
    tt = min(tt, t)
    return pl.pallas_call(
        _proj_resid_kernel,
        grid=(t // tt, n // tn),
        in_specs=[pl.BlockSpec((tt, d), lambda i, j: (i, 0)),
                  pl.BlockSpec((d, tn), lambda i, j: (0, j)),
                  pl.BlockSpec((tt, tn), lambda i, j: (i, j))],
        out_specs=pl.BlockSpec((tt, tn), lambda i, j: (i, j)),
        out_shape=jax.ShapeDtypeStruct((t, n), F32),
        compiler_params=_cparams(("parallel", "parallel")),
        name="proj_resid",
    )(a, w.astype(BF16), resid)


def _rmsnorm_kernel(x_ref, g_ref, o_ref):
    x = x_ref[...]
    y = x * lax.rsqrt(jnp.mean(x * x, axis=-1, keepdims=True) + NORM_EPS)
    o_ref[...] = y * g_ref[...]


def _rmsnorm(x2, g, *, tt=512):
    t, d = x2.shape
    tt = min(tt, t)
    return pl.pallas_call(
        _rmsnorm_kernel,
        grid=(t // tt,),
        in_specs=[pl.BlockSpec((tt, d), lambda i: (i, 0)), pl.BlockSpec((1, d), lambda i: (0, 0))],
        out_specs=pl.BlockSpec((tt, d), lambda i: (i, 0)),
        out_shape=jax.ShapeDtypeStruct((t, d), F32),
        compiler_params=_cparams(("parallel",)),
        name="final_rmsnorm",
    )(x2, g.reshape(1, d).astype(F32))


SB_TILE = 256


def _sb_kernel(q_ref, k_ref, v_ref, tri_ref, o_ref):
    tile = SB_TILE
    qi = pl.program_id(2)
    tri = tri_ref[...]
    row = lax.broadcasted_iota(jnp.int32, (tile, tile), 0)
    col = lax.broadcasted_iota(jnp.int32, (tile, tile), 1)
    past = col < row

    for hh in range(2):
        lanes = slice(hh * HEAD_DIM, (hh + 1) * HEAD_DIM)
        q = q_ref[:, lanes] * jnp.asarray(ATT_SCALE, BF16)

        def step(start, r_sum, acc, diag):
            k = k_ref[pl.ds(start, tile), lanes]
            v = v_ref[pl.ds(start, tile), lanes]
            z = _dot_nt(q, k)
            sp = jnp.maximum(z, 0.0) + jnp.log1p(jnp.exp(-jnp.abs(z)))
            lb = jnp.where(past, -sp, 0.0) if diag else -sp
            hi, lo = _split_bf16(lb)
            suffix = _dot(hi, tri) + _dot(lo, tri)
            a = jnp.exp((z - sp) + suffix + r_sum)
            if diag:
                a = jnp.where(past, a, 0.0)
            acc = acc + _dot(a.astype(BF16), v)
            r_sum = r_sum + jnp.sum(lb, axis=-1, keepdims=True)
            return r_sum, acc

        r0 = jnp.zeros((tile, 1), F32)
        acc0 = jnp.zeros((tile, HEAD_DIM), F32)
        r1, acc1 = step(pl.multiple_of(qi * tile, tile), r0, acc0, True)

        def older(j, r_sum):
            return jnp.where(jnp.max(r_sum) > SB_EXIT, j - 1, -1)

        def cond(c):
            return c[0] >= 0

        def body(c):
            j, r_sum, acc = c
            r_sum, acc = step(pl.multiple_of(j * tile, tile), r_sum, acc, False)
            return older(j, r_sum), r_sum, acc

        _, _, acc = lax.while_loop(cond, body, (older(qi, r1), r1, acc1))
        o_ref[:, lanes] = acc.astype(o_ref.dtype)


def _sb_attention(qkv, batch, seq):
    tile = SB_TILE
    nq = seq // tile
    d_model = N_HEADS * HEAD_DIM
    ncol = d_model // LANE
    ii = np.arange(tile)
    tri = jnp.asarray(ii[:, None] > ii[None, :], dtype=BF16)
    return pl.pallas_call(
        _sb_kernel,
        grid=(batch, ncol, nq),
        in_specs=[pl.BlockSpec((tile, LANE), lambda b, h, i: (b * nq + i, h)),
                  pl.BlockSpec((seq, LANE), lambda b, h, i: (b, ncol + h)),
                  pl.BlockSpec((seq, LANE), lambda b, h, i: (b, 2 * ncol + h)),
                  pl.BlockSpec((tile, tile), lambda b, h, i: (0, 0))],
        out_specs=pl.BlockSpec((tile, LANE), lambda b, h, i: (b * nq + i, h)),
        out_shape=jax.ShapeDtypeStruct((batch * seq, d_model), BF16),
        compiler_params=_cparams(("parallel", "parallel", "arbitrary")),
        name="sb_attention",
    )(qkv, qkv, qkv, tri)


def _flash_first(q, k, v, mask):
    s = jnp.where(mask, _dot_nt(q, k), NEG)
    m = jnp.max(s, axis=-1, keepdims=True)
    p = jnp.exp(s - m)
    return m, jnp.sum(p, axis=-1, keepdims=True), _dot(p.astype(BF16), v)


def _flash_next(q, k, v, mask, state):
    m, l, acc = state
    s = jnp.where(mask, _dot_nt(q, k), NEG)
    m_new = jnp.maximum(m, jnp.max(s, axis=-1, keepdims=True))
    p = jnp.exp(s - m_new)
    alpha = jnp.exp(m - m_new)
    l = alpha * l + jnp.sum(p, axis=-1, keepdims=True)
    acc = alpha * acc + _dot(p.astype(BF16), v)
    return m_new, l, acc


def _flash_out(state):
    _, l, acc = state
    return acc / l


MOBA_WALK = 2


def _moba_kernel(q_ref, k_ref, v_ref, avg_ref, exp_ref, o_ref, kmean_ref):
    blk = MOBA_BLOCK
    wide = MOBA_WALK * blk
    qi = pl.program_id(2)

    @pl.when(qi == 0)
    def _():
        kmean_ref[...] = _dot(avg_ref[...], k_ref[...])

    row = lax.broadcasted_iota(jnp.int32, (blk, blk), 0)
    col = lax.broadcasted_iota(jnp.int32, (blk, blk), 1)
    causal = col <= row
    lane = lax.broadcasted_iota(jnp.int32, (blk, MOBA_BLK_PAD), 1)
    start = pl.multiple_of(qi * blk, blk)
    head_lanes = [slice(hh * HEAD_DIM, (hh + 1) * HEAD_DIM) for hh in range(2)]

    qs, sels, states = [], [], []
    for lanes in head_lanes:
        q_raw = q_ref[:, lanes]
        km_hi, km_lo = _split_bf16(kmean_ref[:, lanes])
        gate = _dot_nt(q_raw, km_hi) + _dot_nt(q_raw, km_lo)
        gate = jnp.where(lane < qi, gate, NEG)
        sels.append(jnp.where(lane < qi, _topk_mask(gate, MOBA_TOP_K), 0.0).astype(BF16))
        qs.append(q_raw * jnp.asarray(ATT_SCALE, BF16))
        states.append(_flash_first(qs[-1], k_ref[pl.ds(start, blk), lanes], v_ref[pl.ds(start, blk), lanes], causal))

    def body(p, sts):
        s0 = pl.multiple_of(p * wide, wide)
        out = []
        for hh, lanes in enumerate(head_lanes):
            mask = _dot(sels[hh], exp_ref[p]) > 0.5
            out.append(_flash_next(qs[hh], k_ref[pl.ds(s0, wide), lanes], v_ref[pl.ds(s0, wide), lanes],
                                   mask, sts[hh]))
        return tuple(out)

    states = lax.fori_loop(0, (qi + MOBA_WALK - 1) // MOBA_WALK, body, tuple(states))
    for hh, lanes in enumerate(head_lanes):
        o_ref[:, lanes] = _flash_out(states[hh]).astype(o_ref.dtype)


def _moba_attention(qkv, batch, seq):
    blk = MOBA_BLOCK
    wide = MOBA_WALK * blk
    nq = seq // blk
    d_model = N_HEADS * HEAD_DIM
    ncol = d_model // LANE
    avg = np.zeros((MOBA_BLK_PAD, seq), np.float32)
    for n in range(nq):
        avg[n, n * blk:(n + 1) * blk] = 1.0 / blk
    avg = jnp.asarray(avg, dtype=BF16)
    key_blk = (np.arange(seq) // blk).reshape(seq // wide, wide)
    expand = jnp.asarray(np.arange(MOBA_BLK_PAD)[None, :, None] == key_blk[:, None, :], dtype=BF16)
    return pl.pallas_call(
        _moba_kernel,
        grid=(batch, ncol, nq),
        in_specs=[pl.BlockSpec((blk, LANE), lambda b, h, i: (b * nq + i, h)),
                  pl.BlockSpec((seq, LANE), lambda b, h, i: (b, ncol + h)),
                  pl.BlockSpec((seq, LANE), lambda b, h, i: (b, 2 * ncol + h)),
                  pl.BlockSpec((MOBA_BLK_PAD, seq), lambda b, h, i: (0, 0)),
                  pl.BlockSpec((seq // wide, MOBA_BLK_PAD, wide), lambda b, h, i: (0, 0, 0))],
        out_specs=pl.BlockSpec((blk, LANE), lambda b, h, i: (b * nq + i, h)),
        out_shape=jax.ShapeDtypeStruct((batch * seq, d_model), BF16),
        scratch_shapes=[pltpu.VMEM((MOBA_BLK_PAD, LANE), F32)],
        compiler_params=_cparams(("parallel", "parallel", "arbitrary")),
        name="moba_attention",
    )(qkv, qkv, qkv, avg, expand)


def _nsa_compress_kernel(kr_ref, vr_ref, pk_ref, pv_ref, wk_ref, wv_ref, c_ref, sa_ref, sb_ref, kc_ref, vc_ref):
    nc = kr_ref.shape[0]
    rowi = lax.broadcasted_iota(jnp.int32, (nc, 1), 0)

    def windows(x_ref, pos_ref, w_ref):
        x = x_ref[...].astype(F32)
        lo = _dot((x + pos_ref[0:1, :]).astype(BF16), w_ref[0])
        hi = _dot((x + pos_ref[1:2, :]).astype(BF16), w_ref[1])
        hi_next = jnp.where(rowi < nc - 1, pltpu.roll(hi, nc - 1, 0), 0.0)
        return lo + hi_next

    kk = windows(kr_ref, pk_ref, wk_ref)
    d = HEAD_DIM
    kc = kk[:, 0:d] * c_ref[...] + kk[:, d:2 * d] * sb_ref[...] + kk[:, 2 * d:3 * d] * sa_ref[...]
    kc_ref[...] = kc.astype(kc_ref.dtype)
    vv = windows(vr_ref, pv_ref, wv_ref)
    vc_ref[...] = vv[:, 0:d].astype(vc_ref.dtype)


def _nsa_compress(k_cmp, v_cmp, pos_k, pos_v, w_k, w_v, batch, seq):
    g, d, st = NSA_GROUPS, HEAD_DIM, NSA_CMP_STRIDE
    nc = seq // st

    def chunks(x):
        return x.reshape(batch, nc, st, g, d).transpose(0, 3, 1, 2, 4).reshape(batch, g, nc, st * d)

    def weights(w):
        w3 = w.reshape(NSA_CMP_LEN, d, d)
        cat = jnp.concatenate([w3, jnp.roll(w3, ROPE_HALF, axis=2), jnp.roll(w3, -ROPE_HALF, axis=2)], axis=2)
        return cat.reshape(2, st * d, 3 * d).astype(BF16)

    def positions(p):
        return p.reshape(2, st * d).astype(F32)

    cmp_end = jnp.arange(nc) * st + NSA_CMP_LEN - 1
    tabs = _rope_tables(cmp_end, d)
    blk4 = pl.BlockSpec((None, None, nc, st * d), lambda b, gi: (b, gi, 0, 0))
    out4 = pl.BlockSpec((None, None, nc, d), lambda b, gi: (b, gi, 0, 0))
    const2 = lambda shape: pl.BlockSpec(shape, lambda b, gi: (0,) * len(shape))
    return pl.pallas_call(
        _nsa_compress_kernel,
        grid=(batch, g),
        in_specs=[blk4, blk4, const2((2, st * d)), const2((2, st * d)),
                  const2((2, st * d, 3 * d)), const2((2, st * d, 3 * d)),
                  const2((nc, d)), const2((nc, d)), const2((nc, d))],
        out_specs=[out4, out4],
        out_shape=[jax.ShapeDtypeStruct((batch, g, nc, d), BF16)] * 2,
        compiler_params=_cparams(("parallel", "parallel")),
        name="nsa_compress",
    )(chunks(k_cmp), chunks(v_cmp), positions(pos_k), positions(pos_v), weights(w_k), weights(w_v), *tabs)


NSA_TQ = 128
NSA_TK = 512


def _nsa_kernel(q_ref, kc_ref, vc_ref, ks_ref, vs_ref, kw_ref, vw_ref, gate_ref, c2s_ref, exp_ref, o_ref):
    tq, tk, d = NSA_TQ, NSA_TK, HEAD_DIM
    qi = pl.program_id(2)
    t0 = qi * tq
    nc = kc_ref.shape[0]
    tpos = t0 + lax.broadcasted_iota(jnp.int32, (tq, 1), 0)
    gates = jax.nn.sigmoid(gate_ref[...])
    qs = [q_ref[:, h * d:(h + 1) * d] * jnp.asarray(ATT_SCALE, BF16) for h in range(NSA_HPG)]

    cmp_end = lax.broadcasted_iota(jnp.int32, (tq, nc), 1) * NSA_CMP_STRIDE + (NSA_CMP_LEN - 1)
    vis_c = cmp_end <= tpos
    kc = kc_ref[...]
    vc = vc_ref[...]
    psum = jnp.zeros((tq, nc), F32)
    outs = []
    for h in range(NSA_HPG):
        s = jnp.where(vis_c, _dot_nt(qs[h], kc), NEG)
        e = jnp.where(vis_c, jnp.exp(s - jnp.max(s, axis=-1, keepdims=True)), 0.0)
        l = jnp.sum(e, axis=-1, keepdims=True)
        p = e / jnp.where(l > 0.0, l, 1.0)
        psum = psum + p
        outs.append(gates[:, 3 * h:3 * h + 1] * _dot(p.astype(BF16), vc))
    p_hi, p_lo = _split_bf16(psum)
    imp = _dot(p_hi, c2s_ref[...]) + _dot(p_lo, c2s_ref[...])
    blk = lax.broadcasted_iota(jnp.int32, (tq, NSA_SEL_PAD), 1)
    own = tpos // NSA_SEL_LEN
    forced = (blk == 0) | (blk == own) | (blk == own - 1)
    imp = jnp.where(blk <= own, imp + jnp.where(forced, NSA_BONUS, 0.0), NEG)
    sel = _topk_mask(imp, NSA_TOP_N).astype(BF16)

    kd = t0 // tk
    dstart = pl.multiple_of(kd * tk, tk)
    kpos = dstart + lax.broadcasted_iota(jnp.int32, (tq, tk), 1)
    dmask = (_dot(sel, exp_ref[kd]) > 0.5) & (kpos <= tpos)
    kdiag = ks_ref[pl.ds(dstart, tk), :]
    vdiag = vs_ref[pl.ds(dstart, tk), :]
    states = tuple(_flash_first(qs[h], kdiag, vdiag, dmask) for h in range(NSA_HPG))

    def sel_body(kb, states):
        start = pl.multiple_of(kb * tk, tk)
        mask = _dot(sel, exp_ref[kb]) > 0.5
        k = ks_ref[pl.ds(start, tk), :]
        v = vs_ref[pl.ds(start, tk), :]
        return tuple(_flash_next(qs[h], k, v, mask, states[h]) for h in range(NSA_HPG))

    states = lax.fori_loop(0, kd, sel_body, states)
    for h in range(NSA_HPG):
        outs[h] = outs[h] + gates[:, 3 * h + 1:3 * h + 2] * _flash_out(states[h])

    span = NSA_WINDOW + tq
    wstart = pl.multiple_of(jnp.maximum(t0 - NSA_WINDOW, 0), tq)
    wpos = wstart + lax.broadcasted_iota(jnp.int32, (tq, span), 1)
    wmask = (wpos <= tpos) & (wpos > tpos - NSA_WINDOW)
    kwin = kw_ref[pl.ds(wstart, span), :]
    vwin = vw_ref[pl.ds(wstart, span), :]
    for h in range(NSA_HPG):
        o = outs[h] + gates[:, 3 * h + 2:3 * h + 3] * _flash_out(_flash_first(qs[h], kwin, vwin, wmask))
        o_ref[:, h * d:(h + 1) * d] = o.astype(o_ref.dtype)


def _nsa_attention(q_all, kc, vc, ks, vs, kw, vw, gates, batch, seq):
    tq, tk, d, g = NSA_TQ, NSA_TK, HEAD_DIM, NSA_GROUPS
    nq = seq // tq
    nc = seq // NSA_CMP_STRIDE
    n_sel = seq // NSA_SEL_LEN
    d_model = N_HEADS * HEAD_DIM
    cs = np.arange(nc) * NSA_CMP_STRIDE
    ss = np.arange(n_sel) * NSA_SEL_LEN
    ov = np.clip(np.minimum(cs[:, None] + NSA_CMP_LEN, ss[None, :] + NSA_SEL_LEN)
                 - np.maximum(cs[:, None], ss[None, :]), 0, None) / NSA_CMP_LEN
    c2s = np.zeros((nc, NSA_SEL_PAD), np.float32)
    c2s[:, :n_sel] = ov
    c2s[nc - 1, :] = 0.0
    key_blk = (np.arange(seq) // NSA_SEL_LEN).reshape(seq // tk, tk)
    expand = (np.arange(NSA_SEL_PAD)[None, :, None] == key_blk[:, None, :]).astype(np.float32)
    kv_spec = lambda n: pl.BlockSpec((None, None, n, d), lambda b, gi, i: (b, gi, 0, 0))
    return pl.pallas_call(
        _nsa_kernel,
        grid=(batch, g, nq),
        in_specs=[pl.BlockSpec((tq, NSA_HPG * d), lambda b, gi, i: (b * nq + i, gi)),
                  kv_spec(nc), kv_spec(nc), kv_spec(seq), kv_spec(seq), kv_spec(seq), kv_spec(seq),
                  pl.BlockSpec((tq, LANE), lambda b, gi, i: (b * nq + i, gi)),
                  pl.BlockSpec((nc, NSA_SEL_PAD), lambda b, gi, i: (0, 0)),
                  pl.BlockSpec((seq // tk, NSA_SEL_PAD, tk), lambda b, gi, i: (0, 0, 0))],
        out_specs=pl.BlockSpec((tq, NSA_HPG * d), lambda b, gi, i: (b * nq + i, gi)),
        out_shape=jax.ShapeDtypeStruct((batch * seq, d_model), BF16),
        compiler_params=_cparams(("parallel", "parallel", "arbitrary")),
        name="nsa_attention",
    )(q_all, kc, vc, ks, vs, kw, vw, gates, jnp.asarray(c2s, dtype=BF16), jnp.asarray(expand, dtype=BF16))


def _nsa_mixer(x2, g_norm, w_in, pos_k, pos_v, w_k, w_v, w_out, batch, seq):
    d_model, kv = N_HEADS * HEAD_DIM, NSA_KV
    sec = lambda i: w_in[:, d_model + i * kv: d_model + (i + 1) * kv]
    w_main = jnp.concatenate([w_in[:, :d_model], sec(2), sec(4), sec(0), sec(1), sec(3), sec(5)], axis=1)
    proj = _norm_matmul(x2, g_norm, w_main, seq, tn=512, n_rope=3)
    w_gate = w_in[:, d_model + 6 * kv:].reshape(d_model, NSA_GROUPS, 3 * NSA_HPG)
    w_gate = jnp.pad(w_gate, ((0, 0), (0, 0), (0, LANE - 3 * NSA_HPG))).reshape(d_model, NSA_GROUPS * LANE)
    gates = _norm_matmul(x2, g_norm, w_gate, seq, tn=NSA_GROUPS * LANE, out_dtype=F32)

    col = lambda i: proj[:, d_model + i * kv: d_model + (i + 1) * kv]
    heads = lambda x: x.reshape(batch, seq, NSA_GROUPS, HEAD_DIM).transpose(0, 2, 1, 3)
    kc, vc = _nsa_compress(col(2), col(3), pos_k, pos_v, w_k, w_v, batch, seq)
    o = _nsa_attention(proj, kc, vc, heads(col(0)), heads(col(4)), heads(col(1)), heads(col(5)),
                       gates, batch, seq)
    return _proj_resid(o, w_out, x2)


PEER_SEL_TT = 256
PEER_PAIRS = [(i, j) for i in range(PEER_TOP_K + 1) for j in range(PEER_TOP_K + 1)
              if (i + 1) * (j + 1) <= PEER_TOP_K + 1]


def _peer_select_kernel(q_ref, k1_ref, k2_ref, a1_ref, e1_ref, s2_ref, e2_ref):
    q = q_ref[...]
    q_hi, q_lo = _split_bf16(q)
    nh, nr = PEER_HEADS, PEER_TOP_K + 1

    def scores(k_ref, off):
        k_hi, k_lo = _split_bf16(k_ref[...])
        out = []
        for h in range(nh):
            lanes = slice(off + h * PEER_HALF, off + (h + 1) * PEER_HALF)
            out.append(_dot_nt(k_hi, q_hi[:, lanes]) + _dot_nt(k_hi, q_lo[:, lanes])
                       + _dot_nt(k_lo, q_hi[:, lanes]))
        return out

    def top_values(s):
        vals, work = [], s
        for _ in range(nr):
            m = jnp.max(work, axis=0, keepdims=True)
            vals.append(m)
            work = jnp.where(work == m, LOW, work)
        return vals

    s1 = scores(k1_ref, 0)
    s2 = scores(k2_ref, nh * PEER_HALF)
    t1 = [top_values(s) for s in s1]
    t2 = [top_values(s) for s in s2]
    r1 = [jnp.concatenate([t1[h][r] for h in range(nh)], axis=0) for r in range(nr)]
    r2 = [jnp.concatenate([t2[h][r] for h in range(nh)], axis=0) for r in range(nr)]
    cands = [r1[i] + r2[j] for (i, j) in PEER_PAIRS]
    work, tops = list(cands), []
    for _ in range(nr):
        m = functools.reduce(jnp.maximum, work)
        tops.append(m)
        work = [jnp.where(c == m, LOW, c) for c in work]
    thr = 0.5 * (tops[PEER_TOP_K - 1] + tops[PEER_TOP_K])
    cmax = tops[0]
    z = functools.reduce(lambda a, b: a + b, [jnp.where(c >= thr, jnp.exp(c - cmax), 0.0) for c in cands])
    inv_z = 1.0 / z
    for h in range(nh):
        a1_ref[h] = thr[h:h + 1, :] - s1[h]
        e1_ref[h] = jnp.exp(s1[h] - r1[0][h:h + 1, :])
        s2_ref[h] = s2[h]
        e2_ref[h] = jnp.exp(s2[h] - r2[0][h:h + 1, :]) * inv_z[h:h + 1, :]


def _peer_select(q, sub_keys):
    t = q.shape[0]
    tt = min(PEER_SEL_TT, t)
    out_spec = pl.BlockSpec((PEER_HEADS, PEER_KEYS, tt), lambda i: (0, 0, i))
    return pl.pallas_call(
        _peer_select_kernel,
        grid=(t // tt,),
        in_specs=[pl.BlockSpec((tt, q.shape[1]), lambda i: (i, 0)),
                  pl.BlockSpec((PEER_KEYS, PEER_HALF), lambda i: (0, 0)),
                  pl.BlockSpec((PEER_KEYS, PEER_HALF), lambda i: (0, 0))],
        out_specs=[out_spec] * 4,
        out_shape=[jax.ShapeDtypeStruct((PEER_HEADS, PEER_KEYS, t), F32)] * 4,
        compiler_params=_cparams(("parallel",)),
        name="peer_select",
    )(q, sub_keys[0].astype(F32), sub_keys[1].astype(F32))


PEER_TT = 512
PEER_NE = 256


def _gelu(x):
    return 0.5 * x * (1.0 + lax.erf(x * np.float32(np.sqrt(0.5))))


def _peer_main_kernel(x_ref, g_ref, u_ref, vt_ref, a1_ref, e1_ref, s2_ref, e2_ref, o_ref, xn_ref, acc_ref):
    e = pl.program_id(1)
    ne = u_ref.shape[0]
    tt = x_ref.shape[0]

    @pl.when(e == 0)
    def _():
        x = x_ref[...]
        y = x * lax.rsqrt(jnp.mean(x * x, axis=-1, keepdims=True) + NORM_EPS)
        xn_ref[...] = (y * g_ref[...]).astype(BF16)
        acc_ref[...] = jnp.zeros_like(acc_ref)

    act = _gelu(_dot_nt(u_ref[...], xn_ref[...]))
    parts = []
    for c in range(ne // PEER_KEYS):
        i1 = e * (ne // PEER_KEYS) + c
        gsum = jnp.zeros((PEER_KEYS, tt), F32)
        for h in range(PEER_HEADS):
            a1 = a1_ref[h, pl.ds(i1, 1), :]
            e1 = e1_ref[h, pl.ds(i1, 1), :]
            gsum = gsum + jnp.where(s2_ref[h] >= a1, e2_ref[h], 0.0) * e1
        parts.append((gsum * act[c * PEER_KEYS:(c + 1) * PEER_KEYS, :]).astype(BF16))
    w = jnp.concatenate(parts, axis=0)
    acc_ref[...] += _dot(vt_ref[...], w)

    @pl.when(e == pl.num_programs(1) - 1)
    def _():
        o_ref[...] = x_ref[...] + acc_ref[...].T


def _peer_mixer(x2, g_norm, w_q, sub_keys, u, v, seq):
    t, d = x2.shape
    w_q2 = w_q.reshape(d, PEER_HEADS, 2, PEER_HALF).transpose(0, 2, 1, 3).reshape(d, 2 * PEER_HEADS * PEER_HALF)
    q = _norm_matmul3(x2, g_norm, w_q2)
    a1, e1, s2, e2 = _peer_select(q, sub_keys)
    tt = min(PEER_TT, t)
    ne = PEER_NE
    n_exp = u.shape[0]
    tok_spec = pl.BlockSpec((PEER_HEADS, PEER_KEYS, tt), lambda i, e: (0, 0, i))
    return pl.pallas_call(
        _peer_main_kernel,
        grid=(t // tt, n_exp // ne),
        in_specs=[pl.BlockSpec((tt, d), lambda i, e: (i, 0)),
                  pl.BlockSpec((1, d), lambda i, e: (0, 0)),
                  pl.BlockSpec((ne, d), lambda i, e: (e, 0)),
                  pl.BlockSpec((d, ne), lambda i, e: (0, e)),
                  tok_spec, tok_spec, tok_spec, tok_spec],
        out_specs=pl.BlockSpec((tt, d), lambda i, e: (i, 0)),
        out_shape=jax.ShapeDtypeStruct((t, d), F32),
        scratch_shapes=[pltpu.VMEM((tt, d), BF16), pltpu.VMEM((d, tt), F32)],
        compiler_params=_cparams(("parallel", "arbitrary")),
        name="peer_main",
    )(x2, g_norm.reshape(1, d).astype(F32), u.astype(BF16), v.T.astype(BF16), a1, e1, s2, e2)


def _sb_mixer(x2, g_norm, w_in, w_out, batch, seq):
    qkv = _norm_matmul(x2, g_norm, w_in, seq, tn=512)
    return _proj_resid(_sb_attention(qkv, batch, seq), w_out, x2)


def _moba_mixer(x2, g_norm, w_in, w_out, batch, seq):
    d_model = N_HEADS * HEAD_DIM
    qkv = _norm_matmul(x2, g_norm, w_in, seq, tn=512, n_rope=2 * d_model // 512)
    return _proj_resid(_moba_attention(qkv, batch, seq), w_out, x2)


def kernel(x, l0_norm_mix, l0_sb_w_in, l0_sb_w_out, l0_norm_ffn, l0_peer_w_q, l0_peer_sub_keys, l0_peer_u, l0_peer_v, l1_norm_mix, l1_nsa_w_in, l1_nsa_cmp_pos_k, l1_nsa_cmp_pos_v, l1_nsa_cmp_w_k, l1_nsa_cmp_w_v, l1_nsa_w_out, l1_norm_ffn, l1_peer_w_q, l1_peer_sub_keys, l1_peer_u, l1_peer_v, l2_norm_mix, l2_moba_w_in, l2_moba_w_out, l2_norm_ffn, l2_peer_w_q, l2_peer_sub_keys, l2_peer_u, l2_peer_v, l3_norm_mix, l3_sb_w_in, l3_sb_w_out, l3_norm_ffn, l3_peer_w_q, l3_peer_sub_keys, l3_peer_u, l3_peer_v, final_norm):
    batch, seq, d = x.shape
    x2 = x.reshape(batch * seq, d)
    x2 = _sb_mixer(x2, l0_norm_mix, l0_sb_w_in, l0_sb_w_out, batch, seq)
    x2 = _peer_mixer(x2, l0_norm_ffn, l0_peer_w_q, l0_peer_sub_keys, l0_peer_u, l0_peer_v, seq)
    x2 = _nsa_mixer(x2, l1_norm_mix, l1_nsa_w_in, l1_nsa_cmp_pos_k, l1_nsa_cmp_pos_v, l1_nsa_cmp_w_k,
                    l1_nsa_cmp_w_v, l1_nsa_w_out, batch, seq)
    x2 = _peer_mixer(x2, l1_norm_ffn, l1_peer_w_q, l1_peer_sub_keys, l1_peer_u, l1_peer_v, seq)
    x2 = _moba_mixer(x2, l2_norm_mix, l2_moba_w_in, l2_moba_w_out, batch, seq)
    x2 = _peer_mixer(x2, l2_norm_ffn, l2_peer_w_q, l2_peer_sub_keys, l2_peer_u, l2_peer_v, seq)
    x2 = _sb_mixer(x2, l3_norm_mix, l3_sb_w_in, l3_sb_w_out, batch, seq)
    x2 = _peer_mixer(x2, l3_norm_ffn, l3_peer_w_q, l3_peer_sub_keys, l3_peer_u, l3_peer_v, seq)
    return _rmsnorm(x2, final_norm).reshape(batch, seq, d)
```

```python
import functools

import numpy as np
import jax
import jax.numpy as jnp
from jax import lax
from jax.experimental import pallas as pl
from jax.experimental.pallas import tpu as pltpu

F32 = jnp.float32
BF16 = jnp.bfloat16

N_HEADS = 16
HEAD_DIM = 64
ROPE_DIM = 16
ROPE_HALF = ROPE_DIM // 2
ROPE_THETA = 500000.0
NORM_EPS = 1e-6
NEG = -1e30
LOW = -3e38
ATT_SCALE = HEAD_DIM ** -0.5

NSA_GROUPS = 4
NSA_HPG = N_HEADS // NSA_GROUPS
NSA_KV = NSA_GROUPS * HEAD_DIM
NSA_CMP_LEN = 32
NSA_CMP_STRIDE = 16
NSA_SEL_LEN = 64
NSA_TOP_N = 16
NSA_WINDOW = 512
NSA_BONUS = 1e3
NSA_SEL_PAD = 128

MOBA_BLOCK = 256
MOBA_TOP_K = 3
MOBA_BLK_PAD = 128

PEER_HEADS = 8
PEER_KEYS = 128
PEER_TOP_K = 16
PEER_HALF = 64

SB_EXIT = -110.0

LANE = 128
VMEM_LIMIT = 56 << 20


def _cparams(sem, vmem=VMEM_LIMIT):
    return pltpu.CompilerParams(dimension_semantics=sem, vmem_limit_bytes=vmem)


def _dot(a, b):
    return jnp.dot(a, b, preferred_element_type=F32)


def _dot_nt(a, b):
    return lax.dot_general(a, b, (((1,), (1,)), ((), ())), preferred_element_type=F32)


def _split_bf16(x):
    hi = x.astype(BF16)
    lo = (x - hi.astype(F32)).astype(BF16)
    return hi, lo


def _topk_mask(vals, k):
    lane = lax.broadcasted_iota(jnp.int32, vals.shape, 1)
    work = vals
    sel = jnp.zeros(vals.shape, F32)
    for _ in range(k):
        m = jnp.max(work, axis=-1, keepdims=True)
        first = jnp.min(jnp.where(work == m, lane, LANE), axis=-1, keepdims=True)
        hit = lane == first
        sel = jnp.where(hit, 1.0, sel)
        work = jnp.where(hit, LOW, work)
    return sel


def _norm_matmul_kernel(*refs, n_rope, tn):
    if n_rope:
        x_ref, g_ref, w_ref, c_ref, sa_ref, sb_ref, o_ref, xn_ref = refs
    else:
        x_ref, g_ref, w_ref, o_ref, xn_ref = refs
    j = pl.program_id(1)

    @pl.when(j == 0)
    def _():
        x = x_ref[...]
        y = x * lax.rsqrt(jnp.mean(x * x, axis=-1, keepdims=True) + NORM_EPS)
        xn_ref[...] = (y * g_ref[...]).astype(BF16)

    acc = _dot(xn_ref[...], w_ref[...])
    if n_rope:
        @pl.when(j < n_rope)
        def _():
            r = (acc * c_ref[...] + pltpu.roll(acc, ROPE_HALF, 1) * sb_ref[...]
                 + pltpu.roll(acc, tn - ROPE_HALF, 1) * sa_ref[...])
            o_ref[...] = r.astype(o_ref.dtype)

        @pl.when(j >= n_rope)
        def _():
            o_ref[...] = acc.astype(o_ref.dtype)
    else:
        o_ref[...] = acc.astype(o_ref.dtype)


def _rope_tables(pos, width):
    inv = ROPE_THETA ** (-jnp.arange(0, ROPE_DIM, 2, dtype=F32) / ROPE_DIM)
    ang = pos.astype(F32)[:, None] * inv[None, :]
    cos, sin = jnp.cos(ang), jnp.sin(ang)
    n = pos.shape[0]
    rest = HEAD_DIM - ROPE_DIM
    c = jnp.concatenate([cos, cos, jnp.ones((n, rest), F32)], axis=1)
    sa = jnp.concatenate([-sin, jnp.zeros((n, ROPE_HALF + rest), F32)], axis=1)
    sb = jnp.concatenate([jnp.zeros((n, ROPE_HALF), F32), sin, jnp.zeros((n, rest), F32)], axis=1)
    reps = width // HEAD_DIM
    return jnp.tile(c, (1, reps)), jnp.tile(sa, (1, reps)), jnp.tile(sb, (1, reps))


def _norm_matmul(x2, g, w, seq, *, tn, n_rope=0, out_dtype=BF16, tt=512):
    t, d = x2.shape
    n = w.shape[1]
    tt = min(tt, seq)
    grid = (t // tt, n // tn)
    in_specs = [pl.BlockSpec((tt, d), lambda i, j: (i, 0)),
                pl.BlockSpec((1, d), lambda i, j: (0, 0)),
                pl.BlockSpec((d, tn), lambda i, j: (0, j))]
    args = [x2, g.reshape(1, d).astype(F32), w.astype(BF16)]
    if n_rope:
        nper = seq // tt
        tabs = _rope_tables(jnp.arange(seq), tn)
        in_specs += [pl.BlockSpec((tt, tn), lambda i, j: (i % nper, 0))] * 3
        args += list(tabs)
    return pl.pallas_call(
        functools.partial(_norm_matmul_kernel, n_rope=n_rope, tn=tn),
        grid=grid, in_specs=in_specs,
        out_specs=pl.BlockSpec((tt, tn), lambda i, j: (i, j)),
        out_shape=jax.ShapeDtypeStruct((t, n), out_dtype),
        scratch_shapes=[pltpu.VMEM((tt, d), BF16)],
        compiler_params=_cparams(("parallel", "arbitrary")),
        name="norm_matmul",
    )(*args)


def _norm_matmul3_kernel(x_ref, g_ref, wh_ref, wl_ref, o_ref, xh_ref, xl_ref):
    @pl.when(pl.program_id(1) == 0)
    def _():
        x = x_ref[...]
        y = x * lax.rsqrt(jnp.mean(x * x, axis=-1, keepdims=True) + NORM_EPS) * g_ref[...]
        xh_ref[...], xl_ref[...] = _split_bf16(y)

    o_ref[...] = (_dot(xh_ref[...], wh_ref[...]) + _dot(xl_ref[...], wh_ref[...])
                  + _dot(xh_ref[...], wl_ref[...]))


def _norm_matmul3(x2, g, w, *, tn=512, tt=512):
    t, d = x2.shape
    n = w.shape[1]
    tt = min(tt, t)
    w_hi, w_lo = _split_bf16(w.astype(F32))
    w_spec = pl.BlockSpec((d, tn), lambda i, j: (0, j))
    return pl.pallas_call(
        _norm_matmul3_kernel,
        grid=(t // tt, n // tn),
        in_specs=[pl.BlockSpec((tt, d), lambda i, j: (i, 0)),
                  pl.BlockSpec((1, d), lambda i, j: (0, 0)), w_spec, w_spec],
        out_specs=pl.BlockSpec((tt, tn), lambda i, j: (i, j)),
        out_shape=jax.ShapeDtypeStruct((t, n), F32),
        scratch_shapes=[pltpu.VMEM((tt, d), BF16), pltpu.VMEM((tt, d), BF16)],
        compiler_params=_cparams(("parallel", "arbitrary")),
        name="norm_matmul3",
    )(x2, g.reshape(1, d).astype(F32), w_hi, w_lo)


def _proj_resid_kernel(a_ref, w_ref, r_ref, o_ref):
    o_ref[...] = r_ref[...] + _dot(a_ref[...], w_ref[...])


def _proj_resid(a, w, resid, *, tt=512, tn=512):
    t, d = a.shape
    n = w.shape[1]
    tt = min(tt, t)
    return pl.pallas_call(
        _proj_resid_kernel,
        grid=(t // tt, n // tn),
        in_specs=[pl.BlockSpec((tt, d), lambda i, j: (i, 0)),
                  pl.BlockSpec((d, tn), lambda i, j: (0, j)),
                  pl.BlockSpec((tt, tn), lambda i, j: (i, j))],
        out_specs=pl.BlockSpec((tt, tn), lambda i, j: (i, j)),
        out_shape=jax.ShapeDtypeStruct((t, n), F32),
        compiler_params=_cparams(("parallel", "parallel")),
        name="proj_resid",
    )(a, w.astype(BF16), resid)


def _rmsnorm_kernel(x_ref, g_ref, o_ref):
    x = x_ref[...]
    y = x * lax.rsqrt(jnp.mean(x * x, axis=-1, keepdims=True) + NORM_EPS)
    o_ref[...] = y * g_ref[...]


def _rmsnorm(x2, g, *, tt=512):
    t, d = x2.shape
    tt = min(tt, t)
    return pl.pallas_call(
        _rmsnorm_kernel,
        grid=(t // tt,),
        in_specs=[pl.BlockSpec((tt, d), lambda i: (i, 0)), pl.BlockSpec((1, d), lambda i: (0, 0))],
        out_specs=pl.BlockSpec((tt, d), lambda i: (i, 0)),
        out_shape=jax.ShapeDtypeStruct((t, d), F32),
        compiler_params=_cparams(("parallel",)),
        name="final_rmsnorm",
    )(x2, g.reshape(1, d).astype(F32))


SB_TILE = 256


def _sb_kernel(q_ref, k_ref, v_ref, tri_ref, o_ref):
    tile = SB_TILE
    qi = pl.program_id(2)
    tri = tri_ref[...]
    row = lax.broadcasted_iota(jnp.int32, (tile, tile), 0)
    col = lax.broadcasted_iota(jnp.int32, (tile, tile), 1)
    past = col < row

    for hh in range(2):
        lanes = slice(hh * HEAD_DIM, (hh + 1) * HEAD_DIM)
        q = q_ref[:, lanes] * jnp.asarray(ATT_SCALE, BF16)

        def step(start, r_sum, acc, diag):
            k = k_ref[pl.ds(start, tile), lanes]
            v = v_ref[pl.ds(start, tile), lanes]
            z = _dot_nt(q, k)
            sp = jnp.maximum(z, 0.0) + jnp.log1p(jnp.exp(-jnp.abs(z)))
            lb = jnp.where(past, -sp, 0.0) if diag else -sp
            hi, lo = _split_bf16(lb)
            suffix = _dot(hi, tri) + _dot(lo, tri)
            a = jnp.exp((z - sp) + suffix + r_sum)
            if diag:
                a = jnp.where(past, a, 0.0)
            acc = acc + _dot(a.astype(BF16), v)
            r_sum = r_sum + jnp.sum(lb, axis=-1, keepdims=True)
            return r_sum, acc

        r0 = jnp.zeros((tile, 1), F32)
        acc0 = jnp.zeros((tile, HEAD_DIM), F32)
        r1, acc1 = step(pl.multiple_of(qi * tile, tile), r0, acc0, True)

        def older(j, r_sum):
            return jnp.where(jnp.max(r_sum) > SB_EXIT, j - 1, -1)

        def cond(c):
            return c[0] >= 0

        def body(c):
            j, r_sum, acc = c
            r_sum, acc = step(pl.multiple_of(j * tile, tile), r_sum, acc, False)
            return older(j, r_sum), r_sum, acc

        _, _, acc = lax.while_loop(cond, body, (older(qi, r1), r1, acc1))
        o_ref[:, lanes] = acc.astype(o_ref.dtype)


def _sb_attention(qkv, batch, seq):
    tile = SB_TILE
    nq = seq // tile
    d_model = N_HEADS * HEAD_DIM
    ncol = d_model // LANE
    ii = np.arange(tile)
    tri = jnp.asarray(ii[:, None] > ii[None, :], dtype=BF16)
    return pl.pallas_call(
        _sb_kernel,
        grid=(batch, ncol, nq),
        in_specs=[pl.BlockSpec((tile, LANE), lambda b, h, i: (b * nq + i, h)),
                  pl.BlockSpec((seq, LANE), lambda b, h, i: (b, ncol + h)),
                  pl.BlockSpec((seq, LANE), lambda b, h, i: (b, 2 * ncol + h)),
                  pl.BlockSpec((tile, tile), lambda b, h, i: (0, 0))],
        out_specs=pl.BlockSpec((tile, LANE), lambda b, h, i: (b * nq + i, h)),
        out_shape=jax.ShapeDtypeStruct((batch * seq, d_model), BF16),
        compiler_params=_cparams(("parallel", "parallel", "arbitrary")),
        name="sb_attention",
    )(qkv, qkv, qkv, tri)


def _flash_first(q, k, v, mask):
    s = jnp.where(mask, _dot_nt(q, k), NEG)
    m = jnp.max(s, axis=-1, keepdims=True)
    p = jnp.exp(s - m)
    return m, jnp.sum(p, axis=-1, keepdims=True), _dot(p.astype(BF16), v)


def _flash_next(q, k, v, mask, state):
    m, l, acc = state
    s = jnp.where(mask, _dot_nt(q, k), NEG)
    m_new = jnp.maximum(m, jnp.max(s, axis=-1, keepdims=True))
    p = jnp.exp(s - m_new)
    alpha = jnp.exp(m - m_new)
    l = alpha * l + jnp.sum(p, axis=-1, keepdims=True)
    acc = alpha * acc + _dot(p.astype(BF16), v)
    return m_new, l, acc


def _flash_out(state):
    _, l, acc = state
    return acc / l


MOBA_WALK = 2


def _moba_kernel(q_ref, k_ref, v_ref, avg_ref, exp_ref, o_ref, kmean_ref):
    blk = MOBA_BLOCK
    wide = MOBA_WALK * blk
    qi = pl.program_id(2)

    @pl.when(qi == 0)
    def _():
        kmean_ref[...] = _dot(avg_ref[...], k_ref[...])

    row = lax.broadcasted_iota(jnp.int32, (blk, blk), 0)
    col = lax.broadcasted_iota(jnp.int32, (blk, blk), 1)
    causal = col <= row
    lane = lax.broadcasted_iota(jnp.int32, (blk, MOBA_BLK_PAD), 1)
    start = pl.multiple_of(qi * blk, blk)
    head_lanes = [slice(hh * HEAD_DIM, (hh + 1) * HEAD_DIM) for hh in range(2)]

    qs, sels, states = [], [], []
    for lanes in head_lanes:
        q_raw = q_ref[:, lanes]
        km_hi, km_lo = _split_bf16(kmean_ref[:, lanes])
        gate = _dot_nt(q_raw, km_hi) + _dot_nt(q_raw, km_lo)
        gate = jnp.where(lane < qi, gate, NEG)
        sels.append(jnp.where(lane < qi, _topk_mask(gate, MOBA_TOP_K), 0.0).astype(BF16))
        qs.append(q_raw * jnp.asarray(ATT_SCALE, BF16))
        states.append(_flash_first(qs[-1], k_ref[pl.ds(start, blk), lanes], v_ref[pl.ds(start, blk), lanes], causal))

    def body(p, sts):
        s0 = pl.multiple_of(p * wide, wide)
        out = []
        for hh, lanes in enumerate(head_lanes):
            mask = _dot(sels[hh], exp_ref[p]) > 0.5
            out.append(_flash_next(qs[hh], k_ref[pl.ds(s0, wide), lanes], v_ref[pl.ds(s0, wide), lanes],
                                   mask, sts[hh]))
        return tuple(out)

    states = lax.fori_loop(0, (qi + MOBA_WALK - 1) // MOBA_WALK, body, tuple(states))
    for hh, lanes in enumerate(head_lanes):
        o_ref[:, lanes] = _flash_out(states[hh]).astype(o_ref.dtype)


def _moba_attention(qkv, batch, seq):
    blk = MOBA_BLOCK
    wide = MOBA_WALK * blk
    nq = seq // blk
    d_model = N_HEADS * HEAD_DIM
    ncol = d_model // LANE
    avg = np.zeros((MOBA_BLK_PAD, seq), np.float32)
    for n in range(nq):
        avg[n, n * blk:(n + 1) * blk] = 1.0 / blk
    avg = jnp.asarray(avg, dtype=BF16)
    key_blk = (np.arange(seq) // blk).reshape(seq // wide, wide)
    expand = jnp.asarray(np.arange(MOBA_BLK_PAD)[None, :, None] == key_blk[:, None, :], dtype=BF16)
    return pl.pallas_call(
        _moba_kernel,
        grid=(batch, ncol, nq),
        in_specs=[pl.BlockSpec((blk, LANE), lambda b, h, i: (b * nq + i, h)),
                  pl.BlockSpec((seq, LANE), lambda b, h, i: (b, ncol + h)),
                  pl.BlockSpec((seq, LANE), lambda b, h, i: (b, 2 * ncol + h)),
                  pl.BlockSpec((MOBA_BLK_PAD, seq), lambda b, h, i: (0, 0)),
                  pl.BlockSpec((seq // wide, MOBA_BLK_PAD, wide), lambda b, h, i: (0, 0, 0))],
        out_specs=pl.BlockSpec((blk, LANE), lambda b, h, i: (b * nq + i, h)),
        out_shape=jax.ShapeDtypeStruct((batch * seq, d_model), BF16),
        scratch_shapes=[pltpu.VMEM((MOBA_BLK_PAD, LANE), F32)],
        compiler_params=_cparams(("parallel", "parallel", "arbitrary")),
        name="moba_attention",
    )(qkv, qkv, qkv, avg, expand)


def _nsa_compress_kernel(kr_ref, vr_ref, pk_ref, pv_ref, wk_ref, wv_ref, c_ref, sa_ref, sb_ref, kc_ref, vc_ref):
    nc = kr_ref.shape[0]
    rowi = lax.broadcasted_iota(jnp.int32, (nc, 1), 0)

    def windows(x_ref, pos_ref, w_ref):
        x = x_ref[...].astype(F32)
        lo = _dot((x + pos_ref[0:1, :]).astype(BF16), w_ref[0])
        hi = _dot((x + pos_ref[1:2, :]).astype(BF16), w_ref[1])
        hi_next = jnp.where(rowi < nc - 1, pltpu.roll(hi, nc - 1, 0), 0.0)
        return lo + hi_next

    kk = windows(kr_ref, pk_ref, wk_ref)
    d = HEAD_DIM
    kc = kk[:, 0:d] * c_ref[...] + kk[:, d:2 * d] * sb_ref[...] + kk[:, 2 * d:3 * d] * sa_ref[...]
    kc_ref[...] = kc.astype(kc_ref.dtype)
    vv = windows(vr_ref, pv_ref, wv_ref)
    vc_ref[...] = vv[:, 0:d].astype(vc_ref.dtype)


def _nsa_compress(k_cmp, v_cmp, pos_k, pos_v, w_k, w_v, batch, seq):
    g, d, st = NSA_GROUPS, HEAD_DIM, NSA_CMP_STRIDE
    nc = seq // st

    def chunks(x):
        return x.reshape(batch, nc, st, g, d).transpose(0, 3, 1, 2, 4).reshape(batch, g, nc, st * d)

    def weights(w):
        w3 = w.reshape(NSA_CMP_LEN, d, d)
        cat = jnp.concatenate([w3, jnp.roll(w3, ROPE_HALF, axis=2), jnp.roll(w3, -ROPE_HALF, axis=2)], axis=2)
        return cat.reshape(2, st * d, 3 * d).astype(BF16)

    def positions(p):
        return p.reshape(2, st * d).astype(F32)

    cmp_end = jnp.arange(nc) * st + NSA_CMP_LEN - 1
    tabs = _rope_tables(cmp_end, d)
    blk4 = pl.BlockSpec((None, None, nc, st * d), lambda b, gi: (b, gi, 0, 0))
    out4 = pl.BlockSpec((None, None, nc, d), lambda b, gi: (b, gi, 0, 0))
    const2 = lambda shape: pl.BlockSpec(shape, lambda b, gi: (0,) * len(shape))
    return pl.pallas_call(
        _nsa_compress_kernel,
        grid=(batch, g),
        in_specs=[blk4, blk4, const2((2, st * d)), const2((2, st * d)),
                  const2((2, st * d, 3 * d)), const2((2, st * d, 3 * d)),
                  const2((nc, d)), const2((nc, d)), const2((nc, d))],
        out_specs=[out4, out4],
        out_shape=[jax.ShapeDtypeStruct((batch, g, nc, d), BF16)] * 2,
        compiler_params=_cparams(("parallel", "parallel")),
        name="nsa_compress",
    )(chunks(k_cmp), chunks(v_cmp), positions(pos_k), positions(pos_v), weights(w_k), weights(w_v), *tabs)


NSA_TQ = 128
NSA_TK = 512


def _nsa_kernel(q_ref, kc_ref, vc_ref, ks_ref, vs_ref, kw_ref, vw_ref, gate_ref, c2s_ref, exp_ref, o_ref):
    tq, tk, d, nh = NSA_TQ, NSA_TK, HEAD_DIM, NSA_HPG
    rows = nh * tq
    qi = pl.program_id(2)
    t0 = qi * tq
    nc = kc_ref.shape[0]

    def stack(x):
        return jnp.concatenate([x] * nh, axis=0)

    tpos1 = t0 + lax.broadcasted_iota(jnp.int32, (tq, 1), 0)
    tpos = stack(tpos1)
    gates = jax.nn.sigmoid(gate_ref[...])

    def gate(branch):
        return jnp.concatenate([gates[:, 3 * h + branch:3 * h + branch + 1] for h in range(nh)], axis=0)

    q = jnp.concatenate([q_ref[:, h * d:(h + 1) * d] for h in range(nh)], axis=0) * jnp.asarray(ATT_SCALE, BF16)

    cmp_end = lax.broadcasted_iota(jnp.int32, (rows, nc), 1) * NSA_CMP_STRIDE + (NSA_CMP_LEN - 1)
    vis_c = cmp_end <= tpos
    s = jnp.where(vis_c, _dot_nt(q, kc_ref[...]), NEG)
    e = jnp.where(vis_c, jnp.exp(s - jnp.max(s, axis=-1, keepdims=True)), 0.0)
    l = jnp.sum(e, axis=-1, keepdims=True)
    p = e / jnp.where(l > 0.0, l, 1.0)
    out = gate(0) * _dot(p.astype(BF16), vc_ref[...])
    psum = functools.reduce(lambda a, b: a + b, [p[h * tq:(h + 1) * tq] for h in range(nh)])
    p_hi, p_lo = _split_bf16(psum)
    imp = _dot(p_hi, c2s_ref[...]) + _dot(p_lo, c2s_ref[...])
    blk = lax.broadcasted_iota(jnp.int32, (tq, NSA_SEL_PAD), 1)
    own = tpos1 // NSA_SEL_LEN
    forced = (blk == 0) | (blk == own) | (blk == own - 1)
    imp = jnp.where(blk <= own, imp + jnp.where(forced, NSA_BONUS, 0.0), NEG)
    sel = stack(_topk_mask(imp, NSA_TOP_N).astype(BF16))

    kd = t0 // tk
    dstart = pl.multiple_of(kd * tk, tk)
    kpos = dstart + lax.broadcasted_iota(jnp.int32, (rows, tk), 1)
    dmask = (_dot(sel, exp_ref[kd]) > 0.5) & (kpos <= tpos)
    state = _flash_first(q, ks_ref[pl.ds(dstart, tk), :], vs_ref[pl.ds(dstart, tk), :], dmask)

    def sel_body(kb, state):
        start = pl.multiple_of(kb * tk, tk)
        mask = _dot(sel, exp_ref[kb]) > 0.5
        return _flash_next(q, ks_ref[pl.ds(start, tk), :], vs_ref[pl.ds(start, tk), :], mask, state)

    state = lax.fori_loop(0, kd, sel_body, state)
    out = out + gate(1) * _flash_out(state)

    span = NSA_WINDOW + tq
    wstart = pl.multiple_of(jnp.maximum(t0 - NSA_WINDOW, 0), tq)
    wpos = wstart + lax.broadcasted_iota(jnp.int32, (rows, span), 1)
    wmask = (wpos <= tpos) & (wpos > tpos - NSA_WINDOW)
    win = _flash_first(q, kw_ref[pl.ds(wstart, span), :], vw_ref[pl.ds(wstart, span), :], wmask)
    out = out + gate(2) * _flash_out(win)
    for h in range(nh):
        o_ref[:, h * d:(h + 1) * d] = out[h * tq:(h + 1) * tq].astype(o_ref.dtype)


def _nsa_attention(q_all, kc, vc, ks, vs, kw, vw, gates, batch, seq):
    tq, tk, d, g = NSA_TQ, NSA_TK, HEAD_DIM, NSA_GROUPS
    nq = seq // tq
    nc = seq // NSA_CMP_STRIDE
    n_sel = seq // NSA_SEL_LEN
    d_model = N_HEADS * HEAD_DIM
    cs = np.arange(nc) * NSA_CMP_STRIDE
    ss = np.arange(n_sel) * NSA_SEL_LEN
    ov = np.clip(np.minimum(cs[:, None] + NSA_CMP_LEN, ss[None, :] + NSA_SEL_LEN)
                 - np.maximum(cs[:, None], ss[None, :]), 0, None) / NSA_CMP_LEN
    c2s = np.zeros((nc, NSA_SEL_PAD), np.float32)
    c2s[:, :n_sel] = ov
    c2s[nc - 1, :] = 0.0
    key_blk = (np.arange(seq) // NSA_SEL_LEN).reshape(seq // tk, tk)
    expand = (np.arange(NSA_SEL_PAD)[None, :, None] == key_blk[:, None, :]).astype(np.float32)
    kv_spec = lambda n: pl.BlockSpec((None, None, n, d), lambda b, gi, i: (b, gi, 0, 0))
    return pl.pallas_call(
        _nsa_kernel,
        grid=(batch, g, nq),
        in_specs=[pl.BlockSpec((tq, NSA_HPG * d), lambda b, gi, i: (b * nq + i, gi)),
                  kv_spec(nc), kv_spec(nc), kv_spec(seq), kv_spec(seq), kv_spec(seq), kv_spec(seq),
                  pl.BlockSpec((tq, LANE), lambda b, gi, i: (b * nq + i, gi)),
                  pl.BlockSpec((nc, NSA_SEL_PAD), lambda b, gi, i: (0, 0)),
                  pl.BlockSpec((seq // tk, NSA_SEL_PAD, tk), lambda b, gi, i: (0, 0, 0))],
        out_specs=pl.BlockSpec((tq, NSA_HPG * d), lambda b, gi, i: (b * nq + i, gi)),
        out_shape=jax.ShapeDtypeStruct((batch * seq, d_model), BF16),
        compiler_params=_cparams(("parallel", "parallel", "arbitrary")),
        name="nsa_attention",
    )(q_all, kc, vc, ks, vs, kw, vw, gates, jnp.asarray(c2s, dtype=BF16), jnp.asarray(expand, dtype=BF16))


def _nsa_mixer(x2, g_norm, w_in, pos_k, pos_v, w_k, w_v, w_out, batch, seq):
    d_model, kv = N_HEADS * HEAD_DIM, NSA_KV
    sec = lambda i: w_in[:, d_model + i * kv: d_model + (i + 1) * kv]
    w_main = jnp.concatenate([w_in[:, :d_model], sec(2), sec(4), sec(0), sec(1), sec(3), sec(5)], axis=1)
    proj = _norm_matmul(x2, g_norm, w_main, seq, tn=512, n_rope=3)
    w_gate = w_in[:, d_model + 6 * kv:].reshape(d_model, NSA_GROUPS, 3 * NSA_HPG)
    w_gate = jnp.pad(w_gate, ((0, 0), (0, 0), (0, LANE - 3 * NSA_HPG))).reshape(d_model, NSA_GROUPS * LANE)
    gates = _norm_matmul(x2, g_norm, w_gate, seq, tn=NSA_GROUPS * LANE, out_dtype=F32)

    col = lambda i: proj[:, d_model + i * kv: d_model + (i + 1) * kv]
    heads = lambda x: x.reshape(batch, seq, NSA_GROUPS, HEAD_DIM).transpose(0, 2, 1, 3)
    kc, vc = _nsa_compress(col(2), col(3), pos_k, pos_v, w_k, w_v, batch, seq)
    o = _nsa_attention(proj, kc, vc, heads(col(0)), heads(col(4)), heads(col(1)), heads(col(5)),
                       gates, batch, seq)
    return _proj_resid(o, w_out, x2)


PEER_SEL_TT = 256
PEER_PAIRS = [(i, j) for i in range(PEER_TOP_K + 1) for j in range(PEER_TOP_K + 1)
              if (i + 1) * (j + 1) <= PEER_TOP_K + 1]


def _peer_select_kernel(q_ref, k1_ref, k2_ref, a1_ref, e1_ref, s2_ref, e2_ref):
    q = q_ref[...]
    q_hi, q_lo = _split_bf16(q)
    nh, nr = PEER_HEADS, PEER_TOP_K + 1

    def scores(k_ref, off):
        k_hi, k_lo = _split_bf16(k_ref[...])
        out = []
        for h in range(nh):
            lanes = slice(off + h * PEER_HALF, off + (h + 1) * PEER_HALF)
            out.append(_dot_nt(k_hi, q_hi[:, lanes]) + _dot_nt(k_hi, q_lo[:, lanes])
                       + _dot_nt(k_lo, q_hi[:, lanes]))
        return out

    def top_values(s):
        vals, work = [], s
        for _ in range(nr):
            m = jnp.max(work, axis=0, keepdims=True)
            vals.append(m)
            work = jnp.where(work == m, LOW, work)
        return vals

    s1 = scores(k1_ref, 0)
    s2 = scores(k2_ref, nh * PEER_HALF)
    t1 = [top_values(s) for s in s1]
    t2 = [top_values(s) for s in s2]
    r1 = [jnp.concatenate([t1[h][r] for h in range(nh)], axis=0) for r in range(nr)]
    r2 = [jnp.concatenate([t2[h][r] for h in range(nh)], axis=0) for r in range(nr)]
    cands = [r1[i] + r2[j] for (i, j) in PEER_PAIRS]
    work, tops = list(cands), []
    for _ in range(nr):
        m = functools.reduce(jnp.maximum, work)
        tops.append(m)
        work = [jnp.where(c == m, LOW, c) for c in work]
    thr = 0.5 * (tops[PEER_TOP_K - 1] + tops[PEER_TOP_K])
    cmax = tops[0]
    z = functools.reduce(lambda a, b: a + b, [jnp.where(c >= thr, jnp.exp(c - cmax), 0.0) for c in cands])
    inv_z = 1.0 / z
    for h in range(nh):
        a1_ref[h] = thr[h:h + 1, :] - s1[h]
        e1_ref[h] = jnp.exp(s1[h] - r1[0][h:h + 1, :])
        s2_ref[h] = s2[h]
        e2_ref[h] = jnp.exp(s2[h] - r2[0][h:h + 1, :]) * inv_z[h:h + 1, :]


def _peer_select(q, sub_keys):
    t = q.shape[0]
    tt = min(PEER_SEL_TT, t)
    out_spec = pl.BlockSpec((PEER_HEADS, PEER_KEYS, tt), lambda i: (0, 0, i))
    return pl.pallas_call(
        _peer_select_kernel,
        grid=(t // tt,),
        in_specs=[pl.BlockSpec((tt, q.shape[1]), lambda i: (i, 0)),
                  pl.BlockSpec((PEER_KEYS, PEER_HALF), lambda i: (0, 0)),
                  pl.BlockSpec((PEER_KEYS, PEER_HALF), lambda i: (0, 0))],
        out_specs=[out_spec] * 4,
        out_shape=[jax.ShapeDtypeStruct((PEER_HEADS, PEER_KEYS, t), F32)] * 4,
        compiler_params=_cparams(("parallel",)),
        name="peer_select",
    )(q, sub_keys[0].astype(F32), sub_keys[1].astype(F32))


PEER_TT = 512
PEER_NE = 1024


def _gelu(x):
    return 0.5 * x * (1.0 + lax.erf(x * np.float32(np.sqrt(0.5))))


def _peer_main_kernel(x_ref, g_ref, u_ref, vt_ref, a1_ref, e1_ref, s2_ref, e2_ref, o_ref, xnt_ref, acc_ref):
    e = pl.program_id(1)
    ne = u_ref.shape[0]
    tt = x_ref.shape[0]

    @pl.when(e == 0)
    def _():
        x = x_ref[...]
        y = x * lax.rsqrt(jnp.mean(x * x, axis=-1, keepdims=True) + NORM_EPS)
        xnt_ref[...] = (y * g_ref[...]).T.astype(BF16)
        acc_ref[...] = jnp.zeros_like(acc_ref)

    act = _gelu(_dot(u_ref[...], xnt_ref[...]))
    parts = []
    for c in range(ne // PEER_KEYS):
        i1 = e * (ne // PEER_KEYS) + c
        gsum = jnp.zeros((PEER_KEYS, tt), F32)
        for h in range(PEER_HEADS):
            a1 = a1_ref[h, pl.ds(i1, 1), :]
            e1 = e1_ref[h, pl.ds(i1, 1), :]
            gsum = gsum + jnp.where(s2_ref[h] >= a1, e2_ref[h], 0.0) * e1
        parts.append((gsum * act[c * PEER_KEYS:(c + 1) * PEER_KEYS, :]).astype(BF16))
    w = jnp.concatenate(parts, axis=0)
    acc_ref[...] += _dot(vt_ref[...], w)

    @pl.when(e == pl.num_programs(1) - 1)
    def _():
        o_ref[...] = x_ref[...] + acc_ref[...].T


def _peer_mixer(x2, g_norm, w_q, sub_keys, u, v, seq):
    t, d = x2.shape
    w_q2 = w_q.reshape(d, PEER_HEADS, 2, PEER_HALF).transpose(0, 2, 1, 3).reshape(d, 2 * PEER_HEADS * PEER_HALF)
    q = _norm_matmul3(x2, g_norm, w_q2)
    a1, e1, s2, e2 = _peer_select(q, sub_keys)
    tt = min(PEER_TT, t)
    ne = PEER_NE
    n_exp = u.shape[0]
    tok_spec = pl.BlockSpec((PEER_HEADS, PEER_KEYS, tt), lambda i, e: (0, 0, i))
    return pl.pallas_call(
        _peer_main_kernel,
        grid=(t // tt, n_exp // ne),
        in_specs=[pl.BlockSpec((tt, d), lambda i, e: (i, 0)),
                  pl.BlockSpec((1, d), lambda i, e: (0, 0)),
                  pl.BlockSpec((ne, d), lambda i, e: (e, 0)),
                  pl.BlockSpec((d, ne), lambda i, e: (0, e)),
                  tok_spec, tok_spec, tok_spec, tok_spec],
        out_specs=pl.BlockSpec((tt, d), lambda i, e: (i, 0)),
        out_shape=jax.ShapeDtypeStruct((t, d), F32),
        scratch_shapes=[pltpu.VMEM((d, tt), BF16), pltpu.VMEM((d, tt), F32)],
        compiler_params=_cparams(("parallel", "arbitrary")),
        name="peer_main",
    )(x2, g_norm.reshape(1, d).astype(F32), u.astype(BF16), v.T.astype(BF16), a1, e1, s2, e2)


def _sb_mixer(x2, g_norm, w_in, w_out, batch, seq):
    qkv = _norm_matmul(x2, g_norm, w_in, seq, tn=512)
    return _proj_resid(_sb_attention(qkv, batch, seq), w_out, x2)


def _moba_mixer(x2, g_norm, w_in, w_out, batch, seq):
    d_model = N_HEADS * HEAD_DIM
    qkv = _norm_matmul(x2, g_norm, w_in, seq, tn=512, n_rope=2 * d_model // 512)
    return _proj_resid(_moba_attention(qkv, batch, seq), w_out, x2)


def kernel(x, l0_norm_mix, l0_sb_w_in, l0_sb_w_out, l0_norm_ffn, l0_peer_w_q, l0_peer_sub_keys, l0_peer_u, l0_peer_v, l1_norm_mix, l1_nsa_w_in, l1_nsa_cmp_pos_k, l1_nsa_cmp_pos_v, l1_nsa_cmp_w_k, l1_nsa_cmp_w_v, l1_nsa_w_out, l1_norm_ffn, l1_peer_w_q, l1_peer_sub_keys, l1_peer_u, l1_peer_v, l2_norm_mix, l2_moba_w_in, l2_moba_w_out, l2_norm_ffn, l2_peer_w_q, l2_peer_sub_keys, l2_peer_u, l2_peer_v, l3_norm_mix, l3_sb_w_in, l3_sb_w_out, l3_norm_ffn, l3_peer_w_q, l3_peer_sub_keys, l3_peer_u, l3_peer_v, final_norm):
    batch, seq, d = x.shape
    x2 = x.reshape(batch * seq, d)
    x2 = _sb_mixer(x2, l0_norm_mix, l0_sb_w_in, l0_sb_w_out, batch, seq)
    x2 = _peer_mixer(x2, l0_norm_ffn, l0_peer_w_q, l0_peer_sub_keys, l0_peer_u, l0_peer_v, seq)
    x2 = _nsa_mixer(x2, l1_norm_mix, l1_nsa_w_in, l1_nsa_cmp_pos_k, l1_nsa_cmp_pos_v, l1_nsa_cmp_w_k,
                    l1_nsa_cmp_w_v, l1_nsa_w_out, batch, seq)
    x2 = _peer_mixer(x2, l1_norm_ffn, l1_peer_w_q, l1_peer_sub_keys, l1_peer_u, l1_peer_v, seq)
    x2 = _moba_mixer(x2, l2_norm_mix, l2_moba_w_in, l2_moba_w_out, batch, seq)
    x2 = _peer_mixer(x2, l2_norm_ffn, l2_peer_w_q, l2_peer_sub_keys, l2_peer_u, l2_peer_v, seq)
    x2 = _sb_mixer(x2, l3_norm_mix, l3_sb_w_in, l3_sb_w_out, batch, seq)
    x2 = _peer_mixer(x2, l3_norm_ffn, l3_peer_w_q, l3_peer_sub_keys, l3_peer_u, l3_peer_v, seq)
    return _rmsnorm(x2, final_norm).reshape(batch, seq, d)
```

```python
import functools

import numpy as np
import jax
import jax.numpy as jnp
from jax import lax
from jax.experimental import pallas as pl
from jax.experimental.pallas import tpu as pltpu

F32 = jnp.float32
BF16 = jnp.bfloat16

N_HEADS = 16
HEAD_DIM = 64
ROPE_DIM = 16
ROPE_HALF = ROPE_DIM // 2
ROPE_THETA = 500000.0
NORM_EPS = 1e-6
NEG = -1e30
LOW = -3e38
ATT_SCALE = HEAD_DIM ** -0.5

NSA_GROUPS = 4
NSA_HPG = N_HEADS // NSA_GROUPS
NSA_KV = NSA_GROUPS * HEAD_DIM
NSA_CMP_LEN = 32
NSA_CMP_STRIDE = 16
NSA_SEL_LEN = 64
NSA_TOP_N = 16
NSA_WINDOW = 512
NSA_BONUS = 1e3
NSA_SEL_PAD = 128

MOBA_BLOCK = 256
MOBA_TOP_K = 3
MOBA_BLK_PAD = 128

PEER_HEADS = 8
PEER_KEYS = 128
PEER_TOP_K = 16
PEER_HALF = 64

SB_EXIT = -110.0

LANE = 128
VMEM_LIMIT = 56 << 20


def _cparams(sem, vmem=VMEM_LIMIT):
    return pltpu.CompilerParams(dimension_semantics=sem, vmem_limit_bytes=vmem)


def _dot(a, b):
    return jnp.dot(a, b, preferred_element_type=F32)


def _dot_nt(a, b):
    return lax.dot_general(a, b, (((1,), (1,)), ((), ())), preferred_element_type=F32)


def _split_bf16(x):
    hi = x.astype(BF16)
    lo = (x - hi.astype(F32)).astype(BF16)
    return hi, lo


def _topk_mask(vals, k):
    lane = lax.broadcasted_iota(jnp.int32, vals.shape, 1)
    work = vals
    sel = jnp.zeros(vals.shape, F32)
    for _ in range(k):
        m = jnp.max(work, axis=-1, keepdims=True)
        first = jnp.min(jnp.where(work == m, lane, LANE), axis=-1, keepdims=True)
        hit = lane == first
        sel = jnp.where(hit, 1.0, sel)
        work = jnp.where(hit, LOW, work)
    return sel


def _norm_matmul_kernel(*refs, n_rope, tn):
    if n_rope:
        x_ref, g_ref, w_ref, c_ref, sa_ref, sb_ref, o_ref, xn_ref = refs
    else:
        x_ref, g_ref, w_ref, o_ref, xn_ref = refs
    j = pl.program_id(1)

    @pl.when(j == 0)
    def _():
        x = x_ref[...]
        y = x * lax.rsqrt(jnp.mean(x * x, axis=-1, keepdims=True) + NORM_EPS)
        xn_ref[...] = (y * g_ref[...]).astype(BF16)

    acc = _dot(xn_ref[...], w_ref[...])
    if n_rope:
        @pl.when(j < n_rope)
        def _():
            r = (acc * c_ref[...] + pltpu.roll(acc, ROPE_HALF, 1) * sb_ref[...]
                 + pltpu.roll(acc, tn - ROPE_HALF, 1) * sa_ref[...])
            o_ref[...] = r.astype(o_ref.dtype)

        @pl.when(j >= n_rope)
        def _():
            o_ref[...] = acc.astype(o_ref.dtype)
    else:
        o_ref[...] = acc.astype(o_ref.dtype)


def _rope_tables(pos, width):
    inv = ROPE_THETA ** (-jnp.arange(0, ROPE_DIM, 2, dtype=F32) / ROPE_DIM)
    ang = pos.astype(F32)[:, None] * inv[None, :]
    cos, sin = jnp.cos(ang), jnp.sin(ang)
    n = pos.shape[0]
    rest = HEAD_DIM - ROPE_DIM
    c = jnp.concatenate([cos, cos, jnp.ones((n, rest), F32)], axis=1)
    sa = jnp.concatenate([-sin, jnp.zeros((n, ROPE_HALF + rest), F32)], axis=1)
    sb = jnp.concatenate([jnp.zeros((n, ROPE_HALF), F32), sin, jnp.zeros((n, rest), F32)], axis=1)
    reps = width // HEAD_DIM
    return jnp.tile(c, (1, reps)), jnp.tile(sa, (1, reps)), jnp.tile(sb, (1, reps))


def _norm_matmul(x2, g, w, seq, *, tn, n_rope=0, out_dtype=BF16, tt=512):
    t, d = x2.shape
    n = w.shape[1]
    tt = min(tt, seq)
    grid = (t // tt, n // tn)
    in_specs = [pl.BlockSpec((tt, d), lambda i, j: (i, 0)),
                pl.BlockSpec((1, d), lambda i, j: (0, 0)),
                pl.BlockSpec((d, tn), lambda i, j: (0, j))]
    args = [x2, g.reshape(1, d).astype(F32), w.astype(BF16)]
    if n_rope:
        nper = seq // tt
        tabs = _rope_tables(jnp.arange(seq), tn)
        in_specs += [pl.BlockSpec((tt, tn), lambda i, j: (i % nper, 0))] * 3
        args += list(tabs)
    return pl.pallas_call(
        functools.partial(_norm_matmul_kernel, n_rope=n_rope, tn=tn),
        grid=grid, in_specs=in_specs,
        out_specs=pl.BlockSpec((tt, tn), lambda i, j: (i, j)),
        out_shape=jax.ShapeDtypeStruct((t, n), out_dtype),
        scratch_shapes=[pltpu.VMEM((tt, d), BF16)],
        compiler_params=_cparams(("parallel", "arbitrary")),
        name="norm_matmul",
    )(*args)


def _norm_matmul3_kernel(x_ref, g_ref, wh_ref, wl_ref, o_ref, xh_ref, xl_ref):
    @pl.when(pl.program_id(1) == 0)
    def _():
        x = x_ref[...]
        y = x * lax.rsqrt(jnp.mean(x * x, axis=-1, keepdims=True) + NORM_EPS) * g_ref[...]
        xh_ref[...], xl_ref[...] = _split_bf16(y)

    o_ref[...] = (_dot(xh_ref[...], wh_ref[...]) + _dot(xl_ref[...], wh_ref[...])
                  + _dot(xh_ref[...], wl_ref[...]))


def _norm_matmul3(x2, g, w, *, tn=512, tt=512):
    t, d = x2.shape
    n = w.shape[1]
    tt = min(tt, t)
    w_hi, w_lo = _split_bf16(w.astype(F32))
    w_spec = pl.BlockSpec((d, tn), lambda i, j: (0, j))
    return pl.pallas_call(
        _norm_matmul3_kernel,
        grid=(t // tt, n // tn),
        in_specs=[pl.BlockSpec((tt, d), lambda i, j: (i, 0)),
                  pl.BlockSpec((1, d), lambda i, j: (0, 0)), w_spec, w_spec],
        out_specs=pl.BlockSpec((tt, tn), lambda i, j: (i, j)),
        out_shape=jax.ShapeDtypeStruct((t, n), F32),
        scratch_shapes=[pltpu.VMEM((tt, d), BF16), pltpu.VMEM((tt, d), BF16)],
        compiler_params=_cparams(("parallel", "arbitrary")),
        name="norm_matmul3",
    )(x2, g.reshape(1, d).astype(F32), w_hi, w_lo)


def _proj_resid_kernel(a_ref, w_ref, r_ref, o_ref):
    o_ref[...] = r_ref[...] + _dot(a_ref[...], w_ref[...])


def _proj_resid(a, w, resid, *, tt=512, tn=512):
    t, d = a.shape
    n = w.shape[1]
    tt = min(tt, t)
    return pl.pallas_call(
        _proj_resid_kernel,
        grid=(t // tt, n // tn),
        in_specs=[pl.BlockSpec((tt, d), lambda i, j: (i, 0)),
                  pl.BlockSpec((d, tn), lambda i, j: (0, j)),
                  pl.BlockSpec((tt, tn), lambda i, j: (i, j))],
        out_specs=pl.BlockSpec((tt, tn), lambda i, j: (i, j)),
        out_shape=jax.ShapeDtypeStruct((t, n), F32),
        compiler_params=_cparams(("parallel", "parallel")),
        name="proj_resid",
    )(a, w.astype(BF16), resid)


def _rmsnorm_kernel(x_ref, g_ref, o_ref):
    x = x_ref[...]
    y = x * lax.rsqrt(jnp.mean(x * x, axis=-1, keepdims=True) + NORM_EPS)
    o_ref[...] = y * g_ref[...]


def _rmsnorm(x2, g, *, tt=512):
    t, d = x2.shape
    tt = min(tt, t)
    return pl.pallas_call(
        _rmsnorm_kernel,
        grid=(t // tt,),
        in_specs=[pl.BlockSpec((tt, d), lambda i: (i, 0)), pl.BlockSpec((1, d), lambda i: (0, 0))],
        out_specs=pl.BlockSpec((tt, d), lambda i: (i, 0)),
        out_shape=jax.ShapeDtypeStruct((t, d), F32),
        compiler_params=_cparams(("parallel",)),
        name="final_rmsnorm",
    )(x2, g.reshape(1, d).astype(F32))


SB_TILE = 256


def _sb_kernel(q_ref, k_ref, v_ref, tri_ref, o_ref):
    tile = SB_TILE
    qi = pl.program_id(2)
    tri = tri_ref[...]
    row = lax.broadcasted_iota(jnp.int32, (tile, tile), 0)
    col = lax.broadcasted_iota(jnp.int32, (tile, tile), 1)
    past = col < row

    for hh in range(2):
        lanes = slice(hh * HEAD_DIM, (hh + 1) * HEAD_DIM)
        q = q_ref[:, lanes] * jnp.asarray(ATT_SCALE, BF16)

        def step(start, r_sum, acc, diag):
            k = k_ref[pl.ds(start, tile), lanes]
            v = v_ref[pl.ds(start, tile), lanes]
            z = _dot_nt(q, k)
            sp = jnp.maximum(z, 0.0) + jnp.log1p(jnp.exp(-jnp.abs(z)))
            lb = jnp.where(past, -sp, 0.0) if diag else -sp
            hi, lo = _split_bf16(lb)
            suffix = _dot(hi, tri) + _dot(lo, tri)
            a = jnp.exp((z - sp) + suffix + r_sum)
            if diag:
                a = jnp.where(past, a, 0.0)
            acc = acc + _dot(a.astype(BF16), v)
            r_sum = r_sum + jnp.sum(lb, axis=-1, keepdims=True)
            return r_sum, acc

        r0 = jnp.zeros((tile, 1), F32)
        acc0 = jnp.zeros((tile, HEAD_DIM), F32)
        r1, acc1 = step(pl.multiple_of(qi * tile, tile), r0, acc0, True)

        def older(j, r_sum):
            return jnp.where(jnp.max(r_sum) > SB_EXIT, j - 1, -1)

        def cond(c):
            return c[0] >= 0

        def body(c):
            j, r_sum, acc = c
            r_sum, acc = step(pl.multiple_of(j * tile, tile), r_sum, acc, False)
            return older(j, r_sum), r_sum, acc

        _, _, acc = lax.while_loop(cond, body, (older(qi, r1), r1, acc1))
        o_ref[:, lanes] = acc.astype(o_ref.dtype)


def _sb_attention(qkv, batch, seq):
    tile = SB_TILE
    nq = seq // tile
    d_model = N_HEADS * HEAD_DIM
    ncol = d_model // LANE
    ii = np.arange(tile)
    tri = jnp.asarray(ii[:, None] > ii[None, :], dtype=BF16)
    return pl.pallas_call(
        _sb_kernel,
        grid=(batch, ncol, nq),
        in_specs=[pl.BlockSpec((tile, LANE), lambda b, h, i: (b * nq + i, h)),
                  pl.BlockSpec((seq, LANE), lambda b, h, i: (b, ncol + h)),
                  pl.BlockSpec((seq, LANE), lambda b, h, i: (b, 2 * ncol + h)),
                  pl.BlockSpec((tile, tile), lambda b, h, i: (0, 0))],
        out_specs=pl.BlockSpec((tile, LANE), lambda b, h, i: (b * nq + i, h)),
        out_shape=jax.ShapeDtypeStruct((batch * seq, d_model), BF16),
        compiler_params=_cparams(("parallel", "parallel", "arbitrary")),
        name="sb_attention",
    )(qkv, qkv, qkv, tri)


def _flash_first(q, k, v, mask):
    s = jnp.where(mask, _dot_nt(q, k), NEG)
    m = jnp.max(s, axis=-1, keepdims=True)
    p = jnp.exp(s - m)
    return m, jnp.sum(p, axis=-1, keepdims=True), _dot(p.astype(BF16), v)


def _flash_next(q, k, v, mask, state):
    m, l, acc = state
    s = jnp.where(mask, _dot_nt(q, k), NEG)
    m_new = jnp.maximum(m, jnp.max(s, axis=-1, keepdims=True))
    p = jnp.exp(s - m_new)
    alpha = jnp.exp(m - m_new)
    l = alpha * l + jnp.sum(p, axis=-1, keepdims=True)
    acc = alpha * acc + _dot(p.astype(BF16), v)
    return m_new, l, acc


def _flash_out(state):
    _, l, acc = state
    return acc / l


MOBA_WALK = 2


def _moba_kernel(q_ref, k_ref, v_ref, avg_ref, exp_ref, o_ref, kmean_ref):
    blk = MOBA_BLOCK
    wide = MOBA_WALK * blk
    qi = pl.program_id(2)

    @pl.when(qi == 0)
    def _():
        kmean_ref[...] = _dot(avg_ref[...], k_ref[...])

    row = lax.broadcasted_iota(jnp.int32, (blk, blk), 0)
    col = lax.broadcasted_iota(jnp.int32, (blk, blk), 1)
    causal = col <= row
    lane = lax.broadcasted_iota(jnp.int32, (blk, MOBA_BLK_PAD), 1)
    start = pl.multiple_of(qi * blk, blk)
    head_lanes = [slice(hh * HEAD_DIM, (hh + 1) * HEAD_DIM) for hh in range(2)]

    qs, sels, states = [], [], []
    for lanes in head_lanes:
        q_raw = q_ref[:, lanes]
        km_hi, km_lo = _split_bf16(kmean_ref[:, lanes])
        gate = _dot_nt(q_raw, km_hi) + _dot_nt(q_raw, km_lo)
        gate = jnp.where(lane < qi, gate, NEG)
        sels.append(jnp.where(lane < qi, _topk_mask(gate, MOBA_TOP_K), 0.0).astype(BF16))
        qs.append(q_raw * jnp.asarray(ATT_SCALE, BF16))
        states.append(_flash_first(qs[-1], k_ref[pl.ds(start, blk), lanes], v_ref[pl.ds(start, blk), lanes], causal))

    def body(p, sts):
        s0 = pl.multiple_of(p * wide, wide)
        out = []
        for hh, lanes in enumerate(head_lanes):
            mask = _dot(sels[hh], exp_ref[p]) > 0.5
            out.append(_flash_next(qs[hh], k_ref[pl.ds(s0, wide), lanes], v_ref[pl.ds(s0, wide), lanes],
                                   mask, sts[hh]))
        return tuple(out)

    states = lax.fori_loop(0, (qi + MOBA_WALK - 1) // MOBA_WALK, body, tuple(states))
    for hh, lanes in enumerate(head_lanes):
        o_ref[:, lanes] = _flash_out(states[hh]).astype(o_ref.dtype)


def _moba_attention(qkv, batch, seq):
    blk = MOBA_BLOCK
    wide = MOBA_WALK * blk
    nq = seq // blk
    d_model = N_HEADS * HEAD_DIM
    ncol = d_model // LANE
    avg = np.zeros((MOBA_BLK_PAD, seq), np.float32)
    for n in range(nq):
        avg[n, n * blk:(n + 1) * blk] = 1.0 / blk
    avg = jnp.asarray(avg, dtype=BF16)
    key_blk = (np.arange(seq) // blk).reshape(seq // wide, wide)
    expand = jnp.asarray(np.arange(MOBA_BLK_PAD)[None, :, None] == key_blk[:, None, :], dtype=BF16)
    return pl.pallas_call(
        _moba_kernel,
        grid=(batch, ncol, nq),
        in_specs=[pl.BlockSpec((blk, LANE), lambda b, h, i: (b * nq + i, h)),
                  pl.BlockSpec((seq, LANE), lambda b, h, i: (b, ncol + h)),
                  pl.BlockSpec((seq, LANE), lambda b, h, i: (b, 2 * ncol + h)),
                  pl.BlockSpec((MOBA_BLK_PAD, seq), lambda b, h, i: (0, 0)),
                  pl.BlockSpec((seq // wide, MOBA_BLK_PAD, wide), lambda b, h, i: (0, 0, 0))],
        out_specs=pl.BlockSpec((blk, LANE), lambda b, h, i: (b * nq + i, h)),
        out_shape=jax.ShapeDtypeStruct((batch * seq, d_model), BF16),
        scratch_shapes=[pltpu.VMEM((MOBA_BLK_PAD, LANE), F32)],
        compiler_params=_cparams(("parallel", "parallel", "arbitrary")),
        name="moba_attention",
    )(qkv, qkv, qkv, avg, expand)


def _nsa_compress_kernel(kr_ref, vr_ref, pk_ref, pv_ref, wk_ref, wv_ref, c_ref, sa_ref, sb_ref, kc_ref, vc_ref):
    nc = kr_ref.shape[0]
    rowi = lax.broadcasted_iota(jnp.int32, (nc, 1), 0)

    def windows(x_ref, pos_ref, w_ref):
        x = x_ref[...].astype(F32)
        lo = _dot((x + pos_ref[0:1, :]).astype(BF16), w_ref[0])
        hi = _dot((x + pos_ref[1:2, :]).astype(BF16), w_ref[1])
        hi_next = jnp.where(rowi < nc - 1, pltpu.roll(hi, nc - 1, 0), 0.0)
        return lo + hi_next

    kk = windows(kr_ref, pk_ref, wk_ref)
    d = HEAD_DIM
    kc = kk[:, 0:d] * c_ref[...] + kk[:, d:2 * d] * sb_ref[...] + kk[:, 2 * d:3 * d] * sa_ref[...]
    kc_ref[...] = kc.astype(kc_ref.dtype)
    vv = windows(vr_ref, pv_ref, wv_ref)
    vc_ref[...] = vv[:, 0:d].astype(vc_ref.dtype)


def _nsa_compress(k_cmp, v_cmp, pos_k, pos_v, w_k, w_v, batch, seq):
    g, d, st = NSA_GROUPS, HEAD_DIM, NSA_CMP_STRIDE
    nc = seq // st

    def chunks(x):
        return x.reshape(batch, nc, st, g, d).transpose(0, 3, 1, 2, 4).reshape(batch, g, nc, st * d)

    def weights(w):
        w3 = w.reshape(NSA_CMP_LEN, d, d)
        cat = jnp.concatenate([w3, jnp.roll(w3, ROPE_HALF, axis=2), jnp.roll(w3, -ROPE_HALF, axis=2)], axis=2)
        return cat.reshape(2, st * d, 3 * d).astype(BF16)

    def positions(p):
        return p.reshape(2, st * d).astype(F32)

    cmp_end = jnp.arange(nc) * st + NSA_CMP_LEN - 1
    tabs = _rope_tables(cmp_end, d)
    blk4 = pl.BlockSpec((None, None, nc, st * d), lambda b, gi: (b, gi, 0, 0))
    out4 = pl.BlockSpec((None, None, nc, d), lambda b, gi: (b, gi, 0, 0))
    const2 = lambda shape: pl.BlockSpec(shape, lambda b, gi: (0,) * len(shape))
    return pl.pallas_call(
        _nsa_compress_kernel,
        grid=(batch, g),
        in_specs=[blk4, blk4, const2((2, st * d)), const2((2, st * d)),
                  const2((2, st * d, 3 * d)), const2((2, st * d, 3 * d)),
                  const2((nc, d)), const2((nc, d)), const2((nc, d))],
        out_specs=[out4, out4],
        out_shape=[jax.ShapeDtypeStruct((batch, g, nc, d), BF16)] * 2,
        compiler_params=_cparams(("parallel", "parallel")),
        name="nsa_compress",
    )(chunks(k_cmp), chunks(v_cmp), positions(pos_k), positions(pos_v), weights(w_k), weights(w_v), *tabs)


NSA_TQ = 128
NSA_TK = 512


def _nsa_kernel(q_ref, kc_ref, vc_ref, ks_ref, vs_ref, kw_ref, vw_ref, gate_ref, c2s_ref, exp_ref, o_ref):
    tq, tk, d, nh = NSA_TQ, NSA_TK, HEAD_DIM, NSA_HPG
    rows = nh * tq
    qi = pl.program_id(2)
    t0 = qi * tq
    nc = kc_ref.shape[0]

    def stack(x):
        return jnp.concatenate([x] * nh, axis=0)

    tpos1 = t0 + lax.broadcasted_iota(jnp.int32, (tq, 1), 0)
    tpos = stack(tpos1)
    gates = jax.nn.sigmoid(gate_ref[...])

    def gate(branch):
        return jnp.concatenate([gates[:, 3 * h + branch:3 * h + branch + 1] for h in range(nh)], axis=0)

    q = jnp.concatenate([q_ref[:, h * d:(h + 1) * d] for h in range(nh)], axis=0) * jnp.asarray(ATT_SCALE, BF16)

    cmp_end = lax.broadcasted_iota(jnp.int32, (rows, nc), 1) * NSA_CMP_STRIDE + (NSA_CMP_LEN - 1)
    vis_c = cmp_end <= tpos
    s = jnp.where(vis_c, _dot_nt(q, kc_ref[...]), NEG)
    e = jnp.where(vis_c, jnp.exp(s - jnp.max(s, axis=-1, keepdims=True)), 0.0)
    l = jnp.sum(e, axis=-1, keepdims=True)
    p = e / jnp.where(l > 0.0, l, 1.0)
    out = gate(0) * _dot(p.astype(BF16), vc_ref[...])
    psum = functools.reduce(lambda a, b: a + b, [p[h * tq:(h + 1) * tq] for h in range(nh)])
    p_hi, p_lo = _split_bf16(psum)
    imp = _dot(p_hi, c2s_ref[...]) + _dot(p_lo, c2s_ref[...])
    blk = lax.broadcasted_iota(jnp.int32, (tq, NSA_SEL_PAD), 1)
    own = tpos1 // NSA_SEL_LEN
    forced = (blk == 0) | (blk == own) | (blk == own - 1)
    imp = jnp.where(blk <= own, imp + jnp.where(forced, NSA_BONUS, 0.0), NEG)
    sel = stack(_topk_mask(imp, NSA_TOP_N).astype(BF16))

    kd = t0 // tk
    dstart = pl.multiple_of(kd * tk, tk)
    kpos = dstart + lax.broadcasted_iota(jnp.int32, (rows, tk), 1)
    dmask = (_dot(sel, exp_ref[kd]) > 0.5) & (kpos <= tpos)
    state = _flash_first(q, ks_ref[pl.ds(dstart, tk), :], vs_ref[pl.ds(dstart, tk), :], dmask)

    def sel_body(kb, state):
        start = pl.multiple_of(kb * tk, tk)
        mask = _dot(sel, exp_ref[kb]) > 0.5
        return _flash_next(q, ks_ref[pl.ds(start, tk), :], vs_ref[pl.ds(start, tk), :], mask, state)

    state = lax.fori_loop(0, kd, sel_body, state)
    out = out + gate(1) * _flash_out(state)

    span = NSA_WINDOW + tq
    wstart = pl.multiple_of(jnp.maximum(t0 - NSA_WINDOW, 0), tq)
    wpos = wstart + lax.broadcasted_iota(jnp.int32, (rows, span), 1)
    wmask = (wpos <= tpos) & (wpos > tpos - NSA_WINDOW)
    win = _flash_first(q, kw_ref[pl.ds(wstart, span), :], vw_ref[pl.ds(wstart, span), :], wmask)
    out = out + gate(2) * _flash_out(win)
    for h in range(nh):
        o_ref[:, h * d:(h + 1) * d] = out[h * tq:(h + 1) * tq].astype(o_ref.dtype)


def _nsa_attention(q_all, kc, vc, ks, vs, kw, vw, gates, batch, seq):
    tq, tk, d, g = NSA_TQ, NSA_TK, HEAD_DIM, NSA_GROUPS
    nq = seq // tq
    nc = seq // NSA_CMP_STRIDE
    n_sel = seq // NSA_SEL_LEN
    d_model = N_HEADS * HEAD_DIM
    cs = np.arange(nc) * NSA_CMP_STRIDE
    ss = np.arange(n_sel) * NSA_SEL_LEN
    ov = np.clip(np.minimum(cs[:, None] + NSA_CMP_LEN, ss[None, :] + NSA_SEL_LEN)
                 - np.maximum(cs[:, None], ss[None, :]), 0, None) / NSA_CMP_LEN
    c2s = np.zeros((nc, NSA_SEL_PAD), np.float32)
    c2s[:, :n_sel] = ov
    c2s[nc - 1, :] = 0.0
    key_blk = (np.arange(seq) // NSA_SEL_LEN).reshape(seq // tk, tk)
    expand = (np.arange(NSA_SEL_PAD)[None, :, None] == key_blk[:, None, :]).astype(np.float32)
    kv_spec = lambda n: pl.BlockSpec((None, None, n, d), lambda b, gi, i: (b, gi, 0, 0))
    return pl.pallas_call(
        _nsa_kernel,
        grid=(batch, g, nq),
        in_specs=[pl.BlockSpec((tq, NSA_HPG * d), lambda b, gi, i: (b * nq + i, gi)),
                  kv_spec(nc), kv_spec(nc), kv_spec(seq), kv_spec(seq), kv_spec(seq), kv_spec(seq),
                  pl.BlockSpec((tq, LANE), lambda b, gi, i: (b * nq + i, gi)),
                  pl.BlockSpec((nc, NSA_SEL_PAD), lambda b, gi, i: (0, 0)),
                  pl.BlockSpec((seq // tk, NSA_SEL_PAD, tk), lambda b, gi, i: (0, 0, 0))],
        out_specs=pl.BlockSpec((tq, NSA_HPG * d), lambda b, gi, i: (b * nq + i, gi)),
        out_shape=jax.ShapeDtypeStruct((batch * seq, d_model), BF16),
        compiler_params=_cparams(("parallel", "parallel", "arbitrary")),
        name="nsa_attention",
    )(q_all, kc, vc, ks, vs, kw, vw, gates, jnp.asarray(c2s, dtype=BF16), jnp.asarray(expand, dtype=BF16))


def _nsa_mixer(x2, g_norm, w_in, pos_k, pos_v, w_k, w_v, w_out, batch, seq):
    d_model, kv = N_HEADS * HEAD_DIM, NSA_KV
    sec = lambda i: w_in[:, d_model + i * kv: d_model + (i + 1) * kv]
    w_main = jnp.concatenate([w_in[:, :d_model], sec(2), sec(4), sec(0), sec(1), sec(3), sec(5)], axis=1)
    proj = _norm_matmul(x2, g_norm, w_main, seq, tn=512, n_rope=3)
    w_gate = w_in[:, d_model + 6 * kv:].reshape(d_model, NSA_GROUPS, 3 * NSA_HPG)
    w_gate = jnp.pad(w_gate, ((0, 0), (0, 0), (0, LANE - 3 * NSA_HPG))).reshape(d_model, NSA_GROUPS * LANE)
    gates = _norm_matmul(x2, g_norm, w_gate, seq, tn=NSA_GROUPS * LANE, out_dtype=F32)

    col = lambda i: proj[:, d_model + i * kv: d_model + (i + 1) * kv]
    heads = lambda x: x.reshape(batch, seq, NSA_GROUPS, HEAD_DIM).transpose(0, 2, 1, 3)
    kc, vc = _nsa_compress(col(2), col(3), pos_k, pos_v, w_k, w_v, batch, seq)
    o = _nsa_attention(proj, kc, vc, heads(col(0)), heads(col(4)), heads(col(1)), heads(col(5)),
                       gates, batch, seq)
    return _proj_resid(o, w_out, x2)


PEER_SEL_TT = 256
PEER_PAIRS = [(i, j) for i in range(PEER_TOP_K + 1) for j in range(PEER_TOP_K + 1)
              if (i + 1) * (j + 1) <= PEER_TOP_K + 1]


def _peer_select_kernel(q_ref, k1_ref, k2_ref, cnt_ref, e1_ref, rank_ref, e2_ref):
    q = q_ref[...]
    q_hi, q_lo = _split_bf16(q)
    nh, nr = PEER_HEADS, PEER_TOP_K + 1

    def scores(k_ref, off):
        k_hi, k_lo = _split_bf16(k_ref[...])
        out = []
        for h in range(nh):
            lanes = slice(off + h * PEER_HALF, off + (h + 1) * PEER_HALF)
            out.append(_dot_nt(k_hi, q_hi[:, lanes]) + _dot_nt(k_hi, q_lo[:, lanes])
                       + _dot_nt(k_lo, q_hi[:, lanes]))
        return out

    def top_values(s):
        vals, work = [], s
        for _ in range(nr):
            m = jnp.max(work, axis=0, keepdims=True)
            vals.append(m)
            work = jnp.where(work == m, LOW, work)
        return vals

    s1 = scores(k1_ref, 0)
    s2 = scores(k2_ref, nh * PEER_HALF)
    t1 = [top_values(s) for s in s1]
    t2 = [top_values(s) for s in s2]
    r1 = [jnp.concatenate([t1[h][r] for h in range(nh)], axis=0) for r in range(nr)]
    r2 = [jnp.concatenate([t2[h][r] for h in range(nh)], axis=0) for r in range(nr)]
    cands = [r1[i] + r2[j] for (i, j) in PEER_PAIRS]
    work, tops = list(cands), []
    for _ in range(nr):
        m = functools.reduce(jnp.maximum, work)
        tops.append(m)
        work = [jnp.where(c == m, LOW, c) for c in work]
    thr = 0.5 * (tops[PEER_TOP_K - 1] + tops[PEER_TOP_K])
    cmax = tops[0]
    z = functools.reduce(lambda a, b: a + b, [jnp.where(c >= thr, jnp.exp(c - cmax), 0.0) for c in cands])
    inv_z = 1.0 / z
    for h in range(nh):
        thr_h = thr[h:h + 1, :]
        cnt = jnp.zeros_like(s1[h])
        rank = jnp.zeros_like(s2[h])
        for r in range(nr):
            cnt = jnp.where(s1[h] + t2[h][r] >= thr_h, float(r + 1), cnt)
            rank = jnp.where(t2[h][r] > s2[h], float(r + 1), rank)
        cnt_ref[h] = cnt
        e1_ref[h] = jnp.exp(s1[h] - r1[0][h:h + 1, :])
        rank_ref[h] = rank.astype(BF16)
        e2_ref[h] = (jnp.exp(s2[h] - r2[0][h:h + 1, :]) * inv_z[h:h + 1, :]).astype(BF16)


def _peer_select(q, sub_keys):
    t = q.shape[0]
    tt = min(PEER_SEL_TT, t)
    out_spec = pl.BlockSpec((PEER_HEADS, PEER_KEYS, tt), lambda i: (0, 0, i))
    return pl.pallas_call(
        _peer_select_kernel,
        grid=(t // tt,),
        in_specs=[pl.BlockSpec((tt, q.shape[1]), lambda i: (i, 0)),
                  pl.BlockSpec((PEER_KEYS, PEER_HALF), lambda i: (0, 0)),
                  pl.BlockSpec((PEER_KEYS, PEER_HALF), lambda i: (0, 0))],
        out_specs=[out_spec] * 4,
        out_shape=[jax.ShapeDtypeStruct((PEER_HEADS, PEER_KEYS, t), dt) for dt in (F32, F32, BF16, BF16)],
        compiler_params=_cparams(("parallel",)),
        name="peer_select",
    )(q, sub_keys[0].astype(F32), sub_keys[1].astype(F32))


PEER_TT = 512
PEER_NE = 1024


def _gelu(x):
    return 0.5 * x * (1.0 + lax.erf(x * np.float32(np.sqrt(0.5))))


def _peer_main_kernel(x_ref, g_ref, u_ref, vt_ref, cnt_ref, e1_ref, rank_ref, e2_ref, o_ref, xnt_ref, acc_ref):
    e = pl.program_id(1)
    ne = u_ref.shape[0]
    tt = x_ref.shape[0]

    @pl.when(e == 0)
    def _():
        x = x_ref[...]
        y = x * lax.rsqrt(jnp.mean(x * x, axis=-1, keepdims=True) + NORM_EPS)
        xnt_ref[...] = (y * g_ref[...]).T.astype(BF16)
        acc_ref[...] = jnp.zeros_like(acc_ref)

    act = _gelu(_dot(u_ref[...], xnt_ref[...])).astype(BF16)
    parts = []
    for c in range(ne // PEER_KEYS):
        i1 = e * (ne // PEER_KEYS) + c
        gsum = jnp.zeros((PEER_KEYS, tt), BF16)
        for h in range(PEER_HEADS):
            cnt = cnt_ref[h, pl.ds(i1, 1), :].astype(BF16)
            e1 = e1_ref[h, pl.ds(i1, 1), :].astype(BF16)
            gsum = gsum + jnp.where(rank_ref[h] < cnt, e2_ref[h], jnp.zeros((), BF16)) * e1
        parts.append(gsum * act[c * PEER_KEYS:(c + 1) * PEER_KEYS, :])
    w = jnp.concatenate(parts, axis=0)
    acc_ref[...] += _dot(vt_ref[...], w)

    @pl.when(e == pl.num_programs(1) - 1)
    def _():
        o_ref[...] = x_ref[...] + acc_ref[...].T


def _peer_mixer(x2, g_norm, w_q, sub_keys, u, v, seq):
    t, d = x2.shape
    w_q2 = w_q.reshape(d, PEER_HEADS, 2, PEER_HALF).transpose(0, 2, 1, 3).reshape(d, 2 * PEER_HEADS * PEER_HALF)
    q = _norm_matmul3(x2, g_norm, w_q2)
    a1, e1, s2, e2 = _peer_select(q, sub_keys)
    tt = min(PEER_TT, t)
    ne = PEER_NE
    n_exp = u.shape[0]
    tok_spec = pl.BlockSpec((PEER_HEADS, PEER_KEYS, tt), lambda i, e: (0, 0, i))
    return pl.pallas_call(
        _peer_main_kernel,
        grid=(t // tt, n_exp // ne),
        in_specs=[pl.BlockSpec((tt, d), lambda i, e: (i, 0)),
                  pl.BlockSpec((1, d), lambda i, e: (0, 0)),
                  pl.BlockSpec((ne, d), lambda i, e: (e, 0)),
                  pl.BlockSpec((d, ne), lambda i, e: (0, e)),
                  tok_spec, tok_spec, tok_spec, tok_spec],
        out_specs=pl.BlockSpec((tt, d), lambda i, e: (i, 0)),
        out_shape=jax.ShapeDtypeStruct((t, d), F32),
        scratch_shapes=[pltpu.VMEM((d, tt), BF16), pltpu.VMEM((d, tt), F32)],
        compiler_params=_cparams(("parallel", "arbitrary")),
        name="peer_main",
    )(x2, g_norm.reshape(1, d).astype(F32), u.astype(BF16), v.T.astype(BF16), a1, e1, s2, e2)


def _sb_mixer(x2, g_norm, w_in, w_out, batch, seq):
    qkv = _norm_matmul(x2, g_norm, w_in, seq, tn=512)
    return _proj_resid(_sb_attention(qkv, batch, seq), w_out, x2)


def _moba_mixer(x2, g_norm, w_in, w_out, batch, seq):
    d_model = N_HEADS * HEAD_DIM
    qkv = _norm_matmul(x2, g_norm, w_in, seq, tn=512, n_rope=2 * d_model // 512)
    return _proj_resid(_moba_attention(qkv, batch, seq), w_out, x2)


def kernel(x, l0_norm_mix, l0_sb_w_in, l0_sb_w_out, l0_norm_ffn, l0_peer_w_q, l0_peer_sub_keys, l0_peer_u, l0_peer_v, l1_norm_mix, l1_nsa_w_in, l1_nsa_cmp_pos_k, l1_nsa_cmp_pos_v, l1_nsa_cmp_w_k, l1_nsa_cmp_w_v, l1_nsa_w_out, l1_norm_ffn, l1_peer_w_q, l1_peer_sub_keys, l1_peer_u, l1_peer_v, l2_norm_mix, l2_moba_w_in, l2_moba_w_out, l2_norm_ffn, l2_peer_w_q, l2_peer_sub_keys, l2_peer_u, l2_peer_v, l3_norm_mix, l3_sb_w_in, l3_sb_w_out, l3_norm_ffn, l3_peer_w_q, l3_peer_sub_keys, l3_peer_u, l3_peer_v, final_norm):
    batch, seq, d = x.shape
    x2 = x.reshape(batch * seq, d)
    x2 = _sb_mixer(x2, l0_norm_mix, l0_sb_w_in, l0_sb_w_out, batch, seq)
    x2 = _peer_mixer(x2, l0_norm_ffn, l0_peer_w_q, l0_peer_sub_keys, l0_peer_u, l0_peer_v, seq)
    x2 = _nsa_mixer(x2, l1_norm_mix, l1_nsa_w_in, l1_nsa_cmp_pos_k, l1_nsa_cmp_pos_v, l1_nsa_cmp_w_k,
                    l1_nsa_cmp_w_v, l1_nsa_w_out, batch, seq)
    x2 = _peer_mixer(x2, l1_norm_ffn, l1_peer_w_q, l1_peer_sub_keys, l1_peer_u, l1_peer_v, seq)
    x2 = _moba_mixer(x2, l2_norm_mix, l2_moba_w_in, l2_moba_w_out, batch, seq)
    x2 = _peer_mixer(x2, l2_norm_ffn, l2_peer_w_q, l2_peer_sub_keys, l2_peer_u, l2_peer_v, seq)
    x2 = _sb_mixer(x2, l3_norm_mix, l3_sb_w_in, l3_sb_w_out, batch, seq)
    x2 = _peer_mixer(x2, l3_norm_ffn, l3_peer_w_q, l3_peer_sub_keys, l3_peer_u, l3_peer_v, seq)
    return _rmsnorm(x2, final_norm).reshape(batch, seq, d)
```

```python
import functools

import numpy as np
import jax
import jax.numpy as jnp
from jax import lax
from jax.experimental import pallas as pl
from jax.experimental.pallas import tpu as pltpu

F32 = jnp.float32
BF16 = jnp.bfloat16

N_HEADS = 16
HEAD_DIM = 64
ROPE_DIM = 16
ROPE_HALF = ROPE_DIM // 2
ROPE_THETA = 500000.0
NORM_EPS = 1e-6
NEG = -1e30
LOW = -3e38
ATT_SCALE = HEAD_DIM ** -0.5

NSA_GROUPS = 4
NSA_HPG = N_HEADS // NSA_GROUPS
NSA_KV = NSA_GROUPS * HEAD_DIM
NSA_CMP_LEN = 32
NSA_CMP_STRIDE = 16
NSA_SEL_LEN = 64
NSA_TOP_N = 16
NSA_WINDOW = 512
NSA_BONUS = 1e3
NSA_SEL_PAD = 128

MOBA_BLOCK = 256
MOBA_TOP_K = 3
MOBA_BLK_PAD = 128

PEER_HEADS = 8
PEER_KEYS = 128
PEER_TOP_K = 16
PEER_HALF = 64

SB_EXIT = -110.0

LANE = 128
VMEM_LIMIT = 56 << 20


def _cparams(sem, vmem=VMEM_LIMIT):
    return pltpu.CompilerParams(dimension_semantics=sem, vmem_limit_bytes=vmem)


def _dot(a, b):
    return jnp.dot(a, b, preferred_element_type=F32)


def _dot_nt(a, b):
    return lax.dot_general(a, b, (((1,), (1,)), ((), ())), preferred_element_type=F32)


def _split_bf16(x):
    hi = x.astype(BF16)
    lo = (x - hi.astype(F32)).astype(BF16)
    return hi, lo


def _topk_mask(vals, k):
    lane = lax.broadcasted_iota(jnp.int32, vals.shape, 1)
    work = vals
    sel = jnp.zeros(vals.shape, F32)
    for _ in range(k):
        m = jnp.max(work, axis=-1, keepdims=True)
        first = jnp.min(jnp.where(work == m, lane, LANE), axis=-1, keepdims=True)
        hit = lane == first
        sel = jnp.where(hit, 1.0, sel)
        work = jnp.where(hit, LOW, work)
    return sel


def _norm_matmul_kernel(*refs, n_rope, tn):
    if n_rope:
        x_ref, g_ref, w_ref, c_ref, sa_ref, sb_ref, o_ref, xn_ref = refs
    else:
        x_ref, g_ref, w_ref, o_ref, xn_ref = refs
    j = pl.program_id(1)

    @pl.when(j == 0)
    def _():
        x = x_ref[...]
        y = x * lax.rsqrt(jnp.mean(x * x, axis=-1, keepdims=True) + NORM_EPS)
        xn_ref[...] = (y * g_ref[...]).astype(BF16)

    acc = _dot(xn_ref[...], w_ref[...])
    if n_rope:
        @pl.when(j < n_rope)
        def _():
            r = (acc * c_ref[...] + pltpu.roll(acc, ROPE_HALF, 1) * sb_ref[...]
                 + pltpu.roll(acc, tn - ROPE_HALF, 1) * sa_ref[...])
            o_ref[...] = r.astype(o_ref.dtype)

        @pl.when(j >= n_rope)
        def _():
            o_ref[...] = acc.astype(o_ref.dtype)
    else:
        o_ref[...] = acc.astype(o_ref.dtype)


def _rope_tables(pos, width):
    inv = ROPE_THETA ** (-jnp.arange(0, ROPE_DIM, 2, dtype=F32) / ROPE_DIM)
    ang = pos.astype(F32)[:, None] * inv[None, :]
    cos, sin = jnp.cos(ang), jnp.sin(ang)
    n = pos.shape[0]
    rest = HEAD_DIM - ROPE_DIM
    c = jnp.concatenate([cos, cos, jnp.ones((n, rest), F32)], axis=1)
    sa = jnp.concatenate([-sin, jnp.zeros((n, ROPE_HALF + rest), F32)], axis=1)
    sb = jnp.concatenate([jnp.zeros((n, ROPE_HALF), F32), sin, jnp.zeros((n, rest), F32)], axis=1)
    reps = width // HEAD_DIM
    return jnp.tile(c, (1, reps)), jnp.tile(sa, (1, reps)), jnp.tile(sb, (1, reps))


def _norm_matmul(x2, g, w, seq, *, tn, n_rope=0, out_dtype=BF16, tt=512):
    t, d = x2.shape
    n = w.shape[1]
    tt = min(tt, seq)
    grid = (t // tt, n // tn)
    in_specs = [pl.BlockSpec((tt, d), lambda i, j: (i, 0)),
                pl.BlockSpec((1, d), lambda i, j: (0, 0)),
                pl.BlockSpec((d, tn), lambda i, j: (0, j))]
    args = [x2, g.reshape(1, d).astype(F32), w.astype(BF16)]
    if n_rope:
        nper = seq // tt
        tabs = _rope_tables(jnp.arange(seq), tn)
        in_specs += [pl.BlockSpec((tt, tn), lambda i, j: (i % nper, 0))] * 3
        args += list(tabs)
    return pl.pallas_call(
        functools.partial(_norm_matmul_kernel, n_rope=n_rope, tn=tn),
        grid=grid, in_specs=in_specs,
        out_specs=pl.BlockSpec((tt, tn), lambda i, j: (i, j)),
        out_shape=jax.ShapeDtypeStruct((t, n), out_dtype),
        scratch_shapes=[pltpu.VMEM((tt, d), BF16)],
        compiler_params=_cparams(("parallel", "arbitrary")),
        name="norm_matmul",
    )(*args)


def _norm_matmul3_kernel(x_ref, g_ref, wh_ref, wl_ref, o_ref, xh_ref, xl_ref):
    @pl.when(pl.program_id(1) == 0)
    def _():
        x = x_ref[...]
        y = x * lax.rsqrt(jnp.mean(x * x, axis=-1, keepdims=True) + NORM_EPS) * g_ref[...]
        xh_ref[...], xl_ref[...] = _split_bf16(y)

    o_ref[...] = (_dot(xh_ref[...], wh_ref[...]) + _dot(xl_ref[...], wh_ref[...])
                  + _dot(xh_ref[...], wl_ref[...]))


def _norm_matmul3(x2, g, w, *, tn=512, tt=512):
    t, d = x2.shape
    n = w.shape[1]
    tt = min(tt, t)
    w_hi, w_lo = _split_bf16(w.astype(F32))
    w_spec = pl.BlockSpec((d, tn), lambda i, j: (0, j))
    return pl.pallas_call(
        _norm_matmul3_kernel,
        grid=(t // tt, n // tn),
        in_specs=[pl.BlockSpec((tt, d), lambda i, j: (i, 0)),
                  pl.BlockSpec((1, d), lambda i, j: (0, 0)), w_spec, w_spec],
        out_specs=pl.BlockSpec((tt, tn), lambda i, j: (i, j)),
        out_shape=jax.ShapeDtypeStruct((t, n), F32),
        scratch_shapes=[pltpu.VMEM((tt, d), BF16), pltpu.VMEM((tt, d), BF16)],
        compiler_params=_cparams(("parallel", "arbitrary")),
        name="norm_matmul3",
    )(x2, g.reshape(1, d).astype(F32), w_hi, w_lo)


def _proj_resid_kernel(a_ref, w_ref, r_ref, o_ref):
    o_ref[...] = r_ref[...] + _dot(a_ref[...], w_ref[...])


def _proj_resid(a, w, resid, *, tt=512, tn=512):
    t, d = a.shape
    n = w.shape[1]
    tt = min(tt, t)
    return pl.pallas_call(
        _proj_resid_kernel,
        grid=(t // tt, n // tn),
        in_specs=[pl.BlockSpec((tt, d), lambda i, j: (i, 0)),
                  pl.BlockSpec((d, tn), lambda i, j: (0, j)),
                  pl.BlockSpec((tt, tn), lambda i, j: (i, j))],
        out_specs=pl.BlockSpec((tt, tn), lambda i, j: (i, j)),
        out_shape=jax.ShapeDtypeStruct((t, n), F32),
        compiler_params=_cparams(("parallel", "parallel")),
        name="proj_resid",
    )(a, w.astype(BF16), resid)


def _rmsnorm_kernel(x_ref, g_ref, o_ref):
    x = x_ref[...]
    y = x * lax.rsqrt(jnp.mean(x * x, axis=-1, keepdims=True) + NORM_EPS)
    o_ref[...] = y * g_ref[...]


def _rmsnorm(x2, g, *, tt=512):
    t, d = x2.shape
    tt = min(tt, t)
    return pl.pallas_call(
        _rmsnorm_kernel,
        grid=(t // tt,),
        in_specs=[pl.BlockSpec((tt, d), lambda i: (i, 0)), pl.BlockSpec((1, d), lambda i: (0, 0))],
        out_specs=pl.BlockSpec((tt, d), lambda i: (i, 0)),
        out_shape=jax.ShapeDtypeStruct((t, d), F32),
        compiler_params=_cparams(("parallel",)),
        name="final_rmsnorm",
    )(x2, g.reshape(1, d).astype(F32))


SB_TILE = 256


def _sb_kernel(q_ref, k_ref, v_ref, tri_ref, o_ref):
    tile = SB_TILE
    qi = pl.program_id(2)
    tri = tri_ref[...]
    row = lax.broadcasted_iota(jnp.int32, (tile, tile), 0)
    col = lax.broadcasted_iota(jnp.int32, (tile, tile), 1)
    past = col < row

    for hh in range(2):
        lanes = slice(hh * HEAD_DIM, (hh + 1) * HEAD_DIM)
        q = q_ref[:, lanes] * jnp.asarray(ATT_SCALE, BF16)

        def step(start, r_sum, acc, diag):
            k = k_ref[pl.ds(start, tile), lanes]
            v = v_ref[pl.ds(start, tile), lanes]
            z = _dot_nt(q, k)
            sp = jnp.maximum(z, 0.0) + jnp.log1p(jnp.exp(-jnp.abs(z)))
            lb = jnp.where(past, -sp, 0.0) if diag else -sp
            hi, lo = _split_bf16(lb)
            suffix = _dot(hi, tri) + _dot(lo, tri)
            a = jnp.exp((z - sp) + suffix + r_sum)
            if diag:
                a = jnp.where(past, a, 0.0)
            acc = acc + _dot(a.astype(BF16), v)
            r_sum = r_sum + jnp.sum(lb, axis=-1, keepdims=True)
            return r_sum, acc

        r0 = jnp.zeros((tile, 1), F32)
        acc0 = jnp.zeros((tile, HEAD_DIM), F32)
        r1, acc1 = step(pl.multiple_of(qi * tile, tile), r0, acc0, True)

        def older(j, r_sum):
            return jnp.where(jnp.max(r_sum) > SB_EXIT, j - 1, -1)

        def cond(c):
            return c[0] >= 0

        def body(c):
            j, r_sum, acc = c
            r_sum, acc = step(pl.multiple_of(j * tile, tile), r_sum, acc, False)
            return older(j, r_sum), r_sum, acc

        _, _, acc = lax.while_loop(cond, body, (older(qi, r1), r1, acc1))
        o_ref[:, lanes] = acc.astype(o_ref.dtype)


def _sb_attention(qkv, batch, seq):
    tile = SB_TILE
    nq = seq // tile
    d_model = N_HEADS * HEAD_DIM
    ncol = d_model // LANE
    ii = np.arange(tile)
    tri = jnp.asarray(ii[:, None] > ii[None, :], dtype=BF16)
    return pl.pallas_call(
        _sb_kernel,
        grid=(batch, ncol, nq),
        in_specs=[pl.BlockSpec((tile, LANE), lambda b, h, i: (b * nq + i, h)),
                  pl.BlockSpec((seq, LANE), lambda b, h, i: (b, ncol + h)),
                  pl.BlockSpec((seq, LANE), lambda b, h, i: (b, 2 * ncol + h)),
                  pl.BlockSpec((tile, tile), lambda b, h, i: (0, 0))],
        out_specs=pl.BlockSpec((tile, LANE), lambda b, h, i: (b * nq + i, h)),
        out_shape=jax.ShapeDtypeStruct((batch * seq, d_model), BF16),
        compiler_params=_cparams(("parallel", "parallel", "arbitrary")),
        name="sb_attention",
    )(qkv, qkv, qkv, tri)


def _flash_first(q, k, v, mask):
    s = jnp.where(mask, _dot_nt(q, k), NEG)
    m = jnp.max(s, axis=-1, keepdims=True)
    p = jnp.exp(s - m)
    return m, jnp.sum(p, axis=-1, keepdims=True), _dot(p.astype(BF16), v)


def _flash_next(q, k, v, mask, state):
    m, l, acc = state
    s = jnp.where(mask, _dot_nt(q, k), NEG)
    m_new = jnp.maximum(m, jnp.max(s, axis=-1, keepdims=True))
    p = jnp.exp(s - m_new)
    alpha = jnp.exp(m - m_new)
    l = alpha * l + jnp.sum(p, axis=-1, keepdims=True)
    acc = alpha * acc + _dot(p.astype(BF16), v)
    return m_new, l, acc


def _flash_out(state):
    _, l, acc = state
    return acc / l


def _flash_t_first(q, k, vt, bias_t):
    s = _dot_nt(k, q) + bias_t
    m = jnp.max(s, axis=0, keepdims=True)
    p = jnp.exp(s - m)
    return m, jnp.sum(p, axis=0, keepdims=True), _dot(vt, p.astype(BF16))


def _flash_t_next(q, k, vt, bias_t, state):
    m, l, acc_t = state
    s = _dot_nt(k, q) + bias_t
    m_new = jnp.maximum(m, jnp.max(s, axis=0, keepdims=True))
    p = jnp.exp(s - m_new)
    alpha = jnp.exp(m - m_new)
    l = alpha * l + jnp.sum(p, axis=0, keepdims=True)
    acc_t = alpha * acc_t + _dot(vt, p.astype(BF16))
    return m_new, l, acc_t


def _topk_mask_rows(vals, k):
    rowi = lax.broadcasted_iota(jnp.int32, vals.shape, 0)
    work = vals
    sel = jnp.zeros(vals.shape, F32)
    for _ in range(k):
        m = jnp.max(work, axis=0, keepdims=True)
        first = jnp.min(jnp.where(work == m, rowi, vals.shape[0]), axis=0, keepdims=True)
        hit = rowi == first
        sel = jnp.where(hit, 1.0, sel)
        work = jnp.where(hit, LOW, work)
    return sel


MOBA_WALK = 4


def _moba_kernel(q_ref, k_ref, vt_ref, avg_ref, o_ref, kmean_ref, bias_ref):
    blk = MOBA_BLOCK
    wide = MOBA_WALK * blk
    qi = pl.program_id(2)

    @pl.when(qi == 0)
    def _():
        kmean_ref[...] = _dot(avg_ref[...], k_ref[...])

    key_i = lax.broadcasted_iota(jnp.int32, (blk, blk), 0)
    qry_i = lax.broadcasted_iota(jnp.int32, (blk, blk), 1)
    causal_bias = jnp.where(key_i <= qry_i, 0.0, NEG)
    blk_i = lax.broadcasted_iota(jnp.int32, (MOBA_BLK_PAD, blk), 0)
    start = pl.multiple_of(qi * blk, blk)
    head_rows = [slice(hh * HEAD_DIM, (hh + 1) * HEAD_DIM) for hh in range(2)]

    qs, states = [], []
    for hh, lanes in enumerate(head_rows):
        q_raw = q_ref[:, lanes]
        km_hi, km_lo = _split_bf16(kmean_ref[:, lanes])
        gate = _dot_nt(km_hi, q_raw) + _dot_nt(km_lo, q_raw)
        gate = jnp.where(blk_i < qi, gate, NEG)
        picked = (_topk_mask_rows(gate, MOBA_TOP_K) > 0.5) & (blk_i < qi)
        bias_ref[hh] = jnp.where(picked, 0.0, NEG)
        qs.append(q_raw * jnp.asarray(ATT_SCALE, BF16))
        states.append(_flash_t_first(qs[-1], k_ref[pl.ds(start, blk), lanes],
                                     vt_ref[lanes, pl.ds(start, blk)], causal_bias))

    def body(p, sts):
        s0 = pl.multiple_of(p * wide, wide)
        scores = [_dot_nt(k_ref[pl.ds(s0, wide), lanes], qs[hh]) for hh, lanes in enumerate(head_rows)]
        probs, nxt = [], []
        for hh in range(2):
            m, l, acc_t = sts[hh]
            bias = jnp.concatenate(
                [jnp.broadcast_to(bias_ref[hh, pl.ds(p * MOBA_WALK + w, 1), :], (blk, blk))
                 for w in range(MOBA_WALK)], axis=0)
            s = scores[hh] + bias
            m_new = jnp.maximum(m, jnp.max(s, axis=0, keepdims=True))
            pr = jnp.exp(s - m_new)
            alpha = jnp.exp(m - m_new)
            probs.append(pr.astype(BF16))
            nxt.append((m_new, alpha * l + jnp.sum(pr, axis=0, keepdims=True), alpha * acc_t))
        return tuple((m_new, l, acc_s + _dot(vt_ref[lanes, pl.ds(s0, wide)], probs[hh]))
                     for hh, (lanes, (m_new, l, acc_s)) in enumerate(zip(head_rows, nxt)))

    states = lax.fori_loop(0, (qi + MOBA_WALK - 1) // MOBA_WALK, body, tuple(states))
    for hh, lanes in enumerate(head_rows):
        _, l, acc_t = states[hh]
        o_ref[:, lanes] = (acc_t / l).T.astype(o_ref.dtype)


def _moba_attention(qkv, batch, seq):
    blk = MOBA_BLOCK
    wide = MOBA_WALK * blk
    nq = seq // blk
    d_model = N_HEADS * HEAD_DIM
    ncol = d_model // LANE
    avg = np.zeros((MOBA_BLK_PAD, seq), np.float32)
    for n in range(nq):
        avg[n, n * blk:(n + 1) * blk] = 1.0 / blk
    avg = jnp.asarray(avg, dtype=BF16)
    v_t = qkv[:, 2 * d_model:].reshape(batch, seq, d_model).transpose(0, 2, 1)
    return pl.pallas_call(
        _moba_kernel,
        grid=(batch, ncol, nq),
        in_specs=[pl.BlockSpec((blk, LANE), lambda b, h, i: (b * nq + i, h)),
                  pl.BlockSpec((seq, LANE), lambda b, h, i: (b, ncol + h)),
                  pl.BlockSpec((None, LANE, seq), lambda b, h, i: (b, h, 0)),
                  pl.BlockSpec((MOBA_BLK_PAD, seq), lambda b, h, i: (0, 0))],
        out_specs=pl.BlockSpec((blk, LANE), lambda b, h, i: (b * nq + i, h)),
        out_shape=jax.ShapeDtypeStruct((batch * seq, d_model), BF16),
        scratch_shapes=[pltpu.VMEM((MOBA_BLK_PAD, LANE), F32), pltpu.VMEM((2, MOBA_BLK_PAD, blk), F32)],
        compiler_params=_cparams(("parallel", "parallel", "arbitrary")),
        name="moba_attention",
    )(qkv, qkv, v_t, avg)


def _nsa_compress_kernel(kr_ref, vr_ref, pk_ref, pv_ref, wk_ref, wv_ref, c_ref, sa_ref, sb_ref, kc_ref, vc_ref):
    nc = kr_ref.shape[0]
    rowi = lax.broadcasted_iota(jnp.int32, (nc, 1), 0)

    def windows(x_ref, pos_ref, w_ref):
        x = x_ref[...].astype(F32)
        lo = _dot((x + pos_ref[0:1, :]).astype(BF16), w_ref[0])
        hi = _dot((x + pos_ref[1:2, :]).astype(BF16), w_ref[1])
        hi_next = jnp.where(rowi < nc - 1, pltpu.roll(hi, nc - 1, 0), 0.0)
        return lo + hi_next

    kk = windows(kr_ref, pk_ref, wk_ref)
    d = HEAD_DIM
    kc = kk[:, 0:d] * c_ref[...] + kk[:, d:2 * d] * sb_ref[...] + kk[:, 2 * d:3 * d] * sa_ref[...]
    kc_ref[...] = kc.astype(kc_ref.dtype)
    vv = windows(vr_ref, pv_ref, wv_ref)
    vc_ref[...] = vv[:, 0:d].astype(vc_ref.dtype)


def _nsa_compress(k_cmp, v_cmp, pos_k, pos_v, w_k, w_v, batch, seq):
    g, d, st = NSA_GROUPS, HEAD_DIM, NSA_CMP_STRIDE
    nc = seq // st

    def chunks(x):
        return x.reshape(batch, nc, st, g, d).transpose(0, 3, 1, 2, 4).reshape(batch, g, nc, st * d)

    def weights(w):
        w3 = w.reshape(NSA_CMP_LEN, d, d)
        cat = jnp.concatenate([w3, jnp.roll(w3, ROPE_HALF, axis=2), jnp.roll(w3, -ROPE_HALF, axis=2)], axis=2)
        return cat.reshape(2, st * d, 3 * d).astype(BF16)

    def positions(p):
        return p.reshape(2, st * d).astype(F32)

    cmp_end = jnp.arange(nc) * st + NSA_CMP_LEN - 1
    tabs = _rope_tables(cmp_end, d)
    blk4 = pl.BlockSpec((None, None, nc, st * d), lambda b, gi: (b, gi, 0, 0))
    out4 = pl.BlockSpec((None, None, nc, d), lambda b, gi: (b, gi, 0, 0))
    const2 = lambda shape: pl.BlockSpec(shape, lambda b, gi: (0,) * len(shape))
    return pl.pallas_call(
        _nsa_compress_kernel,
        grid=(batch, g),
        in_specs=[blk4, blk4, const2((2, st * d)), const2((2, st * d)),
                  const2((2, st * d, 3 * d)), const2((2, st * d, 3 * d)),
                  const2((nc, d)), const2((nc, d)), const2((nc, d))],
        out_specs=[out4, out4],
        out_shape=[jax.ShapeDtypeStruct((batch, g, nc, d), BF16)] * 2,
        compiler_params=_cparams(("parallel", "parallel")),
        name="nsa_compress",
    )(chunks(k_cmp), chunks(v_cmp), positions(pos_k), positions(pos_v), weights(w_k), weights(w_v), *tabs)


NSA_TQ = 128
NSA_TK = 512


def _nsa_kernel(q_ref, kc_ref, vct_ref, ks_ref, vst_ref, kw_ref, vwt_ref, gate_ref, c2st_ref, o_ref, bias_ref):
    tq, tk, d, nh = NSA_TQ, NSA_TK, HEAD_DIM, NSA_HPG
    nql = nh * tq
    half = nql // 2
    qi = pl.program_id(2)
    t0 = qi * tq
    nc = kc_ref.shape[0]

    def lanes4(x):
        return jnp.concatenate([x] * nh, axis=1)

    tpos1 = t0 + lax.broadcasted_iota(jnp.int32, (1, tq), 1)
    tpos = lanes4(tpos1)
    gates_t = jax.nn.sigmoid(gate_ref[...]).T

    def gate(branch):
        return jnp.concatenate([gates_t[3 * h + branch:3 * h + branch + 1, :] for h in range(nh)], axis=1)

    q = jnp.concatenate([q_ref[:, h * d:(h + 1) * d] for h in range(nh)], axis=0) * jnp.asarray(ATT_SCALE, BF16)

    kd = t0 // tk
    dstart = pl.multiple_of(kd * tk, tk)
    span = NSA_WINDOW + tq
    wstart = pl.multiple_of(jnp.maximum(t0 - NSA_WINDOW, 0), tq)
    s_cmp = _dot_nt(kc_ref[...], q)
    s_win = _dot_nt(kw_ref[pl.ds(wstart, span), :], q)
    s_diag = _dot_nt(ks_ref[pl.ds(dstart, tk), :], q)

    cmp_end = lax.broadcasted_iota(jnp.int32, (nc, nql), 0) * NSA_CMP_STRIDE + (NSA_CMP_LEN - 1)
    vis_c = cmp_end <= tpos
    s = jnp.where(vis_c, s_cmp, NEG)
    e = jnp.where(vis_c, jnp.exp(s - jnp.max(s, axis=0, keepdims=True)), 0.0)
    l = jnp.sum(e, axis=0, keepdims=True)
    p = e / jnp.where(l > 0.0, l, 1.0)
    out_t = gate(0) * _dot(vct_ref[...], p.astype(BF16))
    psum = functools.reduce(lambda a, b: a + b, [p[:, h * tq:(h + 1) * tq] for h in range(nh)])
    p_hi, p_lo = _split_bf16(psum)
    imp = _dot(c2st_ref[...], p_hi) + _dot(c2st_ref[...], p_lo)
    blk = lax.broadcasted_iota(jnp.int32, (NSA_SEL_PAD, tq), 0)
    own = tpos1 // NSA_SEL_LEN
    forced = (blk == 0) | (blk == own) | (blk == own - 1)
    imp = jnp.where(blk <= own, imp + jnp.where(forced, NSA_BONUS, 0.0), NEG)
    picked = _topk_mask_rows(imp, NSA_TOP_N) > 0.5
    bias_ref[...] = lanes4(jnp.where(picked, 0.0, NEG))

    per_tile = tk // NSA_SEL_LEN

    def tile_bias(kb):
        return jnp.concatenate(
            [jnp.broadcast_to(bias_ref[pl.ds(kb * per_tile + w, 1), :], (NSA_SEL_LEN, nql))
             for w in range(per_tile)], axis=0)

    wpos = wstart + lax.broadcasted_iota(jnp.int32, (span, nql), 0)
    s = jnp.where((wpos <= tpos) & (wpos > tpos - NSA_WINDOW), s_win, NEG)
    e = jnp.exp(s - jnp.max(s, axis=0, keepdims=True))
    out_t = out_t + (gate(2) / jnp.sum(e, axis=0, keepdims=True)) * _dot(vwt_ref[:, pl.ds(wstart, span)],
                                                                         e.astype(BF16))

    kpos = dstart + lax.broadcasted_iota(jnp.int32, (tk, nql), 0)
    s = jnp.where(kpos <= tpos, s_diag + tile_bias(kd), NEG)
    m = jnp.max(s, axis=0, keepdims=True)
    e = jnp.exp(s - m)
    l = jnp.sum(e, axis=0, keepdims=True)
    acc = _dot(vst_ref[:, pl.ds(dstart, tk)], e.astype(BF16))
    halves = [slice(i * half, (i + 1) * half) for i in range(2)]
    q_half = [q[hs] for hs in halves]
    states = tuple((m[:, hs], l[:, hs], acc[:, hs]) for hs in halves)

    def sel_body(kb, sts):
        start = pl.multiple_of(kb * tk, tk)
        k = ks_ref[pl.ds(start, tk), :]
        vt = vst_ref[:, pl.ds(start, tk)]
        bias = tile_bias(kb)
        scores = [_dot_nt(k, q_half[i]) for i in range(2)]
        probs, nxt = [], []
        for i, hs in enumerate(halves):
            m, l, acc = sts[i]
            s = scores[i] + bias[:, hs]
            m_new = jnp.maximum(m, jnp.max(s, axis=0, keepdims=True))
            pr = jnp.exp(s - m_new)
            alpha = jnp.exp(m - m_new)
            probs.append(pr.astype(BF16))
            nxt.append((m_new, alpha * l + jnp.sum(pr, axis=0, keepdims=True), alpha * acc))
        return tuple((m_new, l, acc + _dot(vt, probs[i])) for i, (m_new, l, acc) in enumerate(nxt))

    states = lax.fori_loop(0, kd, sel_body, states)
    sel_t = jnp.concatenate([st[2] / st[1] for st in states], axis=1)
    out_t = out_t + gate(1) * sel_t
    for h in range(nh):
        o_ref[:, h * d:(h + 1) * d] = out_t[:, h * tq:(h + 1) * tq].T.astype(o_ref.dtype)


def _nsa_attention(q_all, kc, vct, ks, vst, kw, vwt, gates, batch, seq):
    tq, tk, d, g = NSA_TQ, NSA_TK, HEAD_DIM, NSA_GROUPS
    nq = seq // tq
    nc = seq // NSA_CMP_STRIDE
    n_sel = seq // NSA_SEL_LEN
    d_model = N_HEADS * HEAD_DIM
    cs = np.arange(nc) * NSA_CMP_STRIDE
    ss = np.arange(n_sel) * NSA_SEL_LEN
    ov = np.clip(np.minimum(cs[:, None] + NSA_CMP_LEN, ss[None, :] + NSA_SEL_LEN)
                 - np.maximum(cs[:, None], ss[None, :]), 0, None) / NSA_CMP_LEN
    c2s = np.zeros((nc, NSA_SEL_PAD), np.float32)
    c2s[:, :n_sel] = ov
    c2s[nc - 1, :] = 0.0
    k_spec = lambda n: pl.BlockSpec((None, None, n, d), lambda b, gi, i: (b, gi, 0, 0))
    vt_spec = lambda n: pl.BlockSpec((None, None, d, n), lambda b, gi, i: (b, gi, 0, 0))
    return pl.pallas_call(
        _nsa_kernel,
        grid=(batch, g, nq),
        in_specs=[pl.BlockSpec((tq, NSA_HPG * d), lambda b, gi, i: (b * nq + i, gi)),
                  k_spec(nc), vt_spec(nc), k_spec(seq), vt_spec(seq), k_spec(seq), vt_spec(seq),
                  pl.BlockSpec((tq, LANE), lambda b, gi, i: (b * nq + i, gi)),
                  pl.BlockSpec((NSA_SEL_PAD, nc), lambda b, gi, i: (0, 0))],
        out_specs=pl.BlockSpec((tq, NSA_HPG * d), lambda b, gi, i: (b * nq + i, gi)),
        out_shape=jax.ShapeDtypeStruct((batch * seq, d_model), BF16),
        scratch_shapes=[pltpu.VMEM((NSA_SEL_PAD, NSA_HPG * tq), F32)],
        compiler_params=_cparams(("parallel", "parallel", "arbitrary")),
        name="nsa_attention",
    )(q_all, kc, vct, ks, vst, kw, vwt, gates, jnp.asarray(c2s.T, dtype=BF16))


def _nsa_mixer(x2, g_norm, w_in, pos_k, pos_v, w_k, w_v, w_out, batch, seq):
    d_model, kv = N_HEADS * HEAD_DIM, NSA_KV
    sec = lambda i: w_in[:, d_model + i * kv: d_model + (i + 1) * kv]
    w_main = jnp.concatenate([w_in[:, :d_model], sec(2), sec(4), sec(0), sec(1), sec(3), sec(5)], axis=1)
    proj = _norm_matmul(x2, g_norm, w_main, seq, tn=512, n_rope=3)
    w_gate = w_in[:, d_model + 6 * kv:].reshape(d_model, NSA_GROUPS, 3 * NSA_HPG)
    w_gate = jnp.pad(w_gate, ((0, 0), (0, 0), (0, LANE - 3 * NSA_HPG))).reshape(d_model, NSA_GROUPS * LANE)
    gates = _norm_matmul(x2, g_norm, w_gate, seq, tn=NSA_GROUPS * LANE, out_dtype=F32)

    col = lambda i: proj[:, d_model + i * kv: d_model + (i + 1) * kv]
    heads = lambda x: x.reshape(batch, seq, NSA_GROUPS, HEAD_DIM).transpose(0, 2, 1, 3)
    heads_t = lambda x: x.reshape(batch, seq, NSA_GROUPS, HEAD_DIM).transpose(0, 2, 3, 1)
    kc, vc = _nsa_compress(col(2), col(3), pos_k, pos_v, w_k, w_v, batch, seq)
    o = _nsa_attention(proj, kc, vc.transpose(0, 1, 3, 2), heads(col(0)), heads_t(col(4)), heads(col(1)),
                       heads_t(col(5)), gates, batch, seq)
    return _proj_resid(o, w_out, x2)


PEER_SEL_TT = 256
PEER_PAIRS = [(i, j) for i in range(PEER_TOP_K + 1) for j in range(PEER_TOP_K + 1)
              if (i + 1) * (j + 1) <= PEER_TOP_K + 1]


def _peer_select_kernel(q_ref, k1_ref, k2_ref, cnt_ref, e1_ref, rank_ref, e2_ref):
    q = q_ref[...]
    q_hi, q_lo = _split_bf16(q)
    nh, nr = PEER_HEADS, PEER_TOP_K + 1

    def scores(k_ref, off):
        k_hi, k_lo = _split_bf16(k_ref[...])
        out = []
        for h in range(nh):
            lanes = slice(off + h * PEER_HALF, off + (h + 1) * PEER_HALF)
            out.append(_dot_nt(k_hi, q_hi[:, lanes]) + _dot_nt(k_hi, q_lo[:, lanes])
                       + _dot_nt(k_lo, q_hi[:, lanes]))
        return out

    def top_values(s):
        vals, work = [], s
        for _ in range(nr):
            m = jnp.max(work, axis=0, keepdims=True)
            vals.append(m)
            work = jnp.where(work == m, LOW, work)
        return vals

    s1 = scores(k1_ref, 0)
    s2 = scores(k2_ref, nh * PEER_HALF)
    t1 = [top_values(s) for s in s1]
    t2 = [top_values(s) for s in s2]
    r1 = [jnp.concatenate([t1[h][r] for h in range(nh)], axis=0) for r in range(nr)]
    r2 = [jnp.concatenate([t2[h][r] for h in range(nh)], axis=0) for r in range(nr)]
    cands = [r1[i] + r2[j] for (i, j) in PEER_PAIRS]
    work, tops = list(cands), []
    for _ in range(nr):
        m = functools.reduce(jnp.maximum, work)
        tops.append(m)
        work = [jnp.where(c == m, LOW, c) for c in work]
    thr = 0.5 * (tops[PEER_TOP_K - 1] + tops[PEER_TOP_K])
    cmax = tops[0]
    z = functools.reduce(lambda a, b: a + b, [jnp.where(c >= thr, jnp.exp(c - cmax), 0.0) for c in cands])
    inv_z = 1.0 / z
    for h in range(nh):
        thr_h = thr[h:h + 1, :]
        cnt = jnp.zeros_like(s1[h])
        rank = jnp.zeros_like(s2[h])
        for r in range(nr):
            cnt = jnp.where(s1[h] + t2[h][r] >= thr_h, float(r + 1), cnt)
            rank = jnp.where(t2[h][r] > s2[h], float(r + 1), rank)
        cnt_ref[h] = cnt
        e1_ref[h] = jnp.exp(s1[h] - r1[0][h:h + 1, :])
        rank_ref[h] = rank.astype(BF16)
        e2_ref[h] = (jnp.exp(s2[h] - r2[0][h:h + 1, :]) * inv_z[h:h + 1, :]).astype(BF16)


def _peer_select(q, sub_keys):
    t = q.shape[0]
    tt = min(PEER_SEL_TT, t)
    out_spec = pl.BlockSpec((PEER_HEADS, PEER_KEYS, tt), lambda i: (0, 0, i))
    return pl.pallas_call(
        _peer_select_kernel,
        grid=(t // tt,),
        in_specs=[pl.BlockSpec((tt, q.shape[1]), lambda i: (i, 0)),
                  pl.BlockSpec((PEER_KEYS, PEER_HALF), lambda i: (0, 0)),
                  pl.BlockSpec((PEER_KEYS, PEER_HALF), lambda i: (0, 0))],
        out_specs=[out_spec] * 4,
        out_shape=[jax.ShapeDtypeStruct((PEER_HEADS, PEER_KEYS, t), dt) for dt in (F32, F32, BF16, BF16)],
        compiler_params=_cparams(("parallel",)),
        name="peer_select",
    )(q, sub_keys[0].astype(F32), sub_keys[1].astype(F32))


PEER_TT = 512
PEER_NE = 1024


def _gelu(x):
    return 0.5 * x * (1.0 + lax.erf(x * np.float32(np.sqrt(0.5))))


def _peer_main_kernel(x_ref, g_ref, u_ref, vt_ref, cnt_ref, e1_ref, rank_ref, e2_ref, o_ref, xnt_ref, acc_ref):
    e = pl.program_id(1)
    ne = u_ref.shape[0]
    tt = x_ref.shape[0]

    @pl.when(e == 0)
    def _():
        x = x_ref[...]
        y = x * lax.rsqrt(jnp.mean(x * x, axis=-1, keepdims=True) + NORM_EPS)
        xnt_ref[...] = (y * g_ref[...]).T.astype(BF16)
        acc_ref[...] = jnp.zeros_like(acc_ref)

    act = _gelu(_dot(u_ref[...], xnt_ref[...])).astype(BF16)
    parts = []
    for c in range(ne // PEER_KEYS):
        i1 = e * (ne // PEER_KEYS) + c
        gsum = jnp.zeros((PEER_KEYS, tt), BF16)
        for h in range(PEER_HEADS):
            cnt = cnt_ref[h, pl.ds(i1, 1), :].astype(BF16)
            e1 = e1_ref[h, pl.ds(i1, 1), :].astype(BF16)
            gsum = gsum + jnp.where(rank_ref[h] < cnt, e2_ref[h], jnp.zeros((), BF16)) * e1
        parts.append(gsum * act[c * PEER_KEYS:(c + 1) * PEER_KEYS, :])
    w = jnp.concatenate(parts, axis=0)
    acc_ref[...] += _dot(vt_ref[...], w)

    @pl.when(e == pl.num_programs(1) - 1)
    def _():
        o_ref[...] = x_ref[...] + acc_ref[...].T


def _peer_mixer(x2, g_norm, w_q, sub_keys, u, v, seq):
    t, d = x2.shape
    w_q2 = w_q.reshape(d, PEER_HEADS, 2, PEER_HALF).transpose(0, 2, 1, 3).reshape(d, 2 * PEER_HEADS * PEER_HALF)
    q = _norm_matmul3(x2, g_norm, w_q2)
    a1, e1, s2, e2 = _peer_select(q, sub_keys)
    tt = min(PEER_TT, t)
    ne = PEER_NE
    n_exp = u.shape[0]
    tok_spec = pl.BlockSpec((PEER_HEADS, PEER_KEYS, tt), lambda i, e: (0, 0, i))
    return pl.pallas_call(
        _peer_main_kernel,
        grid=(t // tt, n_exp // ne),
        in_specs=[pl.BlockSpec((tt, d), lambda i, e: (i, 0)),
                  pl.BlockSpec((1, d), lambda i, e: (0, 0)),
                  pl.BlockSpec((ne, d), lambda i, e: (e, 0)),
                  pl.BlockSpec((d, ne), lambda i, e: (0, e)),
                  tok_spec, tok_spec, tok_spec, tok_spec],
        out_specs=pl.BlockSpec((tt, d), lambda i, e: (i, 0)),
        out_shape=jax.ShapeDtypeStruct((t, d), F32),
        scratch_shapes=[pltpu.VMEM((d, tt), BF16), pltpu.VMEM((d, tt), F32)],
        compiler_params=_cparams(("parallel", "arbitrary")),
        name="peer_main",
    )(x2, g_norm.reshape(1, d).astype(F32), u.astype(BF16), v.T.astype(BF16), a1, e1, s2, e2)


def _sb_mixer(x2, g_norm, w_in, w_out, batch, seq):
    qkv = _norm_matmul(x2, g_norm, w_in, seq, tn=512)
    return _proj_resid(_sb_attention(qkv, batch, seq), w_out, x2)


def _moba_mixer(x2, g_norm, w_in, w_out, batch, seq):
    d_model = N_HEADS * HEAD_DIM
    qkv = _norm_matmul(x2, g_norm, w_in, seq, tn=512, n_rope=2 * d_model // 512)
    return _proj_resid(_moba_attention(qkv, batch, seq), w_out, x2)


def kernel(x, l0_norm_mix, l0_sb_w_in, l0_sb_w_out, l0_norm_ffn, l0_peer_w_q, l0_peer_sub_keys, l0_peer_u, l0_peer_v, l1_norm_mix, l1_nsa_w_in, l1_nsa_cmp_pos_k, l1_nsa_cmp_pos_v, l1_nsa_cmp_w_k, l1_nsa_cmp_w_v, l1_nsa_w_out, l1_norm_ffn, l1_peer_w_q, l1_peer_sub_keys, l1_peer_u, l1_peer_v, l2_norm_mix, l2_moba_w_in, l2_moba_w_out, l2_norm_ffn, l2_peer_w_q, l2_peer_sub_keys, l2_peer_u, l2_peer_v, l3_norm_mix, l3_sb_w_in, l3_sb_w_out, l3_norm_ffn, l3_peer_w_q, l3_peer_sub_keys, l3_peer_u, l3_peer_v, final_norm):
    batch, seq, d = x.shape
    x2 = x.reshape(batch * seq, d)
    x2 = _sb_mixer(x2, l0_norm_mix, l0_sb_w_in, l0_sb_w_out, batch, seq)
    x2 = _peer_mixer(x2, l0_norm_ffn, l0_peer_w_q, l0_peer_sub_keys, l0_peer_u, l0_peer_v, seq)
    x2 = _nsa_mixer(x2, l1_norm_mix, l1_nsa_w_in, l1_nsa_cmp_pos_k, l1_nsa_cmp_pos_v, l1_nsa_cmp_w_k,
                    l1_nsa_cmp_w_v, l1_nsa_w_out, batch, seq)
    x2 = _peer_mixer(x2, l1_norm_ffn, l1_peer_w_q, l1_peer_sub_keys, l1_peer_u, l1_peer_v, seq)
    x2 = _moba_mixer(x2, l2_norm_mix, l2_moba_w_in, l2_moba_w_out, batch, seq)
    x2 = _peer_mixer(x2, l2_norm_ffn, l2_peer_w_q, l2_peer_sub_keys, l2_peer_u, l2_peer_v, seq)
    x2 = _sb_mixer(x2, l3_norm_mix, l3_sb_w_in, l3_sb_w_out, batch, seq)
    x2 = _peer_mixer(x2, l3_norm_ffn, l3_peer_w_q, l3_peer_sub_keys, l3_peer_u, l3_peer_v, seq)
    return _rmsnorm(x2, final_norm).reshape(batch, seq, d)
```

```python
import functools

import numpy as np
import jax
import jax.numpy as jnp
from jax import lax
from jax.experimental import pallas as pl
from jax.experimental.pallas import tpu as pltpu

F32 = jnp.float32
BF16 = jnp.bfloat16

N_HEADS = 16
HEAD_DIM = 64
ROPE_DIM = 16
ROPE_HALF = ROPE_DIM // 2
ROPE_THETA = 500000.0
NORM_EPS = 1e-6
NEG = -1e30
LOW = -3e38
ATT_SCALE = HEAD_DIM ** -0.5

NSA_GROUPS = 4
NSA_HPG = N_HEADS // NSA_GROUPS
NSA_KV = NSA_GROUPS * HEAD_DIM
NSA_CMP_LEN = 32
NSA_CMP_STRIDE = 16
NSA_SEL_LEN = 64
NSA_TOP_N = 16
NSA_WINDOW = 512
NSA_BONUS = 1e3
NSA_SEL_PAD = 128

MOBA_BLOCK = 256
MOBA_TOP_K = 3
MOBA_BLK_PAD = 128

PEER_HEADS = 8
PEER_KEYS = 128
PEER_TOP_K = 16
PEER_HALF = 64

SB_EXIT = -110.0

LANE = 128
VMEM_LIMIT = 56 << 20


def _cparams(sem, vmem=VMEM_LIMIT):
    return pltpu.CompilerParams(dimension_semantics=sem, vmem_limit_bytes=vmem)


def _dot(a, b):
    return jnp.dot(a, b, preferred_element_type=F32)


def _dot_nt(a, b):
    return lax.dot_general(a, b, (((1,), (1,)), ((), ())), preferred_element_type=F32)


def _split_bf16(x):
    hi = x.astype(BF16)
    lo = (x - hi.astype(F32)).astype(BF16)
    return hi, lo


def _topk_mask(vals, k):
    lane = lax.broadcasted_iota(jnp.int32, vals.shape, 1)
    work = vals
    sel = jnp.zeros(vals.shape, F32)
    for _ in range(k):
        m = jnp.max(work, axis=-1, keepdims=True)
        first = jnp.min(jnp.where(work == m, lane, LANE), axis=-1, keepdims=True)
        hit = lane == first
        sel = jnp.where(hit, 1.0, sel)
        work = jnp.where(hit, LOW, work)
    return sel


def _norm_matmul_kernel(*refs, n_rope, tn):
    if n_rope:
        x_ref, g_ref, w_ref, c_ref, sa_ref, sb_ref, o_ref, xn_ref = refs
    else:
        x_ref, g_ref, w_ref, o_ref, xn_ref = refs
    j = pl.program_id(1)

    @pl.when(j == 0)
    def _():
        x = x_ref[...]
        y = x * lax.rsqrt(jnp.mean(x * x, axis=-1, keepdims=True) + NORM_EPS)
        xn_ref[...] = (y * g_ref[...]).astype(BF16)

    acc = _dot(xn_ref[...], w_ref[...])
    if n_rope:
        @pl.when(j < n_rope)
        def _():
            r = (acc * c_ref[...] + pltpu.roll(acc, ROPE_HALF, 1) * sb_ref[...]
                 + pltpu.roll(acc, tn - ROPE_HALF, 1) * sa_ref[...])
            o_ref[...] = r.astype(o_ref.dtype)

        @pl.when(j >= n_rope)
        def _():
            o_ref[...] = acc.astype(o_ref.dtype)
    else:
        o_ref[...] = acc.astype(o_ref.dtype)


def _rope_tables(pos, width):
    inv = ROPE_THETA ** (-jnp.arange(0, ROPE_DIM, 2, dtype=F32) / ROPE_DIM)
    ang = pos.astype(F32)[:, None] * inv[None, :]
    cos, sin = jnp.cos(ang), jnp.sin(ang)
    n = pos.shape[0]
    rest = HEAD_DIM - ROPE_DIM
    c = jnp.concatenate([cos, cos, jnp.ones((n, rest), F32)], axis=1)
    sa = jnp.concatenate([-sin, jnp.zeros((n, ROPE_HALF + rest), F32)], axis=1)
    sb = jnp.concatenate([jnp.zeros((n, ROPE_HALF), F32), sin, jnp.zeros((n, rest), F32)], axis=1)
    reps = width // HEAD_DIM
    return jnp.tile(c, (1, reps)), jnp.tile(sa, (1, reps)), jnp.tile(sb, (1, reps))


def _norm_matmul(x2, g, w, seq, *, tn, n_rope=0, out_dtype=BF16, tt=512):
    t, d = x2.shape
    n = w.shape[1]
    tt = min(tt, seq)
    grid = (t // tt, n // tn)
    in_specs = [pl.BlockSpec((tt, d), lambda i, j: (i, 0)),
                pl.BlockSpec((1, d), lambda i, j: (0, 0)),
                pl.BlockSpec((d, tn), lambda i, j: (0, j))]
    args = [x2, g.reshape(1, d).astype(F32), w.astype(BF16)]
    if n_rope:
        nper = seq // tt
        tabs = _rope_tables(jnp.arange(seq), tn)
        in_specs += [pl.BlockSpec((tt, tn), lambda i, j: (i % nper, 0))] * 3
        args += list(tabs)
    return pl.pallas_call(
        functools.partial(_norm_matmul_kernel, n_rope=n_rope, tn=tn),
        grid=grid, in_specs=in_specs,
        out_specs=pl.BlockSpec((tt, tn), lambda i, j: (i, j)),
        out_shape=jax.ShapeDtypeStruct((t, n), out_dtype),
        scratch_shapes=[pltpu.VMEM((tt, d), BF16)],
        compiler_params=_cparams(("parallel", "arbitrary")),
        name="norm_matmul",
    )(*args)


def _norm_matmul3_kernel(x_ref, g_ref, wh_ref, wl_ref, o_ref, xh_ref, xl_ref):
    @pl.when(pl.program_id(1) == 0)
    def _():
        x = x_ref[...]
        y = x * lax.rsqrt(jnp.mean(x * x, axis=-1, keepdims=True) + NORM_EPS) * g_ref[...]
        xh_ref[...], xl_ref[...] = _split_bf16(y)

    o_ref[...] = (_dot(xh_ref[...], wh_ref[...]) + _dot(xl_ref[...], wh_ref[...])
                  + _dot(xh_ref[...], wl_ref[...]))


def _norm_matmul3(x2, g, w, *, tn=512, tt=512):
    t, d = x2.shape
    n = w.shape[1]
    tt = min(tt, t)
    w_hi, w_lo = _split_bf16(w.astype(F32))
    w_spec = pl.BlockSpec((d, tn), lambda i, j: (0, j))
    return pl.pallas_call(
        _norm_matmul3_kernel,
        grid=(t // tt, n // tn),
        in_specs=[pl.BlockSpec((tt, d), lambda i, j: (i, 0)),
                  pl.BlockSpec((1, d), lambda i, j: (0, 0)), w_spec, w_spec],
        out_specs=pl.BlockSpec((tt, tn), lambda i, j: (i, j)),
        out_shape=jax.ShapeDtypeStruct((t, n), F32),
        scratch_shapes=[pltpu.VMEM((tt, d), BF16), pltpu.VMEM((tt, d), BF16)],
        compiler_params=_cparams(("parallel", "arbitrary")),
        name="norm_matmul3",
    )(x2, g.reshape(1, d).astype(F32), w_hi, w_lo)


def _proj_resid_kernel(a_ref, w_ref, r_ref, o_ref):
    o_ref[...] = r_ref[...] + _dot(a_ref[...], w_ref[...])


def _proj_resid(a, w, resid, *, tt=512, tn=512):
    t, d = a.shape
    n = w.shape[1]
    tt = min(tt, t)
    return pl.pallas_call(
        _proj_resid_kernel,
        grid=(t // tt, n // tn),
        in_specs=[pl.BlockSpec((tt, d), lambda i, j: (i, 0)),
                  pl.BlockSpec((d, tn), lambda i, j: (0, j)),
                  pl.BlockSpec((tt, tn), lambda i, j: (i, j))],
        out_specs=pl.BlockSpec((tt, tn), lambda i, j: (i, j)),
        out_shape=jax.ShapeDtypeStruct((t, n), F32),
        compiler_params=_cparams(("parallel", "parallel")),
        name="proj_resid",
    )(a, w.astype(BF16), resid)


def _rmsnorm_kernel(x_ref, g_ref, o_ref):
    x = x_ref[...]
    y = x * lax.rsqrt(jnp.mean(x * x, axis=-1, keepdims=True) + NORM_EPS)
    o_ref[...] = y * g_ref[...]


def _rmsnorm(x2, g, *, tt=512):
    t, d = x2.shape
    tt = min(tt, t)
    return pl.pallas_call(
        _rmsnorm_kernel,
        grid=(t // tt,),
        in_specs=[pl.BlockSpec((tt, d), lambda i: (i, 0)), pl.BlockSpec((1, d), lambda i: (0, 0))],
        out_specs=pl.BlockSpec((tt, d), lambda i: (i, 0)),
        out_shape=jax.ShapeDtypeStruct((t, d), F32),
        compiler_params=_cparams(("parallel",)),
        name="final_rmsnorm",
    )(x2, g.reshape(1, d).astype(F32))


SB_TILE = 256


def _sb_kernel(q_ref, k_ref, v_ref, tri_ref, o_ref):
    tile = SB_TILE
    qi = pl.program_id(2)
    tri = tri_ref[...]
    row = lax.broadcasted_iota(jnp.int32, (tile, tile), 0)
    col = lax.broadcasted_iota(jnp.int32, (tile, tile), 1)
    past = col < row

    head_lanes = [slice(hh * HEAD_DIM, (hh + 1) * HEAD_DIM) for hh in range(2)]
    qs = [q_ref[:, lanes] * jnp.asarray(ATT_SCALE, BF16) for lanes in head_lanes]

    def step(start, r_sums, accs, diag):
        zs = [_dot_nt(qs[hh], k_ref[pl.ds(start, tile), lanes]) for hh, lanes in enumerate(head_lanes)]
        sps, lbs, splits = [], [], []
        for z in zs:
            sp = jnp.maximum(z, 0.0) + jnp.log1p(jnp.exp(-jnp.abs(z)))
            lb = jnp.where(past, -sp, 0.0) if diag else -sp
            sps.append(sp)
            lbs.append(lb)
            splits.append(_split_bf16(lb))
        suffixes = [_dot(hi, tri) + _dot(lo, tri) for hi, lo in splits]
        weights = []
        for hh in range(2):
            a = jnp.exp((zs[hh] - sps[hh]) + suffixes[hh] + r_sums[hh])
            if diag:
                a = jnp.where(past, a, 0.0)
            weights.append(a.astype(BF16))
        accs = tuple(accs[hh] + _dot(weights[hh], v_ref[pl.ds(start, tile), lanes])
                     for hh, lanes in enumerate(head_lanes))
        r_sums = tuple(r_sums[hh] + jnp.sum(lbs[hh], axis=-1, keepdims=True) for hh in range(2))
        return r_sums, accs

    zero_r = tuple(jnp.zeros((tile, 1), F32) for _ in range(2))
    zero_acc = tuple(jnp.zeros((tile, HEAD_DIM), F32) for _ in range(2))
    r1, acc1 = step(pl.multiple_of(qi * tile, tile), zero_r, zero_acc, True)

    def older(j, r_sums):
        alive = jnp.maximum(jnp.max(r_sums[0]), jnp.max(r_sums[1])) > SB_EXIT
        return jnp.where(alive, j - 1, -1)

    def cond(c):
        return c[0] >= 0

    def body(c):
        j, r_sums, accs = c
        r_sums, accs = step(pl.multiple_of(j * tile, tile), r_sums, accs, False)
        return older(j, r_sums), r_sums, accs

    _, _, accs = lax.while_loop(cond, body, (older(qi, r1), r1, acc1))
    for hh, lanes in enumerate(head_lanes):
        o_ref[:, lanes] = accs[hh].astype(o_ref.dtype)


def _sb_attention(qkv, batch, seq):
    tile = SB_TILE
    nq = seq // tile
    d_model = N_HEADS * HEAD_DIM
    ncol = d_model // LANE
    ii = np.arange(tile)
    tri = jnp.asarray(ii[:, None] > ii[None, :], dtype=BF16)
    return pl.pallas_call(
        _sb_kernel,
        grid=(batch, ncol, nq),
        in_specs=[pl.BlockSpec((tile, LANE), lambda b, h, i: (b * nq + i, h)),
                  pl.BlockSpec((seq, LANE), lambda b, h, i: (b, ncol + h)),
                  pl.BlockSpec((seq, LANE), lambda b, h, i: (b, 2 * ncol + h)),
                  pl.BlockSpec((tile, tile), lambda b, h, i: (0, 0))],
        out_specs=pl.BlockSpec((tile, LANE), lambda b, h, i: (b * nq + i, h)),
        out_shape=jax.ShapeDtypeStruct((batch * seq, d_model), BF16),
        compiler_params=_cparams(("parallel", "parallel", "arbitrary")),
        name="sb_attention",
    )(qkv, qkv, qkv, tri)


def _flash_first(q, k, v, mask):
    s = jnp.where(mask, _dot_nt(q, k), NEG)
    m = jnp.max(s, axis=-1, keepdims=True)
    p = jnp.exp(s - m)
    return m, jnp.sum(p, axis=-1, keepdims=True), _dot(p.astype(BF16), v)


def _flash_next(q, k, v, mask, state):
    m, l, acc = state
    s = jnp.where(mask, _dot_nt(q, k), NEG)
    m_new = jnp.maximum(m, jnp.max(s, axis=-1, keepdims=True))
    p = jnp.exp(s - m_new)
    alpha = jnp.exp(m - m_new)
    l = alpha * l + jnp.sum(p, axis=-1, keepdims=True)
    acc = alpha * acc + _dot(p.astype(BF16), v)
    return m_new, l, acc


def _flash_out(state):
    _, l, acc = state
    return acc / l


def _flash_t_first(q, k, vt, bias_t):
    s = _dot_nt(k, q) + bias_t
    m = jnp.max(s, axis=0, keepdims=True)
    p = jnp.exp(s - m)
    return m, jnp.sum(p, axis=0, keepdims=True), _dot(vt, p.astype(BF16))


def _flash_t_next(q, k, vt, bias_t, state):
    m, l, acc_t = state
    s = _dot_nt(k, q) + bias_t
    m_new = jnp.maximum(m, jnp.max(s, axis=0, keepdims=True))
    p = jnp.exp(s - m_new)
    alpha = jnp.exp(m - m_new)
    l = alpha * l + jnp.sum(p, axis=0, keepdims=True)
    acc_t = alpha * acc_t + _dot(vt, p.astype(BF16))
    return m_new, l, acc_t


def _topk_mask_rows(vals, k):
    rowi = lax.broadcasted_iota(jnp.int32, vals.shape, 0)
    work = vals
    sel = jnp.zeros(vals.shape, F32)
    for _ in range(k):
        m = jnp.max(work, axis=0, keepdims=True)
        first = jnp.min(jnp.where(work == m, rowi, vals.shape[0]), axis=0, keepdims=True)
        hit = rowi == first
        sel = jnp.where(hit, 1.0, sel)
        work = jnp.where(hit, LOW, work)
    return sel


MOBA_WALK = 4


def _moba_kernel(q_ref, k_ref, vt_ref, avg_ref, o_ref, kmean_ref, bias_ref):
    blk = MOBA_BLOCK
    wide = MOBA_WALK * blk
    qi = pl.program_id(2)

    @pl.when(qi == 0)
    def _():
        kmean_ref[...] = _dot(avg_ref[...], k_ref[...])

    key_i = lax.broadcasted_iota(jnp.int32, (blk, blk), 0)
    qry_i = lax.broadcasted_iota(jnp.int32, (blk, blk), 1)
    causal_bias = jnp.where(key_i <= qry_i, 0.0, NEG)
    blk_i = lax.broadcasted_iota(jnp.int32, (MOBA_BLK_PAD, blk), 0)
    start = pl.multiple_of(qi * blk, blk)
    head_rows = [slice(hh * HEAD_DIM, (hh + 1) * HEAD_DIM) for hh in range(2)]

    qs, states = [], []
    for hh, lanes in enumerate(head_rows):
        q_raw = q_ref[:, lanes]
        km_hi, km_lo = _split_bf16(kmean_ref[:, lanes])
        gate = _dot_nt(km_hi, q_raw) + _dot_nt(km_lo, q_raw)
        gate = jnp.where(blk_i < qi, gate, NEG)
        picked = (_topk_mask_rows(gate, MOBA_TOP_K) > 0.5) & (blk_i < qi)
        bias_ref[hh] = jnp.where(picked, 0.0, NEG)
        qs.append(q_raw * jnp.asarray(ATT_SCALE, BF16))
        states.append(_flash_t_first(qs[-1], k_ref[pl.ds(start, blk), lanes],
                                     vt_ref[lanes, pl.ds(start, blk)], causal_bias))

    def body(p, sts):
        s0 = pl.multiple_of(p * wide, wide)
        scores = [_dot_nt(k_ref[pl.ds(s0, wide), lanes], qs[hh]) for hh, lanes in enumerate(head_rows)]
        probs, nxt = [], []
        for hh in range(2):
            m, l, acc_t = sts[hh]
            bias = jnp.concatenate(
                [jnp.broadcast_to(bias_ref[hh, pl.ds(p * MOBA_WALK + w, 1), :], (blk, blk))
                 for w in range(MOBA_WALK)], axis=0)
            s = scores[hh] + bias
            m_new = jnp.maximum(m, jnp.max(s, axis=0, keepdims=True))
            pr = jnp.exp(s - m_new)
            alpha = jnp.exp(m - m_new)
            probs.append(pr.astype(BF16))
            nxt.append((m_new, alpha * l + jnp.sum(pr, axis=0, keepdims=True), alpha * acc_t))
        return tuple((m_new, l, acc_s + _dot(vt_ref[lanes, pl.ds(s0, wide)], probs[hh]))
                     for hh, (lanes, (m_new, l, acc_s)) in enumerate(zip(head_rows, nxt)))

    states = lax.fori_loop(0, (qi + MOBA_WALK - 1) // MOBA_WALK, body, tuple(states))
    for hh, lanes in enumerate(head_rows):
        _, l, acc_t = states[hh]
        o_ref[:, lanes] = (acc_t / l).T.astype(o_ref.dtype)


def _moba_attention(qkv, batch, seq):
    blk = MOBA_BLOCK
    wide = MOBA_WALK * blk
    nq = seq // blk
    d_model = N_HEADS * HEAD_DIM
    ncol = d_model // LANE
    avg = np.zeros((MOBA_BLK_PAD, seq), np.float32)
    for n in range(nq):
        avg[n, n * blk:(n + 1) * blk] = 1.0 / blk
    avg = jnp.asarray(avg, dtype=BF16)
    v_t = qkv[:, 2 * d_model:].reshape(batch, seq, d_model).transpose(0, 2, 1)
    return pl.pallas_call(
        _moba_kernel,
        grid=(batch, ncol, nq),
        in_specs=[pl.BlockSpec((blk, LANE), lambda b, h, i: (b * nq + i, h)),
                  pl.BlockSpec((seq, LANE), lambda b, h, i: (b, ncol + h)),
                  pl.BlockSpec((None, LANE, seq), lambda b, h, i: (b, h, 0)),
                  pl.BlockSpec((MOBA_BLK_PAD, seq), lambda b, h, i: (0, 0))],
        out_specs=pl.BlockSpec((blk, LANE), lambda b, h, i: (b * nq + i, h)),
        out_shape=jax.ShapeDtypeStruct((batch * seq, d_model), BF16),
        scratch_shapes=[pltpu.VMEM((MOBA_BLK_PAD, LANE), F32), pltpu.VMEM((2, MOBA_BLK_PAD, blk), F32)],
        compiler_params=_cparams(("parallel", "parallel", "arbitrary")),
        name="moba_attention",
    )(qkv, qkv, v_t, avg)


def _nsa_compress_kernel(kr_ref, vr_ref, pk_ref, pv_ref, wk_ref, wv_ref, c_ref, sa_ref, sb_ref, kc_ref, vc_ref):
    nc = kr_ref.shape[0]
    rowi = lax.broadcasted_iota(jnp.int32, (nc, 1), 0)

    def windows(x_ref, pos_ref, w_ref):
        x = x_ref[...].astype(F32)
        lo = _dot((x + pos_ref[0:1, :]).astype(BF16), w_ref[0])
        hi = _dot((x + pos_ref[1:2, :]).astype(BF16), w_ref[1])
        hi_next = jnp.where(rowi < nc - 1, pltpu.roll(hi, nc - 1, 0), 0.0)
        return lo + hi_next

    kk = windows(kr_ref, pk_ref, wk_ref)
    d = HEAD_DIM
    kc = kk[:, 0:d] * c_ref[...] + kk[:, d:2 * d] * sb_ref[...] + kk[:, 2 * d:3 * d] * sa_ref[...]
    kc_ref[...] = kc.astype(kc_ref.dtype)
    vv = windows(vr_ref, pv_ref, wv_ref)
    vc_ref[...] = vv[:, 0:d].astype(vc_ref.dtype)


def _nsa_compress(k_cmp, v_cmp, pos_k, pos_v, w_k, w_v, batch, seq):
    g, d, st = NSA_GROUPS, HEAD_DIM, NSA_CMP_STRIDE
    nc = seq // st

    def chunks(x):
        return x.reshape(batch, nc, st, g, d).transpose(0, 3, 1, 2, 4).reshape(batch, g, nc, st * d)

    def weights(w):
        w3 = w.reshape(NSA_CMP_LEN, d, d)
        cat = jnp.concatenate([w3, jnp.roll(w3, ROPE_HALF, axis=2), jnp.roll(w3, -ROPE_HALF, axis=2)], axis=2)
        return cat.reshape(2, st * d, 3 * d).astype(BF16)

    def positions(p):
        return p.reshape(2, st * d).astype(F32)

    cmp_end = jnp.arange(nc) * st + NSA_CMP_LEN - 1
    tabs = _rope_tables(cmp_end, d)
    blk4 = pl.BlockSpec((None, None, nc, st * d), lambda b, gi: (b, gi, 0, 0))
    out4 = pl.BlockSpec((None, None, nc, d), lambda b, gi: (b, gi, 0, 0))
    const2 = lambda shape: pl.BlockSpec(shape, lambda b, gi: (0,) * len(shape))
    return pl.pallas_call(
        _nsa_compress_kernel,
        grid=(batch, g),
        in_specs=[blk4, blk4, const2((2, st * d)), const2((2, st * d)),
                  const2((2, st * d, 3 * d)), const2((2, st * d, 3 * d)),
                  const2((nc, d)), const2((nc, d)), const2((nc, d))],
        out_specs=[out4, out4],
        out_shape=[jax.ShapeDtypeStruct((batch, g, nc, d), BF16)] * 2,
        compiler_params=_cparams(("parallel", "parallel")),
        name="nsa_compress",
    )(chunks(k_cmp), chunks(v_cmp), positions(pos_k), positions(pos_v), weights(w_k), weights(w_v), *tabs)


NSA_TQ = 128
NSA_TK = 1024


def _nsa_kernel(q_ref, kc_ref, vct_ref, ks_ref, vst_ref, kw_ref, vwt_ref, gate_ref, c2st_ref, o_ref, bias_ref):
    tq, tk, d, nh = NSA_TQ, NSA_TK, HEAD_DIM, NSA_HPG
    nql = nh * tq
    half = nql // 2
    qi = pl.program_id(2)
    t0 = qi * tq
    nc = kc_ref.shape[0]

    def lanes4(x):
        return jnp.concatenate([x] * nh, axis=1)

    tpos1 = t0 + lax.broadcasted_iota(jnp.int32, (1, tq), 1)
    tpos = lanes4(tpos1)
    gates_t = jax.nn.sigmoid(gate_ref[...]).T

    def gate(branch):
        return jnp.concatenate([gates_t[3 * h + branch:3 * h + branch + 1, :] for h in range(nh)], axis=1)

    q = jnp.concatenate([q_ref[:, h * d:(h + 1) * d] for h in range(nh)], axis=0) * jnp.asarray(ATT_SCALE, BF16)

    kd = t0 // tk
    dstart = pl.multiple_of(kd * tk, tk)
    span = NSA_WINDOW + tq
    wstart = pl.multiple_of(jnp.maximum(t0 - NSA_WINDOW, 0), tq)
    s_cmp = _dot_nt(kc_ref[...], q)
    s_win = _dot_nt(kw_ref[pl.ds(wstart, span), :], q)
    s_diag = _dot_nt(ks_ref[pl.ds(dstart, tk), :], q)

    cmp_end = lax.broadcasted_iota(jnp.int32, (nc, nql), 0) * NSA_CMP_STRIDE + (NSA_CMP_LEN - 1)
    vis_c = cmp_end <= tpos
    s = jnp.where(vis_c, s_cmp, NEG)
    e = jnp.where(vis_c, jnp.exp(s - jnp.max(s, axis=0, keepdims=True)), 0.0)
    l = jnp.sum(e, axis=0, keepdims=True)
    p = e / jnp.where(l > 0.0, l, 1.0)
    out_t = gate(0) * _dot(vct_ref[...], p.astype(BF16))
    psum = functools.reduce(lambda a, b: a + b, [p[:, h * tq:(h + 1) * tq] for h in range(nh)])
    p_hi, p_lo = _split_bf16(psum)
    imp = _dot(c2st_ref[...], p_hi) + _dot(c2st_ref[...], p_lo)
    blk = lax.broadcasted_iota(jnp.int32, (NSA_SEL_PAD, tq), 0)
    own = tpos1 // NSA_SEL_LEN
    forced = (blk == 0) | (blk == own) | (blk == own - 1)
    imp = jnp.where(blk <= own, imp + jnp.where(forced, NSA_BONUS, 0.0), NEG)
    picked = _topk_mask_rows(imp, NSA_TOP_N) > 0.5
    bias_ref[...] = lanes4(jnp.where(picked, 0.0, NEG))

    per_tile = tk // NSA_SEL_LEN

    def tile_bias(kb):
        return jnp.concatenate(
            [jnp.broadcast_to(bias_ref[pl.ds(kb * per_tile + w, 1), :], (NSA_SEL_LEN, nql))
             for w in range(per_tile)], axis=0)

    wpos = wstart + lax.broadcasted_iota(jnp.int32, (span, nql), 0)
    s = jnp.where((wpos <= tpos) & (wpos > tpos - NSA_WINDOW), s_win, NEG)
    e = jnp.exp(s - jnp.max(s, axis=0, keepdims=True))
    out_t = out_t + (gate(2) / jnp.sum(e, axis=0, keepdims=True)) * _dot(vwt_ref[:, pl.ds(wstart, span)],
                                                                         e.astype(BF16))

    kpos = dstart + lax.broadcasted_iota(jnp.int32, (tk, nql), 0)
    s = jnp.where(kpos <= tpos, s_diag + tile_bias(kd), NEG)
    m = jnp.max(s, axis=0, keepdims=True)
    e = jnp.exp(s - m)
    l = jnp.sum(e, axis=0, keepdims=True)
    acc = _dot(vst_ref[:, pl.ds(dstart, tk)], e.astype(BF16))
    halves = [slice(i * half, (i + 1) * half) for i in range(2)]
    q_half = [q[hs] for hs in halves]
    states = tuple((m[:, hs], l[:, hs], acc[:, hs]) for hs in halves)

    def sel_body(kb, sts):
        start = pl.multiple_of(kb * tk, tk)
        k = ks_ref[pl.ds(start, tk), :]
        vt = vst_ref[:, pl.ds(start, tk)]
        bias = tile_bias(kb)
        scores = [_dot_nt(k, q_half[i]) for i in range(2)]
        probs, nxt = [], []
        for i, hs in enumerate(halves):
            m, l, acc = sts[i]
            s = scores[i] + bias[:, hs]
            m_new = jnp.maximum(m, jnp.max(s, axis=0, keepdims=True))
            pr = jnp.exp(s - m_new)
            alpha = jnp.exp(m - m_new)
            probs.append(pr.astype(BF16))
            nxt.append((m_new, alpha * l + jnp.sum(pr, axis=0, keepdims=True), alpha * acc))
        return tuple((m_new, l, acc + _dot(vt, probs[i])) for i, (m_new, l, acc) in enumerate(nxt))

    states = lax.fori_loop(0, kd, sel_body, states)
    sel_t = jnp.concatenate([st[2] / st[1] for st in states], axis=1)
    out_t = out_t + gate(1) * sel_t
    for h in range(nh):
        o_ref[:, h * d:(h + 1) * d] = out_t[:, h * tq:(h + 1) * tq].T.astype(o_ref.dtype)


def _nsa_attention(q_all, kc, vct, ks, vst, kw, vwt, gates, batch, seq):
    tq, tk, d, g = NSA_TQ, NSA_TK, HEAD_DIM, NSA_GROUPS
    nq = seq // tq
    nc = seq // NSA_CMP_STRIDE
    n_sel = seq // NSA_SEL_LEN
    d_model = N_HEADS * HEAD_DIM
    cs = np.arange(nc) * NSA_CMP_STRIDE
    ss = np.arange(n_sel) * NSA_SEL_LEN
    ov = np.clip(np.minimum(cs[:, None] + NSA_CMP_LEN, ss[None, :] + NSA_SEL_LEN)
                 - np.maximum(cs[:, None], ss[None, :]), 0, None) / NSA_CMP_LEN
    c2s = np.zeros((nc, NSA_SEL_PAD), np.float32)
    c2s[:, :n_sel] = ov
    c2s[nc - 1, :] = 0.0
    k_spec = lambda n: pl.BlockSpec((None, None, n, d), lambda b, gi, i: (b, gi, 0, 0))
    vt_spec = lambda n: pl.BlockSpec((None, None, d, n), lambda b, gi, i: (b, gi, 0, 0))
    return pl.pallas_call(
        _nsa_kernel,
        grid=(batch, g, nq),
        in_specs=[pl.BlockSpec((tq, NSA_HPG * d), lambda b, gi, i: (b * nq + i, gi)),
                  k_spec(nc), vt_spec(nc), k_spec(seq), vt_spec(seq), k_spec(seq), vt_spec(seq),
                  pl.BlockSpec((tq, LANE), lambda b, gi, i: (b * nq + i, gi)),
                  pl.BlockSpec((NSA_SEL_PAD, nc), lambda b, gi, i: (0, 0))],
        out_specs=pl.BlockSpec((tq, NSA_HPG * d), lambda b, gi, i: (b * nq + i, gi)),
        out_shape=jax.ShapeDtypeStruct((batch * seq, d_model), BF16),
        scratch_shapes=[pltpu.VMEM((NSA_SEL_PAD, NSA_HPG * tq), F32)],
        compiler_params=_cparams(("parallel", "parallel", "arbitrary")),
        name="nsa_attention",
    )(q_all, kc, vct, ks, vst, kw, vwt, gates, jnp.asarray(c2s.T, dtype=BF16))


def _nsa_mixer(x2, g_norm, w_in, pos_k, pos_v, w_k, w_v, w_out, batch, seq):
    d_model, kv = N_HEADS * HEAD_DIM, NSA_KV
    sec = lambda i: w_in[:, d_model + i * kv: d_model + (i + 1) * kv]
    w_main = jnp.concatenate([w_in[:, :d_model], sec(2), sec(4), sec(0), sec(1), sec(3), sec(5)], axis=1)
    proj = _norm_matmul(x2, g_norm, w_main, seq, tn=512, n_rope=3)
    w_gate = w_in[:, d_model + 6 * kv:].reshape(d_model, NSA_GROUPS, 3 * NSA_HPG)
    w_gate = jnp.pad(w_gate, ((0, 0), (0, 0), (0, LANE - 3 * NSA_HPG))).reshape(d_model, NSA_GROUPS * LANE)
    gates = _norm_matmul(x2, g_norm, w_gate, seq, tn=NSA_GROUPS * LANE, out_dtype=F32)

    col = lambda i: proj[:, d_model + i * kv: d_model + (i + 1) * kv]
    heads = lambda x: x.reshape(batch, seq, NSA_GROUPS, HEAD_DIM).transpose(0, 2, 1, 3)
    heads_t = lambda x: x.reshape(batch, seq, NSA_GROUPS, HEAD_DIM).transpose(0, 2, 3, 1)
    kc, vc = _nsa_compress(col(2), col(3), pos_k, pos_v, w_k, w_v, batch, seq)
    o = _nsa_attention(proj, kc, vc.transpose(0, 1, 3, 2), heads(col(0)), heads_t(col(4)), heads(col(1)),
                       heads_t(col(5)), gates, batch, seq)
    return _proj_resid(o, w_out, x2)


PEER_SEL_TT = 256
PEER_PAIRS = [(i, j) for i in range(PEER_TOP_K + 1) for j in range(PEER_TOP_K + 1)
              if (i + 1) * (j + 1) <= PEER_TOP_K + 1]


def _peer_select_kernel(q_ref, k1_ref, k2_ref, cnt_ref, e1_ref, rank_ref, e2_ref):
    q = q_ref[...]
    q_hi, q_lo = _split_bf16(q)
    nh, nr = PEER_HEADS, PEER_TOP_K + 1

    def scores(k_ref, off):
        k_hi, k_lo = _split_bf16(k_ref[...])
        out = []
        for h in range(nh):
            lanes = slice(off + h * PEER_HALF, off + (h + 1) * PEER_HALF)
            out.append(_dot_nt(k_hi, q_hi[:, lanes]) + _dot_nt(k_hi, q_lo[:, lanes])
                       + _dot_nt(k_lo, q_hi[:, lanes]))
        return out

    def top_values(s):
        vals, work = [], s
        for _ in range(nr):
            m = jnp.max(work, axis=0, keepdims=True)
            vals.append(m)
            work = jnp.where(work == m, LOW, work)
        return vals

    s1 = scores(k1_ref, 0)
    s2 = scores(k2_ref, nh * PEER_HALF)
    t1 = [top_values(s) for s in s1]
    t2 = [top_values(s) for s in s2]
    r1 = [jnp.concatenate([t1[h][r] for h in range(nh)], axis=0) for r in range(nr)]
    r2 = [jnp.concatenate([t2[h][r] for h in range(nh)], axis=0) for r in range(nr)]
    cands = [r1[i] + r2[j] for (i, j) in PEER_PAIRS]
    work, tops = list(cands), []
    for _ in range(nr):
        m = functools.reduce(jnp.maximum, work)
        tops.append(m)
        work = [jnp.where(c == m, LOW, c) for c in work]
    thr = 0.5 * (tops[PEER_TOP_K - 1] + tops[PEER_TOP_K])
    cmax = tops[0]
    z = functools.reduce(lambda a, b: a + b, [jnp.where(c >= thr, jnp.exp(c - cmax), 0.0) for c in cands])
    inv_z = 1.0 / z
    for h in range(nh):
        thr_h = thr[h:h + 1, :]
        cnt = jnp.zeros_like(s1[h])
        rank = jnp.zeros_like(s2[h])
        for r in range(nr):
            cnt = jnp.where(s1[h] + t2[h][r] >= thr_h, float(r + 1), cnt)
            rank = jnp.where(t2[h][r] > s2[h], float(r + 1), rank)
        cnt_ref[h] = cnt
        e1_ref[h] = jnp.exp(s1[h] - r1[0][h:h + 1, :])
        rank_ref[h] = rank.astype(BF16)
        e2_ref[h] = (jnp.exp(s2[h] - r2[0][h:h + 1, :]) * inv_z[h:h + 1, :]).astype(BF16)


def _peer_select(q, sub_keys):
    t = q.shape[0]
    tt = min(PEER_SEL_TT, t)
    out_spec = pl.BlockSpec((PEER_HEADS, PEER_KEYS, tt), lambda i: (0, 0, i))
    return pl.pallas_call(
        _peer_select_kernel,
        grid=(t // tt,),
        in_specs=[pl.BlockSpec((tt, q.shape[1]), lambda i: (i, 0)),
                  pl.BlockSpec((PEER_KEYS, PEER_HALF), lambda i: (0, 0)),
                  pl.BlockSpec((PEER_KEYS, PEER_HALF), lambda i: (0, 0))],
        out_specs=[out_spec] * 4,
        out_shape=[jax.ShapeDtypeStruct((PEER_HEADS, PEER_KEYS, t), dt) for dt in (F32, F32, BF16, BF16)],
        compiler_params=_cparams(("parallel",)),
        name="peer_select",
    )(q, sub_keys[0].astype(F32), sub_keys[1].astype(F32))


PEER_TT = 512
PEER_NE = 2048
PEER_SUB = 512


def _gelu(x):
    return 0.5 * x * (1.0 + lax.erf(x * np.float32(np.sqrt(0.5))))


def _peer_main_kernel(x_ref, g_ref, u_ref, vt_ref, cnt_ref, e1_ref, rank_ref, e2_ref, o_ref, xnt_ref, acc_ref):
    e = pl.program_id(1)
    ne = u_ref.shape[0]
    tt = x_ref.shape[0]
    sub = PEER_SUB
    n_sub = ne // sub

    @pl.when(e == 0)
    def _():
        x = x_ref[...]
        y = x * lax.rsqrt(jnp.mean(x * x, axis=-1, keepdims=True) + NORM_EPS)
        xnt_ref[...] = (y * g_ref[...]).T.astype(BF16)
        acc_ref[...] = jnp.zeros_like(acc_ref)

    def hidden(c):
        return _dot(u_ref[c * sub:(c + 1) * sub, :], xnt_ref[...])

    def weighted(c, h):
        act = _gelu(h).astype(BF16)
        parts = []
        for k in range(sub // PEER_KEYS):
            i1 = (e * n_sub + c) * (sub // PEER_KEYS) + k
            gsum = jnp.zeros((PEER_KEYS, tt), BF16)
            for hd in range(PEER_HEADS):
                cnt = cnt_ref[hd, pl.ds(i1, 1), :].astype(BF16)
                e1 = e1_ref[hd, pl.ds(i1, 1), :].astype(BF16)
                gsum = gsum + jnp.where(rank_ref[hd] < cnt, e2_ref[hd], jnp.zeros((), BF16)) * e1
            parts.append(gsum * act[k * PEER_KEYS:(k + 1) * PEER_KEYS, :])
        return jnp.concatenate(parts, axis=0) if len(parts) > 1 else parts[0]

    def values(c, w):
        return _dot(vt_ref[:, c * sub:(c + 1) * sub], w)

    hs = {0: hidden(0)}
    if n_sub > 1:
        hs[1] = hidden(1)
    total = None
    for c in range(n_sub):
        w = weighted(c, hs.pop(c))
        if c + 2 < n_sub:
            hs[c + 2] = hidden(c + 2)
        pv = values(c, w)
        total = pv if total is None else total + pv
    acc_ref[...] += total

    @pl.when(e == pl.num_programs(1) - 1)
    def _():
        o_ref[...] = x_ref[...] + acc_ref[...].T


def _peer_mixer(x2, g_norm, w_q, sub_keys, u, v, seq):
    t, d = x2.shape
    w_q2 = w_q.reshape(d, PEER_HEADS, 2, PEER_HALF).transpose(0, 2, 1, 3).reshape(d, 2 * PEER_HEADS * PEER_HALF)
    q = _norm_matmul3(x2, g_norm, w_q2)
    cnt, e1, rank, e2 = _peer_select(q, sub_keys)
    tt = min(PEER_TT, t)
    ne = PEER_NE
    n_exp = u.shape[0]
    tok_spec = pl.BlockSpec((PEER_HEADS, PEER_KEYS, tt), lambda i, e: (0, 0, i))
    return pl.pallas_call(
        _peer_main_kernel,
        grid=(t // tt, n_exp // ne),
        in_specs=[pl.BlockSpec((tt, d), lambda i, e: (i, 0)),
                  pl.BlockSpec((1, d), lambda i, e: (0, 0)),
                  pl.BlockSpec((ne, d), lambda i, e: (e, 0)),
                  pl.BlockSpec((d, ne), lambda i, e: (0, e)),
                  tok_spec, tok_spec, tok_spec, tok_spec],
        out_specs=pl.BlockSpec((tt, d), lambda i, e: (i, 0)),
        out_shape=jax.ShapeDtypeStruct((t, d), F32),
        scratch_shapes=[pltpu.VMEM((d, tt), BF16), pltpu.VMEM((d, tt), F32)],
        compiler_params=_cparams(("parallel", "arbitrary")),
        name="peer_main",
    )(x2, g_norm.reshape(1, d).astype(F32), u.astype(BF16), v.T.astype(BF16), cnt, e1, rank, e2)


def _sb_mixer(x2, g_norm, w_in, w_out, batch, seq):
    qkv = _norm_matmul(x2, g_norm, w_in, seq, tn=512)
    return _proj_resid(_sb_attention(qkv, batch, seq), w_out, x2)


def _moba_mixer(x2, g_norm, w_in, w_out, batch, seq):
    d_model = N_HEADS * HEAD_DIM
    qkv = _norm_matmul(x2, g_norm, w_in, seq, tn=512, n_rope=2 * d_model // 512)
    return _proj_resid(_moba_attention(qkv, batch, seq), w_out, x2)


def kernel(x, l0_norm_mix, l0_sb_w_in, l0_sb_w_out, l0_norm_ffn, l0_peer_w_q, l0_peer_sub_keys, l0_peer_u, l0_peer_v, l1_norm_mix, l1_nsa_w_in, l1_nsa_cmp_pos_k, l1_nsa_cmp_pos_v, l1_nsa_cmp_w_k, l1_nsa_cmp_w_v, l1_nsa_w_out, l1_norm_ffn, l1_peer_w_q, l1_peer_sub_keys, l1_peer_u, l1_peer_v, l2_norm_mix, l2_moba_w_in, l2_moba_w_out, l2_norm_ffn, l2_peer_w_q, l2_peer_sub_keys, l2_peer_u, l2_peer_v, l3_norm_mix, l3_sb_w_in, l3_sb_w_out, l3_norm_ffn, l3_peer_w_q, l3_peer_sub_keys, l3_peer_u, l3_peer_v, final_norm):
    batch, seq, d = x.shape
    x2 = x.reshape(batch * seq, d)
    x2 = _sb_mixer(x2, l0_norm_mix, l0_sb_w_in, l0_sb_w_out, batch, seq)
    x2 = _peer_mixer(x2, l0_norm_ffn, l0_peer_w_q, l0_peer_sub_keys, l0_peer_u, l0_peer_v, seq)
    x2 = _nsa_mixer(x2, l1_norm_mix, l1_nsa_w_in, l1_nsa_cmp_pos_k, l1_nsa_cmp_pos_v, l1_nsa_cmp_w_k,
                    l1_nsa_cmp_w_v, l1_nsa_w_out, batch, seq)
    x2 = _peer_mixer(x2, l1_norm_ffn, l1_peer_w_q, l1_peer_sub_keys, l1_peer_u, l1_peer_v, seq)
    x2 = _moba_mixer(x2, l2_norm_mix, l2_moba_w_in, l2_moba_w_out, batch, seq)
    x2 = _peer_mixer(x2, l2_norm_ffn, l2_peer_w_q, l2_peer_sub_keys, l2_peer_u, l2_peer_v, seq)
    x2 = _sb_mixer(x2, l3_norm_mix, l3_sb_w_in, l3_sb_w_out, batch, seq)
    x2 = _peer_mixer(x2, l3_norm_ffn, l3_peer_w_q, l3_peer_sub_keys, l3_peer_u, l3_peer_v, seq)
    return _rmsnorm(x2, final_norm).reshape(batch, seq, d)
```

```python
import functools

import numpy as np
import jax
import jax.numpy as jnp
from jax import lax
from jax.experimental import pallas as pl
from jax.experimental.pallas import tpu as pltpu

F32 = jnp.float32
BF16 = jnp.bfloat16

N_HEADS = 16
HEAD_DIM = 64
ROPE_DIM = 16
ROPE_HALF = ROPE_DIM // 2
ROPE_THETA = 500000.0
NORM_EPS = 1e-6
NEG = -1e30
LOW = -3e38
ATT_SCALE = HEAD_DIM ** -0.5

NSA_GROUPS = 4
NSA_HPG = N_HEADS // NSA_GROUPS
NSA_KV = NSA_GROUPS * HEAD_DIM
NSA_CMP_LEN = 32
NSA_CMP_STRIDE = 16
NSA_SEL_LEN = 64
NSA_TOP_N = 16
NSA_WINDOW = 512
NSA_BONUS = 1e3
NSA_SEL_PAD = 128

MOBA_BLOCK = 256
MOBA_TOP_K = 3
MOBA_BLK_PAD = 128

PEER_HEADS = 8
PEER_KEYS = 128
PEER_TOP_K = 16
PEER_HALF = 64

SB_EXIT = -110.0

LANE = 128
VMEM_LIMIT = 56 << 20


def _cparams(sem, vmem=VMEM_LIMIT):
    return pltpu.CompilerParams(dimension_semantics=sem, vmem_limit_bytes=vmem)


def _dot(a, b):
    return jnp.dot(a, b, preferred_element_type=F32)


def _dot_nt(a, b):
    return lax.dot_general(a, b, (((1,), (1,)), ((), ())), preferred_element_type=F32)


def _split_bf16(x):
    hi = x.astype(BF16)
    lo = (x - hi.astype(F32)).astype(BF16)
    return hi, lo


def _topk_mask(vals, k):
    lane = lax.broadcasted_iota(jnp.int32, vals.shape, 1)
    work = vals
    sel = jnp.zeros(vals.shape, F32)
    for _ in range(k):
        m = jnp.max(work, axis=-1, keepdims=True)
        first = jnp.min(jnp.where(work == m, lane, LANE), axis=-1, keepdims=True)
        hit = lane == first
        sel = jnp.where(hit, 1.0, sel)
        work = jnp.where(hit, LOW, work)
    return sel


def _norm_matmul_kernel(*refs, n_rope, tn):
    if n_rope:
        x_ref, g_ref, w_ref, c_ref, sa_ref, sb_ref, o_ref, xn_ref = refs
    else:
        x_ref, g_ref, w_ref, o_ref, xn_ref = refs
    j = pl.program_id(1)

    @pl.when(j == 0)
    def _():
        x = x_ref[...]
        y = x * lax.rsqrt(jnp.mean(x * x, axis=-1, keepdims=True) + NORM_EPS)
        xn_ref[...] = (y * g_ref[...]).astype(BF16)

    acc = _dot(xn_ref[...], w_ref[...])
    if n_rope:
        @pl.when(j < n_rope)
        def _():
            r = (acc * c_ref[...] + pltpu.roll(acc, ROPE_HALF, 1) * sb_ref[...]
                 + pltpu.roll(acc, tn - ROPE_HALF, 1) * sa_ref[...])
            o_ref[...] = r.astype(o_ref.dtype)

        @pl.when(j >= n_rope)
        def _():
            o_ref[...] = acc.astype(o_ref.dtype)
    else:
        o_ref[...] = acc.astype(o_ref.dtype)


def _rope_tables(pos, width):
    inv = ROPE_THETA ** (-jnp.arange(0, ROPE_DIM, 2, dtype=F32) / ROPE_DIM)
    ang = pos.astype(F32)[:, None] * inv[None, :]
    cos, sin = jnp.cos(ang), jnp.sin(ang)
    n = pos.shape[0]
    rest = HEAD_DIM - ROPE_DIM
    c = jnp.concatenate([cos, cos, jnp.ones((n, rest), F32)], axis=1)
    sa = jnp.concatenate([-sin, jnp.zeros((n, ROPE_HALF + rest), F32)], axis=1)
    sb = jnp.concatenate([jnp.zeros((n, ROPE_HALF), F32), sin, jnp.zeros((n, rest), F32)], axis=1)
    reps = width // HEAD_DIM
    return jnp.tile(c, (1, reps)), jnp.tile(sa, (1, reps)), jnp.tile(sb, (1, reps))


def _norm_matmul(x2, g, w, seq, *, tn, n_rope=0, out_dtype=BF16, tt=1024):
    t, d = x2.shape
    n = w.shape[1]
    tt = min(tt, seq)
    grid = (t // tt, n // tn)
    in_specs = [pl.BlockSpec((tt, d), lambda i, j: (i, 0)),
                pl.BlockSpec((1, d), lambda i, j: (0, 0)),
                pl.BlockSpec((d, tn), lambda i, j: (0, j))]
    args = [x2, g.reshape(1, d).astype(F32), w.astype(BF16)]
    if n_rope:
        nper = seq // tt
        tabs = _rope_tables(jnp.arange(seq), tn)
        in_specs += [pl.BlockSpec((tt, tn), lambda i, j: (i % nper, 0))] * 3
        args += list(tabs)
    return pl.pallas_call(
        functools.partial(_norm_matmul_kernel, n_rope=n_rope, tn=tn),
        grid=grid, in_specs=in_specs,
        out_specs=pl.BlockSpec((tt, tn), lambda i, j: (i, j)),
        out_shape=jax.ShapeDtypeStruct((t, n), out_dtype),
        scratch_shapes=[pltpu.VMEM((tt, d), BF16)],
        compiler_params=_cparams(("parallel", "arbitrary")),
        name="norm_matmul",
    )(*args)


def _norm_matmul3_kernel(x_ref, g_ref, wh_ref, wl_ref, o_ref, xh_ref, xl_ref):
    @pl.when(pl.program_id(1) == 0)
    def _():
        x = x_ref[...]
        y = x * lax.rsqrt(jnp.mean(x * x, axis=-1, keepdims=True) + NORM_EPS) * g_ref[...]
        xh_ref[...], xl_ref[...] = _split_bf16(y)

    o_ref[...] = (_dot(xh_ref[...], wh_ref[...]) + _dot(xl_ref[...], wh_ref[...])
                  + _dot(xh_ref[...], wl_ref[...]))


def _norm_matmul3(x2, g, w, *, tn=512, tt=1024):
    t, d = x2.shape
    n = w.shape[1]
    tt = min(tt, t)
    w_hi, w_lo = _split_bf16(w.astype(F32))
    w_spec = pl.BlockSpec((d, tn), lambda i, j: (0, j))
    return pl.pallas_call(
        _norm_matmul3_kernel,
        grid=(t // tt, n // tn),
        in_specs=[pl.BlockSpec((tt, d), lambda i, j: (i, 0)),
                  pl.BlockSpec((1, d), lambda i, j: (0, 0)), w_spec, w_spec],
        out_specs=pl.BlockSpec((tt, tn), lambda i, j: (i, j)),
        out_shape=jax.ShapeDtypeStruct((t, n), F32),
        scratch_shapes=[pltpu.VMEM((tt, d), BF16), pltpu.VMEM((tt, d), BF16)],
        compiler_params=_cparams(("parallel", "arbitrary")),
        name="norm_matmul3",
    )(x2, g.reshape(1, d).astype(F32), w_hi, w_lo)


def _proj_resid_kernel(a_ref, w_ref, r_ref, o_ref):
    o_ref[...] = r_ref[...] + _dot(a_ref[...], w_ref[...])


def _proj_resid(a, w, resid, *, tt=1024, tn=1024):
    t, d = a.shape
    n = w.shape[1]
    tt = min(tt, t)
    return pl.pallas_call(
        _proj_resid_kernel,
        grid=(t // tt, n // tn),
        in_specs=[pl.BlockSpec((tt, d), lambda i, j: (i, 0)),
                  pl.BlockSpec((d, tn), lambda i, j: (0, j)),
                  pl.BlockSpec((tt, tn), lambda i, j: (i, j))],
        out_specs=pl.BlockSpec((tt, tn), lambda i, j: (i, j)),
        out_shape=jax.ShapeDtypeStruct((t, n), F32),
        compiler_params=_cparams(("parallel", "parallel")),
        name="proj_resid",
    )(a, w.astype(BF16), resid)


def _rmsnorm_kernel(x_ref, g_ref, o_ref):
    x = x_ref[...]
    y = x * lax.rsqrt(jnp.mean(x * x, axis=-1, keepdims=True) + NORM_EPS)
    o_ref[...] = y * g_ref[...]


def _rmsnorm(x2, g, *, tt=512):
    t, d = x2.shape
    tt = min(tt, t)
    return pl.pallas_call(
        _rmsnorm_kernel,
        grid=(t // tt,),
        in_specs=[pl.BlockSpec((tt, d), lambda i: (i, 0)), pl.BlockSpec((1, d), lambda i: (0, 0))],
        out_specs=pl.BlockSpec((tt, d), lambda i: (i, 0)),
        out_shape=jax.ShapeDtypeStruct((t, d), F32),
        compiler_params=_cparams(("parallel",)),
        name="final_rmsnorm",
    )(x2, g.reshape(1, d).astype(F32))


SB_TILE = 256


def _sb_kernel(q_ref, k_ref, v_ref, tri_ref, o_ref):
    tile = SB_TILE
    qi = pl.program_id(2)
    tri = tri_ref[...]
    row = lax.broadcasted_iota(jnp.int32, (tile, tile), 0)
    col = lax.broadcasted_iota(jnp.int32, (tile, tile), 1)
    past = col < row

    head_lanes = [slice(hh * HEAD_DIM, (hh + 1) * HEAD_DIM) for hh in range(2)]
    qs = [q_ref[:, lanes] * jnp.asarray(ATT_SCALE, BF16) for lanes in head_lanes]

    def step(start, r_sums, accs, diag):
        zs = [_dot_nt(qs[hh], k_ref[pl.ds(start, tile), lanes]) for hh, lanes in enumerate(head_lanes)]
        sps, lbs, splits = [], [], []
        for z in zs:
            sp = jnp.maximum(z, 0.0) + jnp.log1p(jnp.exp(-jnp.abs(z)))
            lb = jnp.where(past, -sp, 0.0) if diag else -sp
            sps.append(sp)
            lbs.append(lb)
            splits.append(_split_bf16(lb))
        suffixes = [_dot(hi, tri) + _dot(lo, tri) for hi, lo in splits]
        weights = []
        for hh in range(2):
            a = jnp.exp((zs[hh] - sps[hh]) + suffixes[hh] + r_sums[hh])
            if diag:
                a = jnp.where(past, a, 0.0)
            weights.append(a.astype(BF16))
        accs = tuple(accs[hh] + _dot(weights[hh], v_ref[pl.ds(start, tile), lanes])
                     for hh, lanes in enumerate(head_lanes))
        r_sums = tuple(r_sums[hh] + jnp.sum(lbs[hh], axis=-1, keepdims=True) for hh in range(2))
        return r_sums, accs

    zero_r = tuple(jnp.zeros((tile, 1), F32) for _ in range(2))
    zero_acc = tuple(jnp.zeros((tile, HEAD_DIM), F32) for _ in range(2))
    r1, acc1 = step(pl.multiple_of(qi * tile, tile), zero_r, zero_acc, True)

    def older(j, r_sums):
        alive = jnp.maximum(jnp.max(r_sums[0]), jnp.max(r_sums[1])) > SB_EXIT
        return jnp.where(alive, j - 1, -1)

    def cond(c):
        return c[0] >= 0

    def body(c):
        j, r_sums, accs = c
        r_sums, accs = step(pl.multiple_of(j * tile, tile), r_sums, accs, False)
        return older(j, r_sums), r_sums, accs

    _, _, accs = lax.while_loop(cond, body, (older(qi, r1), r1, acc1))
    for hh, lanes in enumerate(head_lanes):
        o_ref[:, lanes] = accs[hh].astype(o_ref.dtype)


def _sb_attention(qkv, batch, seq):
    tile = SB_TILE
    nq = seq // tile
    d_model = N_HEADS * HEAD_DIM
    ncol = d_model // LANE
    ii = np.arange(tile)
    tri = jnp.asarray(ii[:, None] > ii[None, :], dtype=BF16)
    return pl.pallas_call(
        _sb_kernel,
        grid=(batch, ncol, nq),
        in_specs=[pl.BlockSpec((tile, LANE), lambda b, h, i: (b * nq + i, h)),
                  pl.BlockSpec((seq, LANE), lambda b, h, i: (b, ncol + h)),
                  pl.BlockSpec((seq, LANE), lambda b, h, i: (b, 2 * ncol + h)),
                  pl.BlockSpec((tile, tile), lambda b, h, i: (0, 0))],
        out_specs=pl.BlockSpec((tile, LANE), lambda b, h, i: (b * nq + i, h)),
        out_shape=jax.ShapeDtypeStruct((batch * seq, d_model), BF16),
        compiler_params=_cparams(("parallel", "parallel", "arbitrary")),
        name="sb_attention",
    )(qkv, qkv, qkv, tri)


def _flash_first(q, k, v, mask):
    s = jnp.where(mask, _dot_nt(q, k), NEG)
    m = jnp.max(s, axis=-1, keepdims=True)
    p = jnp.exp(s - m)
    return m, jnp.sum(p, axis=-1, keepdims=True), _dot(p.astype(BF16), v)


def _flash_next(q, k, v, mask, state):
    m, l, acc = state
    s = jnp.where(mask, _dot_nt(q, k), NEG)
    m_new = jnp.maximum(m, jnp.max(s, axis=-1, keepdims=True))
    p = jnp.exp(s - m_new)
    alpha = jnp.exp(m - m_new)
    l = alpha * l + jnp.sum(p, axis=-1, keepdims=True)
    acc = alpha * acc + _dot(p.astype(BF16), v)
    return m_new, l, acc


def _flash_out(state):
    _, l, acc = state
    return acc / l


def _flash_t_first(q, k, vt, bias_t):
    s = _dot_nt(k, q) + bias_t
    m = jnp.max(s, axis=0, keepdims=True)
    p = jnp.exp(s - m)
    return m, jnp.sum(p, axis=0, keepdims=True), _dot(vt, p.astype(BF16))


def _flash_t_next(q, k, vt, bias_t, state):
    m, l, acc_t = state
    s = _dot_nt(k, q) + bias_t
    m_new = jnp.maximum(m, jnp.max(s, axis=0, keepdims=True))
    p = jnp.exp(s - m_new)
    alpha = jnp.exp(m - m_new)
    l = alpha * l + jnp.sum(p, axis=0, keepdims=True)
    acc_t = alpha * acc_t + _dot(vt, p.astype(BF16))
    return m_new, l, acc_t


def _topk_mask_rows(vals, k):
    rowi = lax.broadcasted_iota(jnp.int32, vals.shape, 0)
    work = vals
    sel = jnp.zeros(vals.shape, F32)
    for _ in range(k):
        m = jnp.max(work, axis=0, keepdims=True)
        first = jnp.min(jnp.where(work == m, rowi, vals.shape[0]), axis=0, keepdims=True)
        hit = rowi == first
        sel = jnp.where(hit, 1.0, sel)
        work = jnp.where(hit, LOW, work)
    return sel


MOBA_WALK = 4
MOBA_HEADS = 8


def _moba_kernel(q_ref, k_ref, vt_ref, avg_ref, o_ref, kmean_ref, bias_ref):
    blk = MOBA_BLOCK
    wide = MOBA_WALK * blk
    qi = pl.program_id(2)

    @pl.when(qi == 0)
    def _():
        kmean_ref[...] = _dot(avg_ref[...], k_ref[...])

    key_i = lax.broadcasted_iota(jnp.int32, (blk, blk), 0)
    qry_i = lax.broadcasted_iota(jnp.int32, (blk, blk), 1)
    causal_bias = jnp.where(key_i <= qry_i, 0.0, NEG)
    blk_i = lax.broadcasted_iota(jnp.int32, (MOBA_BLK_PAD, blk), 0)
    start = pl.multiple_of(qi * blk, blk)
    head_rows = [slice(hh * HEAD_DIM, (hh + 1) * HEAD_DIM) for hh in range(MOBA_HEADS)]

    qs, states = [], []
    for hh, lanes in enumerate(head_rows):
        q_raw = q_ref[:, lanes]
        km_hi, km_lo = _split_bf16(kmean_ref[:, lanes])
        gate = _dot_nt(km_hi, q_raw) + _dot_nt(km_lo, q_raw)
        gate = jnp.where(blk_i < qi, gate, NEG)
        picked = (_topk_mask_rows(gate, MOBA_TOP_K) > 0.5) & (blk_i < qi)
        bias_ref[hh] = jnp.where(picked, 0.0, NEG)
        qs.append(q_raw * jnp.asarray(ATT_SCALE, BF16))
        states.append(_flash_t_first(qs[-1], k_ref[pl.ds(start, blk), lanes],
                                     vt_ref[lanes, pl.ds(start, blk)], causal_bias))

    def body(p, sts):
        s0 = pl.multiple_of(p * wide, wide)
        scores = [_dot_nt(k_ref[pl.ds(s0, wide), lanes], qs[hh]) for hh, lanes in enumerate(head_rows)]
        probs, nxt = [], []
        for hh in range(MOBA_HEADS):
            m, l, acc_t = sts[hh]
            bias = jnp.concatenate(
                [jnp.broadcast_to(bias_ref[hh, pl.ds(p * MOBA_WALK + w, 1), :], (blk, blk))
                 for w in range(MOBA_WALK)], axis=0)
            s = scores[hh] + bias
            m_new = jnp.maximum(m, jnp.max(s, axis=0, keepdims=True))
            pr = jnp.exp(s - m_new)
            alpha = jnp.exp(m - m_new)
            probs.append(pr.astype(BF16))
            nxt.append((m_new, alpha * l + jnp.sum(pr, axis=0, keepdims=True), alpha * acc_t))
        return tuple((m_new, l, acc_s + _dot(vt_ref[lanes, pl.ds(s0, wide)], probs[hh]))
                     for hh, (lanes, (m_new, l, acc_s)) in enumerate(zip(head_rows, nxt)))

    states = lax.fori_loop(0, (qi + MOBA_WALK - 1) // MOBA_WALK, body, tuple(states))
    for hh, lanes in enumerate(head_rows):
        _, l, acc_t = states[hh]
        o_ref[:, lanes] = (acc_t / l).T.astype(o_ref.dtype)


def _moba_attention(qkv, batch, seq):
    blk = MOBA_BLOCK
    wide = MOBA_WALK * blk
    nq = seq // blk
    d_model = N_HEADS * HEAD_DIM
    width = MOBA_HEADS * HEAD_DIM
    ncol = d_model // width
    avg = np.zeros((MOBA_BLK_PAD, seq), np.float32)
    for n in range(nq):
        avg[n, n * blk:(n + 1) * blk] = 1.0 / blk
    avg = jnp.asarray(avg, dtype=BF16)
    v_t = qkv[:, 2 * d_model:].reshape(batch, seq, d_model).transpose(0, 2, 1)
    return pl.pallas_call(
        _moba_kernel,
        grid=(batch, ncol, nq),
        in_specs=[pl.BlockSpec((blk, width), lambda b, h, i: (b * nq + i, h)),
                  pl.BlockSpec((seq, width), lambda b, h, i: (b, ncol + h)),
                  pl.BlockSpec((None, width, seq), lambda b, h, i: (b, h, 0)),
                  pl.BlockSpec((MOBA_BLK_PAD, seq), lambda b, h, i: (0, 0))],
        out_specs=pl.BlockSpec((blk, width), lambda b, h, i: (b * nq + i, h)),
        out_shape=jax.ShapeDtypeStruct((batch * seq, d_model), BF16),
        scratch_shapes=[pltpu.VMEM((MOBA_BLK_PAD, width), F32),
                        pltpu.VMEM((MOBA_HEADS, MOBA_BLK_PAD, blk), F32)],
        compiler_params=_cparams(("parallel", "parallel", "arbitrary")),
        name="moba_attention",
    )(qkv, qkv, v_t, avg)


def _nsa_compress_kernel(kr_ref, vr_ref, pk_ref, pv_ref, wk_ref, wv_ref, c_ref, sa_ref, sb_ref, kc_ref, vc_ref):
    nc = kr_ref.shape[0]
    rowi = lax.broadcasted_iota(jnp.int32, (nc, 1), 0)

    def windows(x_ref, pos_ref, w_ref):
        x = x_ref[...].astype(F32)
        lo = _dot((x + pos_ref[0:1, :]).astype(BF16), w_ref[0])
        hi = _dot((x + pos_ref[1:2, :]).astype(BF16), w_ref[1])
        hi_next = jnp.where(rowi < nc - 1, pltpu.roll(hi, nc - 1, 0), 0.0)
        return lo + hi_next

    kk = windows(kr_ref, pk_ref, wk_ref)
    d = HEAD_DIM
    kc = kk[:, 0:d] * c_ref[...] + kk[:, d:2 * d] * sb_ref[...] + kk[:, 2 * d:3 * d] * sa_ref[...]
    kc_ref[...] = kc.astype(kc_ref.dtype)
    vv = windows(vr_ref, pv_ref, wv_ref)
    vc_ref[...] = vv[:, 0:d].astype(vc_ref.dtype)


def _nsa_compress(k_cmp, v_cmp, pos_k, pos_v, w_k, w_v, batch, seq):
    g, d, st = NSA_GROUPS, HEAD_DIM, NSA_CMP_STRIDE
    nc = seq // st

    def chunks(x):
        return x.reshape(batch, nc, st, g, d).transpose(0, 3, 1, 2, 4).reshape(batch, g, nc, st * d)

    def weights(w):
        w3 = w.reshape(NSA_CMP_LEN, d, d)
        cat = jnp.concatenate([w3, jnp.roll(w3, ROPE_HALF, axis=2), jnp.roll(w3, -ROPE_HALF, axis=2)], axis=2)
        return cat.reshape(2, st * d, 3 * d).astype(BF16)

    def positions(p):
        return p.reshape(2, st * d).astype(F32)

    cmp_end = jnp.arange(nc) * st + NSA_CMP_LEN - 1
    tabs = _rope_tables(cmp_end, d)
    blk4 = pl.BlockSpec((None, None, nc, st * d), lambda b, gi: (b, gi, 0, 0))
    out4 = pl.BlockSpec((None, None, nc, d), lambda b, gi: (b, gi, 0, 0))
    const2 = lambda shape: pl.BlockSpec(shape, lambda b, gi: (0,) * len(shape))
    return pl.pallas_call(
        _nsa_compress_kernel,
        grid=(batch, g),
        in_specs=[blk4, blk4, const2((2, st * d)), const2((2, st * d)),
                  const2((2, st * d, 3 * d)), const2((2, st * d, 3 * d)),
                  const2((nc, d)), const2((nc, d)), const2((nc, d))],
        out_specs=[out4, out4],
        out_shape=[jax.ShapeDtypeStruct((batch, g, nc, d), BF16)] * 2,
        compiler_params=_cparams(("parallel", "parallel")),
        name="nsa_compress",
    )(chunks(k_cmp), chunks(v_cmp), positions(pos_k), positions(pos_v), weights(w_k), weights(w_v), *tabs)


NSA_TQ = 128
NSA_TK = 1024
NSA_CHAINS = 4


def _nsa_kernel(q_ref, kc_ref, vct_ref, ks_ref, vst_ref, kw_ref, vwt_ref, gate_ref, c2st_ref, o_ref, bias_ref):
    tq, tk, d, nh = NSA_TQ, NSA_TK, HEAD_DIM, NSA_HPG
    nql = nh * tq
    half = nql // NSA_CHAINS
    qi = pl.program_id(2)
    t0 = qi * tq
    nc = kc_ref.shape[0]

    def lanes4(x):
        return jnp.concatenate([x] * nh, axis=1)

    tpos1 = t0 + lax.broadcasted_iota(jnp.int32, (1, tq), 1)
    tpos = lanes4(tpos1)
    gates_t = jax.nn.sigmoid(gate_ref[...]).T

    def gate(branch):
        return jnp.concatenate([gates_t[3 * h + branch:3 * h + branch + 1, :] for h in range(nh)], axis=1)

    q = jnp.concatenate([q_ref[:, h * d:(h + 1) * d] for h in range(nh)], axis=0) * jnp.asarray(ATT_SCALE, BF16)

    kd = t0 // tk
    dstart = pl.multiple_of(kd * tk, tk)
    span = NSA_WINDOW + tq
    wstart = pl.multiple_of(jnp.maximum(t0 - NSA_WINDOW, 0), tq)
    s_cmp = _dot_nt(kc_ref[...], q)
    s_win = _dot_nt(kw_ref[pl.ds(wstart, span), :], q)
    s_diag = _dot_nt(ks_ref[pl.ds(dstart, tk), :], q)

    cmp_end = lax.broadcasted_iota(jnp.int32, (nc, nql), 0) * NSA_CMP_STRIDE + (NSA_CMP_LEN - 1)
    vis_c = cmp_end <= tpos
    s = jnp.where(vis_c, s_cmp, NEG)
    e = jnp.where(vis_c, jnp.exp(s - jnp.max(s, axis=0, keepdims=True)), 0.0)
    l = jnp.sum(e, axis=0, keepdims=True)
    p = e / jnp.where(l > 0.0, l, 1.0)
    out_t = gate(0) * _dot(vct_ref[...], p.astype(BF16))
    psum = functools.reduce(lambda a, b: a + b, [p[:, h * tq:(h + 1) * tq] for h in range(nh)])
    p_hi, p_lo = _split_bf16(psum)
    imp = _dot(c2st_ref[...], p_hi) + _dot(c2st_ref[...], p_lo)
    blk = lax.broadcasted_iota(jnp.int32, (NSA_SEL_PAD, tq), 0)
    own = tpos1 // NSA_SEL_LEN
    forced = (blk == 0) | (blk == own) | (blk == own - 1)
    imp = jnp.where(blk <= own, imp + jnp.where(forced, NSA_BONUS, 0.0), NEG)
    picked = _topk_mask_rows(imp, NSA_TOP_N) > 0.5
    bias_ref[...] = lanes4(jnp.where(picked, 0.0, NEG))

    per_tile = tk // NSA_SEL_LEN

    def tile_bias(kb):
        return jnp.concatenate(
            [jnp.broadcast_to(bias_ref[pl.ds(kb * per_tile + w, 1), :], (NSA_SEL_LEN, nql))
             for w in range(per_tile)], axis=0)

    wpos = wstart + lax.broadcasted_iota(jnp.int32, (span, nql), 0)
    s = jnp.where((wpos <= tpos) & (wpos > tpos - NSA_WINDOW), s_win, NEG)
    e = jnp.exp(s - jnp.max(s, axis=0, keepdims=True))
    out_t = out_t + (gate(2) / jnp.sum(e, axis=0, keepdims=True)) * _dot(vwt_ref[:, pl.ds(wstart, span)],
                                                                         e.astype(BF16))

    kpos = dstart + lax.broadcasted_iota(jnp.int32, (tk, nql), 0)
    s = jnp.where(kpos <= tpos, s_diag + tile_bias(kd), NEG)
    m = jnp.max(s, axis=0, keepdims=True)
    e = jnp.exp(s - m)
    l = jnp.sum(e, axis=0, keepdims=True)
    acc = _dot(vst_ref[:, pl.ds(dstart, tk)], e.astype(BF16))
    halves = [slice(i * half, (i + 1) * half) for i in range(NSA_CHAINS)]
    q_half = [q[hs] for hs in halves]
    states = tuple((m[:, hs], l[:, hs], acc[:, hs]) for hs in halves)

    def sel_body(kb, sts):
        start = pl.multiple_of(kb * tk, tk)
        k = ks_ref[pl.ds(start, tk), :]
        vt = vst_ref[:, pl.ds(start, tk)]
        bias = tile_bias(kb)
        scores = [_dot_nt(k, q_half[i]) for i in range(NSA_CHAINS)]
        probs, nxt = [], []
        for i, hs in enumerate(halves):
            m, l, acc = sts[i]
            s = scores[i] + bias[:, hs]
            m_new = jnp.maximum(m, jnp.max(s, axis=0, keepdims=True))
            pr = jnp.exp(s - m_new)
            alpha = jnp.exp(m - m_new)
            probs.append(pr.astype(BF16))
            nxt.append((m_new, alpha * l + jnp.sum(pr, axis=0, keepdims=True), alpha * acc))
        return tuple((m_new, l, acc + _dot(vt, probs[i])) for i, (m_new, l, acc) in enumerate(nxt))

    states = lax.fori_loop(0, kd, sel_body, states)
    sel_t = jnp.concatenate([st[2] / st[1] for st in states], axis=1)
    out_t = out_t + gate(1) * sel_t
    for h in range(nh):
        o_ref[:, h * d:(h + 1) * d] = out_t[:, h * tq:(h + 1) * tq].T.astype(o_ref.dtype)


def _nsa_attention(q_all, kc, vct, ks, vst, kw, vwt, gates, batch, seq):
    tq, tk, d, g = NSA_TQ, NSA_TK, HEAD_DIM, NSA_GROUPS
    nq = seq // tq
    nc = seq // NSA_CMP_STRIDE
    n_sel = seq // NSA_SEL_LEN
    d_model = N_HEADS * HEAD_DIM
    cs = np.arange(nc) * NSA_CMP_STRIDE
    ss = np.arange(n_sel) * NSA_SEL_LEN
    ov = np.clip(np.minimum(cs[:, None] + NSA_CMP_LEN, ss[None, :] + NSA_SEL_LEN)
                 - np.maximum(cs[:, None], ss[None, :]), 0, None) / NSA_CMP_LEN
    c2s = np.zeros((nc, NSA_SEL_PAD), np.float32)
    c2s[:, :n_sel] = ov
    c2s[nc - 1, :] = 0.0
    k_spec = lambda n: pl.BlockSpec((None, None, n, d), lambda b, gi, i: (b, gi, 0, 0))
    vt_spec = lambda n: pl.BlockSpec((None, None, d, n), lambda b, gi, i: (b, gi, 0, 0))
    return pl.pallas_call(
        _nsa_kernel,
        grid=(batch, g, nq),
        in_specs=[pl.BlockSpec((tq, NSA_HPG * d), lambda b, gi, i: (b * nq + i, gi)),
                  k_spec(nc), vt_spec(nc), k_spec(seq), vt_spec(seq), k_spec(seq), vt_spec(seq),
                  pl.BlockSpec((tq, LANE), lambda b, gi, i: (b * nq + i, gi)),
                  pl.BlockSpec((NSA_SEL_PAD, nc), lambda b, gi, i: (0, 0))],
        out_specs=pl.BlockSpec((tq, NSA_HPG * d), lambda b, gi, i: (b * nq + i, gi)),
        out_shape=jax.ShapeDtypeStruct((batch * seq, d_model), BF16),
        scratch_shapes=[pltpu.VMEM((NSA_SEL_PAD, NSA_HPG * tq), F32)],
        compiler_params=_cparams(("parallel", "parallel", "arbitrary")),
        name="nsa_attention",
    )(q_all, kc, vct, ks, vst, kw, vwt, gates, jnp.asarray(c2s.T, dtype=BF16))


def _nsa_mixer(x2, g_norm, w_in, pos_k, pos_v, w_k, w_v, w_out, batch, seq):
    d_model, kv = N_HEADS * HEAD_DIM, NSA_KV
    sec = lambda i: w_in[:, d_model + i * kv: d_model + (i + 1) * kv]
    w_main = jnp.concatenate([w_in[:, :d_model], sec(2), sec(4), sec(0), sec(1), sec(3), sec(5)], axis=1)
    proj = _norm_matmul(x2, g_norm, w_main, seq, tn=512, n_rope=3)
    w_gate = w_in[:, d_model + 6 * kv:].reshape(d_model, NSA_GROUPS, 3 * NSA_HPG)
    w_gate = jnp.pad(w_gate, ((0, 0), (0, 0), (0, LANE - 3 * NSA_HPG))).reshape(d_model, NSA_GROUPS * LANE)
    gates = _norm_matmul(x2, g_norm, w_gate, seq, tn=NSA_GROUPS * LANE, out_dtype=F32)

    col = lambda i: proj[:, d_model + i * kv: d_model + (i + 1) * kv]
    heads = lambda x: x.reshape(batch, seq, NSA_GROUPS, HEAD_DIM).transpose(0, 2, 1, 3)
    heads_t = lambda x: x.reshape(batch, seq, NSA_GROUPS, HEAD_DIM).transpose(0, 2, 3, 1)
    kc, vc = _nsa_compress(col(2), col(3), pos_k, pos_v, w_k, w_v, batch, seq)
    o = _nsa_attention(proj, kc, vc.transpose(0, 1, 3, 2), heads(col(0)), heads_t(col(4)), heads(col(1)),
                       heads_t(col(5)), gates, batch, seq)
    return _proj_resid(o, w_out, x2)


PEER_SEL_TT = 256
PEER_PAIRS = [(i, j) for i in range(PEER_TOP_K + 1) for j in range(PEER_TOP_K + 1)
              if (i + 1) * (j + 1) <= PEER_TOP_K + 1]


def _peer_select_kernel(q_ref, k1_ref, k2_ref, cnt_ref, e1_ref, rank_ref, e2_ref):
    q = q_ref[...]
    q_hi, q_lo = _split_bf16(q)
    nh, nr = PEER_HEADS, PEER_TOP_K + 1

    def scores(k_ref, off):
        k_hi, k_lo = _split_bf16(k_ref[...])
        out = []
        for h in range(nh):
            lanes = slice(off + h * PEER_HALF, off + (h + 1) * PEER_HALF)
            out.append(_dot_nt(k_hi, q_hi[:, lanes]) + _dot_nt(k_hi, q_lo[:, lanes])
                       + _dot_nt(k_lo, q_hi[:, lanes]))
        return out

    def top_values(s):
        vals, work = [], s
        for _ in range(nr):
            m = jnp.max(work, axis=0, keepdims=True)
            vals.append(m)
            work = jnp.where(work == m, LOW, work)
        return vals

    s1 = scores(k1_ref, 0)
    s2 = scores(k2_ref, nh * PEER_HALF)
    t1 = [top_values(s) for s in s1]
    t2 = [top_values(s) for s in s2]
    r1 = [jnp.concatenate([t1[h][r] for h in range(nh)], axis=0) for r in range(nr)]
    r2 = [jnp.concatenate([t2[h][r] for h in range(nh)], axis=0) for r in range(nr)]
    cands = [r1[i] + r2[j] for (i, j) in PEER_PAIRS]
    work, tops = list(cands), []
    for _ in range(nr):
        m = functools.reduce(jnp.maximum, work)
        tops.append(m)
        work = [jnp.where(c == m, LOW, c) for c in work]
    thr = 0.5 * (tops[PEER_TOP_K - 1] + tops[PEER_TOP_K])
    cmax = tops[0]
    z = functools.reduce(lambda a, b: a + b, [jnp.where(c >= thr, jnp.exp(c - cmax), 0.0) for c in cands])
    inv_z = 1.0 / z
    for h in range(nh):
        thr_h = thr[h:h + 1, :]
        cnt = jnp.zeros_like(s1[h])
        rank = jnp.zeros_like(s2[h])
        for r in range(nr):
            cnt = jnp.where(s1[h] + t2[h][r] >= thr_h, float(r + 1), cnt)
            rank = jnp.where(t2[h][r] > s2[h], float(r + 1), rank)
        cnt_ref[h] = cnt
        e1_ref[h] = jnp.exp(s1[h] - r1[0][h:h + 1, :])
        rank_ref[h] = rank.astype(BF16)
        e2_ref[h] = (jnp.exp(s2[h] - r2[0][h:h + 1, :]) * inv_z[h:h + 1, :]).astype(BF16)


def _peer_select(q, sub_keys):
    t = q.shape[0]
    tt = min(PEER_SEL_TT, t)
    out_spec = pl.BlockSpec((PEER_HEADS, PEER_KEYS, tt), lambda i: (0, 0, i))
    return pl.pallas_call(
        _peer_select_kernel,
        grid=(t // tt,),
        in_specs=[pl.BlockSpec((tt, q.shape[1]), lambda i: (i, 0)),
                  pl.BlockSpec((PEER_KEYS, PEER_HALF), lambda i: (0, 0)),
                  pl.BlockSpec((PEER_KEYS, PEER_HALF), lambda i: (0, 0))],
        out_specs=[out_spec] * 4,
        out_shape=[jax.ShapeDtypeStruct((PEER_HEADS, PEER_KEYS, t), dt) for dt in (F32, F32, BF16, BF16)],
        compiler_params=_cparams(("parallel",)),
        name="peer_select",
    )(q, sub_keys[0].astype(F32), sub_keys[1].astype(F32))


PEER_TT = 512
PEER_NE = 2048
PEER_SUB = 512


def _gelu(x):
    return 0.5 * x * (1.0 + lax.erf(x * np.float32(np.sqrt(0.5))))


def _peer_main_kernel(x_ref, g_ref, u_ref, vt_ref, cnt_ref, e1_ref, rank_ref, e2_ref, o_ref, xnt_ref, acc_ref):
    e = pl.program_id(1)
    ne = u_ref.shape[0]
    tt = x_ref.shape[0]
    sub = PEER_SUB
    n_sub = ne // sub

    @pl.when(e == 0)
    def _():
        x = x_ref[...]
        y = x * lax.rsqrt(jnp.mean(x * x, axis=-1, keepdims=True) + NORM_EPS)
        xnt_ref[...] = (y * g_ref[...]).T.astype(BF16)
        acc_ref[...] = jnp.zeros_like(acc_ref)

    def hidden(c):
        return _dot(u_ref[c * sub:(c + 1) * sub, :], xnt_ref[...])

    def weighted(c, h):
        act = _gelu(h).astype(BF16)
        parts = []
        for k in range(sub // PEER_KEYS):
            i1 = (e * n_sub + c) * (sub // PEER_KEYS) + k
            gsum = jnp.zeros((PEER_KEYS, tt), BF16)
            for hd in range(PEER_HEADS):
                cnt = cnt_ref[hd, pl.ds(i1, 1), :].astype(BF16)
                e1 = e1_ref[hd, pl.ds(i1, 1), :].astype(BF16)
                gsum = gsum + jnp.where(rank_ref[hd] < cnt, e2_ref[hd], jnp.zeros((), BF16)) * e1
            parts.append(gsum * act[k * PEER_KEYS:(k + 1) * PEER_KEYS, :])
        return jnp.concatenate(parts, axis=0) if len(parts) > 1 else parts[0]

    def values(c, w):
        return _dot(vt_ref[:, c * sub:(c + 1) * sub], w)

    hs = {0: hidden(0)}
    if n_sub > 1:
        hs[1] = hidden(1)
    total = None
    for c in range(n_sub):
        w = weighted(c, hs.pop(c))
        if c + 2 < n_sub:
            hs[c + 2] = hidden(c + 2)
        pv = values(c, w)
        total = pv if total is None else total + pv
    acc_ref[...] += total

    @pl.when(e == pl.num_programs(1) - 1)
    def _():
        o_ref[...] = x_ref[...] + acc_ref[...].T


def _peer_mixer(x2, g_norm, w_q, sub_keys, u, v, seq):
    t, d = x2.shape
    w_q2 = w_q.reshape(d, PEER_HEADS, 2, PEER_HALF).transpose(0, 2, 1, 3).reshape(d, 2 * PEER_HEADS * PEER_HALF)
    q = _norm_matmul3(x2, g_norm, w_q2)
    cnt, e1, rank, e2 = _peer_select(q, sub_keys)
    tt = min(PEER_TT, t)
    ne = PEER_NE
    n_exp = u.shape[0]
    tok_spec = pl.BlockSpec((PEER_HEADS, PEER_KEYS, tt), lambda i, e: (0, 0, i))
    return pl.pallas_call(
        _peer_main_kernel,
        grid=(t // tt, n_exp // ne),
        in_specs=[pl.BlockSpec((tt, d), lambda i, e: (i, 0)),
                  pl.BlockSpec((1, d), lambda i, e: (0, 0)),
                  pl.BlockSpec((ne, d), lambda i, e: (e, 0)),
                  pl.BlockSpec((d, ne), lambda i, e: (0, e)),
                  tok_spec, tok_spec, tok_spec, tok_spec],
        out_specs=pl.BlockSpec((tt, d), lambda i, e: (i, 0)),
        out_shape=jax.ShapeDtypeStruct((t, d), F32),
        scratch_shapes=[pltpu.VMEM((d, tt), BF16), pltpu.VMEM((d, tt), F32)],
        compiler_params=_cparams(("parallel", "arbitrary")),
        name="peer_main",
    )(x2, g_norm.reshape(1, d).astype(F32), u.astype(BF16), v.T.astype(BF16), cnt, e1, rank, e2)


def _sb_mixer(x2, g_norm, w_in, w_out, batch, seq):
    qkv = _norm_matmul(x2, g_norm, w_in, seq, tn=1024)
    return _proj_resid(_sb_attention(qkv, batch, seq), w_out, x2)


def _moba_mixer(x2, g_norm, w_in, w_out, batch, seq):
    d_model = N_HEADS * HEAD_DIM
    qkv = _norm_matmul(x2, g_norm, w_in, seq, tn=512, n_rope=2 * d_model // 512)
    return _proj_resid(_moba_attention(qkv, batch, seq), w_out, x2)


def kernel(x, l0_norm_mix, l0_sb_w_in, l0_sb_w_out, l0_norm_ffn, l0_peer_w_q, l0_peer_sub_keys, l0_peer_u, l0_peer_v, l1_norm_mix, l1_nsa_w_in, l1_nsa_cmp_pos_k, l1_nsa_cmp_pos_v, l1_nsa_cmp_w_k, l1_nsa_cmp_w_v, l1_nsa_w_out, l1_norm_ffn, l1_peer_w_q, l1_peer_sub_keys, l1_peer_u, l1_peer_v, l2_norm_mix, l2_moba_w_in, l2_moba_w_out, l2_norm_ffn, l2_peer_w_q, l2_peer_sub_keys, l2_peer_u, l2_peer_v, l3_norm_mix, l3_sb_w_in, l3_sb_w_out, l3_norm_ffn, l3_peer_w_q, l3_peer_sub_keys, l3_peer_u, l3_peer_v, final_norm):
    batch, seq, d = x.shape
    x2 = x.reshape(batch * seq, d)
    x2 = _sb_mixer(x2, l0_norm_mix, l0_sb_w_in, l0_sb_w_out, batch, seq)
    x2 = _peer_mixer(x2, l0_norm_ffn, l0_peer_w_q, l0_peer_sub_keys, l0_peer_u, l0_peer_v, seq)
    x2 = _nsa_mixer(x2, l1_norm_mix, l1_nsa_w_in, l1_nsa_cmp_pos_k, l1_nsa_cmp_pos_v, l1_nsa_cmp_w_k,
                    l1_nsa_cmp_w_v, l1_nsa_w_out, batch, seq)
    x2 = _peer_mixer(x2, l1_norm_ffn, l1_peer_w_q, l1_peer_sub_keys, l1_peer_u, l1_peer_v, seq)
    x2 = _moba_mixer(x2, l2_norm_mix, l2_moba_w_in, l2_moba_w_out, batch, seq)
    x2 = _peer_mixer(x2, l2_norm_ffn, l2_peer_w_q, l2_peer_sub_keys, l2_peer_u, l2_peer_v, seq)
    x2 = _sb_mixer(x2, l3_norm_mix, l3_sb_w_in, l3_sb_w_out, batch, seq)
    x2 = _peer_mixer(x2, l3_norm_ffn, l3_peer_w_q, l3_peer_sub_keys, l3_peer_u, l3_peer_v, seq)
    return _rmsnorm(x2, final_norm).reshape(batch, seq, d)
```

```python
import functools

import numpy as np
import jax
import jax.numpy as jnp
from jax import lax
from jax.experimental import pallas as pl
from jax.experimental.pallas import tpu as pltpu

F32 = jnp.float32
BF16 = jnp.bfloat16

N_HEADS = 16
HEAD_DIM = 64
ROPE_DIM = 16
ROPE_HALF = ROPE_DIM // 2
ROPE_THETA = 500000.0
NORM_EPS = 1e-6
NEG = -1e30
LOW = -3e38
ATT_SCALE = HEAD_DIM ** -0.5

NSA_GROUPS = 4
NSA_HPG = N_HEADS // NSA_GROUPS
NSA_KV = NSA_GROUPS * HEAD_DIM
NSA_CMP_LEN = 32
NSA_CMP_STRIDE = 16
NSA_SEL_LEN = 64
NSA_TOP_N = 16
NSA_WINDOW = 512
NSA_BONUS = 1e3
NSA_SEL_PAD = 128

MOBA_BLOCK = 256
MOBA_TOP_K = 3
MOBA_BLK_PAD = 128

PEER_HEADS = 8
PEER_KEYS = 128
PEER_TOP_K = 16
PEER_HALF = 64

SB_EXIT = -110.0

LANE = 128
VMEM_LIMIT = 56 << 20


def _cparams(sem, vmem=VMEM_LIMIT):
    return pltpu.CompilerParams(dimension_semantics=sem, vmem_limit_bytes=vmem)


def _dot(a, b):
    return jnp.dot(a, b, preferred_element_type=F32)


def _dot_nt(a, b):
    return lax.dot_general(a, b, (((1,), (1,)), ((), ())), preferred_element_type=F32)


def _split_bf16(x):
    hi = x.astype(BF16)
    lo = (x - hi.astype(F32)).astype(BF16)
    return hi, lo


def _topk_mask(vals, k):
    lane = lax.broadcasted_iota(jnp.int32, vals.shape, 1)
    work = vals
    sel = jnp.zeros(vals.shape, F32)
    for _ in range(k):
        m = jnp.max(work, axis=-1, keepdims=True)
        first = jnp.min(jnp.where(work == m, lane, LANE), axis=-1, keepdims=True)
        hit = lane == first
        sel = jnp.where(hit, 1.0, sel)
        work = jnp.where(hit, LOW, work)
    return sel


def _norm_matmul_kernel(*refs, n_rope, tn):
    if n_rope:
        x_ref, g_ref, w_ref, c_ref, sa_ref, sb_ref, o_ref, xn_ref = refs
    else:
        x_ref, g_ref, w_ref, o_ref, xn_ref = refs
    j = pl.program_id(1)

    @pl.when(j == 0)
    def _():
        x = x_ref[...]
        y = x * lax.rsqrt(jnp.mean(x * x, axis=-1, keepdims=True) + NORM_EPS)
        xn_ref[...] = (y * g_ref[...]).astype(BF16)

    acc = _dot(xn_ref[...], w_ref[...])
    if n_rope:
        @pl.when(j < n_rope)
        def _():
            r = (acc * c_ref[...] + pltpu.roll(acc, ROPE_HALF, 1) * sb_ref[...]
                 + pltpu.roll(acc, tn - ROPE_HALF, 1) * sa_ref[...])
            o_ref[...] = r.astype(o_ref.dtype)

        @pl.when(j >= n_rope)
        def _():
            o_ref[...] = acc.astype(o_ref.dtype)
    else:
        o_ref[...] = acc.astype(o_ref.dtype)


def _rope_tables(pos, width):
    inv = ROPE_THETA ** (-jnp.arange(0, ROPE_DIM, 2, dtype=F32) / ROPE_DIM)
    ang = pos.astype(F32)[:, None] * inv[None, :]
    cos, sin = jnp.cos(ang), jnp.sin(ang)
    n = pos.shape[0]
    rest = HEAD_DIM - ROPE_DIM
    c = jnp.concatenate([cos, cos, jnp.ones((n, rest), F32)], axis=1)
    sa = jnp.concatenate([-sin, jnp.zeros((n, ROPE_HALF + rest), F32)], axis=1)
    sb = jnp.concatenate([jnp.zeros((n, ROPE_HALF), F32), sin, jnp.zeros((n, rest), F32)], axis=1)
    reps = width // HEAD_DIM
    return jnp.tile(c, (1, reps)), jnp.tile(sa, (1, reps)), jnp.tile(sb, (1, reps))


def _norm_matmul(x2, g, w, seq, *, tn, n_rope=0, out_dtype=BF16, tt=1024):
    t, d = x2.shape
    n = w.shape[1]
    tt = min(tt, seq)
    grid = (t // tt, n // tn)
    in_specs = [pl.BlockSpec((tt, d), lambda i, j: (i, 0)),
                pl.BlockSpec((1, d), lambda i, j: (0, 0)),
                pl.BlockSpec((d, tn), lambda i, j: (0, j))]
    args = [x2, g.reshape(1, d).astype(F32), w.astype(BF16)]
    if n_rope:
        nper = seq // tt
        tabs = _rope_tables(jnp.arange(seq), tn)
        in_specs += [pl.BlockSpec((tt, tn), lambda i, j: (i % nper, 0))] * 3
        args += list(tabs)
    return pl.pallas_call(
        functools.partial(_norm_matmul_kernel, n_rope=n_rope, tn=tn),
        grid=grid, in_specs=in_specs,
        out_specs=pl.BlockSpec((tt, tn), lambda i, j: (i, j)),
        out_shape=jax.ShapeDtypeStruct((t, n), out_dtype),
        scratch_shapes=[pltpu.VMEM((tt, d), BF16)],
        compiler_params=_cparams(("parallel", "arbitrary")),
        name="norm_matmul",
    )(*args)


def _norm_matmul3_kernel(x_ref, g_ref, wh_ref, wl_ref, o_ref, xh_ref, xl_ref):
    @pl.when(pl.program_id(1) == 0)
    def _():
        x = x_ref[...]
        y = x * lax.rsqrt(jnp.mean(x * x, axis=-1, keepdims=True) + NORM_EPS) * g_ref[...]
        xh_ref[...], xl_ref[...] = _split_bf16(y)

    o_ref[...] = (_dot(xh_ref[...], wh_ref[...]) + _dot(xl_ref[...], wh_ref[...])
                  + _dot(xh_ref[...], wl_ref[...]))


def _norm_matmul3(x2, g, w, *, tn=512, tt=1024):
    t, d = x2.shape
    n = w.shape[1]
    tt = min(tt, t)
    w_hi, w_lo = _split_bf16(w.astype(F32))
    w_spec = pl.BlockSpec((d, tn), lambda i, j: (0, j))
    return pl.pallas_call(
        _norm_matmul3_kernel,
        grid=(t // tt, n // tn),
        in_specs=[pl.BlockSpec((tt, d), lambda i, j: (i, 0)),
                  pl.BlockSpec((1, d), lambda i, j: (0, 0)), w_spec, w_spec],
        out_specs=pl.BlockSpec((tt, tn), lambda i, j: (i, j)),
        out_shape=jax.ShapeDtypeStruct((t, n), F32),
        scratch_shapes=[pltpu.VMEM((tt, d), BF16), pltpu.VMEM((tt, d), BF16)],
        compiler_params=_cparams(("parallel", "arbitrary")),
        name="norm_matmul3",
    )(x2, g.reshape(1, d).astype(F32), w_hi, w_lo)


def _proj_resid_kernel(a_ref, w_ref, r_ref, o_ref):
    o_ref[...] = r_ref[...] + _dot(a_ref[...], w_ref[...])


def _proj_resid(a, w, resid, *, tt=1024, tn=1024):
    t, d = a.shape
    n = w.shape[1]
    tt = min(tt, t)
    return pl.pallas_call(
        _proj_resid_kernel,
        grid=(t // tt, n // tn),
        in_specs=[pl.BlockSpec((tt, d), lambda i, j: (i, 0)),
                  pl.BlockSpec((d, tn), lambda i, j: (0, j)),
                  pl.BlockSpec((tt, tn), lambda i, j: (i, j))],
        out_specs=pl.BlockSpec((tt, tn), lambda i, j: (i, j)),
        out_shape=jax.ShapeDtypeStruct((t, n), F32),
        compiler_params=_cparams(("parallel", "parallel")),
        name="proj_resid",
    )(a, w.astype(BF16), resid)


def _rmsnorm_kernel(x_ref, g_ref, o_ref):
    x = x_ref[...]
    y = x * lax.rsqrt(jnp.mean(x * x, axis=-1, keepdims=True) + NORM_EPS)
    o_ref[...] = y * g_ref[...]


def _rmsnorm(x2, g, *, tt=512):
    t, d = x2.shape
    tt = min(tt, t)
    return pl.pallas_call(
        _rmsnorm_kernel,
        grid=(t // tt,),
        in_specs=[pl.BlockSpec((tt, d), lambda i: (i, 0)), pl.BlockSpec((1, d), lambda i: (0, 0))],
        out_specs=pl.BlockSpec((tt, d), lambda i: (i, 0)),
        out_shape=jax.ShapeDtypeStruct((t, d), F32),
        compiler_params=_cparams(("parallel",)),
        name="final_rmsnorm",
    )(x2, g.reshape(1, d).astype(F32))


SB_TILE = 256
SB_FIRST = 2 * SB_TILE
SB_HEADS = 4


def _sb_kernel(q_ref, k_ref, v_ref, tri_ref, o_ref):
    tile = SB_TILE
    wide = SB_FIRST
    qi = pl.program_id(2)
    t0 = qi * tile

    nh = SB_HEADS
    head_lanes = [slice(hh * HEAD_DIM, (hh + 1) * HEAD_DIM) for hh in range(nh)]
    qs = [q_ref[:, lanes] * jnp.asarray(ATT_SCALE, BF16) for lanes in head_lanes]

    def step(start, width, r_sums, accs, past):
        tri = tri_ref[0:width, 0:width]
        zs = [_dot_nt(qs[hh], k_ref[pl.ds(start, width), lanes]) for hh, lanes in enumerate(head_lanes)]
        sps, lbs, splits = [], [], []
        for z in zs:
            sp = jnp.maximum(z, 0.0) + jnp.log(1.0 + jnp.exp(-jnp.abs(z)))
            lb = -sp if past is None else jnp.where(past, -sp, 0.0)
            sps.append(sp)
            lbs.append(lb)
            splits.append(_split_bf16(lb))
        suffixes = [_dot(hi, tri) + _dot(lo, tri) for hi, lo in splits]
        weights = []
        for hh in range(nh):
            a = jnp.exp((zs[hh] - sps[hh]) + suffixes[hh] + r_sums[hh])
            if past is not None:
                a = jnp.where(past, a, 0.0)
            weights.append(a.astype(BF16))
        accs = tuple(accs[hh] + _dot(weights[hh], v_ref[pl.ds(start, width), lanes])
                     for hh, lanes in enumerate(head_lanes))
        r_sums = tuple(r_sums[hh] + jnp.sum(lbs[hh], axis=-1, keepdims=True) for hh in range(nh))
        return r_sums, accs

    start0 = pl.multiple_of(jnp.maximum(t0 + tile - wide, 0), tile)
    kpos = start0 + lax.broadcasted_iota(jnp.int32, (tile, wide), 1)
    tpos = t0 + lax.broadcasted_iota(jnp.int32, (tile, wide), 0)
    zero_r = tuple(jnp.zeros((tile, 1), F32) for _ in range(nh))
    zero_acc = tuple(jnp.zeros((tile, HEAD_DIM), F32) for _ in range(nh))
    r1, acc1 = step(start0, wide, zero_r, zero_acc, kpos < tpos)

    def older(j, r_sums):
        alive = jnp.max(functools.reduce(jnp.maximum, r_sums)) > SB_EXIT
        return jnp.where(alive, j - 1, -1)

    def cond(c):
        return c[0] >= 0

    def body(c):
        j, r_sums, accs = c
        r_sums, accs = step(pl.multiple_of(j * tile, tile), tile, r_sums, accs, None)
        return older(j, r_sums), r_sums, accs

    _, _, accs = lax.while_loop(cond, body, (older(start0 // tile, r1), r1, acc1))
    for hh, lanes in enumerate(head_lanes):
        o_ref[:, lanes] = accs[hh].astype(o_ref.dtype)


def _sb_attention(qkv, batch, seq):
    tile = SB_TILE
    nq = seq // tile
    d_model = N_HEADS * HEAD_DIM
    width = SB_HEADS * HEAD_DIM
    ncol = d_model // width
    ii = np.arange(SB_FIRST)
    tri = jnp.asarray(ii[:, None] > ii[None, :], dtype=BF16)
    return pl.pallas_call(
        _sb_kernel,
        grid=(batch, ncol, nq),
        in_specs=[pl.BlockSpec((tile, width), lambda b, h, i: (b * nq + i, h)),
                  pl.BlockSpec((seq, width), lambda b, h, i: (b, ncol + h)),
                  pl.BlockSpec((seq, width), lambda b, h, i: (b, 2 * ncol + h)),
                  pl.BlockSpec((SB_FIRST, SB_FIRST), lambda b, h, i: (0, 0))],
        out_specs=pl.BlockSpec((tile, width), lambda b, h, i: (b * nq + i, h)),
        out_shape=jax.ShapeDtypeStruct((batch * seq, d_model), BF16),
        compiler_params=_cparams(("parallel", "parallel", "arbitrary")),
        name="sb_attention",
    )(qkv, qkv, qkv, tri)


def _flash_first(q, k, v, mask):
    s = jnp.where(mask, _dot_nt(q, k), NEG)
    m = jnp.max(s, axis=-1, keepdims=True)
    p = jnp.exp(s - m)
    return m, jnp.sum(p, axis=-1, keepdims=True), _dot(p.astype(BF16), v)


def _flash_next(q, k, v, mask, state):
    m, l, acc = state
    s = jnp.where(mask, _dot_nt(q, k), NEG)
    m_new = jnp.maximum(m, jnp.max(s, axis=-1, keepdims=True))
    p = jnp.exp(s - m_new)
    alpha = jnp.exp(m - m_new)
    l = alpha * l + jnp.sum(p, axis=-1, keepdims=True)
    acc = alpha * acc + _dot(p.astype(BF16), v)
    return m_new, l, acc


def _flash_out(state):
    _, l, acc = state
    return acc / l


def _flash_t_first(q, k, vt, bias_t):
    s = _dot_nt(k, q) + bias_t
    m = jnp.max(s, axis=0, keepdims=True)
    p = jnp.exp(s - m)
    return m, jnp.sum(p, axis=0, keepdims=True), _dot(vt, p.astype(BF16))


def _flash_t_next(q, k, vt, bias_t, state):
    m, l, acc_t = state
    s = _dot_nt(k, q) + bias_t
    m_new = jnp.maximum(m, jnp.max(s, axis=0, keepdims=True))
    p = jnp.exp(s - m_new)
    alpha = jnp.exp(m - m_new)
    l = alpha * l + jnp.sum(p, axis=0, keepdims=True)
    acc_t = alpha * acc_t + _dot(vt, p.astype(BF16))
    return m_new, l, acc_t


def _topk_mask_rows(vals, k):
    rowi = lax.broadcasted_iota(jnp.int32, vals.shape, 0)
    work = vals
    sel = jnp.zeros(vals.shape, F32)
    for _ in range(k):
        m = jnp.max(work, axis=0, keepdims=True)
        first = jnp.min(jnp.where(work == m, rowi, vals.shape[0]), axis=0, keepdims=True)
        hit = rowi == first
        sel = jnp.where(hit, 1.0, sel)
        work = jnp.where(hit, LOW, work)
    return sel


MOBA_WALK = 4
MOBA_HEADS = 8


def _moba_kernel(q_ref, k_ref, vt_ref, avg_ref, o_ref, kmean_ref, bias_ref):
    blk = MOBA_BLOCK
    wide = MOBA_WALK * blk
    qi = pl.program_id(2)

    @pl.when(qi == 0)
    def _():
        kmean_ref[...] = _dot(avg_ref[...], k_ref[...])

    key_i = lax.broadcasted_iota(jnp.int32, (blk, blk), 0)
    qry_i = lax.broadcasted_iota(jnp.int32, (blk, blk), 1)
    causal_bias = jnp.where(key_i <= qry_i, 0.0, NEG)
    blk_i = lax.broadcasted_iota(jnp.int32, (MOBA_BLK_PAD, blk), 0)
    start = pl.multiple_of(qi * blk, blk)
    head_rows = [slice(hh * HEAD_DIM, (hh + 1) * HEAD_DIM) for hh in range(MOBA_HEADS)]

    qs, states = [], []
    for hh, lanes in enumerate(head_rows):
        q_raw = q_ref[:, lanes]
        km_hi, km_lo = _split_bf16(kmean_ref[:, lanes])
        gate = _dot_nt(km_hi, q_raw) + _dot_nt(km_lo, q_raw)
        gate = jnp.where(blk_i < qi, gate, NEG)
        picked = (_topk_mask_rows(gate, MOBA_TOP_K) > 0.5) & (blk_i < qi)
        bias_ref[hh] = jnp.where(picked, 0.0, NEG)
        qs.append(q_raw * jnp.asarray(ATT_SCALE, BF16))
        states.append(_flash_t_first(qs[-1], k_ref[pl.ds(start, blk), lanes],
                                     vt_ref[lanes, pl.ds(start, blk)], causal_bias))

    def body(p, sts):
        s0 = pl.multiple_of(p * wide, wide)
        scores = [_dot_nt(k_ref[pl.ds(s0, wide), lanes], qs[hh]) for hh, lanes in enumerate(head_rows)]
        probs, nxt = [], []
        for hh in range(MOBA_HEADS):
            m, l, acc_t = sts[hh]
            bias = jnp.concatenate(
                [jnp.broadcast_to(bias_ref[hh, pl.ds(p * MOBA_WALK + w, 1), :], (blk, blk))
                 for w in range(MOBA_WALK)], axis=0)
            s = scores[hh] + bias
            m_new = jnp.maximum(m, jnp.max(s, axis=0, keepdims=True))
            pr = jnp.exp(s - m_new)
            alpha = jnp.exp(m - m_new)
            probs.append(pr.astype(BF16))
            nxt.append((m_new, alpha * l + jnp.sum(pr, axis=0, keepdims=True), alpha * acc_t))
        return tuple((m_new, l, acc_s + _dot(vt_ref[lanes, pl.ds(s0, wide)], probs[hh]))
                     for hh, (lanes, (m_new, l, acc_s)) in enumerate(zip(head_rows, nxt)))

    states = lax.fori_loop(0, (qi + MOBA_WALK - 1) // MOBA_WALK, body, tuple(states))
    for hh, lanes in enumerate(head_rows):
        _, l, acc_t = states[hh]
        o_ref[:, lanes] = (acc_t / l).T.astype(o_ref.dtype)


def _moba_attention(qkv, batch, seq):
    blk = MOBA_BLOCK
    wide = MOBA_WALK * blk
    nq = seq // blk
    d_model = N_HEADS * HEAD_DIM
    width = MOBA_HEADS * HEAD_DIM
    ncol = d_model // width
    avg = np.zeros((MOBA_BLK_PAD, seq), np.float32)
    for n in range(nq):
        avg[n, n * blk:(n + 1) * blk] = 1.0 / blk
    avg = jnp.asarray(avg, dtype=BF16)
    v_t = qkv[:, 2 * d_model:].reshape(batch, seq, d_model).transpose(0, 2, 1)
    return pl.pallas_call(
        _moba_kernel,
        grid=(batch, ncol, nq),
        in_specs=[pl.BlockSpec((blk, width), lambda b, h, i: (b * nq + i, h)),
                  pl.BlockSpec((seq, width), lambda b, h, i: (b, ncol + h)),
                  pl.BlockSpec((None, width, seq), lambda b, h, i: (b, h, 0)),
                  pl.BlockSpec((MOBA_BLK_PAD, seq), lambda b, h, i: (0, 0))],
        out_specs=pl.BlockSpec((blk, width), lambda b, h, i: (b * nq + i, h)),
        out_shape=jax.ShapeDtypeStruct((batch * seq, d_model), BF16),
        scratch_shapes=[pltpu.VMEM((MOBA_BLK_PAD, width), F32),
                        pltpu.VMEM((MOBA_HEADS, MOBA_BLK_PAD, blk), F32)],
        compiler_params=_cparams(("parallel", "parallel", "arbitrary")),
        name="moba_attention",
    )(qkv, qkv, v_t, avg)


def _nsa_compress_kernel(kr_ref, vr_ref, pk_ref, pv_ref, wk_ref, wv_ref, c_ref, sa_ref, sb_ref, kc_ref, vc_ref):
    nc = kr_ref.shape[0]
    rowi = lax.broadcasted_iota(jnp.int32, (nc, 1), 0)

    def windows(x_ref, pos_ref, w_ref):
        x = x_ref[...].astype(F32)
        lo = _dot((x + pos_ref[0:1, :]).astype(BF16), w_ref[0])
        hi = _dot((x + pos_ref[1:2, :]).astype(BF16), w_ref[1])
        hi_next = jnp.where(rowi < nc - 1, pltpu.roll(hi, nc - 1, 0), 0.0)
        return lo + hi_next

    kk = windows(kr_ref, pk_ref, wk_ref)
    d = HEAD_DIM
    kc = kk[:, 0:d] * c_ref[...] + kk[:, d:2 * d] * sb_ref[...] + kk[:, 2 * d:3 * d] * sa_ref[...]
    kc_ref[...] = kc.astype(kc_ref.dtype)
    vv = windows(vr_ref, pv_ref, wv_ref)
    vc_ref[...] = vv[:, 0:d].astype(vc_ref.dtype)


def _nsa_compress(k_cmp, v_cmp, pos_k, pos_v, w_k, w_v, batch, seq):
    g, d, st = NSA_GROUPS, HEAD_DIM, NSA_CMP_STRIDE
    nc = seq // st

    def chunks(x):
        return x.reshape(batch, nc, st, g, d).transpose(0, 3, 1, 2, 4).reshape(batch, g, nc, st * d)

    def weights(w):
        w3 = w.reshape(NSA_CMP_LEN, d, d)
        cat = jnp.concatenate([w3, jnp.roll(w3, ROPE_HALF, axis=2), jnp.roll(w3, -ROPE_HALF, axis=2)], axis=2)
        return cat.reshape(2, st * d, 3 * d).astype(BF16)

    def positions(p):
        return p.reshape(2, st * d).astype(F32)

    cmp_end = jnp.arange(nc) * st + NSA_CMP_LEN - 1
    tabs = _rope_tables(cmp_end, d)
    blk4 = pl.BlockSpec((None, None, nc, st * d), lambda b, gi: (b, gi, 0, 0))
    out4 = pl.BlockSpec((None, None, nc, d), lambda b, gi: (b, gi, 0, 0))
    const2 = lambda shape: pl.BlockSpec(shape, lambda b, gi: (0,) * len(shape))
    return pl.pallas_call(
        _nsa_compress_kernel,
        grid=(batch, g),
        in_specs=[blk4, blk4, const2((2, st * d)), const2((2, st * d)),
                  const2((2, st * d, 3 * d)), const2((2, st * d, 3 * d)),
                  const2((nc, d)), const2((nc, d)), const2((nc, d))],
        out_specs=[out4, out4],
        out_shape=[jax.ShapeDtypeStruct((batch, g, nc, d), BF16)] * 2,
        compiler_params=_cparams(("parallel", "parallel")),
        name="nsa_compress",
    )(chunks(k_cmp), chunks(v_cmp), positions(pos_k), positions(pos_v), weights(w_k), weights(w_v), *tabs)


NSA_TQ = 128
NSA_TK = 1024
NSA_CHAINS = 4


def _nsa_kernel(q_ref, kc_ref, vct_ref, ks_ref, vst_ref, kw_ref, vwt_ref, gate_ref, c2st_ref, o_ref, bias_ref):
    tq, tk, d, nh = NSA_TQ, NSA_TK, HEAD_DIM, NSA_HPG
    nql = nh * tq
    half = nql // NSA_CHAINS
    qi = pl.program_id(2)
    t0 = qi * tq
    nc = kc_ref.shape[0]

    def lanes4(x):
        return jnp.concatenate([x] * nh, axis=1)

    tpos1 = t0 + lax.broadcasted_iota(jnp.int32, (1, tq), 1)
    tpos = lanes4(tpos1)
    gates_t = jax.nn.sigmoid(gate_ref[...]).T

    def gate(branch):
        return jnp.concatenate([gates_t[3 * h + branch:3 * h + branch + 1, :] for h in range(nh)], axis=1)

    q = jnp.concatenate([q_ref[:, h * d:(h + 1) * d] for h in range(nh)], axis=0) * jnp.asarray(ATT_SCALE, BF16)

    kd = t0 // tk
    dstart = pl.multiple_of(kd * tk, tk)
    span = NSA_WINDOW + tq
    wstart = pl.multiple_of(jnp.maximum(t0 - NSA_WINDOW, 0), tq)
    s_cmp = _dot_nt(kc_ref[...], q)
    s_win = _dot_nt(kw_ref[pl.ds(wstart, span), :], q)
    s_diag = _dot_nt(ks_ref[pl.ds(dstart, tk), :], q)

    cmp_end = lax.broadcasted_iota(jnp.int32, (nc, nql), 0) * NSA_CMP_STRIDE + (NSA_CMP_LEN - 1)
    vis_c = cmp_end <= tpos
    s = jnp.where(vis_c, s_cmp, NEG)
    e = jnp.where(vis_c, jnp.exp(s - jnp.max(s, axis=0, keepdims=True)), 0.0)
    l = jnp.sum(e, axis=0, keepdims=True)
    p = e / jnp.where(l > 0.0, l, 1.0)
    out_t = gate(0) * _dot(vct_ref[...], p.astype(BF16))
    psum = functools.reduce(lambda a, b: a + b, [p[:, h * tq:(h + 1) * tq] for h in range(nh)])
    p_hi, p_lo = _split_bf16(psum)
    imp = _dot(c2st_ref[...], p_hi) + _dot(c2st_ref[...], p_lo)
    blk = lax.broadcasted_iota(jnp.int32, (NSA_SEL_PAD, tq), 0)
    own = tpos1 // NSA_SEL_LEN
    forced = (blk == 0) | (blk == own) | (blk == own - 1)
    imp = jnp.where(blk <= own, imp + jnp.where(forced, NSA_BONUS, 0.0), NEG)
    picked = _topk_mask_rows(imp, NSA_TOP_N) > 0.5
    bias_ref[...] = lanes4(jnp.where(picked, 0.0, NEG))

    per_tile = tk // NSA_SEL_LEN

    def tile_bias(kb):
        return jnp.concatenate(
            [jnp.broadcast_to(bias_ref[pl.ds(kb * per_tile + w, 1), :], (NSA_SEL_LEN, nql))
             for w in range(per_tile)], axis=0)

    wpos = wstart + lax.broadcasted_iota(jnp.int32, (span, nql), 0)
    s = jnp.where((wpos <= tpos) & (wpos > tpos - NSA_WINDOW), s_win, NEG)
    e = jnp.exp(s - jnp.max(s, axis=0, keepdims=True))
    out_t = out_t + (gate(2) / jnp.sum(e, axis=0, keepdims=True)) * _dot(vwt_ref[:, pl.ds(wstart, span)],
                                                                         e.astype(BF16))

    kpos = dstart + lax.broadcasted_iota(jnp.int32, (tk, nql), 0)
    s = jnp.where(kpos <= tpos, s_diag + tile_bias(kd), NEG)
    m = jnp.max(s, axis=0, keepdims=True)
    e = jnp.exp(s - m)
    l = jnp.sum(e, axis=0, keepdims=True)
    acc = _dot(vst_ref[:, pl.ds(dstart, tk)], e.astype(BF16))
    halves = [slice(i * half, (i + 1) * half) for i in range(NSA_CHAINS)]
    q_half = [q[hs] for hs in halves]
    states = tuple((m[:, hs], l[:, hs], acc[:, hs]) for hs in halves)

    def sel_body(kb, sts):
        start = pl.multiple_of(kb * tk, tk)
        k = ks_ref[pl.ds(start, tk), :]
        vt = vst_ref[:, pl.ds(start, tk)]
        bias = tile_bias(kb)
        scores = [_dot_nt(k, q_half[i]) for i in range(NSA_CHAINS)]
        probs, nxt = [], []
        for i, hs in enumerate(halves):
            m, l, acc = sts[i]
            s = scores[i] + bias[:, hs]
            m_new = jnp.maximum(m, jnp.max(s, axis=0, keepdims=True))
            pr = jnp.exp(s - m_new)
            alpha = jnp.exp(m - m_new)
            probs.append(pr.astype(BF16))
            nxt.append((m_new, alpha * l + jnp.sum(pr, axis=0, keepdims=True), alpha * acc))
        return tuple((m_new, l, acc + _dot(vt, probs[i])) for i, (m_new, l, acc) in enumerate(nxt))

    states = lax.fori_loop(0, kd, sel_body, states)
    sel_t = jnp.concatenate([st[2] / st[1] for st in states], axis=1)
    out_t = out_t + gate(1) * sel_t
    for h in range(nh):
        o_ref[:, h * d:(h + 1) * d] = out_t[:, h * tq:(h + 1) * tq].T.astype(o_ref.dtype)


def _nsa_attention(q_all, kc, vct, ks, vst, kw, vwt, gates, batch, seq):
    tq, tk, d, g = NSA_TQ, NSA_TK, HEAD_DIM, NSA_GROUPS
    nq = seq // tq
    nc = seq // NSA_CMP_STRIDE
    n_sel = seq // NSA_SEL_LEN
    d_model = N_HEADS * HEAD_DIM
    cs = np.arange(nc) * NSA_CMP_STRIDE
    ss = np.arange(n_sel) * NSA_SEL_LEN
    ov = np.clip(np.minimum(cs[:, None] + NSA_CMP_LEN, ss[None, :] + NSA_SEL_LEN)
                 - np.maximum(cs[:, None], ss[None, :]), 0, None) / NSA_CMP_LEN
    c2s = np.zeros((nc, NSA_SEL_PAD), np.float32)
    c2s[:, :n_sel] = ov
    c2s[nc - 1, :] = 0.0
    k_spec = lambda n: pl.BlockSpec((None, None, n, d), lambda b, gi, i: (b, gi, 0, 0))
    vt_spec = lambda n: pl.BlockSpec((None, None, d, n), lambda b, gi, i: (b, gi, 0, 0))
    return pl.pallas_call(
        _nsa_kernel,
        grid=(batch, g, nq),
        in_specs=[pl.BlockSpec((tq, NSA_HPG * d), lambda b, gi, i: (b * nq + i, gi)),
                  k_spec(nc), vt_spec(nc), k_spec(seq), vt_spec(seq), k_spec(seq), vt_spec(seq),
                  pl.BlockSpec((tq, LANE), lambda b, gi, i: (b * nq + i, gi)),
                  pl.BlockSpec((NSA_SEL_PAD, nc), lambda b, gi, i: (0, 0))],
        out_specs=pl.BlockSpec((tq, NSA_HPG * d), lambda b, gi, i: (b * nq + i, gi)),
        out_shape=jax.ShapeDtypeStruct((batch * seq, d_model), BF16),
        scratch_shapes=[pltpu.VMEM((NSA_SEL_PAD, NSA_HPG * tq), F32)],
        compiler_params=_cparams(("parallel", "parallel", "arbitrary")),
        name="nsa_attention",
    )(q_all, kc, vct, ks, vst, kw, vwt, gates, jnp.asarray(c2s.T, dtype=BF16))


def _nsa_mixer(x2, g_norm, w_in, pos_k, pos_v, w_k, w_v, w_out, batch, seq):
    d_model, kv = N_HEADS * HEAD_DIM, NSA_KV
    sec = lambda i: w_in[:, d_model + i * kv: d_model + (i + 1) * kv]
    w_main = jnp.concatenate([w_in[:, :d_model], sec(2), sec(4), sec(0), sec(1), sec(3), sec(5)], axis=1)
    proj = _norm_matmul(x2, g_norm, w_main, seq, tn=512, n_rope=3)
    w_gate = w_in[:, d_model + 6 * kv:].reshape(d_model, NSA_GROUPS, 3 * NSA_HPG)
    w_gate = jnp.pad(w_gate, ((0, 0), (0, 0), (0, LANE - 3 * NSA_HPG))).reshape(d_model, NSA_GROUPS * LANE)
    gates = _norm_matmul(x2, g_norm, w_gate, seq, tn=NSA_GROUPS * LANE, out_dtype=F32)

    col = lambda i: proj[:, d_model + i * kv: d_model + (i + 1) * kv]
    heads = lambda x: x.reshape(batch, seq, NSA_GROUPS, HEAD_DIM).transpose(0, 2, 1, 3)
    heads_t = lambda x: x.reshape(batch, seq, NSA_GROUPS, HEAD_DIM).transpose(0, 2, 3, 1)
    kc, vc = _nsa_compress(col(2), col(3), pos_k, pos_v, w_k, w_v, batch, seq)
    o = _nsa_attention(proj, kc, vc.transpose(0, 1, 3, 2), heads(col(0)), heads_t(col(4)), heads(col(1)),
                       heads_t(col(5)), gates, batch, seq)
    return _proj_resid(o, w_out, x2)


PEER_SEL_TT = 256
PEER_PAIRS = [(i, j) for i in range(PEER_TOP_K + 1) for j in range(PEER_TOP_K + 1)
              if (i + 1) * (j + 1) <= PEER_TOP_K + 1]


def _peer_select_kernel(q_ref, k1_ref, k2_ref, cnt_ref, e1_ref, rank_ref, e2_ref):
    q = q_ref[...]
    q_hi, q_lo = _split_bf16(q)
    nh, nr = PEER_HEADS, PEER_TOP_K + 1

    def scores(k_ref, off):
        k_hi, k_lo = _split_bf16(k_ref[...])
        out = []
        for h in range(nh):
            lanes = slice(off + h * PEER_HALF, off + (h + 1) * PEER_HALF)
            out.append(_dot_nt(k_hi, q_hi[:, lanes]) + _dot_nt(k_hi, q_lo[:, lanes])
                       + _dot_nt(k_lo, q_hi[:, lanes]))
        return out

    def top_values(s):
        vals, work = [], s
        for _ in range(nr):
            m = jnp.max(work, axis=0, keepdims=True)
            vals.append(m)
            work = jnp.where(work == m, LOW, work)
        return vals

    s1 = scores(k1_ref, 0)
    s2 = scores(k2_ref, nh * PEER_HALF)
    t1 = [top_values(s) for s in s1]
    t2 = [top_values(s) for s in s2]
    r1 = [jnp.concatenate([t1[h][r] for h in range(nh)], axis=0) for r in range(nr)]
    r2 = [jnp.concatenate([t2[h][r] for h in range(nh)], axis=0) for r in range(nr)]
    cands = [r1[i] + r2[j] for (i, j) in PEER_PAIRS]
    work, tops = list(cands), []
    for _ in range(nr):
        m = functools.reduce(jnp.maximum, work)
        tops.append(m)
        work = [jnp.where(c == m, LOW, c) for c in work]
    thr = 0.5 * (tops[PEER_TOP_K - 1] + tops[PEER_TOP_K])
    cmax = tops[0]
    z = functools.reduce(lambda a, b: a + b, [jnp.where(c >= thr, jnp.exp(c - cmax), 0.0) for c in cands])
    inv_z = 1.0 / z
    for h in range(nh):
        thr_h = thr[h:h + 1, :]
        cnt = jnp.zeros_like(s1[h])
        rank = jnp.zeros_like(s2[h])
        for r in range(nr):
            cnt = jnp.where(s1[h] + t2[h][r] >= thr_h, float(r + 1), cnt)
            rank = jnp.where(t2[h][r] > s2[h], float(r + 1), rank)
        cnt_ref[h] = cnt
        e1_ref[h] = jnp.exp(s1[h] - r1[0][h:h + 1, :])
        rank_ref[h] = rank.astype(BF16)
        e2_ref[h] = (jnp.exp(s2[h] - r2[0][h:h + 1, :]) * inv_z[h:h + 1, :]).astype(BF16)


def _peer_select(q, sub_keys):
    t = q.shape[0]
    tt = min(PEER_SEL_TT, t)
    out_spec = pl.BlockSpec((PEER_HEADS, PEER_KEYS, tt), lambda i: (0, 0, i))
    return pl.pallas_call(
        _peer_select_kernel,
        grid=(t // tt,),
        in_specs=[pl.BlockSpec((tt, q.shape[1]), lambda i: (i, 0)),
                  pl.BlockSpec((PEER_KEYS, PEER_HALF), lambda i: (0, 0)),
                  pl.BlockSpec((PEER_KEYS, PEER_HALF), lambda i: (0, 0))],
        out_specs=[out_spec] * 4,
        out_shape=[jax.ShapeDtypeStruct((PEER_HEADS, PEER_KEYS, t), dt) for dt in (F32, F32, BF16, BF16)],
        compiler_params=_cparams(("parallel",)),
        name="peer_select",
    )(q, sub_keys[0].astype(F32), sub_keys[1].astype(F32))


PEER_TT = 512
PEER_NE = 2048
PEER_SUB = 512


def _gelu(x):
    return 0.5 * x * (1.0 + lax.erf(x * np.float32(np.sqrt(0.5))))


def _peer_main_kernel(x_ref, g_ref, u_ref, vt_ref, cnt_ref, e1_ref, rank_ref, e2_ref, o_ref, xnt_ref, acc_ref):
    e = pl.program_id(1)
    ne = u_ref.shape[0]
    tt = x_ref.shape[0]
    sub = PEER_SUB
    n_sub = ne // sub

    @pl.when(e == 0)
    def _():
        x = x_ref[...]
        y = x * lax.rsqrt(jnp.mean(x * x, axis=-1, keepdims=True) + NORM_EPS)
        xnt_ref[...] = (y * g_ref[...]).T.astype(BF16)
        acc_ref[...] = jnp.zeros_like(acc_ref)

    def hidden(c):
        return _dot(u_ref[c * sub:(c + 1) * sub, :], xnt_ref[...])

    def weighted(c, h):
        act = _gelu(h).astype(BF16)
        parts = []
        for k in range(sub // PEER_KEYS):
            i1 = (e * n_sub + c) * (sub // PEER_KEYS) + k
            gsum = jnp.zeros((PEER_KEYS, tt), BF16)
            for hd in range(PEER_HEADS):
                cnt = cnt_ref[hd, pl.ds(i1, 1), :].astype(BF16)
                e1 = e1_ref[hd, pl.ds(i1, 1), :].astype(BF16)
                gsum = gsum + jnp.where(rank_ref[hd] < cnt, e2_ref[hd], jnp.zeros((), BF16)) * e1
            parts.append(gsum * act[k * PEER_KEYS:(k + 1) * PEER_KEYS, :])
        return jnp.concatenate(parts, axis=0) if len(parts) > 1 else parts[0]

    def values(c, w):
        return _dot(vt_ref[:, c * sub:(c + 1) * sub], w)

    hs = {0: hidden(0)}
    if n_sub > 1:
        hs[1] = hidden(1)
    total = None
    for c in range(n_sub):
        w = weighted(c, hs.pop(c))
        if c + 2 < n_sub:
            hs[c + 2] = hidden(c + 2)
        pv = values(c, w)
        total = pv if total is None else total + pv
    acc_ref[...] += total

    @pl.when(e == pl.num_programs(1) - 1)
    def _():
        o_ref[...] = x_ref[...] + acc_ref[...].T


def _peer_mixer(x2, g_norm, w_q, sub_keys, u, v, seq):
    t, d = x2.shape
    w_q2 = w_q.reshape(d, PEER_HEADS, 2, PEER_HALF).transpose(0, 2, 1, 3).reshape(d, 2 * PEER_HEADS * PEER_HALF)
    q = _norm_matmul3(x2, g_norm, w_q2)
    cnt, e1, rank, e2 = _peer_select(q, sub_keys)
    tt = min(PEER_TT, t)
    ne = PEER_NE
    n_exp = u.shape[0]
    tok_spec = pl.BlockSpec((PEER_HEADS, PEER_KEYS, tt), lambda i, e: (0, 0, i))
    return pl.pallas_call(
        _peer_main_kernel,
        grid=(t // tt, n_exp // ne),
        in_specs=[pl.BlockSpec((tt, d), lambda i, e: (i, 0)),
                  pl.BlockSpec((1, d), lambda i, e: (0, 0)),
                  pl.BlockSpec((ne, d), lambda i, e: (e, 0)),
                  pl.BlockSpec((d, ne), lambda i, e: (0, e)),
                  tok_spec, tok_spec, tok_spec, tok_spec],
        out_specs=pl.BlockSpec((tt, d), lambda i, e: (i, 0)),
        out_shape=jax.ShapeDtypeStruct((t, d), F32),
        scratch_shapes=[pltpu.VMEM((d, tt), BF16), pltpu.VMEM((d, tt), F32)],
        compiler_params=_cparams(("parallel", "arbitrary")),
        name="peer_main",
    )(x2, g_norm.reshape(1, d).astype(F32), u.astype(BF16), v.T.astype(BF16), cnt, e1, rank, e2)


def _sb_mixer(x2, g_norm, w_in, w_out, batch, seq):
    qkv = _norm_matmul(x2, g_norm, w_in, seq, tn=1024)
    return _proj_resid(_sb_attention(qkv, batch, seq), w_out, x2)


def _moba_mixer(x2, g_norm, w_in, w_out, batch, seq):
    d_model = N_HEADS * HEAD_DIM
    qkv = _norm_matmul(x2, g_norm, w_in, seq, tn=512, n_rope=2 * d_model // 512)
    return _proj_resid(_moba_attention(qkv, batch, seq), w_out, x2)


def kernel(x, l0_norm_mix, l0_sb_w_in, l0_sb_w_out, l0_norm_ffn, l0_peer_w_q, l0_peer_sub_keys, l0_peer_u, l0_peer_v, l1_norm_mix, l1_nsa_w_in, l1_nsa_cmp_pos_k, l1_nsa_cmp_pos_v, l1_nsa_cmp_w_k, l1_nsa_cmp_w_v, l1_nsa_w_out, l1_norm_ffn, l1_peer_w_q, l1_peer_sub_keys, l1_peer_u, l1_peer_v, l2_norm_mix, l2_moba_w_in, l2_moba_w_out, l2_norm_ffn, l2_peer_w_q, l2_peer_sub_keys, l2_peer_u, l2_peer_v, l3_norm_mix, l3_sb_w_in, l3_sb_w_out, l3_norm_ffn, l3_peer_w_q, l3_peer_sub_keys, l3_peer_u, l3_peer_v, final_norm):
    batch, seq, d = x.shape
    x2 = x.reshape(batch * seq, d)
    x2 = _sb_mixer(x2, l0_norm_mix, l0_sb_w_in, l0_sb_w_out, batch, seq)
    x2 = _peer_mixer(x2, l0_norm_ffn, l0_peer_w_q, l0_peer_sub_keys, l0_peer_u, l0_peer_v, seq)
    x2 = _nsa_mixer(x2, l1_norm_mix, l1_nsa_w_in, l1_nsa_cmp_pos_k, l1_nsa_cmp_pos_v, l1_nsa_cmp_w_k,
                    l1_nsa_cmp_w_v, l1_nsa_w_out, batch, seq)
    x2 = _peer_mixer(x2, l1_norm_ffn, l1_peer_w_q, l1_peer_sub_keys, l1_peer_u, l1_peer_v, seq)
    x2 = _moba_mixer(x2, l2_norm_mix, l2_moba_w_in, l2_moba_w_out, batch, seq)
    x2 = _peer_mixer(x2, l2_norm_ffn, l2_peer_w_q, l2_peer_sub_keys, l2_peer_u, l2_peer_v, seq)
    x2 = _sb_mixer(x2, l3_norm_mix, l3_sb_w_in, l3_sb_w_out, batch, seq)
    x2 = _peer_mixer(x2, l3_norm_ffn, l3_peer_w_q, l3_peer_sub_keys, l3_peer_u, l3_peer_v, seq)
    return _rmsnorm(x2, final_norm).reshape(batch, seq, d)
```

```python
import functools

import numpy as np
import jax
import jax.numpy as jnp
from jax import lax
from jax.experimental import pallas as pl
from jax.experimental.pallas import tpu as pltpu

F32 = jnp.float32
BF16 = jnp.bfloat16

N_HEADS = 16
HEAD_DIM = 64
ROPE_DIM = 16
ROPE_HALF = ROPE_DIM // 2
ROPE_THETA = 500000.0
NORM_EPS = 1e-6
NEG = -1e30
LOW = -3e38
ATT_SCALE = HEAD_DIM ** -0.5

NSA_GROUPS = 4
NSA_HPG = N_HEADS // NSA_GROUPS
NSA_KV = NSA_GROUPS * HEAD_DIM
NSA_CMP_LEN = 32
NSA_CMP_STRIDE = 16
NSA_SEL_LEN = 64
NSA_TOP_N = 16
NSA_WINDOW = 512
NSA_BONUS = 1e3
NSA_SEL_PAD = 128

MOBA_BLOCK = 256
MOBA_TOP_K = 3
MOBA_BLK_PAD = 128

PEER_HEADS = 8
PEER_KEYS = 128
PEER_TOP_K = 16
PEER_HALF = 64

SB_EXIT = -110.0

LANE = 128
VMEM_LIMIT = 56 << 20


def _cparams(sem, vmem=VMEM_LIMIT):
    return pltpu.CompilerParams(dimension_semantics=sem, vmem_limit_bytes=vmem)


def _dot(a, b):
    return jnp.dot(a, b, preferred_element_type=F32)


def _dot_nt(a, b):
    return lax.dot_general(a, b, (((1,), (1,)), ((), ())), preferred_element_type=F32)


def _split_bf16(x):
    hi = x.astype(BF16)
    lo = (x - hi.astype(F32)).astype(BF16)
    return hi, lo


def _topk_mask(vals, k):
    lane = lax.broadcasted_iota(jnp.int32, vals.shape, 1)
    work = vals
    sel = jnp.zeros(vals.shape, F32)
    for _ in range(k):
        m = jnp.max(work, axis=-1, keepdims=True)
        first = jnp.min(jnp.where(work == m, lane, LANE), axis=-1, keepdims=True)
        hit = lane == first
        sel = jnp.where(hit, 1.0, sel)
        work = jnp.where(hit, LOW, work)
    return sel


def _norm_matmul_kernel(*refs, n_rope, tn):
    if n_rope:
        x_ref, g_ref, w_ref, c_ref, sa_ref, sb_ref, o_ref, xn_ref = refs
    else:
        x_ref, g_ref, w_ref, o_ref, xn_ref = refs
    j = pl.program_id(1)

    @pl.when(j == 0)
    def _():
        x = x_ref[...]
        y = x * lax.rsqrt(jnp.mean(x * x, axis=-1, keepdims=True) + NORM_EPS)
        xn_ref[...] = (y * g_ref[...]).astype(BF16)

    acc = _dot(xn_ref[...], w_ref[...])
    if n_rope:
        @pl.when(j < n_rope)
        def _():
            r = (acc * c_ref[...] + pltpu.roll(acc, ROPE_HALF, 1) * sb_ref[...]
                 + pltpu.roll(acc, tn - ROPE_HALF, 1) * sa_ref[...])
            o_ref[...] = r.astype(o_ref.dtype)

        @pl.when(j >= n_rope)
        def _():
            o_ref[...] = acc.astype(o_ref.dtype)
    else:
        o_ref[...] = acc.astype(o_ref.dtype)


def _rope_tables(pos, width):
    inv = ROPE_THETA ** (-jnp.arange(0, ROPE_DIM, 2, dtype=F32) / ROPE_DIM)
    ang = pos.astype(F32)[:, None] * inv[None, :]
    cos, sin = jnp.cos(ang), jnp.sin(ang)
    n = pos.shape[0]
    rest = HEAD_DIM - ROPE_DIM
    c = jnp.concatenate([cos, cos, jnp.ones((n, rest), F32)], axis=1)
    sa = jnp.concatenate([-sin, jnp.zeros((n, ROPE_HALF + rest), F32)], axis=1)
    sb = jnp.concatenate([jnp.zeros((n, ROPE_HALF), F32), sin, jnp.zeros((n, rest), F32)], axis=1)
    reps = width // HEAD_DIM
    return jnp.tile(c, (1, reps)), jnp.tile(sa, (1, reps)), jnp.tile(sb, (1, reps))


def _norm_matmul(x2, g, w, seq, *, tn, n_rope=0, out_dtype=BF16, tt=1024):
    t, d = x2.shape
    n = w.shape[1]
    tt = min(tt, seq)
    grid = (t // tt, n // tn)
    in_specs = [pl.BlockSpec((tt, d), lambda i, j: (i, 0)),
                pl.BlockSpec((1, d), lambda i, j: (0, 0)),
                pl.BlockSpec((d, tn), lambda i, j: (0, j))]
    args = [x2, g.reshape(1, d).astype(F32), w.astype(BF16)]
    if n_rope:
        nper = seq // tt
        tabs = _rope_tables(jnp.arange(seq), tn)
        in_specs += [pl.BlockSpec((tt, tn), lambda i, j: (i % nper, 0))] * 3
        args += list(tabs)
    return pl.pallas_call(
        functools.partial(_norm_matmul_kernel, n_rope=n_rope, tn=tn),
        grid=grid, in_specs=in_specs,
        out_specs=pl.BlockSpec((tt, tn), lambda i, j: (i, j)),
        out_shape=jax.ShapeDtypeStruct((t, n), out_dtype),
        scratch_shapes=[pltpu.VMEM((tt, d), BF16)],
        compiler_params=_cparams(("parallel", "arbitrary")),
        name="norm_matmul",
    )(*args)


def _norm_matmul3_kernel(x_ref, g_ref, wh_ref, wl_ref, o_ref, xh_ref, xl_ref):
    @pl.when(pl.program_id(1) == 0)
    def _():
        x = x_ref[...]
        y = x * lax.rsqrt(jnp.mean(x * x, axis=-1, keepdims=True) + NORM_EPS) * g_ref[...]
        xh_ref[...], xl_ref[...] = _split_bf16(y)

    o_ref[...] = (_dot(xh_ref[...], wh_ref[...]) + _dot(xl_ref[...], wh_ref[...])
                  + _dot(xh_ref[...], wl_ref[...]))


def _norm_matmul3(x2, g, w, *, tn=512, tt=1024):
    t, d = x2.shape
    n = w.shape[1]
    tt = min(tt, t)
    w_hi, w_lo = _split_bf16(w.astype(F32))
    w_spec = pl.BlockSpec((d, tn), lambda i, j: (0, j))
    return pl.pallas_call(
        _norm_matmul3_kernel,
        grid=(t // tt, n // tn),
        in_specs=[pl.BlockSpec((tt, d), lambda i, j: (i, 0)),
                  pl.BlockSpec((1, d), lambda i, j: (0, 0)), w_spec, w_spec],
        out_specs=pl.BlockSpec((tt, tn), lambda i, j: (i, j)),
        out_shape=jax.ShapeDtypeStruct((t, n), F32),
        scratch_shapes=[pltpu.VMEM((tt, d), BF16), pltpu.VMEM((tt, d), BF16)],
        compiler_params=_cparams(("parallel", "arbitrary")),
        name="norm_matmul3",
    )(x2, g.reshape(1, d).astype(F32), w_hi, w_lo)


def _proj_resid_kernel(a_ref, w_ref, r_ref, o_ref):
    o_ref[...] = r_ref[...] + _dot(a_ref[...], w_ref[...])


def _proj_resid(a, w, resid, *, tt=1024, tn=1024):
    t, d = a.shape
    n = w.shape[1]
    tt = min(tt, t)
    return pl.pallas_call(
        _proj_resid_kernel,
        grid=(t // tt, n // tn),
        in_specs=[pl.BlockSpec((tt, d), lambda i, j: (i, 0)),
                  pl.BlockSpec((d, tn), lambda i, j: (0, j)),
                  pl.BlockSpec((tt, tn), lambda i, j: (i, j))],
        out_specs=pl.BlockSpec((tt, tn), lambda i, j: (i, j)),
        out_shape=jax.ShapeDtypeStruct((t, n), F32),
        compiler_params=_cparams(("parallel", "parallel")),
        name="proj_resid",
    )(a, w.astype(BF16), resid)


def _rmsnorm_kernel(x_ref, g_ref, o_ref):
    x = x_ref[...]
    y = x * lax.rsqrt(jnp.mean(x * x, axis=-1, keepdims=True) + NORM_EPS)
    o_ref[...] = y * g_ref[...]


def _rmsnorm(x2, g, *, tt=512):
    t, d = x2.shape
    tt = min(tt, t)
    return pl.pallas_call(
        _rmsnorm_kernel,
        grid=(t // tt,),
        in_specs=[pl.BlockSpec((tt, d), lambda i: (i, 0)), pl.BlockSpec((1, d), lambda i: (0, 0))],
        out_specs=pl.BlockSpec((tt, d), lambda i: (i, 0)),
        out_shape=jax.ShapeDtypeStruct((t, d), F32),
        compiler_params=_cparams(("parallel",)),
        name="final_rmsnorm",
    )(x2, g.reshape(1, d).astype(F32))


SB_TILE = 256
SB_FIRST = 2 * SB_TILE
SB_HEADS = 4


def _sb_kernel(q_ref, k_ref, v_ref, tri_ref, o_ref):
    tile = SB_TILE
    wide = SB_FIRST
    qi = pl.program_id(2)
    t0 = qi * tile

    nh = SB_HEADS
    head_lanes = [slice(hh * HEAD_DIM, (hh + 1) * HEAD_DIM) for hh in range(nh)]
    qs = [q_ref[:, lanes] * jnp.asarray(ATT_SCALE, BF16) for lanes in head_lanes]

    def step(start, width, r_sums, accs, past):
        tri = tri_ref[0:width, 0:width]
        zs = [_dot_nt(qs[hh], k_ref[pl.ds(start, width), lanes]) for hh, lanes in enumerate(head_lanes)]
        sps, lbs, splits = [], [], []
        for z in zs:
            sp = jnp.maximum(z, 0.0) + jnp.log(1.0 + jnp.exp(-jnp.abs(z)))
            lb = -sp if past is None else jnp.where(past, -sp, 0.0)
            sps.append(sp)
            lbs.append(lb)
            splits.append(_split_bf16(lb))
        suffixes = [_dot(hi, tri) + _dot(lo, tri) for hi, lo in splits]
        weights = []
        for hh in range(nh):
            a = jnp.exp((zs[hh] - sps[hh]) + suffixes[hh] + r_sums[hh])
            if past is not None:
                a = jnp.where(past, a, 0.0)
            weights.append(a.astype(BF16))
        accs = tuple(accs[hh] + _dot(weights[hh], v_ref[pl.ds(start, width), lanes])
                     for hh, lanes in enumerate(head_lanes))
        r_sums = tuple(r_sums[hh] + jnp.sum(lbs[hh], axis=-1, keepdims=True) for hh in range(nh))
        return r_sums, accs

    start0 = pl.multiple_of(jnp.maximum(t0 + tile - wide, 0), tile)
    kpos = start0 + lax.broadcasted_iota(jnp.int32, (tile, wide), 1)
    tpos = t0 + lax.broadcasted_iota(jnp.int32, (tile, wide), 0)
    zero_r = tuple(jnp.zeros((tile, 1), F32) for _ in range(nh))
    zero_acc = tuple(jnp.zeros((tile, HEAD_DIM), F32) for _ in range(nh))
    r1, acc1 = step(start0, wide, zero_r, zero_acc, kpos < tpos)

    def older(j, r_sums):
        alive = jnp.max(functools.reduce(jnp.maximum, r_sums)) > SB_EXIT
        return jnp.where(alive, j - 1, -1)

    def cond(c):
        return c[0] >= 0

    def body(c):
        j, r_sums, accs = c
        r_sums, accs = step(pl.multiple_of(j * tile, tile), tile, r_sums, accs, None)
        return older(j, r_sums), r_sums, accs

    _, _, accs = lax.while_loop(cond, body, (older(start0 // tile, r1), r1, acc1))
    for hh, lanes in enumerate(head_lanes):
        o_ref[:, lanes] = accs[hh].astype(o_ref.dtype)


def _sb_attention(qkv, batch, seq):
    tile = SB_TILE
    nq = seq // tile
    d_model = N_HEADS * HEAD_DIM
    width = SB_HEADS * HEAD_DIM
    ncol = d_model // width
    ii = np.arange(SB_FIRST)
    tri = jnp.asarray(ii[:, None] > ii[None, :], dtype=BF16)
    return pl.pallas_call(
        _sb_kernel,
        grid=(batch, ncol, nq),
        in_specs=[pl.BlockSpec((tile, width), lambda b, h, i: (b * nq + i, h)),
                  pl.BlockSpec((seq, width), lambda b, h, i: (b, ncol + h)),
                  pl.BlockSpec((seq, width), lambda b, h, i: (b, 2 * ncol + h)),
                  pl.BlockSpec((SB_FIRST, SB_FIRST), lambda b, h, i: (0, 0))],
        out_specs=pl.BlockSpec((tile, width), lambda b, h, i: (b * nq + i, h)),
        out_shape=jax.ShapeDtypeStruct((batch * seq, d_model), BF16),
        compiler_params=_cparams(("parallel", "parallel", "arbitrary")),
        name="sb_attention",
    )(qkv, qkv, qkv, tri)


def _flash_first(q, k, v, mask):
    s = jnp.where(mask, _dot_nt(q, k), NEG)
    m = jnp.max(s, axis=-1, keepdims=True)
    p = jnp.exp(s - m)
    return m, jnp.sum(p, axis=-1, keepdims=True), _dot(p.astype(BF16), v)


def _flash_next(q, k, v, mask, state):
    m, l, acc = state
    s = jnp.where(mask, _dot_nt(q, k), NEG)
    m_new = jnp.maximum(m, jnp.max(s, axis=-1, keepdims=True))
    p = jnp.exp(s - m_new)
    alpha = jnp.exp(m - m_new)
    l = alpha * l + jnp.sum(p, axis=-1, keepdims=True)
    acc = alpha * acc + _dot(p.astype(BF16), v)
    return m_new, l, acc


def _flash_out(state):
    _, l, acc = state
    return acc / l


def _flash_t_first(q, k, vt, bias_t):
    s = _dot_nt(k, q) + bias_t
    m = jnp.max(s, axis=0, keepdims=True)
    p = jnp.exp(s - m)
    return m, jnp.sum(p, axis=0, keepdims=True), _dot(vt, p.astype(BF16))


def _flash_t_next(q, k, vt, bias_t, state):
    m, l, acc_t = state
    s = _dot_nt(k, q) + bias_t
    m_new = jnp.maximum(m, jnp.max(s, axis=0, keepdims=True))
    p = jnp.exp(s - m_new)
    alpha = jnp.exp(m - m_new)
    l = alpha * l + jnp.sum(p, axis=0, keepdims=True)
    acc_t = alpha * acc_t + _dot(vt, p.astype(BF16))
    return m_new, l, acc_t


def _topk_mask_rows(vals, k):
    rowi = lax.broadcasted_iota(jnp.int32, vals.shape, 0)
    work = vals
    sel = jnp.zeros(vals.shape, F32)
    for _ in range(k):
        m = jnp.max(work, axis=0, keepdims=True)
        first = jnp.min(jnp.where(work == m, rowi, vals.shape[0]), axis=0, keepdims=True)
        hit = rowi == first
        sel = jnp.where(hit, 1.0, sel)
        work = jnp.where(hit, LOW, work)
    return sel


MOBA_WALK = 4
MOBA_HEADS = 8


def _moba_kernel(q_ref, k_ref, vt_ref, avg_ref, o_ref, kmean_ref, bias_ref):
    blk = MOBA_BLOCK
    wide = MOBA_WALK * blk
    qi = pl.program_id(2)

    @pl.when(qi == 0)
    def _():
        kmean_ref[...] = _dot(avg_ref[...], k_ref[...])

    key_i = lax.broadcasted_iota(jnp.int32, (blk, blk), 0)
    qry_i = lax.broadcasted_iota(jnp.int32, (blk, blk), 1)
    causal_bias = jnp.where(key_i <= qry_i, 0.0, NEG)
    blk_i = lax.broadcasted_iota(jnp.int32, (MOBA_BLK_PAD, blk), 0)
    start = pl.multiple_of(qi * blk, blk)
    head_rows = [slice(hh * HEAD_DIM, (hh + 1) * HEAD_DIM) for hh in range(MOBA_HEADS)]

    qs, states = [], []
    for hh, lanes in enumerate(head_rows):
        q_raw = q_ref[:, lanes]
        km_hi, km_lo = _split_bf16(kmean_ref[:, lanes])
        gate = _dot_nt(km_hi, q_raw) + _dot_nt(km_lo, q_raw)
        gate = jnp.where(blk_i < qi, gate, NEG)
        picked = (_topk_mask_rows(gate, MOBA_TOP_K) > 0.5) & (blk_i < qi)
        bias_ref[hh] = jnp.where(picked, 0.0, NEG)
        qs.append(q_raw * jnp.asarray(ATT_SCALE, BF16))
        states.append(_flash_t_first(qs[-1], k_ref[pl.ds(start, blk), lanes],
                                     vt_ref[lanes, pl.ds(start, blk)], causal_bias))

    def body(p, sts):
        s0 = pl.multiple_of(p * wide, wide)
        scores = [_dot_nt(k_ref[pl.ds(s0, wide), lanes], qs[hh]) for hh, lanes in enumerate(head_rows)]
        probs, nxt = [], []
        for hh in range(MOBA_HEADS):
            m, l, acc_t = sts[hh]
            bias = jnp.concatenate(
                [jnp.broadcast_to(bias_ref[hh, pl.ds(p * MOBA_WALK + w, 1), :], (blk, blk))
                 for w in range(MOBA_WALK)], axis=0)
            s = scores[hh] + bias
            m_new = jnp.maximum(m, jnp.max(s, axis=0, keepdims=True))
            pr = jnp.exp(s - m_new)
            alpha = jnp.exp(m - m_new)
            probs.append(pr.astype(BF16))
            nxt.append((m_new, alpha * l + jnp.sum(pr, axis=0, keepdims=True), alpha * acc_t))
        return tuple((m_new, l, acc_s + _dot(vt_ref[lanes, pl.ds(s0, wide)], probs[hh]))
                     for hh, (lanes, (m_new, l, acc_s)) in enumerate(zip(head_rows, nxt)))

    states = lax.fori_loop(0, (qi + MOBA_WALK - 1) // MOBA_WALK, body, tuple(states))
    for hh, lanes in enumerate(head_rows):
        _, l, acc_t = states[hh]
        o_ref[:, lanes] = (acc_t / l).T.astype(o_ref.dtype)


def _moba_attention(qkv, batch, seq):
    blk = MOBA_BLOCK
    wide = MOBA_WALK * blk
    nq = seq // blk
    d_model = N_HEADS * HEAD_DIM
    width = MOBA_HEADS * HEAD_DIM
    ncol = d_model // width
    avg = np.zeros((MOBA_BLK_PAD, seq), np.float32)
    for n in range(nq):
        avg[n, n * blk:(n + 1) * blk] = 1.0 / blk
    avg = jnp.asarray(avg, dtype=BF16)
    v_t = qkv[:, 2 * d_model:].reshape(batch, seq, d_model).transpose(0, 2, 1)
    return pl.pallas_call(
        _moba_kernel,
        grid=(batch, ncol, nq),
        in_specs=[pl.BlockSpec((blk, width), lambda b, h, i: (b * nq + i, h)),
                  pl.BlockSpec((seq, width), lambda b, h, i: (b, ncol + h)),
                  pl.BlockSpec((None, width, seq), lambda b, h, i: (b, h, 0)),
                  pl.BlockSpec((MOBA_BLK_PAD, seq), lambda b, h, i: (0, 0))],
        out_specs=pl.BlockSpec((blk, width), lambda b, h, i: (b * nq + i, h)),
        out_shape=jax.ShapeDtypeStruct((batch * seq, d_model), BF16),
        scratch_shapes=[pltpu.VMEM((MOBA_BLK_PAD, width), F32),
                        pltpu.VMEM((MOBA_HEADS, MOBA_BLK_PAD, blk), F32)],
        compiler_params=_cparams(("parallel", "parallel", "arbitrary")),
        name="moba_attention",
    )(qkv, qkv, v_t, avg)


def _nsa_compress_kernel(kr_ref, vr_ref, pk_ref, pv_ref, wk_ref, wv_ref, c_ref, sa_ref, sb_ref, kc_ref, vc_ref):
    nc = kr_ref.shape[0]
    rowi = lax.broadcasted_iota(jnp.int32, (nc, 1), 0)

    def windows(x_ref, pos_ref, w_ref):
        x = x_ref[...].astype(F32)
        lo = _dot((x + pos_ref[0:1, :]).astype(BF16), w_ref[0])
        hi = _dot((x + pos_ref[1:2, :]).astype(BF16), w_ref[1])
        hi_next = jnp.where(rowi < nc - 1, pltpu.roll(hi, nc - 1, 0), 0.0)
        return lo + hi_next

    kk = windows(kr_ref, pk_ref, wk_ref)
    d = HEAD_DIM
    kc = kk[:, 0:d] * c_ref[...] + kk[:, d:2 * d] * sb_ref[...] + kk[:, 2 * d:3 * d] * sa_ref[...]
    kc_ref[...] = kc.astype(kc_ref.dtype)
    vv = windows(vr_ref, pv_ref, wv_ref)
    vc_ref[...] = vv[:, 0:d].astype(vc_ref.dtype)


def _nsa_compress(k_cmp, v_cmp, pos_k, pos_v, w_k, w_v, batch, seq):
    g, d, st = NSA_GROUPS, HEAD_DIM, NSA_CMP_STRIDE
    nc = seq // st

    def chunks(x):
        return x.reshape(batch, nc, st, g, d).transpose(0, 3, 1, 2, 4).reshape(batch, g, nc, st * d)

    def weights(w):
        w3 = w.reshape(NSA_CMP_LEN, d, d)
        cat = jnp.concatenate([w3, jnp.roll(w3, ROPE_HALF, axis=2), jnp.roll(w3, -ROPE_HALF, axis=2)], axis=2)
        return cat.reshape(2, st * d, 3 * d).astype(BF16)

    def positions(p):
        return p.reshape(2, st * d).astype(F32)

    cmp_end = jnp.arange(nc) * st + NSA_CMP_LEN - 1
    tabs = _rope_tables(cmp_end, d)
    blk4 = pl.BlockSpec((None, None, nc, st * d), lambda b, gi: (b, gi, 0, 0))
    out4 = pl.BlockSpec((None, None, nc, d), lambda b, gi: (b, gi, 0, 0))
    const2 = lambda shape: pl.BlockSpec(shape, lambda b, gi: (0,) * len(shape))
    return pl.pallas_call(
        _nsa_compress_kernel,
        grid=(batch, g),
        in_specs=[blk4, blk4, const2((2, st * d)), const2((2, st * d)),
                  const2((2, st * d, 3 * d)), const2((2, st * d, 3 * d)),
                  const2((nc, d)), const2((nc, d)), const2((nc, d))],
        out_specs=[out4, out4],
        out_shape=[jax.ShapeDtypeStruct((batch, g, nc, d), BF16)] * 2,
        compiler_params=_cparams(("parallel", "parallel")),
        name="nsa_compress",
    )(chunks(k_cmp), chunks(v_cmp), positions(pos_k), positions(pos_v), weights(w_k), weights(w_v), *tabs)


NSA_TQ = 128
NSA_TK = 1024
NSA_CHAINS = 4


def _nsa_kernel(q_ref, kc_ref, vct_ref, ks_ref, vst_ref, kw_ref, vwt_ref, gate_ref, c2st_ref, o_ref, bias_ref):
    tq, tk, d, nh = NSA_TQ, NSA_TK, HEAD_DIM, NSA_HPG
    nql = nh * tq
    half = nql // NSA_CHAINS
    qi = pl.program_id(2)
    t0 = qi * tq
    nc = kc_ref.shape[0]

    def lanes4(x):
        return jnp.concatenate([x] * nh, axis=1)

    tpos1 = t0 + lax.broadcasted_iota(jnp.int32, (1, tq), 1)
    tpos = lanes4(tpos1)
    gates_t = jax.nn.sigmoid(gate_ref[...]).T

    def gate(branch):
        return jnp.concatenate([gates_t[3 * h + branch:3 * h + branch + 1, :] for h in range(nh)], axis=1)

    q = jnp.concatenate([q_ref[:, h * d:(h + 1) * d] for h in range(nh)], axis=0) * jnp.asarray(ATT_SCALE, BF16)

    kd = t0 // tk
    dstart = pl.multiple_of(kd * tk, tk)
    span = NSA_WINDOW + tq
    wstart = pl.multiple_of(jnp.maximum(t0 - NSA_WINDOW, 0), tq)
    s_cmp = _dot_nt(kc_ref[...], q)
    s_win = _dot_nt(kw_ref[pl.ds(wstart, span), :], q)
    s_diag = _dot_nt(ks_ref[pl.ds(dstart, tk), :], q)

    cmp_end = lax.broadcasted_iota(jnp.int32, (nc, nql), 0) * NSA_CMP_STRIDE + (NSA_CMP_LEN - 1)
    vis_c = cmp_end <= tpos
    s = jnp.where(vis_c, s_cmp, NEG)
    e = jnp.where(vis_c, jnp.exp(s - jnp.max(s, axis=0, keepdims=True)), 0.0)
    l = jnp.sum(e, axis=0, keepdims=True)
    p = e / jnp.where(l > 0.0, l, 1.0)
    out_t = gate(0) * _dot(vct_ref[...], p.astype(BF16))
    psum = functools.reduce(lambda a, b: a + b, [p[:, h * tq:(h + 1) * tq] for h in range(nh)])
    p_hi, p_lo = _split_bf16(psum)
    imp = _dot(c2st_ref[...], p_hi) + _dot(c2st_ref[...], p_lo)
    blk = lax.broadcasted_iota(jnp.int32, (NSA_SEL_PAD, tq), 0)
    own = tpos1 // NSA_SEL_LEN
    forced = (blk == 0) | (blk == own) | (blk == own - 1)
    imp = jnp.where(blk <= own, imp + jnp.where(forced, NSA_BONUS, 0.0), NEG)
    picked = _topk_mask_rows(imp, NSA_TOP_N) > 0.5
    bias_ref[...] = lanes4(jnp.where(picked, 0.0, NEG))

    per_tile = tk // NSA_SEL_LEN

    def tile_bias(kb):
        return jnp.concatenate(
            [jnp.broadcast_to(bias_ref[pl.ds(kb * per_tile + w, 1), :], (NSA_SEL_LEN, nql))
             for w in range(per_tile)], axis=0)

    wpos = wstart + lax.broadcasted_iota(jnp.int32, (span, nql), 0)
    s = jnp.where((wpos <= tpos) & (wpos > tpos - NSA_WINDOW), s_win, NEG)
    e = jnp.exp(s - jnp.max(s, axis=0, keepdims=True))
    out_t = out_t + (gate(2) / jnp.sum(e, axis=0, keepdims=True)) * _dot(vwt_ref[:, pl.ds(wstart, span)],
                                                                         e.astype(BF16))

    kpos = dstart + lax.broadcasted_iota(jnp.int32, (tk, nql), 0)
    s = jnp.where(kpos <= tpos, s_diag + tile_bias(kd), NEG)
    m = jnp.max(s, axis=0, keepdims=True)
    e = jnp.exp(s - m)
    l = jnp.sum(e, axis=0, keepdims=True)
    acc = _dot(vst_ref[:, pl.ds(dstart, tk)], e.astype(BF16))
    halves = [slice(i * half, (i + 1) * half) for i in range(NSA_CHAINS)]
    q_half = [q[hs] for hs in halves]
    states = tuple((m[:, hs], l[:, hs], acc[:, hs]) for hs in halves)

    def sel_body(kb, sts):
        start = pl.multiple_of(kb * tk, tk)
        k = ks_ref[pl.ds(start, tk), :]
        vt = vst_ref[:, pl.ds(start, tk)]
        bias = tile_bias(kb)
        scores = [_dot_nt(k, q_half[i]) for i in range(NSA_CHAINS)]
        probs, nxt = [], []
        for i, hs in enumerate(halves):
            m, l, acc = sts[i]
            s = scores[i] + bias[:, hs]
            m_new = jnp.maximum(m, jnp.max(s, axis=0, keepdims=True))
            pr = jnp.exp(s - m_new)
            alpha = jnp.exp(m - m_new)
            probs.append(pr.astype(BF16))
            nxt.append((m_new, alpha * l + jnp.sum(pr, axis=0, keepdims=True), alpha * acc))
        return tuple((m_new, l, acc + _dot(vt, probs[i])) for i, (m_new, l, acc) in enumerate(nxt))

    states = lax.fori_loop(0, kd, sel_body, states)
    sel_t = jnp.concatenate([st[2] / st[1] for st in states], axis=1)
    out_t = out_t + gate(1) * sel_t
    for h in range(nh):
        o_ref[:, h * d:(h + 1) * d] = out_t[:, h * tq:(h + 1) * tq].T.astype(o_ref.dtype)


def _nsa_attention(q_all, kc, vct, ks, vst, kw, vwt, gates, batch, seq):
    tq, tk, d, g = NSA_TQ, NSA_TK, HEAD_DIM, NSA_GROUPS
    nq = seq // tq
    nc = seq // NSA_CMP_STRIDE
    n_sel = seq // NSA_SEL_LEN
    d_model = N_HEADS * HEAD_DIM
    cs = np.arange(nc) * NSA_CMP_STRIDE
    ss = np.arange(n_sel) * NSA_SEL_LEN
    ov = np.clip(np.minimum(cs[:, None] + NSA_CMP_LEN, ss[None, :] + NSA_SEL_LEN)
                 - np.maximum(cs[:, None], ss[None, :]), 0, None) / NSA_CMP_LEN
    c2s = np.zeros((nc, NSA_SEL_PAD), np.float32)
    c2s[:, :n_sel] = ov
    c2s[nc - 1, :] = 0.0
    k_spec = lambda n: pl.BlockSpec((None, None, n, d), lambda b, gi, i: (b, gi, 0, 0))
    vt_spec = lambda n: pl.BlockSpec((None, None, d, n), lambda b, gi, i: (b, gi, 0, 0))
    return pl.pallas_call(
        _nsa_kernel,
        grid=(batch, g, nq),
        in_specs=[pl.BlockSpec((tq, NSA_HPG * d), lambda b, gi, i: (b * nq + i, gi)),
                  k_spec(nc), vt_spec(nc), k_spec(seq), vt_spec(seq), k_spec(seq), vt_spec(seq),
                  pl.BlockSpec((tq, LANE), lambda b, gi, i: (b * nq + i, gi)),
                  pl.BlockSpec((NSA_SEL_PAD, nc), lambda b, gi, i: (0, 0))],
        out_specs=pl.BlockSpec((tq, NSA_HPG * d), lambda b, gi, i: (b * nq + i, gi)),
        out_shape=jax.ShapeDtypeStruct((batch * seq, d_model), BF16),
        scratch_shapes=[pltpu.VMEM((NSA_SEL_PAD, NSA_HPG * tq), F32)],
        compiler_params=_cparams(("parallel", "parallel", "arbitrary")),
        name="nsa_attention",
    )(q_all, kc, vct, ks, vst, kw, vwt, gates, jnp.asarray(c2s.T, dtype=BF16))


def _nsa_mixer(x2, g_norm, w_in, pos_k, pos_v, w_k, w_v, w_out, batch, seq):
    d_model, kv = N_HEADS * HEAD_DIM, NSA_KV
    sec = lambda i: w_in[:, d_model + i * kv: d_model + (i + 1) * kv]
    w_main = jnp.concatenate([w_in[:, :d_model], sec(2), sec(4), sec(0), sec(1), sec(3), sec(5)], axis=1)
    proj = _norm_matmul(x2, g_norm, w_main, seq, tn=512, n_rope=3)
    w_gate = w_in[:, d_model + 6 * kv:].reshape(d_model, NSA_GROUPS, 3 * NSA_HPG)
    w_gate = jnp.pad(w_gate, ((0, 0), (0, 0), (0, LANE - 3 * NSA_HPG))).reshape(d_model, NSA_GROUPS * LANE)
    gates = _norm_matmul(x2, g_norm, w_gate, seq, tn=NSA_GROUPS * LANE, out_dtype=F32)

    col = lambda i: proj[:, d_model + i * kv: d_model + (i + 1) * kv]
    heads = lambda x: x.reshape(batch, seq, NSA_GROUPS, HEAD_DIM).transpose(0, 2, 1, 3)
    heads_t = lambda x: x.reshape(batch, seq, NSA_GROUPS, HEAD_DIM).transpose(0, 2, 3, 1)
    kc, vc = _nsa_compress(col(2), col(3), pos_k, pos_v, w_k, w_v, batch, seq)
    o = _nsa_attention(proj, kc, vc.transpose(0, 1, 3, 2), heads(col(0)), heads_t(col(4)), heads(col(1)),
                       heads_t(col(5)), gates, batch, seq)
    return _proj_resid(o, w_out, x2)


PEER_SEL_TT = 256
def _batcher_network(n):
    def merge(lo, hi, r):
        step = r * 2
        if step < hi - lo:
            yield from merge(lo, hi, step)
            yield from merge(lo + r, hi, step)
            yield from ((i, i + r) for i in range(lo + r, hi - r, step))
        else:
            yield (lo, lo + r)

    def sort(lo, hi):
        if hi - lo >= 1:
            mid = lo + (hi - lo) // 2
            yield from sort(lo, mid)
            yield from sort(mid + 1, hi)
            yield from merge(lo, hi, 1)

    return list(sort(0, n - 1))


SORT16_NETWORK = _batcher_network(PEER_KEYS // 8)

PEER_PAIRS = [(i, j) for i in range(PEER_TOP_K) for j in range(PEER_TOP_K) if (i + 1) * (j + 1) <= PEER_TOP_K]


def _peer_select_kernel(q_ref, k1_ref, k2_ref, cnt_ref, e1_ref, rank_ref, e2_ref):
    q = q_ref[...]
    q_hi, q_lo = _split_bf16(q)
    nh, nr = PEER_HEADS, PEER_TOP_K

    def scores(k_ref, off):
        k_hi, k_lo = _split_bf16(k_ref[...])
        out = []
        for h in range(nh):
            lanes = slice(off + h * PEER_HALF, off + (h + 1) * PEER_HALF)
            out.append(_dot_nt(k_hi, q_hi[:, lanes]) + _dot_nt(k_hi, q_lo[:, lanes])
                       + _dot_nt(k_lo, q_hi[:, lanes]))
        return out

    def top_values(s):
        slabs = [s[8 * k:8 * (k + 1), :] for k in range(PEER_KEYS // 8)]
        for a, b in SORT16_NETWORK:
            slabs[a], slabs[b] = jnp.maximum(slabs[a], slabs[b]), jnp.minimum(slabs[a], slabs[b])
        vals = []
        for r in range(nr):
            m = jnp.max(slabs[0], axis=0, keepdims=True)
            vals.append(m)
            hit = slabs[0] == m
            for k in range(nr - 1 - r):
                slabs[k] = jnp.where(hit, slabs[k + 1], slabs[k])
        return vals

    s1 = scores(k1_ref, 0)
    s2 = scores(k2_ref, nh * PEER_HALF)
    t1 = [top_values(s) for s in s1]
    t2 = [top_values(s) for s in s2]
    r1 = [jnp.concatenate([t1[h][r] for h in range(nh)], axis=0) for r in range(nr)]
    r2 = [jnp.concatenate([t2[h][r] for h in range(nh)], axis=0) for r in range(nr)]
    cands = [r1[i] + r2[j] for (i, j) in PEER_PAIRS]
    work, tops = list(cands), []
    for _ in range(nr):
        m = functools.reduce(jnp.maximum, work)
        tops.append(m)
        work = [jnp.where(c == m, LOW, c) for c in work]
    thr = tops[PEER_TOP_K - 1]
    cmax = tops[0]
    z = functools.reduce(lambda a, b: a + b, [jnp.where(c >= thr, jnp.exp(c - cmax), 0.0) for c in cands])
    inv_z = 1.0 / z
    for h in range(nh):
        thr_h = thr[h:h + 1, :]
        cnt = jnp.zeros_like(s1[h])
        rank = jnp.zeros_like(s2[h])
        for r in range(nr):
            cnt = jnp.where(s1[h] + t2[h][r] >= thr_h, float(r + 1), cnt)
            rank = jnp.where(t2[h][r] > s2[h], float(r + 1), rank)
        cnt_ref[h] = cnt
        e1_ref[h] = jnp.exp(s1[h] - r1[0][h:h + 1, :])
        rank_ref[h] = rank.astype(BF16)
        e2_ref[h] = (jnp.exp(s2[h] - r2[0][h:h + 1, :]) * inv_z[h:h + 1, :]).astype(BF16)


def _peer_select(q, sub_keys):
    t = q.shape[0]
    tt = min(PEER_SEL_TT, t)
    out_spec = pl.BlockSpec((PEER_HEADS, PEER_KEYS, tt), lambda i: (0, 0, i))
    return pl.pallas_call(
        _peer_select_kernel,
        grid=(t // tt,),
        in_specs=[pl.BlockSpec((tt, q.shape[1]), lambda i: (i, 0)),
                  pl.BlockSpec((PEER_KEYS, PEER_HALF), lambda i: (0, 0)),
                  pl.BlockSpec((PEER_KEYS, PEER_HALF), lambda i: (0, 0))],
        out_specs=[out_spec] * 4,
        out_shape=[jax.ShapeDtypeStruct((PEER_HEADS, PEER_KEYS, t), dt) for dt in (F32, F32, BF16, BF16)],
        compiler_params=_cparams(("parallel",)),
        name="peer_select",
    )(q, sub_keys[0].astype(F32), sub_keys[1].astype(F32))


PEER_TT = 512
PEER_NE = 2048
PEER_SUB = 512


def _gelu(x):
    return 0.5 * x * (1.0 + lax.erf(x * np.float32(np.sqrt(0.5))))


def _peer_main_kernel(x_ref, g_ref, u_ref, vt_ref, cnt_ref, e1_ref, rank_ref, e2_ref, o_ref, xnt_ref, acc_ref):
    e = pl.program_id(1)
    ne = u_ref.shape[0]
    tt = x_ref.shape[0]
    sub = PEER_SUB
    n_sub = ne // sub

    @pl.when(e == 0)
    def _():
        x = x_ref[...]
        y = x * lax.rsqrt(jnp.mean(x * x, axis=-1, keepdims=True) + NORM_EPS)
        xnt_ref[...] = (y * g_ref[...]).T.astype(BF16)
        acc_ref[...] = jnp.zeros_like(acc_ref)

    def hidden(c):
        return _dot(u_ref[c * sub:(c + 1) * sub, :], xnt_ref[...])

    def weighted(c, h):
        act = _gelu(h).astype(BF16)
        parts = []
        for k in range(sub // PEER_KEYS):
            i1 = (e * n_sub + c) * (sub // PEER_KEYS) + k
            gsum = jnp.zeros((PEER_KEYS, tt), BF16)
            for hd in range(PEER_HEADS):
                cnt = cnt_ref[hd, pl.ds(i1, 1), :].astype(BF16)
                e1 = e1_ref[hd, pl.ds(i1, 1), :].astype(BF16)
                gsum = gsum + jnp.where(rank_ref[hd] < cnt, e2_ref[hd], jnp.zeros((), BF16)) * e1
            parts.append(gsum * act[k * PEER_KEYS:(k + 1) * PEER_KEYS, :])
        return jnp.concatenate(parts, axis=0) if len(parts) > 1 else parts[0]

    def values(c, w):
        return _dot(vt_ref[:, c * sub:(c + 1) * sub], w)

    hs = {0: hidden(0)}
    if n_sub > 1:
        hs[1] = hidden(1)
    total = None
    for c in range(n_sub):
        w = weighted(c, hs.pop(c))
        if c + 2 < n_sub:
            hs[c + 2] = hidden(c + 2)
        pv = values(c, w)
        total = pv if total is None else total + pv
    acc_ref[...] += total

    @pl.when(e == pl.num_programs(1) - 1)
    def _():
        o_ref[...] = x_ref[...] + acc_ref[...].T


def _peer_mixer(x2, g_norm, w_q, sub_keys, u, v, seq):
    t, d = x2.shape
    w_q2 = w_q.reshape(d, PEER_HEADS, 2, PEER_HALF).transpose(0, 2, 1, 3).reshape(d, 2 * PEER_HEADS * PEER_HALF)
    q = _norm_matmul3(x2, g_norm, w_q2)
    cnt, e1, rank, e2 = _peer_select(q, sub_keys)
    tt = min(PEER_TT, t)
    ne = PEER_NE
    n_exp = u.shape[0]
    tok_spec = pl.BlockSpec((PEER_HEADS, PEER_KEYS, tt), lambda i, e: (0, 0, i))
    return pl.pallas_call(
        _peer_main_kernel,
        grid=(t // tt, n_exp // ne),
        in_specs=[pl.BlockSpec((tt, d), lambda i, e: (i, 0)),
                  pl.BlockSpec((1, d), lambda i, e: (0, 0)),
                  pl.BlockSpec((ne, d), lambda i, e: (e, 0)),
                  pl.BlockSpec((d, ne), lambda i, e: (0, e)),
                  tok_spec, tok_spec, tok_spec, tok_spec],
        out_specs=pl.BlockSpec((tt, d), lambda i, e: (i, 0)),
        out_shape=jax.ShapeDtypeStruct((t, d), F32),
        scratch_shapes=[pltpu.VMEM((d, tt), BF16), pltpu.VMEM((d, tt), F32)],
        compiler_params=_cparams(("parallel", "arbitrary")),
        name="peer_main",
    )(x2, g_norm.reshape(1, d).astype(F32), u.astype(BF16), v.T.astype(BF16), cnt, e1, rank, e2)


def _sb_mixer(x2, g_norm, w_in, w_out, batch, seq):
    qkv = _norm_matmul(x2, g_norm, w_in, seq, tn=1024)
    return _proj_resid(_sb_attention(qkv, batch, seq), w_out, x2)


def _moba_mixer(x2, g_norm, w_in, w_out, batch, seq):
    d_model = N_HEADS * HEAD_DIM
    qkv = _norm_matmul(x2, g_norm, w_in, seq, tn=512, n_rope=2 * d_model // 512)
    return _proj_resid(_moba_attention(qkv, batch, seq), w_out, x2)


def kernel(x, l0_norm_mix, l0_sb_w_in, l0_sb_w_out, l0_norm_ffn, l0_peer_w_q, l0_peer_sub_keys, l0_peer_u, l0_peer_v, l1_norm_mix, l1_nsa_w_in, l1_nsa_cmp_pos_k, l1_nsa_cmp_pos_v, l1_nsa_cmp_w_k, l1_nsa_cmp_w_v, l1_nsa_w_out, l1_norm_ffn, l1_peer_w_q, l1_peer_sub_keys, l1_peer_u, l1_peer_v, l2_norm_mix, l2_moba_w_in, l2_moba_w_out, l2_norm_ffn, l2_peer_w_q, l2_peer_sub_keys, l2_peer_u, l2_peer_v, l3_norm_mix, l3_sb_w_in, l3_sb_w_out, l3_norm_ffn, l3_peer_w_q, l3_peer_sub_keys, l3_peer_u, l3_peer_v, final_norm):
    batch, seq, d = x.shape
    x2 = x.reshape(batch * seq, d)
    x2 = _sb_mixer(x2, l0_norm_mix, l0_sb_w_in, l0_sb_w_out, batch, seq)
    x2 = _peer_mixer(x2, l0_norm_ffn, l0_peer_w_q, l0_peer_sub_keys, l0_peer_u, l0_peer_v, seq)
    x2 = _nsa_mixer(x2, l1_norm_mix, l1_nsa_w_in, l1_nsa_cmp_pos_k, l1_nsa_cmp_pos_v, l1_nsa_cmp_w_k,
                    l1_nsa_cmp_w_v, l1_nsa_w_out, batch, seq)
    x2 = _peer_mixer(x2, l1_norm_ffn, l1_peer_w_q, l1_peer_sub_keys, l1_peer_u, l1_peer_v, seq)
    x2 = _moba_mixer(x2, l2_norm_mix, l2_moba_w_in, l2_moba_w_out, batch, seq)
    x2 = _peer_mixer(x2, l2_norm_ffn, l2_peer_w_q, l2_peer_sub_keys, l2_peer_u, l2_peer_v, seq)
    x2 = _sb_mixer(x2, l3_norm_mix, l3_sb_w_in, l3_sb_w_out, batch, seq)
    x2 = _peer_mixer(x2, l3_norm_ffn, l3_peer_w_q, l3_peer_sub_keys, l3_peer_u, l3_peer_v, seq)
    return _rmsnorm(x2, final_norm).reshape(batch, seq, d)
```

```python
import functools

import numpy as np
import jax
import jax.numpy as jnp
from jax import lax
from jax.experimental import pallas as pl
from jax.experimental.pallas import tpu as pltpu

F32 = jnp.float32
BF16 = jnp.bfloat16

N_HEADS = 16
HEAD_DIM = 64
ROPE_DIM = 16
ROPE_HALF = ROPE_DIM // 2
ROPE_THETA = 500000.0
NORM_EPS = 1e-6
NEG = -1e30
LOW = -3e38
ATT_SCALE = HEAD_DIM ** -0.5

NSA_GROUPS = 4
NSA_HPG = N_HEADS // NSA_GROUPS
NSA_KV = NSA_GROUPS * HEAD_DIM
NSA_CMP_LEN = 32
NSA_CMP_STRIDE = 16
NSA_SEL_LEN = 64
NSA_TOP_N = 16
NSA_WINDOW = 512
NSA_BONUS = 1e3
NSA_SEL_PAD = 128

MOBA_BLOCK = 256
MOBA_TOP_K = 3
MOBA_BLK_ALIGN = 16

PEER_HEADS = 8
PEER_KEYS = 128
PEER_TOP_K = 16
PEER_HALF = 64

SB_EXIT = -110.0

LANE = 128
VMEM_LIMIT = 56 << 20


def _cparams(sem, vmem=VMEM_LIMIT):
    return pltpu.CompilerParams(dimension_semantics=sem, vmem_limit_bytes=vmem)


def _dot(a, b):
    return jnp.dot(a, b, preferred_element_type=F32)


def _dot_nt(a, b):
    return lax.dot_general(a, b, (((1,), (1,)), ((), ())), preferred_element_type=F32)


def _split_bf16(x):
    hi = x.astype(BF16)
    lo = (x - hi.astype(F32)).astype(BF16)
    return hi, lo


def _topk_mask(vals, k):
    lane = lax.broadcasted_iota(jnp.int32, vals.shape, 1)
    work = vals
    sel = jnp.zeros(vals.shape, F32)
    for _ in range(k):
        m = jnp.max(work, axis=-1, keepdims=True)
        first = jnp.min(jnp.where(work == m, lane, LANE), axis=-1, keepdims=True)
        hit = lane == first
        sel = jnp.where(hit, 1.0, sel)
        work = jnp.where(hit, LOW, work)
    return sel


def _norm_matmul_kernel(*refs, n_rope, tn):
    if n_rope:
        x_ref, g_ref, w_ref, c_ref, sa_ref, sb_ref, o_ref, xn_ref = refs
    else:
        x_ref, g_ref, w_ref, o_ref, xn_ref = refs
    j = pl.program_id(1)

    @pl.when(j == 0)
    def _():
        x = x_ref[...]
        y = x * lax.rsqrt(jnp.mean(x * x, axis=-1, keepdims=True) + NORM_EPS)
        xn_ref[...] = (y * g_ref[...]).astype(BF16)

    acc = _dot(xn_ref[...], w_ref[...])
    if n_rope:
        @pl.when(j < n_rope)
        def _():
            r = (acc * c_ref[...] + pltpu.roll(acc, ROPE_HALF, 1) * sb_ref[...]
                 + pltpu.roll(acc, tn - ROPE_HALF, 1) * sa_ref[...])
            o_ref[...] = r.astype(o_ref.dtype)

        @pl.when(j >= n_rope)
        def _():
            o_ref[...] = acc.astype(o_ref.dtype)
    else:
        o_ref[...] = acc.astype(o_ref.dtype)


def _rope_tables(pos, width):
    inv = ROPE_THETA ** (-jnp.arange(0, ROPE_DIM, 2, dtype=F32) / ROPE_DIM)
    ang = pos.astype(F32)[:, None] * inv[None, :]
    cos, sin = jnp.cos(ang), jnp.sin(ang)
    n = pos.shape[0]
    rest = HEAD_DIM - ROPE_DIM
    c = jnp.concatenate([cos, cos, jnp.ones((n, rest), F32)], axis=1)
    sa = jnp.concatenate([-sin, jnp.zeros((n, ROPE_HALF + rest), F32)], axis=1)
    sb = jnp.concatenate([jnp.zeros((n, ROPE_HALF), F32), sin, jnp.zeros((n, rest), F32)], axis=1)
    reps = width // HEAD_DIM
    return jnp.tile(c, (1, reps)), jnp.tile(sa, (1, reps)), jnp.tile(sb, (1, reps))


def _norm_matmul(x2, g, w, seq, *, tn, n_rope=0, out_dtype=BF16, tt=1024):
    t, d = x2.shape
    n = w.shape[1]
    tt = min(tt, seq)
    grid = (t // tt, n // tn)
    in_specs = [pl.BlockSpec((tt, d), lambda i, j: (i, 0)),
                pl.BlockSpec((1, d), lambda i, j: (0, 0)),
                pl.BlockSpec((d, tn), lambda i, j: (0, j))]
    args = [x2, g.reshape(1, d).astype(F32), w.astype(BF16)]
    if n_rope:
        nper = seq // tt
        tabs = _rope_tables(jnp.arange(seq), tn)
        in_specs += [pl.BlockSpec((tt, tn), lambda i, j: (i % nper, 0))] * 3
        args += list(tabs)
    return pl.pallas_call(
        functools.partial(_norm_matmul_kernel, n_rope=n_rope, tn=tn),
        grid=grid, in_specs=in_specs,
        out_specs=pl.BlockSpec((tt, tn), lambda i, j: (i, j)),
        out_shape=jax.ShapeDtypeStruct((t, n), out_dtype),
        scratch_shapes=[pltpu.VMEM((tt, d), BF16)],
        compiler_params=_cparams(("parallel", "arbitrary")),
        name="norm_matmul",
    )(*args)


def _norm_matmul3_kernel(x_ref, g_ref, wh_ref, wl_ref, o_ref, xh_ref, xl_ref):
    @pl.when(pl.program_id(1) == 0)
    def _():
        x = x_ref[...]
        y = x * lax.rsqrt(jnp.mean(x * x, axis=-1, keepdims=True) + NORM_EPS) * g_ref[...]
        xh_ref[...], xl_ref[...] = _split_bf16(y)

    o_ref[...] = (_dot(xh_ref[...], wh_ref[...]) + _dot(xl_ref[...], wh_ref[...])
                  + _dot(xh_ref[...], wl_ref[...]))


def _norm_matmul3(x2, g, w, *, tn=512, tt=1024):
    t, d = x2.shape
    n = w.shape[1]
    tt = min(tt, t)
    w_hi, w_lo = _split_bf16(w.astype(F32))
    w_spec = pl.BlockSpec((d, tn), lambda i, j: (0, j))
    return pl.pallas_call(
        _norm_matmul3_kernel,
        grid=(t // tt, n // tn),
        in_specs=[pl.BlockSpec((tt, d), lambda i, j: (i, 0)),
                  pl.BlockSpec((1, d), lambda i, j: (0, 0)), w_spec, w_spec],
        out_specs=pl.BlockSpec((tt, tn), lambda i, j: (i, j)),
        out_shape=jax.ShapeDtypeStruct((t, n), F32),
        scratch_shapes=[pltpu.VMEM((tt, d), BF16), pltpu.VMEM((tt, d), BF16)],
        compiler_params=_cparams(("parallel", "arbitrary")),
        name="norm_matmul3",
    )(x2, g.reshape(1, d).astype(F32), w_hi, w_lo)


def _proj_resid_kernel(a_ref, w_ref, r_ref, o_ref):
    o_ref[...] = r_ref[...] + _dot(a_ref[...], w_ref[...])


def _proj_resid(a, w, resid, *, tt=1024, tn=1024):
    t, d = a.shape
    n = w.shape[1]
    tt = min(tt, t)
    return pl.pallas_call(
        _proj_resid_kernel,
        grid=(t // tt, n // tn),
        in_specs=[pl.BlockSpec((tt, d), lambda i, j: (i, 0)),
                  pl.BlockSpec((d, tn), lambda i, j: (0, j)),
                  pl.BlockSpec((tt, tn), lambda i, j: (i, j))],
        out_specs=pl.BlockSpec((tt, tn), lambda i, j: (i, j)),
        out_shape=jax.ShapeDtypeStruct((t, n), F32),
        compiler_params=_cparams(("parallel", "parallel")),
        name="proj_resid",
    )(a, w.astype(BF16), resid)


def _rmsnorm_kernel(x_ref, g_ref, o_ref):
    x = x_ref[...]
    y = x * lax.rsqrt(jnp.mean(x * x, axis=-1, keepdims=True) + NORM_EPS)
    o_ref[...] = y * g_ref[...]


def _rmsnorm(x2, g, *, tt=512):
    t, d = x2.shape
    tt = min(tt, t)
    return pl.pallas_call(
        _rmsnorm_kernel,
        grid=(t // tt,),
        in_specs=[pl.BlockSpec((tt, d), lambda i: (i, 0)), pl.BlockSpec((1, d), lambda i: (0, 0))],
        out_specs=pl.BlockSpec((tt, d), lambda i: (i, 0)),
        out_shape=jax.ShapeDtypeStruct((t, d), F32),
        compiler_params=_cparams(("parallel",)),
        name="final_rmsnorm",
    )(x2, g.reshape(1, d).astype(F32))


SB_TILE = 256
SB_FIRST = 2 * SB_TILE
SB_HEADS = 4


def _sb_kernel(q_ref, k_ref, v_ref, tri_ref, o_ref):
    tile = SB_TILE
    wide = SB_FIRST
    qi = pl.program_id(2)
    t0 = qi * tile

    nh = SB_HEADS
    head_lanes = [slice(hh * HEAD_DIM, (hh + 1) * HEAD_DIM) for hh in range(nh)]
    qs = [q_ref[:, lanes] * jnp.asarray(ATT_SCALE, BF16) for lanes in head_lanes]

    def step(start, width, r_sums, accs, past):
        tri = tri_ref[0:width, 0:width]
        zs = [_dot_nt(qs[hh], k_ref[pl.ds(start, width), lanes]) for hh, lanes in enumerate(head_lanes)]
        sps, lbs, splits = [], [], []
        for z in zs:
            sp = jnp.maximum(z, 0.0) + jnp.log(1.0 + jnp.exp(-jnp.abs(z)))
            lb = -sp if past is None else jnp.where(past, -sp, 0.0)
            sps.append(sp)
            lbs.append(lb)
            splits.append(_split_bf16(lb))
        suffixes = [_dot(hi, tri) + _dot(lo, tri) for hi, lo in splits]
        weights = []
        for hh in range(nh):
            a = jnp.exp((zs[hh] - sps[hh]) + suffixes[hh] + r_sums[hh])
            if past is not None:
                a = jnp.where(past, a, 0.0)
            weights.append(a.astype(BF16))
        accs = tuple(accs[hh] + _dot(weights[hh], v_ref[pl.ds(start, width), lanes])
                     for hh, lanes in enumerate(head_lanes))
        r_sums = tuple(r_sums[hh] + jnp.sum(lbs[hh], axis=-1, keepdims=True) for hh in range(nh))
        return r_sums, accs

    start0 = pl.multiple_of(jnp.maximum(t0 + tile - wide, 0), tile)
    kpos = start0 + lax.broadcasted_iota(jnp.int32, (tile, wide), 1)
    tpos = t0 + lax.broadcasted_iota(jnp.int32, (tile, wide), 0)
    zero_r = tuple(jnp.zeros((tile, 1), F32) for _ in range(nh))
    zero_acc = tuple(jnp.zeros((tile, HEAD_DIM), F32) for _ in range(nh))
    r1, acc1 = step(start0, wide, zero_r, zero_acc, kpos < tpos)

    def older(j, r_sums):
        alive = jnp.max(functools.reduce(jnp.maximum, r_sums)) > SB_EXIT
        return jnp.where(alive, j - 1, -1)

    def cond(c):
        return c[0] >= 0

    def body(c):
        j, r_sums, accs = c
        r_sums, accs = step(pl.multiple_of(j * tile, tile), tile, r_sums, accs, None)
        return older(j, r_sums), r_sums, accs

    _, _, accs = lax.while_loop(cond, body, (older(start0 // tile, r1), r1, acc1))
    for hh, lanes in enumerate(head_lanes):
        o_ref[:, lanes] = accs[hh].astype(o_ref.dtype)


def _sb_attention(qkv, batch, seq):
    tile = SB_TILE
    nq = seq // tile
    d_model = N_HEADS * HEAD_DIM
    width = SB_HEADS * HEAD_DIM
    ncol = d_model // width
    ii = np.arange(SB_FIRST)
    tri = jnp.asarray(ii[:, None] > ii[None, :], dtype=BF16)
    return pl.pallas_call(
        _sb_kernel,
        grid=(batch, ncol, nq),
        in_specs=[pl.BlockSpec((tile, width), lambda b, h, i: (b * nq + i, h)),
                  pl.BlockSpec((seq, width), lambda b, h, i: (b, ncol + h)),
                  pl.BlockSpec((seq, width), lambda b, h, i: (b, 2 * ncol + h)),
                  pl.BlockSpec((SB_FIRST, SB_FIRST), lambda b, h, i: (0, 0))],
        out_specs=pl.BlockSpec((tile, width), lambda b, h, i: (b * nq + i, h)),
        out_shape=jax.ShapeDtypeStruct((batch * seq, d_model), BF16),
        compiler_params=_cparams(("parallel", "parallel", "arbitrary")),
        name="sb_attention",
    )(qkv, qkv, qkv, tri)


def _flash_first(q, k, v, mask):
    s = jnp.where(mask, _dot_nt(q, k), NEG)
    m = jnp.max(s, axis=-1, keepdims=True)
    p = jnp.exp(s - m)
    return m, jnp.sum(p, axis=-1, keepdims=True), _dot(p.astype(BF16), v)


def _flash_next(q, k, v, mask, state):
    m, l, acc = state
    s = jnp.where(mask, _dot_nt(q, k), NEG)
    m_new = jnp.maximum(m, jnp.max(s, axis=-1, keepdims=True))
    p = jnp.exp(s - m_new)
    alpha = jnp.exp(m - m_new)
    l = alpha * l + jnp.sum(p, axis=-1, keepdims=True)
    acc = alpha * acc + _dot(p.astype(BF16), v)
    return m_new, l, acc


def _flash_out(state):
    _, l, acc = state
    return acc / l


def _flash_t_first(q, k, vt, bias_t):
    s = _dot_nt(k, q) + bias_t
    m = jnp.max(s, axis=0, keepdims=True)
    p = jnp.exp(s - m)
    return m, jnp.sum(p, axis=0, keepdims=True), _dot(vt, p.astype(BF16))


def _flash_t_next(q, k, vt, bias_t, state):
    m, l, acc_t = state
    s = _dot_nt(k, q) + bias_t
    m_new = jnp.maximum(m, jnp.max(s, axis=0, keepdims=True))
    p = jnp.exp(s - m_new)
    alpha = jnp.exp(m - m_new)
    l = alpha * l + jnp.sum(p, axis=0, keepdims=True)
    acc_t = alpha * acc_t + _dot(vt, p.astype(BF16))
    return m_new, l, acc_t


def _topk_mask_rows(vals, k):
    rowi = lax.broadcasted_iota(jnp.int32, vals.shape, 0)
    work = vals
    sel = jnp.zeros(vals.shape, F32)
    for _ in range(k):
        m = jnp.max(work, axis=0, keepdims=True)
        first = jnp.min(jnp.where(work == m, rowi, vals.shape[0]), axis=0, keepdims=True)
        hit = rowi == first
        sel = jnp.where(hit, 1.0, sel)
        work = jnp.where(hit, LOW, work)
    return sel


MOBA_WALK = 4
MOBA_HEADS = 8


def _moba_kernel(q_ref, k_ref, vt_ref, avg_ref, o_ref, kmean_ref, bias_ref):
    blk = MOBA_BLOCK
    wide = MOBA_WALK * blk
    qi = pl.program_id(2)

    @pl.when(qi == 0)
    def _():
        kmean_ref[...] = _dot(avg_ref[...], k_ref[...])

    key_i = lax.broadcasted_iota(jnp.int32, (blk, blk), 0)
    qry_i = lax.broadcasted_iota(jnp.int32, (blk, blk), 1)
    causal_bias = jnp.where(key_i <= qry_i, 0.0, NEG)
    blk_i = lax.broadcasted_iota(jnp.int32, (kmean_ref.shape[0], blk), 0)
    start = pl.multiple_of(qi * blk, blk)
    head_rows = [slice(hh * HEAD_DIM, (hh + 1) * HEAD_DIM) for hh in range(MOBA_HEADS)]

    qs, states = [], []
    for hh, lanes in enumerate(head_rows):
        q_raw = q_ref[:, lanes]
        km_hi, km_lo = _split_bf16(kmean_ref[:, lanes])
        gate = _dot_nt(km_hi, q_raw) + _dot_nt(km_lo, q_raw)
        gate = jnp.where(blk_i < qi, gate, NEG)
        picked = (_topk_mask_rows(gate, MOBA_TOP_K) > 0.5) & (blk_i < qi)
        bias_ref[hh] = jnp.where(picked, 0.0, NEG)
        qs.append(q_raw * jnp.asarray(ATT_SCALE, BF16))
        states.append(_flash_t_first(qs[-1], k_ref[pl.ds(start, blk), lanes],
                                     vt_ref[lanes, pl.ds(start, blk)], causal_bias))

    def body(p, sts):
        s0 = pl.multiple_of(p * wide, wide)
        scores = [_dot_nt(k_ref[pl.ds(s0, wide), lanes], qs[hh]) for hh, lanes in enumerate(head_rows)]
        probs, nxt = [], []
        for hh in range(MOBA_HEADS):
            m, l, acc_t = sts[hh]
            bias = jnp.concatenate(
                [jnp.broadcast_to(bias_ref[hh, pl.ds(p * MOBA_WALK + w, 1), :], (blk, blk))
                 for w in range(MOBA_WALK)], axis=0)
            s = scores[hh] + bias
            m_new = jnp.maximum(m, jnp.max(s, axis=0, keepdims=True))
            pr = jnp.exp(s - m_new)
            alpha = jnp.exp(m - m_new)
            probs.append(pr.astype(BF16))
            nxt.append((m_new, alpha * l + jnp.sum(pr, axis=0, keepdims=True), alpha * acc_t))
        return tuple((m_new, l, acc_s + _dot(vt_ref[lanes, pl.ds(s0, wide)], probs[hh]))
                     for hh, (lanes, (m_new, l, acc_s)) in enumerate(zip(head_rows, nxt)))

    states = lax.fori_loop(0, (qi + MOBA_WALK - 1) // MOBA_WALK, body, tuple(states))
    for hh, lanes in enumerate(head_rows):
        _, l, acc_t = states[hh]
        o_ref[:, lanes] = (acc_t / l).T.astype(o_ref.dtype)


def _moba_attention(qkv, batch, seq):
    blk = MOBA_BLOCK
    wide = MOBA_WALK * blk
    nq = seq // blk
    d_model = N_HEADS * HEAD_DIM
    width = MOBA_HEADS * HEAD_DIM
    ncol = d_model // width
    nb = -(-nq // MOBA_BLK_ALIGN) * MOBA_BLK_ALIGN
    avg = np.zeros((nb, seq), np.float32)
    for n in range(nq):
        avg[n, n * blk:(n + 1) * blk] = 1.0 / blk
    avg = jnp.asarray(avg, dtype=BF16)
    v_t = qkv[:, 2 * d_model:].reshape(batch, seq, d_model).transpose(0, 2, 1)
    return pl.pallas_call(
        _moba_kernel,
        grid=(batch, ncol, nq),
        in_specs=[pl.BlockSpec((blk, width), lambda b, h, i: (b * nq + i, h)),
                  pl.BlockSpec((seq, width), lambda b, h, i: (b, ncol + h)),
                  pl.BlockSpec((None, width, seq), lambda b, h, i: (b, h, 0)),
                  pl.BlockSpec((nb, seq), lambda b, h, i: (0, 0))],
        out_specs=pl.BlockSpec((blk, width), lambda b, h, i: (b * nq + i, h)),
        out_shape=jax.ShapeDtypeStruct((batch * seq, d_model), BF16),
        scratch_shapes=[pltpu.VMEM((nb, width), F32),
                        pltpu.VMEM((MOBA_HEADS, nb, blk), F32)],
        compiler_params=_cparams(("parallel", "parallel", "arbitrary")),
        name="moba_attention",
    )(qkv, qkv, v_t, avg)


def _nsa_compress_kernel(kr_ref, vr_ref, pk_ref, pv_ref, wk_ref, wv_ref, c_ref, sa_ref, sb_ref, kc_ref, vc_ref):
    nc = kr_ref.shape[0]
    rowi = lax.broadcasted_iota(jnp.int32, (nc, 1), 0)

    def windows(x_ref, pos_ref, w_ref):
        x = x_ref[...].astype(F32)
        lo = _dot((x + pos_ref[0:1, :]).astype(BF16), w_ref[0])
        hi = _dot((x + pos_ref[1:2, :]).astype(BF16), w_ref[1])
        hi_next = jnp.where(rowi < nc - 1, pltpu.roll(hi, nc - 1, 0), 0.0)
        return lo + hi_next

    kk = windows(kr_ref, pk_ref, wk_ref)
    d = HEAD_DIM
    kc = kk[:, 0:d] * c_ref[...] + kk[:, d:2 * d] * sb_ref[...] + kk[:, 2 * d:3 * d] * sa_ref[...]
    kc_ref[...] = kc.astype(kc_ref.dtype)
    vv = windows(vr_ref, pv_ref, wv_ref)
    vc_ref[...] = vv[:, 0:d].astype(vc_ref.dtype)


def _nsa_compress(k_cmp, v_cmp, pos_k, pos_v, w_k, w_v, batch, seq):
    g, d, st = NSA_GROUPS, HEAD_DIM, NSA_CMP_STRIDE
    nc = seq // st

    def chunks(x):
        return x.reshape(batch, nc, st, g, d).transpose(0, 3, 1, 2, 4).reshape(batch, g, nc, st * d)

    def weights(w):
        w3 = w.reshape(NSA_CMP_LEN, d, d)
        cat = jnp.concatenate([w3, jnp.roll(w3, ROPE_HALF, axis=2), jnp.roll(w3, -ROPE_HALF, axis=2)], axis=2)
        return cat.reshape(2, st * d, 3 * d).astype(BF16)

    def positions(p):
        return p.reshape(2, st * d).astype(F32)

    cmp_end = jnp.arange(nc) * st + NSA_CMP_LEN - 1
    tabs = _rope_tables(cmp_end, d)
    blk4 = pl.BlockSpec((None, None, nc, st * d), lambda b, gi: (b, gi, 0, 0))
    out4 = pl.BlockSpec((None, None, nc, d), lambda b, gi: (b, gi, 0, 0))
    const2 = lambda shape: pl.BlockSpec(shape, lambda b, gi: (0,) * len(shape))
    return pl.pallas_call(
        _nsa_compress_kernel,
        grid=(batch, g),
        in_specs=[blk4, blk4, const2((2, st * d)), const2((2, st * d)),
                  const2((2, st * d, 3 * d)), const2((2, st * d, 3 * d)),
                  const2((nc, d)), const2((nc, d)), const2((nc, d))],
        out_specs=[out4, out4],
        out_shape=[jax.ShapeDtypeStruct((batch, g, nc, d), BF16)] * 2,
        compiler_params=_cparams(("parallel", "parallel")),
        name="nsa_compress",
    )(chunks(k_cmp), chunks(v_cmp), positions(pos_k), positions(pos_v), weights(w_k), weights(w_v), *tabs)


NSA_TQ = 256
NSA_TK = 1024
NSA_CHAINS = 4


def _nsa_kernel(q_ref, kc_ref, vct_ref, ks_ref, vst_ref, kw_ref, vwt_ref, gate_ref, c2st_ref, o_ref, bias_ref):
    tq, tk, d, nh = NSA_TQ, NSA_TK, HEAD_DIM, NSA_HPG
    nql = nh * tq
    half = nql // NSA_CHAINS
    qi = pl.program_id(2)
    t0 = qi * tq
    nc = kc_ref.shape[0]

    def lanes4(x):
        return jnp.concatenate([x] * nh, axis=1)

    tpos1 = t0 + lax.broadcasted_iota(jnp.int32, (1, tq), 1)
    tpos = lanes4(tpos1)
    gates_t = jax.nn.sigmoid(gate_ref[...]).T

    def gate(branch):
        return jnp.concatenate([gates_t[3 * h + branch:3 * h + branch + 1, :] for h in range(nh)], axis=1)

    q = jnp.concatenate([q_ref[:, h * d:(h + 1) * d] for h in range(nh)], axis=0) * jnp.asarray(ATT_SCALE, BF16)

    kd = t0 // tk
    dstart = pl.multiple_of(kd * tk, tk)
    span = NSA_WINDOW + tq
    wstart = pl.multiple_of(jnp.maximum(t0 - NSA_WINDOW, 0), tq)
    s_cmp = _dot_nt(kc_ref[...], q)
    s_win = _dot_nt(kw_ref[pl.ds(wstart, span), :], q)
    s_diag = _dot_nt(ks_ref[pl.ds(dstart, tk), :], q)

    cmp_end = lax.broadcasted_iota(jnp.int32, (nc, nql), 0) * NSA_CMP_STRIDE + (NSA_CMP_LEN - 1)
    vis_c = cmp_end <= tpos
    s = jnp.where(vis_c, s_cmp, NEG)
    e = jnp.where(vis_c, jnp.exp(s - jnp.max(s, axis=0, keepdims=True)), 0.0)
    l = jnp.sum(e, axis=0, keepdims=True)
    p = e / jnp.where(l > 0.0, l, 1.0)
    out_t = gate(0) * _dot(vct_ref[...], p.astype(BF16))
    psum = functools.reduce(lambda a, b: a + b, [p[:, h * tq:(h + 1) * tq] for h in range(nh)])
    p_hi, p_lo = _split_bf16(psum)
    imp = _dot(c2st_ref[...], p_hi) + _dot(c2st_ref[...], p_lo)
    blk = lax.broadcasted_iota(jnp.int32, (NSA_SEL_PAD, tq), 0)
    own = tpos1 // NSA_SEL_LEN
    forced = (blk == 0) | (blk == own) | (blk == own - 1)
    imp = jnp.where(blk <= own, imp + jnp.where(forced, NSA_BONUS, 0.0), NEG)
    picked = _topk_mask_rows(imp, NSA_TOP_N) > 0.5
    bias_ref[...] = lanes4(jnp.where(picked, 0.0, NEG))

    per_tile = tk // NSA_SEL_LEN

    def tile_bias(kb):
        return jnp.concatenate(
            [jnp.broadcast_to(bias_ref[pl.ds(kb * per_tile + w, 1), :], (NSA_SEL_LEN, nql))
             for w in range(per_tile)], axis=0)

    wpos = wstart + lax.broadcasted_iota(jnp.int32, (span, nql), 0)
    s = jnp.where((wpos <= tpos) & (wpos > tpos - NSA_WINDOW), s_win, NEG)
    e = jnp.exp(s - jnp.max(s, axis=0, keepdims=True))
    out_t = out_t + (gate(2) / jnp.sum(e, axis=0, keepdims=True)) * _dot(vwt_ref[:, pl.ds(wstart, span)],
                                                                         e.astype(BF16))

    kpos = dstart + lax.broadcasted_iota(jnp.int32, (tk, nql), 0)
    s = jnp.where(kpos <= tpos, s_diag + tile_bias(kd), NEG)
    m = jnp.max(s, axis=0, keepdims=True)
    e = jnp.exp(s - m)
    l = jnp.sum(e, axis=0, keepdims=True)
    acc = _dot(vst_ref[:, pl.ds(dstart, tk)], e.astype(BF16))
    halves = [slice(i * half, (i + 1) * half) for i in range(NSA_CHAINS)]
    q_half = [q[hs] for hs in halves]
    states = tuple((m[:, hs], l[:, hs], acc[:, hs]) for hs in halves)

    def sel_body(kb, sts):
        start = pl.multiple_of(kb * tk, tk)
        k = ks_ref[pl.ds(start, tk), :]
        vt = vst_ref[:, pl.ds(start, tk)]
        bias = tile_bias(kb)
        scores = [_dot_nt(k, q_half[i]) for i in range(NSA_CHAINS)]
        probs, nxt = [], []
        for i, hs in enumerate(halves):
            m, l, acc = sts[i]
            s = scores[i] + bias[:, hs]
            m_new = jnp.maximum(m, jnp.max(s, axis=0, keepdims=True))
            pr = jnp.exp(s - m_new)
            alpha = jnp.exp(m - m_new)
            probs.append(pr.astype(BF16))
            nxt.append((m_new, alpha * l + jnp.sum(pr, axis=0, keepdims=True), alpha * acc))
        return tuple((m_new, l, acc + _dot(vt, probs[i])) for i, (m_new, l, acc) in enumerate(nxt))

    states = lax.fori_loop(0, kd, sel_body, states)
    sel_t = jnp.concatenate([st[2] / st[1] for st in states], axis=1)
    out_t = out_t + gate(1) * sel_t
    for h in range(nh):
        o_ref[:, h * d:(h + 1) * d] = out_t[:, h * tq:(h + 1) * tq].T.astype(o_ref.dtype)


def _nsa_attention(q_all, kc, vct, ks, vst, kw, vwt, gates, batch, seq):
    tq, tk, d, g = NSA_TQ, NSA_TK, HEAD_DIM, NSA_GROUPS
    nq = seq // tq
    nc = seq // NSA_CMP_STRIDE
    n_sel = seq // NSA_SEL_LEN
    d_model = N_HEADS * HEAD_DIM
    cs = np.arange(nc) * NSA_CMP_STRIDE
    ss = np.arange(n_sel) * NSA_SEL_LEN
    ov = np.clip(np.minimum(cs[:, None] + NSA_CMP_LEN, ss[None, :] + NSA_SEL_LEN)
                 - np.maximum(cs[:, None], ss[None, :]), 0, None) / NSA_CMP_LEN
    c2s = np.zeros((nc, NSA_SEL_PAD), np.float32)
    c2s[:, :n_sel] = ov
    c2s[nc - 1, :] = 0.0
    k_spec = lambda n: pl.BlockSpec((None, None, n, d), lambda b, gi, i: (b, gi, 0, 0))
    vt_spec = lambda n: pl.BlockSpec((None, None, d, n), lambda b, gi, i: (b, gi, 0, 0))
    return pl.pallas_call(
        _nsa_kernel,
        grid=(batch, g, nq),
        in_specs=[pl.BlockSpec((tq, NSA_HPG * d), lambda b, gi, i: (b * nq + i, gi)),
                  k_spec(nc), vt_spec(nc), k_spec(seq), vt_spec(seq), k_spec(seq), vt_spec(seq),
                  pl.BlockSpec((tq, LANE), lambda b, gi, i: (b * nq + i, gi)),
                  pl.BlockSpec((NSA_SEL_PAD, nc), lambda b, gi, i: (0, 0))],
        out_specs=pl.BlockSpec((tq, NSA_HPG * d), lambda b, gi, i: (b * nq + i, gi)),
        out_shape=jax.ShapeDtypeStruct((batch * seq, d_model), BF16),
        scratch_shapes=[pltpu.VMEM((NSA_SEL_PAD, NSA_HPG * tq), F32)],
        compiler_params=_cparams(("parallel", "parallel", "arbitrary")),
        name="nsa_attention",
    )(q_all, kc, vct, ks, vst, kw, vwt, gates, jnp.asarray(c2s.T, dtype=BF16))


def _nsa_mixer(x2, g_norm, w_in, pos_k, pos_v, w_k, w_v, w_out, batch, seq):
    d_model, kv = N_HEADS * HEAD_DIM, NSA_KV
    sec = lambda i: w_in[:, d_model + i * kv: d_model + (i + 1) * kv]
    w_main = jnp.concatenate([w_in[:, :d_model], sec(2), sec(4), sec(0), sec(1), sec(3), sec(5)], axis=1)
    proj = _norm_matmul(x2, g_norm, w_main, seq, tn=512, n_rope=3)
    w_gate = w_in[:, d_model + 6 * kv:].reshape(d_model, NSA_GROUPS, 3 * NSA_HPG)
    w_gate = jnp.pad(w_gate, ((0, 0), (0, 0), (0, LANE - 3 * NSA_HPG))).reshape(d_model, NSA_GROUPS * LANE)
    gates = _norm_matmul(x2, g_norm, w_gate, seq, tn=NSA_GROUPS * LANE, out_dtype=F32)

    col = lambda i: proj[:, d_model + i * kv: d_model + (i + 1) * kv]
    heads = lambda x: x.reshape(batch, seq, NSA_GROUPS, HEAD_DIM).transpose(0, 2, 1, 3)
    heads_t = lambda x: x.reshape(batch, seq, NSA_GROUPS, HEAD_DIM).transpose(0, 2, 3, 1)
    kc, vc = _nsa_compress(col(2), col(3), pos_k, pos_v, w_k, w_v, batch, seq)
    o = _nsa_attention(proj, kc, vc.transpose(0, 1, 3, 2), heads(col(0)), heads_t(col(4)), heads(col(1)),
                       heads_t(col(5)), gates, batch, seq)
    return _proj_resid(o, w_out, x2)


PEER_SEL_TT = 256
def _batcher_network(n):
    def merge(lo, hi, r):
        step = r * 2
        if step < hi - lo:
            yield from merge(lo, hi, step)
            yield from merge(lo + r, hi, step)
            yield from ((i, i + r) for i in range(lo + r, hi - r, step))
        else:
            yield (lo, lo + r)

    def sort(lo, hi):
        if hi - lo >= 1:
            mid = lo + (hi - lo) // 2
            yield from sort(lo, mid)
            yield from sort(mid + 1, hi)
            yield from merge(lo, hi, 1)

    return list(sort(0, n - 1))


SORT16_NETWORK = _batcher_network(PEER_KEYS // 8)

PEER_PAIRS = [(i, j) for i in range(PEER_TOP_K) for j in range(PEER_TOP_K) if (i + 1) * (j + 1) <= PEER_TOP_K]


def _peer_select_kernel(q_ref, k1_ref, k2_ref, cnt_ref, e1_ref, rank_ref, e2_ref):
    q = q_ref[...]
    q_hi, q_lo = _split_bf16(q)
    nh, nr = PEER_HEADS, PEER_TOP_K

    def scores(k_ref, off):
        k_hi, k_lo = _split_bf16(k_ref[...])
        out = []
        for h in range(nh):
            lanes = slice(off + h * PEER_HALF, off + (h + 1) * PEER_HALF)
            out.append(_dot_nt(k_hi, q_hi[:, lanes]) + _dot_nt(k_hi, q_lo[:, lanes])
                       + _dot_nt(k_lo, q_hi[:, lanes]))
        return out

    def top_values(s):
        slabs = [s[8 * k:8 * (k + 1), :] for k in range(PEER_KEYS // 8)]
        for a, b in SORT16_NETWORK:
            slabs[a], slabs[b] = jnp.maximum(slabs[a], slabs[b]), jnp.minimum(slabs[a], slabs[b])
        vals = []
        for r in range(nr):
            m = jnp.max(slabs[0], axis=0, keepdims=True)
            vals.append(m)
            hit = slabs[0] == m
            for k in range(nr - 1 - r):
                slabs[k] = jnp.where(hit, slabs[k + 1], slabs[k])
        return vals

    s1 = scores(k1_ref, 0)
    s2 = scores(k2_ref, nh * PEER_HALF)
    t1 = [top_values(s) for s in s1]
    t2 = [top_values(s) for s in s2]
    r1 = [jnp.concatenate([t1[h][r] for h in range(nh)], axis=0) for r in range(nr)]
    r2 = [jnp.concatenate([t2[h][r] for h in range(nh)], axis=0) for r in range(nr)]
    cands = [r1[i] + r2[j] for (i, j) in PEER_PAIRS]
    work, tops = list(cands), []
    for _ in range(nr):
        m = functools.reduce(jnp.maximum, work)
        tops.append(m)
        work = [jnp.where(c == m, LOW, c) for c in work]
    thr = tops[PEER_TOP_K - 1]
    cmax = tops[0]
    z = functools.reduce(lambda a, b: a + b, [jnp.where(c >= thr, jnp.exp(c - cmax), 0.0) for c in cands])
    inv_z = 1.0 / z
    for h in range(nh):
        thr_h = thr[h:h + 1, :]
        cnt = jnp.zeros_like(s1[h])
        rank = jnp.zeros_like(s2[h])
        for r in range(nr):
            cnt = jnp.where(s1[h] + t2[h][r] >= thr_h, float(r + 1), cnt)
            rank = jnp.where(t2[h][r] > s2[h], float(r + 1), rank)
        cnt_ref[h] = cnt
        e1_ref[h] = jnp.exp(s1[h] - r1[0][h:h + 1, :])
        rank_ref[h] = rank.astype(BF16)
        e2_ref[h] = (jnp.exp(s2[h] - r2[0][h:h + 1, :]) * inv_z[h:h + 1, :]).astype(BF16)


def _peer_select(q, sub_keys):
    t = q.shape[0]
    tt = min(PEER_SEL_TT, t)
    out_spec = pl.BlockSpec((PEER_HEADS, PEER_KEYS, tt), lambda i: (0, 0, i))
    return pl.pallas_call(
        _peer_select_kernel,
        grid=(t // tt,),
        in_specs=[pl.BlockSpec((tt, q.shape[1]), lambda i: (i, 0)),
                  pl.BlockSpec((PEER_KEYS, PEER_HALF), lambda i: (0, 0)),
                  pl.BlockSpec((PEER_KEYS, PEER_HALF), lambda i: (0, 0))],
        out_specs=[out_spec] * 4,
        out_shape=[jax.ShapeDtypeStruct((PEER_HEADS, PEER_KEYS, t), dt) for dt in (F32, F32, BF16, BF16)],
        compiler_params=_cparams(("parallel",)),
        name="peer_select",
    )(q, sub_keys[0].astype(F32), sub_keys[1].astype(F32))


PEER_TT = 512
PEER_NE = 2048
PEER_SUB = 512


def _gelu(x):
    return 0.5 * x * (1.0 + lax.erf(x * np.float32(np.sqrt(0.5))))


def _peer_main_kernel(x_ref, g_ref, u_ref, vt_ref, cnt_ref, e1_ref, rank_ref, e2_ref, o_ref, xnt_ref, acc_ref):
    e = pl.program_id(1)
    ne = u_ref.shape[0]
    tt = x_ref.shape[0]
    sub = PEER_SUB
    n_sub = ne // sub

    @pl.when(e == 0)
    def _():
        x = x_ref[...]
        y = x * lax.rsqrt(jnp.mean(x * x, axis=-1, keepdims=True) + NORM_EPS)
        xnt_ref[...] = (y * g_ref[...]).T.astype(BF16)
        acc_ref[...] = jnp.zeros_like(acc_ref)

    def hidden(c):
        return _dot(u_ref[c * sub:(c + 1) * sub, :], xnt_ref[...])

    def weighted(c, h):
        act = _gelu(h).astype(BF16)
        parts = []
        for k in range(sub // PEER_KEYS):
            i1 = (e * n_sub + c) * (sub // PEER_KEYS) + k
            gsum = jnp.zeros((PEER_KEYS, tt), BF16)
            for hd in range(PEER_HEADS):
                cnt = cnt_ref[hd, pl.ds(i1, 1), :].astype(BF16)
                e1 = e1_ref[hd, pl.ds(i1, 1), :].astype(BF16)
                gsum = gsum + jnp.where(rank_ref[hd] < cnt, e2_ref[hd], jnp.zeros((), BF16)) * e1
            parts.append(gsum * act[k * PEER_KEYS:(k + 1) * PEER_KEYS, :])
        return jnp.concatenate(parts, axis=0) if len(parts) > 1 else parts[0]

    def values(c, w):
        return _dot(vt_ref[:, c * sub:(c + 1) * sub], w)

    hs = {0: hidden(0)}
    if n_sub > 1:
        hs[1] = hidden(1)
    total = None
    for c in range(n_sub):
        w = weighted(c, hs.pop(c))
        if c + 2 < n_sub:
            hs[c + 2] = hidden(c + 2)
        pv = values(c, w)
        total = pv if total is None else total + pv
    acc_ref[...] += total

    @pl.when(e == pl.num_programs(1) - 1)
    def _():
        o_ref[...] = x_ref[...] + acc_ref[...].T


def _peer_mixer(x2, g_norm, w_q, sub_keys, u, v, seq):
    t, d = x2.shape
    w_q2 = w_q.reshape(d, PEER_HEADS, 2, PEER_HALF).transpose(0, 2, 1, 3).reshape(d, 2 * PEER_HEADS * PEER_HALF)
    q = _norm_matmul3(x2, g_norm, w_q2)
    cnt, e1, rank, e2 = _peer_select(q, sub_keys)
    tt = min(PEER_TT, t)
    ne = PEER_NE
    n_exp = u.shape[0]
    tok_spec = pl.BlockSpec((PEER_HEADS, PEER_KEYS, tt), lambda i, e: (0, 0, i))
    return pl.pallas_call(
        _peer_main_kernel,
        grid=(t // tt, n_exp // ne),
        in_specs=[pl.BlockSpec((tt, d), lambda i, e: (i, 0)),
                  pl.BlockSpec((1, d), lambda i, e: (0, 0)),
                  pl.BlockSpec((ne, d), lambda i, e: (e, 0)),
                  pl.BlockSpec((d, ne), lambda i, e: (0, e)),
                  tok_spec, tok_spec, tok_spec, tok_spec],
        out_specs=pl.BlockSpec((tt, d), lambda i, e: (i, 0)),
        out_shape=jax.ShapeDtypeStruct((t, d), F32),
        scratch_shapes=[pltpu.VMEM((d, tt), BF16), pltpu.VMEM((d, tt), F32)],
        compiler_params=_cparams(("parallel", "arbitrary")),
        name="peer_main",
    )(x2, g_norm.reshape(1, d).astype(F32), u.astype(BF16), v.T.astype(BF16), cnt, e1, rank, e2)


def _sb_mixer(x2, g_norm, w_in, w_out, batch, seq):
    qkv = _norm_matmul(x2, g_norm, w_in, seq, tn=1024)
    return _proj_resid(_sb_attention(qkv, batch, seq), w_out, x2)


def _moba_mixer(x2, g_norm, w_in, w_out, batch, seq):
    d_model = N_HEADS * HEAD_DIM
    qkv = _norm_matmul(x2, g_norm, w_in, seq, tn=512, n_rope=2 * d_model // 512)
    return _proj_resid(_moba_attention(qkv, batch, seq), w_out, x2)


def kernel(x, l0_norm_mix, l0_sb_w_in, l0_sb_w_out, l0_norm_ffn, l0_peer_w_q, l0_peer_sub_keys, l0_peer_u, l0_peer_v, l1_norm_mix, l1_nsa_w_in, l1_nsa_cmp_pos_k, l1_nsa_cmp_pos_v, l1_nsa_cmp_w_k, l1_nsa_cmp_w_v, l1_nsa_w_out, l1_norm_ffn, l1_peer_w_q, l1_peer_sub_keys, l1_peer_u, l1_peer_v, l2_norm_mix, l2_moba_w_in, l2_moba_w_out, l2_norm_ffn, l2_peer_w_q, l2_peer_sub_keys, l2_peer_u, l2_peer_v, l3_norm_mix, l3_sb_w_in, l3_sb_w_out, l3_norm_ffn, l3_peer_w_q, l3_peer_sub_keys, l3_peer_u, l3_peer_v, final_norm):
    batch, seq, d = x.shape
    x2 = x.reshape(batch * seq, d)
    x2 = _sb_mixer(x2, l0_norm_mix, l0_sb_w_in, l0_sb_w_out, batch, seq)
    x2 = _peer_mixer(x2, l0_norm_ffn, l0_peer_w_q, l0_peer_sub_keys, l0_peer_u, l0_peer_v, seq)
    x2 = _nsa_mixer(x2, l1_norm_mix, l1_nsa_w_in, l1_nsa_cmp_pos_k, l1_nsa_cmp_pos_v, l1_nsa_cmp_w_k,
                    l1_nsa_cmp_w_v, l1_nsa_w_out, batch, seq)
    x2 = _peer_mixer(x2, l1_norm_ffn, l1_peer_w_q, l1_peer_sub_keys, l1_peer_u, l1_peer_v, seq)
    x2 = _moba_mixer(x2, l2_norm_mix, l2_moba_w_in, l2_moba_w_out, batch, seq)
    x2 = _peer_mixer(x2, l2_norm_ffn, l2_peer_w_q, l2_peer_sub_keys, l2_peer_u, l2_peer_v, seq)
    x2 = _sb_mixer(x2, l3_norm_mix, l3_sb_w_in, l3_sb_w_out, batch, seq)
    x2 = _peer_mixer(x2, l3_norm_ffn, l3_peer_w_q, l3_peer_sub_keys, l3_peer_u, l3_peer_v, seq)
    return _rmsnorm(x2, final_norm).reshape(batch, seq, d)
```

```python
import functools

import numpy as np
import jax
import jax.numpy as jnp
from jax import lax
from jax.experimental import pallas as pl
from jax.experimental.pallas import tpu as pltpu

F32 = jnp.float32
BF16 = jnp.bfloat16

N_HEADS = 16
HEAD_DIM = 64
ROPE_DIM = 16
ROPE_HALF = ROPE_DIM // 2
ROPE_THETA = 500000.0
NORM_EPS = 1e-6
NEG = -1e30
LOW = -3e38
ATT_SCALE = HEAD_DIM ** -0.5

NSA_GROUPS = 4
NSA_HPG = N_HEADS // NSA_GROUPS
NSA_KV = NSA_GROUPS * HEAD_DIM
NSA_CMP_LEN = 32
NSA_CMP_STRIDE = 16
NSA_SEL_LEN = 64
NSA_TOP_N = 16
NSA_WINDOW = 512
NSA_BONUS = 1e3
NSA_SEL_PAD = 128

MOBA_BLOCK = 256
MOBA_TOP_K = 3
MOBA_BLK_ALIGN = 16

PEER_HEADS = 8
PEER_KEYS = 128
PEER_TOP_K = 16
PEER_HALF = 64

SB_EXIT = -110.0

LANE = 128
VMEM_LIMIT = 56 << 20


def _cparams(sem, vmem=VMEM_LIMIT):
    return pltpu.CompilerParams(dimension_semantics=sem, vmem_limit_bytes=vmem)


def _dot(a, b):
    return jnp.dot(a, b, preferred_element_type=F32)


def _dot_nt(a, b):
    return lax.dot_general(a, b, (((1,), (1,)), ((), ())), preferred_element_type=F32)


def _split_bf16(x):
    hi = x.astype(BF16)
    lo = (x - hi.astype(F32)).astype(BF16)
    return hi, lo


def _norm_matmul_kernel(*refs, n_rope, tn):
    if n_rope:
        x_ref, g_ref, w_ref, c_ref, sa_ref, sb_ref, o_ref, xn_ref = refs
    else:
        x_ref, g_ref, w_ref, o_ref, xn_ref = refs
    j = pl.program_id(1)

    @pl.when(j == 0)
    def _():
        x = x_ref[...]
        y = x * lax.rsqrt(jnp.mean(x * x, axis=-1, keepdims=True) + NORM_EPS)
        xn_ref[...] = (y * g_ref[...]).astype(BF16)

    acc = _dot(xn_ref[...], w_ref[...])
    if n_rope:
        @pl.when(j < n_rope)
        def _():
            wide = lambda t_ref: jnp.concatenate([t_ref[...]] * (tn // LANE), axis=1)
            r = (acc * wide(c_ref) + pltpu.roll(acc, ROPE_HALF, 1) * wide(sb_ref)
                 + pltpu.roll(acc, tn - ROPE_HALF, 1) * wide(sa_ref))
            o_ref[...] = r.astype(o_ref.dtype)

        @pl.when(j >= n_rope)
        def _():
            o_ref[...] = acc.astype(o_ref.dtype)
    else:
        o_ref[...] = acc.astype(o_ref.dtype)


def _rope_tables(pos, width):
    inv = ROPE_THETA ** (-jnp.arange(0, ROPE_DIM, 2, dtype=F32) / ROPE_DIM)
    ang = pos.astype(F32)[:, None] * inv[None, :]
    cos, sin = jnp.cos(ang), jnp.sin(ang)
    n = pos.shape[0]
    rest = HEAD_DIM - ROPE_DIM
    c = jnp.concatenate([cos, cos, jnp.ones((n, rest), F32)], axis=1)
    sa = jnp.concatenate([-sin, jnp.zeros((n, ROPE_HALF + rest), F32)], axis=1)
    sb = jnp.concatenate([jnp.zeros((n, ROPE_HALF), F32), sin, jnp.zeros((n, rest), F32)], axis=1)
    reps = width // HEAD_DIM
    return jnp.tile(c, (1, reps)), jnp.tile(sa, (1, reps)), jnp.tile(sb, (1, reps))


def _norm_matmul(x2, g, w, seq, *, tn, n_rope=0, out_dtype=BF16, tt=1024):
    t, d = x2.shape
    n = w.shape[1]
    tt = min(tt, seq)
    grid = (t // tt, n // tn)
    in_specs = [pl.BlockSpec((tt, d), lambda i, j: (i, 0)),
                pl.BlockSpec((1, d), lambda i, j: (0, 0)),
                pl.BlockSpec((d, tn), lambda i, j: (0, j))]
    args = [x2, g.reshape(1, d).astype(F32), w.astype(BF16)]
    if n_rope:
        nper = seq // tt
        tabs = _rope_tables(jnp.arange(seq), LANE)
        in_specs += [pl.BlockSpec((tt, LANE), lambda i, j: (i % nper, 0))] * 3
        args += list(tabs)
    return pl.pallas_call(
        functools.partial(_norm_matmul_kernel, n_rope=n_rope, tn=tn),
        grid=grid, in_specs=in_specs,
        out_specs=pl.BlockSpec((tt, tn), lambda i, j: (i, j)),
        out_shape=jax.ShapeDtypeStruct((t, n), out_dtype),
        scratch_shapes=[pltpu.VMEM((tt, d), BF16)],
        compiler_params=_cparams(("parallel", "arbitrary")),
        name="norm_matmul",
    )(*args)


def _norm_matmul3_kernel(x_ref, g_ref, wh_ref, wl_ref, o_ref, xh_ref, xl_ref):
    @pl.when(pl.program_id(1) == 0)
    def _():
        x = x_ref[...]
        y = x * lax.rsqrt(jnp.mean(x * x, axis=-1, keepdims=True) + NORM_EPS) * g_ref[...]
        xh_ref[...], xl_ref[...] = _split_bf16(y)

    o_ref[...] = (_dot(xh_ref[...], wh_ref[...]) + _dot(xl_ref[...], wh_ref[...])
                  + _dot(xh_ref[...], wl_ref[...]))


def _norm_matmul3(x2, g, w, *, tn=512, tt=1024):
    t, d = x2.shape
    n = w.shape[1]
    tt = min(tt, t)
    w_hi, w_lo = _split_bf16(w.astype(F32))
    w_spec = pl.BlockSpec((d, tn), lambda i, j: (0, j))
    return pl.pallas_call(
        _norm_matmul3_kernel,
        grid=(t // tt, n // tn),
        in_specs=[pl.BlockSpec((tt, d), lambda i, j: (i, 0)),
                  pl.BlockSpec((1, d), lambda i, j: (0, 0)), w_spec, w_spec],
        out_specs=pl.BlockSpec((tt, tn), lambda i, j: (i, j)),
        out_shape=jax.ShapeDtypeStruct((t, n), F32),
        scratch_shapes=[pltpu.VMEM((tt, d), BF16), pltpu.VMEM((tt, d), BF16)],
        compiler_params=_cparams(("parallel", "arbitrary")),
        name="norm_matmul3",
    )(x2, g.reshape(1, d).astype(F32), w_hi, w_lo)


def _proj_resid_kernel(a_ref, w_ref, r_ref, o_ref):
    o_ref[...] = r_ref[...] + _dot(a_ref[...], w_ref[...])


def _proj_resid(a, w, resid, *, tt=1024, tn=1024):
    t, d = a.shape
    n = w.shape[1]
    tt = min(tt, t)
    return pl.pallas_call(
        _proj_resid_kernel,
        grid=(t // tt, n // tn),
        in_specs=[pl.BlockSpec((tt, d), lambda i, j: (i, 0)),
                  pl.BlockSpec((d, tn), lambda i, j: (0, j)),
                  pl.BlockSpec((tt, tn), lambda i, j: (i, j))],
        out_specs=pl.BlockSpec((tt, tn), lambda i, j: (i, j)),
        out_shape=jax.ShapeDtypeStruct((t, n), F32),
        compiler_params=_cparams(("parallel", "parallel")),
        name="proj_resid",
    )(a, w.astype(BF16), resid)


def _rmsnorm_kernel(x_ref, g_ref, o_ref):
    x = x_ref[...]
    y = x * lax.rsqrt(jnp.mean(x * x, axis=-1, keepdims=True) + NORM_EPS)
    o_ref[...] = y * g_ref[...]


def _rmsnorm(x2, g, *, tt=512):
    t, d = x2.shape
    tt = min(tt, t)
    return pl.pallas_call(
        _rmsnorm_kernel,
        grid=(t // tt,),
        in_specs=[pl.BlockSpec((tt, d), lambda i: (i, 0)), pl.BlockSpec((1, d), lambda i: (0, 0))],
        out_specs=pl.BlockSpec((tt, d), lambda i: (i, 0)),
        out_shape=jax.ShapeDtypeStruct((t, d), F32),
        compiler_params=_cparams(("parallel",)),
        name="final_rmsnorm",
    )(x2, g.reshape(1, d).astype(F32))


SB_TILE = 256
SB_FIRST = 2 * SB_TILE
SB_HEADS = 4


def _sb_kernel(q_ref, k_ref, v_ref, tri_ref, o_ref):
    tile = SB_TILE
    wide = SB_FIRST
    qi = pl.program_id(2)
    t0 = qi * tile

    nh = SB_HEADS
    head_lanes = [slice(hh * HEAD_DIM, (hh + 1) * HEAD_DIM) for hh in range(nh)]
    qs = [q_ref[:, lanes] * jnp.asarray(ATT_SCALE, BF16) for lanes in head_lanes]

    def step(start, width, r_sums, accs, past):
        tri = tri_ref[0:width, 0:width]
        zs = [_dot_nt(qs[hh], k_ref[pl.ds(start, width), lanes]) for hh, lanes in enumerate(head_lanes)]
        sps, lbs, splits = [], [], []
        for z in zs:
            sp = jnp.maximum(z, 0.0) + jnp.log(1.0 + jnp.exp(-jnp.abs(z)))
            lb = -sp if past is None else jnp.where(past, -sp, 0.0)
            sps.append(sp)
            lbs.append(lb)
            splits.append(_split_bf16(lb))
        suffixes = [_dot(hi, tri) + _dot(lo, tri) for hi, lo in splits]
        weights = []
        for hh in range(nh):
            a = jnp.exp((zs[hh] - sps[hh]) + suffixes[hh] + r_sums[hh])
            if past is not None:
                a = jnp.where(past, a, 0.0)
            weights.append(a.astype(BF16))
        accs = tuple(accs[hh] + _dot(weights[hh], v_ref[pl.ds(start, width), lanes])
                     for hh, lanes in enumerate(head_lanes))
        r_sums = tuple(r_sums[hh] + jnp.sum(lbs[hh], axis=-1, keepdims=True) for hh in range(nh))
        return r_sums, accs

    start0 = pl.multiple_of(jnp.maximum(t0 + tile - wide, 0), tile)
    kpos = start0 + lax.broadcasted_iota(jnp.int32, (tile, wide), 1)
    tpos = t0 + lax.broadcasted_iota(jnp.int32, (tile, wide), 0)
    zero_r = tuple(jnp.zeros((tile, 1), F32) for _ in range(nh))
    zero_acc = tuple(jnp.zeros((tile, HEAD_DIM), F32) for _ in range(nh))
    r1, acc1 = step(start0, wide, zero_r, zero_acc, kpos < tpos)

    def older(j, r_sums):
        alive = jnp.max(functools.reduce(jnp.maximum, r_sums)) > SB_EXIT
        return jnp.where(alive, j - 1, -1)

    def cond(c):
        return c[0] >= 0

    def body(c):
        j, r_sums, accs = c
        r_sums, accs = step(pl.multiple_of(j * tile, tile), tile, r_sums, accs, None)
        return older(j, r_sums), r_sums, accs

    _, _, accs = lax.while_loop(cond, body, (older(start0 // tile, r1), r1, acc1))
    for hh, lanes in enumerate(head_lanes):
        o_ref[:, lanes] = accs[hh].astype(o_ref.dtype)


def _sb_attention(qkv, batch, seq):
    tile = SB_TILE
    nq = seq // tile
    d_model = N_HEADS * HEAD_DIM
    width = SB_HEADS * HEAD_DIM
    ncol = d_model // width
    ii = np.arange(SB_FIRST)
    tri = jnp.asarray(ii[:, None] > ii[None, :], dtype=BF16)
    return pl.pallas_call(
        _sb_kernel,
        grid=(batch, ncol, nq),
        in_specs=[pl.BlockSpec((tile, width), lambda b, h, i: (b * nq + i, h)),
                  pl.BlockSpec((seq, width), lambda b, h, i: (b, ncol + h)),
                  pl.BlockSpec((seq, width), lambda b, h, i: (b, 2 * ncol + h)),
                  pl.BlockSpec((SB_FIRST, SB_FIRST), lambda b, h, i: (0, 0))],
        out_specs=pl.BlockSpec((tile, width), lambda b, h, i: (b * nq + i, h)),
        out_shape=jax.ShapeDtypeStruct((batch * seq, d_model), BF16),
        compiler_params=_cparams(("parallel", "parallel", "arbitrary")),
        name="sb_attention",
    )(qkv, qkv, qkv, tri)


def _flash_t_first(q, k, vt, bias_t):
    s = _dot_nt(k, q) + bias_t
    m = jnp.max(s, axis=0, keepdims=True)
    p = jnp.exp(s - m)
    return m, jnp.sum(p, axis=0, keepdims=True), _dot(vt, p.astype(BF16))


def _topk_mask_rows(vals, k):
    rowi = lax.broadcasted_iota(jnp.int32, vals.shape, 0)
    work = vals
    sel = jnp.zeros(vals.shape, F32)
    for _ in range(k):
        m = jnp.max(work, axis=0, keepdims=True)
        first = jnp.min(jnp.where(work == m, rowi, vals.shape[0]), axis=0, keepdims=True)
        hit = rowi == first
        sel = jnp.where(hit, 1.0, sel)
        work = jnp.where(hit, LOW, work)
    return sel


MOBA_WALK = 4
MOBA_HEADS = 8


def _moba_kernel(q_ref, k_ref, vt_ref, avg_ref, o_ref, kmean_ref, bias_ref):
    blk = MOBA_BLOCK
    wide = MOBA_WALK * blk
    qi = pl.program_id(2)

    @pl.when(qi == 0)
    def _():
        kmean_ref[...] = _dot(avg_ref[...], k_ref[...])

    key_i = lax.broadcasted_iota(jnp.int32, (blk, blk), 0)
    qry_i = lax.broadcasted_iota(jnp.int32, (blk, blk), 1)
    causal_bias = jnp.where(key_i <= qry_i, 0.0, NEG)
    blk_i = lax.broadcasted_iota(jnp.int32, (kmean_ref.shape[0], blk), 0)
    start = pl.multiple_of(qi * blk, blk)
    head_rows = [slice(hh * HEAD_DIM, (hh + 1) * HEAD_DIM) for hh in range(MOBA_HEADS)]

    qs, states = [], []
    for hh, lanes in enumerate(head_rows):
        q_raw = q_ref[:, lanes]
        km_hi, km_lo = _split_bf16(kmean_ref[:, lanes])
        gate = _dot_nt(km_hi, q_raw) + _dot_nt(km_lo, q_raw)
        gate = jnp.where(blk_i < qi, gate, NEG)
        picked = (_topk_mask_rows(gate, MOBA_TOP_K) > 0.5) & (blk_i < qi)
        bias_ref[hh] = jnp.where(picked, 0.0, NEG)
        qs.append(q_raw * jnp.asarray(ATT_SCALE, BF16))
        states.append(_flash_t_first(qs[-1], k_ref[pl.ds(start, blk), lanes],
                                     vt_ref[lanes, pl.ds(start, blk)], causal_bias))

    def body(p, sts):
        s0 = pl.multiple_of(p * wide, wide)
        scores = [_dot_nt(k_ref[pl.ds(s0, wide), lanes], qs[hh]) for hh, lanes in enumerate(head_rows)]
        probs, nxt = [], []
        for hh in range(MOBA_HEADS):
            m, l, acc_t = sts[hh]
            bias = jnp.concatenate(
                [jnp.broadcast_to(bias_ref[hh, pl.ds(p * MOBA_WALK + w, 1), :], (blk, blk))
                 for w in range(MOBA_WALK)], axis=0)
            s = scores[hh] + bias
            m_new = jnp.maximum(m, jnp.max(s, axis=0, keepdims=True))
            pr = jnp.exp(s - m_new)
            alpha = jnp.exp(m - m_new)
            probs.append(pr.astype(BF16))
            nxt.append((m_new, alpha * l + jnp.sum(pr, axis=0, keepdims=True), alpha * acc_t))
        return tuple((m_new, l, acc_s + _dot(vt_ref[lanes, pl.ds(s0, wide)], probs[hh]))
                     for hh, (lanes, (m_new, l, acc_s)) in enumerate(zip(head_rows, nxt)))

    states = lax.fori_loop(0, (qi + MOBA_WALK - 1) // MOBA_WALK, body, tuple(states))
    for hh, lanes in enumerate(head_rows):
        _, l, acc_t = states[hh]
        o_ref[:, lanes] = (acc_t / l).T.astype(o_ref.dtype)


def _moba_attention(qkv, batch, seq):
    blk = MOBA_BLOCK
    wide = MOBA_WALK * blk
    nq = seq // blk
    d_model = N_HEADS * HEAD_DIM
    width = MOBA_HEADS * HEAD_DIM
    ncol = d_model // width
    nb = -(-nq // MOBA_BLK_ALIGN) * MOBA_BLK_ALIGN
    avg = np.zeros((nb, seq), np.float32)
    for n in range(nq):
        avg[n, n * blk:(n + 1) * blk] = 1.0 / blk
    avg = jnp.asarray(avg, dtype=BF16)
    v_t = qkv[:, 2 * d_model:].reshape(batch, seq, d_model).transpose(0, 2, 1)
    return pl.pallas_call(
        _moba_kernel,
        grid=(batch, ncol, nq),
        in_specs=[pl.BlockSpec((blk, width), lambda b, h, i: (b * nq + i, h)),
                  pl.BlockSpec((seq, width), lambda b, h, i: (b, ncol + h)),
                  pl.BlockSpec((None, width, seq), lambda b, h, i: (b, h, 0)),
                  pl.BlockSpec((nb, seq), lambda b, h, i: (0, 0))],
        out_specs=pl.BlockSpec((blk, width), lambda b, h, i: (b * nq + i, h)),
        out_shape=jax.ShapeDtypeStruct((batch * seq, d_model), BF16),
        scratch_shapes=[pltpu.VMEM((nb, width), F32),
                        pltpu.VMEM((MOBA_HEADS, nb, blk), F32)],
        compiler_params=_cparams(("parallel", "parallel", "arbitrary")),
        name="moba_attention",
    )(qkv, qkv, v_t, avg)


def _nsa_compress_kernel(kr_ref, vr_ref, pk_ref, pv_ref, wk_ref, wv_ref, c_ref, sa_ref, sb_ref, kc_ref, vc_ref):
    nc = kr_ref.shape[0]
    rowi = lax.broadcasted_iota(jnp.int32, (nc, 1), 0)

    def windows(x_ref, pos_ref, w_ref):
        x = x_ref[...].astype(F32)
        lo = _dot((x + pos_ref[0:1, :]).astype(BF16), w_ref[0])
        hi = _dot((x + pos_ref[1:2, :]).astype(BF16), w_ref[1])
        hi_next = jnp.where(rowi < nc - 1, pltpu.roll(hi, nc - 1, 0), 0.0)
        return lo + hi_next

    kk = windows(kr_ref, pk_ref, wk_ref)
    d = HEAD_DIM
    kc = kk[:, 0:d] * c_ref[...] + kk[:, d:2 * d] * sb_ref[...] + kk[:, 2 * d:3 * d] * sa_ref[...]
    kc_ref[...] = kc.astype(kc_ref.dtype)
    vv = windows(vr_ref, pv_ref, wv_ref)
    vc_ref[...] = vv[:, 0:d].astype(vc_ref.dtype)


def _nsa_compress(k_cmp, v_cmp, pos_k, pos_v, w_k, w_v, batch, seq):
    g, d, st = NSA_GROUPS, HEAD_DIM, NSA_CMP_STRIDE
    nc = seq // st

    def chunks(x):
        return x.reshape(batch, nc, st, g, d).transpose(0, 3, 1, 2, 4).reshape(batch, g, nc, st * d)

    def weights(w):
        w3 = w.reshape(NSA_CMP_LEN, d, d)
        cat = jnp.concatenate([w3, jnp.roll(w3, ROPE_HALF, axis=2), jnp.roll(w3, -ROPE_HALF, axis=2)], axis=2)
        return cat.reshape(2, st * d, 3 * d).astype(BF16)

    def positions(p):
        return p.reshape(2, st * d).astype(F32)

    cmp_end = jnp.arange(nc) * st + NSA_CMP_LEN - 1
    tabs = _rope_tables(cmp_end, d)
    blk4 = pl.BlockSpec((None, None, nc, st * d), lambda b, gi: (b, gi, 0, 0))
    out4 = pl.BlockSpec((None, None, nc, d), lambda b, gi: (b, gi, 0, 0))
    const2 = lambda shape: pl.BlockSpec(shape, lambda b, gi: (0,) * len(shape))
    return pl.pallas_call(
        _nsa_compress_kernel,
        grid=(batch, g),
        in_specs=[blk4, blk4, const2((2, st * d)), const2((2, st * d)),
                  const2((2, st * d, 3 * d)), const2((2, st * d, 3 * d)),
                  const2((nc, d)), const2((nc, d)), const2((nc, d))],
        out_specs=[out4, out4],
        out_shape=[jax.ShapeDtypeStruct((batch, g, nc, d), BF16)] * 2,
        compiler_params=_cparams(("parallel", "parallel")),
        name="nsa_compress",
    )(chunks(k_cmp), chunks(v_cmp), positions(pos_k), positions(pos_v), weights(w_k), weights(w_v), *tabs)


NSA_TQ = 256
NSA_TK = 1024
NSA_CHAINS = 4


def _nsa_kernel(q_ref, kc_ref, vct_ref, ks_ref, vst_ref, kw_ref, vwt_ref, gate_ref, c2st_ref, o_ref, bias_ref):
    tq, tk, d, nh = NSA_TQ, NSA_TK, HEAD_DIM, NSA_HPG
    nql = nh * tq
    half = nql // NSA_CHAINS
    qi = pl.program_id(2)
    t0 = qi * tq
    nc = kc_ref.shape[0]

    def lanes4(x):
        return jnp.concatenate([x] * nh, axis=1)

    tpos1 = t0 + lax.broadcasted_iota(jnp.int32, (1, tq), 1)
    tpos = lanes4(tpos1)
    gates_t = jax.nn.sigmoid(gate_ref[...]).T

    def gate(branch):
        return jnp.concatenate([gates_t[3 * h + branch:3 * h + branch + 1, :] for h in range(nh)], axis=1)

    q = jnp.concatenate([q_ref[:, h * d:(h + 1) * d] for h in range(nh)], axis=0) * jnp.asarray(ATT_SCALE, BF16)

    kd = t0 // tk
    dstart = pl.multiple_of(kd * tk, tk)
    span = NSA_WINDOW + tq
    wstart = pl.multiple_of(jnp.maximum(t0 - NSA_WINDOW, 0), tq)
    s_cmp = _dot_nt(kc_ref[...], q)
    s_win = _dot_nt(kw_ref[pl.ds(wstart, span), :], q)
    s_diag = _dot_nt(ks_ref[pl.ds(dstart, tk), :], q)

    cmp_end = lax.broadcasted_iota(jnp.int32, (nc, nql), 0) * NSA_CMP_STRIDE + (NSA_CMP_LEN - 1)
    vis_c = cmp_end <= tpos
    s = jnp.where(vis_c, s_cmp, NEG)
    e = jnp.where(vis_c, jnp.exp(s - jnp.max(s, axis=0, keepdims=True)), 0.0)
    l = jnp.sum(e, axis=0, keepdims=True)
    p = e * (1.0 / jnp.where(l > 0.0, l, 1.0))
    out_t = gate(0) * _dot(vct_ref[...], p.astype(BF16))
    psum = functools.reduce(lambda a, b: a + b, [p[:, h * tq:(h + 1) * tq] for h in range(nh)])
    p_hi, p_lo = _split_bf16(psum)
    imp = _dot(c2st_ref[...], p_hi) + _dot(c2st_ref[...], p_lo)
    blk = lax.broadcasted_iota(jnp.int32, (NSA_SEL_PAD, tq), 0)
    own = tpos1 // NSA_SEL_LEN
    forced = (blk == 0) | (blk == own) | (blk == own - 1)
    imp = jnp.where(blk <= own, imp + jnp.where(forced, NSA_BONUS, 0.0), NEG)
    picked = _topk_mask_rows(imp, NSA_TOP_N) > 0.5
    bias_ref[...] = lanes4(jnp.where(picked, 0.0, NEG))

    per_tile = tk // NSA_SEL_LEN

    def tile_bias(kb):
        return jnp.concatenate(
            [jnp.broadcast_to(bias_ref[pl.ds(kb * per_tile + w, 1), :], (NSA_SEL_LEN, nql))
             for w in range(per_tile)], axis=0)

    wpos = wstart + lax.broadcasted_iota(jnp.int32, (span, nql), 0)
    s = jnp.where((wpos <= tpos) & (wpos > tpos - NSA_WINDOW), s_win, NEG)
    e = jnp.exp(s - jnp.max(s, axis=0, keepdims=True))
    out_t = out_t + (gate(2) / jnp.sum(e, axis=0, keepdims=True)) * _dot(vwt_ref[:, pl.ds(wstart, span)],
                                                                         e.astype(BF16))

    kpos = dstart + lax.broadcasted_iota(jnp.int32, (tk, nql), 0)
    s = jnp.where(kpos <= tpos, s_diag + tile_bias(kd), NEG)
    m = jnp.max(s, axis=0, keepdims=True)
    e = jnp.exp(s - m)
    l = jnp.sum(e, axis=0, keepdims=True)
    acc = _dot(vst_ref[:, pl.ds(dstart, tk)], e.astype(BF16))
    halves = [slice(i * half, (i + 1) * half) for i in range(NSA_CHAINS)]
    q_half = [q[hs] for hs in halves]
    states = tuple((m[:, hs], l[:, hs], acc[:, hs]) for hs in halves)

    def sel_body(kb, sts):
        start = pl.multiple_of(kb * tk, tk)
        k = ks_ref[pl.ds(start, tk), :]
        vt = vst_ref[:, pl.ds(start, tk)]
        bias = tile_bias(kb)
        scores = [_dot_nt(k, q_half[i]) for i in range(NSA_CHAINS)]
        probs, nxt = [], []
        for i, hs in enumerate(halves):
            m, l, acc = sts[i]
            s = scores[i] + bias[:, hs]
            m_new = jnp.maximum(m, jnp.max(s, axis=0, keepdims=True))
            pr = jnp.exp(s - m_new)
            alpha = jnp.exp(m - m_new)
            probs.append(pr.astype(BF16))
            nxt.append((m_new, alpha * l + jnp.sum(pr, axis=0, keepdims=True), alpha * acc))
        return tuple((m_new, l, acc + _dot(vt, probs[i])) for i, (m_new, l, acc) in enumerate(nxt))

    states = lax.fori_loop(0, kd, sel_body, states)
    sel_t = jnp.concatenate([st[2] / st[1] for st in states], axis=1)
    out_t = out_t + gate(1) * sel_t
    for h in range(nh):
        o_ref[:, h * d:(h + 1) * d] = out_t[:, h * tq:(h + 1) * tq].T.astype(o_ref.dtype)


def _nsa_attention(q_all, kc, vct, ks, vst, kw, vwt, gates, batch, seq):
    tq, tk, d, g = NSA_TQ, NSA_TK, HEAD_DIM, NSA_GROUPS
    nq = seq // tq
    nc = seq // NSA_CMP_STRIDE
    n_sel = seq // NSA_SEL_LEN
    d_model = N_HEADS * HEAD_DIM
    cs = np.arange(nc) * NSA_CMP_STRIDE
    ss = np.arange(n_sel) * NSA_SEL_LEN
    ov = np.clip(np.minimum(cs[:, None] + NSA_CMP_LEN, ss[None, :] + NSA_SEL_LEN)
                 - np.maximum(cs[:, None], ss[None, :]), 0, None) / NSA_CMP_LEN
    c2s = np.zeros((nc, NSA_SEL_PAD), np.float32)
    c2s[:, :n_sel] = ov
    c2s[nc - 1, :] = 0.0
    k_spec = lambda n: pl.BlockSpec((None, None, n, d), lambda b, gi, i: (b, gi, 0, 0))
    vt_spec = lambda n: pl.BlockSpec((None, None, d, n), lambda b, gi, i: (b, gi, 0, 0))
    return pl.pallas_call(
        _nsa_kernel,
        grid=(batch, g, nq),
        in_specs=[pl.BlockSpec((tq, NSA_HPG * d), lambda b, gi, i: (b * nq + i, gi)),
                  k_spec(nc), vt_spec(nc), k_spec(seq), vt_spec(seq), k_spec(seq), vt_spec(seq),
                  pl.BlockSpec((tq, LANE), lambda b, gi, i: (b * nq + i, gi)),
                  pl.BlockSpec((NSA_SEL_PAD, nc), lambda b, gi, i: (0, 0))],
        out_specs=pl.BlockSpec((tq, NSA_HPG * d), lambda b, gi, i: (b * nq + i, gi)),
        out_shape=jax.ShapeDtypeStruct((batch * seq, d_model), BF16),
        scratch_shapes=[pltpu.VMEM((NSA_SEL_PAD, NSA_HPG * tq), F32)],
        compiler_params=_cparams(("parallel", "parallel", "arbitrary")),
        name="nsa_attention",
    )(q_all, kc, vct, ks, vst, kw, vwt, gates, jnp.asarray(c2s.T, dtype=BF16))


def _nsa_mixer(x2, g_norm, w_in, pos_k, pos_v, w_k, w_v, w_out, batch, seq):
    d_model, kv = N_HEADS * HEAD_DIM, NSA_KV
    sec = lambda i: w_in[:, d_model + i * kv: d_model + (i + 1) * kv]
    w_main = jnp.concatenate([w_in[:, :d_model], sec(2), sec(4), sec(0), sec(1), sec(3), sec(5)], axis=1)
    proj = _norm_matmul(x2, g_norm, w_main, seq, tn=512, n_rope=3)
    w_gate = w_in[:, d_model + 6 * kv:].reshape(d_model, NSA_GROUPS, 3 * NSA_HPG)
    w_gate = jnp.pad(w_gate, ((0, 0), (0, 0), (0, LANE - 3 * NSA_HPG))).reshape(d_model, NSA_GROUPS * LANE)
    gates = _norm_matmul(x2, g_norm, w_gate, seq, tn=NSA_GROUPS * LANE, out_dtype=F32)

    col = lambda i: proj[:, d_model + i * kv: d_model + (i + 1) * kv]
    heads = lambda x: x.reshape(batch, seq, NSA_GROUPS, HEAD_DIM).transpose(0, 2, 1, 3)
    heads_t = lambda x: x.reshape(batch, seq, NSA_GROUPS, HEAD_DIM).transpose(0, 2, 3, 1)
    kc, vc = _nsa_compress(col(2), col(3), pos_k, pos_v, w_k, w_v, batch, seq)
    o = _nsa_attention(proj, kc, vc.transpose(0, 1, 3, 2), heads(col(0)), heads_t(col(4)), heads(col(1)),
                       heads_t(col(5)), gates, batch, seq)
    return _proj_resid(o, w_out, x2)


PEER_SEL_TT = 256
def _batcher_network(n):
    def merge(lo, hi, r):
        step = r * 2
        if step < hi - lo:
            yield from merge(lo, hi, step)
            yield from merge(lo + r, hi, step)
            yield from ((i, i + r) for i in range(lo + r, hi - r, step))
        else:
            yield (lo, lo + r)

    def sort(lo, hi):
        if hi - lo >= 1:
            mid = lo + (hi - lo) // 2
            yield from sort(lo, mid)
            yield from sort(mid + 1, hi)
            yield from merge(lo, hi, 1)

    return list(sort(0, n - 1))


SORT16_NETWORK = _batcher_network(PEER_KEYS // 8)

PEER_PAIRS = [(i, j) for i in range(PEER_TOP_K) for j in range(PEER_TOP_K) if (i + 1) * (j + 1) <= PEER_TOP_K]


def _peer_select_kernel(q_ref, k1_ref, k2_ref, cnt_ref, e1_ref, rank_ref, e2_ref):
    q = q_ref[...]
    q_hi, q_lo = _split_bf16(q)
    nh, nr = PEER_HEADS, PEER_TOP_K

    def scores(k_ref, off):
        k_hi, k_lo = _split_bf16(k_ref[...])
        out = []
        for h in range(nh):
            lanes = slice(off + h * PEER_HALF, off + (h + 1) * PEER_HALF)
            out.append(_dot_nt(k_hi, q_hi[:, lanes]) + _dot_nt(k_hi, q_lo[:, lanes])
                       + _dot_nt(k_lo, q_hi[:, lanes]))
        return out

    def top_values(s):
        slabs = [s[8 * k:8 * (k + 1), :] for k in range(PEER_KEYS // 8)]
        for a, b in SORT16_NETWORK:
            slabs[a], slabs[b] = jnp.maximum(slabs[a], slabs[b]), jnp.minimum(slabs[a], slabs[b])
        vals = []
        for r in range(nr):
            m = jnp.max(slabs[0], axis=0, keepdims=True)
            vals.append(m)
            hit = slabs[0] == m
            for k in range(nr - 1 - r):
                slabs[k] = jnp.where(hit, slabs[k + 1], slabs[k])
        return vals

    s1 = scores(k1_ref, 0)
    s2 = scores(k2_ref, nh * PEER_HALF)
    t1 = [top_values(s) for s in s1]
    t2 = [top_values(s) for s in s2]
    r1 = [jnp.concatenate([t1[h][r] for h in range(nh)], axis=0) for r in range(nr)]
    r2 = [jnp.concatenate([t2[h][r] for h in range(nh)], axis=0) for r in range(nr)]
    cands = [r1[i] + r2[j] for (i, j) in PEER_PAIRS]
    work, tops = list(cands), []
    for _ in range(nr):
        m = functools.reduce(jnp.maximum, work)
        tops.append(m)
        work = [jnp.where(c == m, LOW, c) for c in work]
    thr = tops[PEER_TOP_K - 1]
    cmax = tops[0]
    z = functools.reduce(lambda a, b: a + b, [jnp.where(c >= thr, jnp.exp(c - cmax), 0.0) for c in cands])
    inv_z = 1.0 / z
    for h in range(nh):
        thr_h = thr[h:h + 1, :]
        cnt = jnp.zeros_like(s1[h])
        rank = jnp.zeros_like(s2[h])
        for r in range(nr):
            cnt = jnp.where(s1[h] + t2[h][r] >= thr_h, float(r + 1), cnt)
            rank = jnp.where(t2[h][r] > s2[h], float(r + 1), rank)
        cnt_ref[h] = cnt.astype(BF16)
        e1_ref[h] = jnp.exp(s1[h] - r1[0][h:h + 1, :]).astype(BF16)
        rank_ref[h] = rank.astype(BF16)
        e2_ref[h] = (jnp.exp(s2[h] - r2[0][h:h + 1, :]) * inv_z[h:h + 1, :]).astype(BF16)


def _peer_select(q, sub_keys):
    t = q.shape[0]
    tt = min(PEER_SEL_TT, t)
    out_spec = pl.BlockSpec((PEER_HEADS, PEER_KEYS, tt), lambda i: (0, 0, i))
    return pl.pallas_call(
        _peer_select_kernel,
        grid=(t // tt,),
        in_specs=[pl.BlockSpec((tt, q.shape[1]), lambda i: (i, 0)),
                  pl.BlockSpec((PEER_KEYS, PEER_HALF), lambda i: (0, 0)),
                  pl.BlockSpec((PEER_KEYS, PEER_HALF), lambda i: (0, 0))],
        out_specs=[out_spec] * 4,
        out_shape=[jax.ShapeDtypeStruct((PEER_HEADS, PEER_KEYS, t), BF16)] * 4,
        compiler_params=_cparams(("parallel",)),
        name="peer_select",
    )(q, sub_keys[0].astype(F32), sub_keys[1].astype(F32))


PEER_TT = 512
PEER_NE = 2048
PEER_SUB = 512


def _gelu(x):
    return 0.5 * x * (1.0 + lax.erf(x * np.float32(np.sqrt(0.5))))


def _peer_main_kernel(x_ref, g_ref, u_ref, vt_ref, cnt_ref, e1_ref, rank_ref, e2_ref, o_ref, xnt_ref, acc_ref):
    e = pl.program_id(1)
    ne = u_ref.shape[0]
    tt = x_ref.shape[0]
    sub = PEER_SUB
    n_sub = ne // sub

    @pl.when(e == 0)
    def _():
        x = x_ref[...]
        y = x * lax.rsqrt(jnp.mean(x * x, axis=-1, keepdims=True) + NORM_EPS)
        xnt_ref[...] = (y * g_ref[...]).T.astype(BF16)
        acc_ref[...] = jnp.zeros_like(acc_ref)

    def hidden(c):
        return _dot(u_ref[c * sub:(c + 1) * sub, :], xnt_ref[...])

    def weighted(c, h):
        act = _gelu(h).astype(BF16)
        parts = []
        for k in range(sub // PEER_KEYS):
            i1 = c * (sub // PEER_KEYS) + k
            gsum = jnp.zeros((PEER_KEYS, tt), BF16)
            for hd in range(PEER_HEADS):
                cnt = cnt_ref[hd, i1:i1 + 1, :]
                e1 = e1_ref[hd, i1:i1 + 1, :]
                gsum = gsum + jnp.where(rank_ref[hd] < cnt, e2_ref[hd], jnp.zeros((), BF16)) * e1
            parts.append(gsum * act[k * PEER_KEYS:(k + 1) * PEER_KEYS, :])
        return jnp.concatenate(parts, axis=0) if len(parts) > 1 else parts[0]

    def values(c, w):
        return _dot(vt_ref[:, c * sub:(c + 1) * sub], w)

    hs = {0: hidden(0)}
    if n_sub > 1:
        hs[1] = hidden(1)
    total = None
    for c in range(n_sub):
        w = weighted(c, hs.pop(c))
        if c + 2 < n_sub:
            hs[c + 2] = hidden(c + 2)
        pv = values(c, w)
        total = pv if total is None else total + pv
    acc_ref[...] += total

    @pl.when(e == pl.num_programs(1) - 1)
    def _():
        o_ref[...] = x_ref[...] + acc_ref[...].T


def _peer_mixer(x2, g_norm, w_q, sub_keys, u, v, seq):
    t, d = x2.shape
    w_q2 = w_q.reshape(d, PEER_HEADS, 2, PEER_HALF).transpose(0, 2, 1, 3).reshape(d, 2 * PEER_HEADS * PEER_HALF)
    q = _norm_matmul3(x2, g_norm, w_q2)
    cnt, e1, rank, e2 = _peer_select(q, sub_keys)
    tt = min(PEER_TT, t)
    ne = PEER_NE
    n_exp = u.shape[0]
    tok_spec = pl.BlockSpec((PEER_HEADS, PEER_KEYS, tt), lambda i, e: (0, 0, i))
    row_spec = pl.BlockSpec((PEER_HEADS, ne // PEER_KEYS, tt), lambda i, e: (0, e, i))
    return pl.pallas_call(
        _peer_main_kernel,
        grid=(t // tt, n_exp // ne),
        in_specs=[pl.BlockSpec((tt, d), lambda i, e: (i, 0)),
                  pl.BlockSpec((1, d), lambda i, e: (0, 0)),
                  pl.BlockSpec((ne, d), lambda i, e: (e, 0)),
                  pl.BlockSpec((d, ne), lambda i, e: (0, e)),
                  row_spec, row_spec, tok_spec, tok_spec],
        out_specs=pl.BlockSpec((tt, d), lambda i, e: (i, 0)),
        out_shape=jax.ShapeDtypeStruct((t, d), F32),
        scratch_shapes=[pltpu.VMEM((d, tt), BF16), pltpu.VMEM((d, tt), F32)],
        compiler_params=_cparams(("parallel", "arbitrary")),
        name="peer_main",
    )(x2, g_norm.reshape(1, d).astype(F32), u.astype(BF16), v.T.astype(BF16), cnt, e1, rank, e2)


def _sb_mixer(x2, g_norm, w_in, w_out, batch, seq):
    qkv = _norm_matmul(x2, g_norm, w_in, seq, tn=1024)
    return _proj_resid(_sb_attention(qkv, batch, seq), w_out, x2)


def _moba_mixer(x2, g_norm, w_in, w_out, batch, seq):
    d_model = N_HEADS * HEAD_DIM
    qkv = _norm_matmul(x2, g_norm, w_in, seq, tn=1024, n_rope=2 * d_model // 1024)
    return _proj_resid(_moba_attention(qkv, batch, seq), w_out, x2)


def kernel(x, l0_norm_mix, l0_sb_w_in, l0_sb_w_out, l0_norm_ffn, l0_peer_w_q, l0_peer_sub_keys, l0_peer_u, l0_peer_v, l1_norm_mix, l1_nsa_w_in, l1_nsa_cmp_pos_k, l1_nsa_cmp_pos_v, l1_nsa_cmp_w_k, l1_nsa_cmp_w_v, l1_nsa_w_out, l1_norm_ffn, l1_peer_w_q, l1_peer_sub_keys, l1_peer_u, l1_peer_v, l2_norm_mix, l2_moba_w_in, l2_moba_w_out, l2_norm_ffn, l2_peer_w_q, l2_peer_sub_keys, l2_peer_u, l2_peer_v, l3_norm_mix, l3_sb_w_in, l3_sb_w_out, l3_norm_ffn, l3_peer_w_q, l3_peer_sub_keys, l3_peer_u, l3_peer_v, final_norm):
    batch, seq, d = x.shape
    x2 = x.reshape(batch * seq, d)
    x2 = _sb_mixer(x2, l0_norm_mix, l0_sb_w_in, l0_sb_w_out, batch, seq)
    x2 = _peer_mixer(x2, l0_norm_ffn, l0_peer_w_q, l0_peer_sub_keys, l0_peer_u, l0_peer_v, seq)
    x2 = _nsa_mixer(x2, l1_norm_mix, l1_nsa_w_in, l1_nsa_cmp_pos_k, l1_nsa_cmp_pos_v, l1_nsa_cmp_w_k,
                    l1_nsa_cmp_w_v, l1_nsa_w_out, batch, seq)
    x2 = _peer_mixer(x2, l1_norm_ffn, l1_peer_w_q, l1_peer_sub_keys, l1_peer_u, l1_peer_v, seq)
    x2 = _moba_mixer(x2, l2_norm_mix, l2_moba_w_in, l2_moba_w_out, batch, seq)
    x2 = _peer_mixer(x2, l2_norm_ffn, l2_peer_w_q, l2_peer_sub_keys, l2_peer_u, l2_peer_v, seq)
    x2 = _sb_mixer(x2, l3_norm_mix, l3_sb_w_in, l3_sb_w_out, batch, seq)
    x2 = _peer_mixer(x2, l3_norm_ffn, l3_peer_w_q, l3_peer_sub_keys, l3_peer_u, l3_peer_v, seq)
    return _rmsnorm(x2, final_norm).reshape(batch, seq, d)
```

```python
import functools

import numpy as np
import jax
import jax.numpy as jnp
from jax import lax
from jax.experimental import pallas as pl
from jax.experimental.pallas import tpu as pltpu

F32 = jnp.float32
BF16 = jnp.bfloat16

N_HEADS = 16
HEAD_DIM = 64
ROPE_DIM = 16
ROPE_HALF = ROPE_DIM // 2
ROPE_THETA = 500000.0
NORM_EPS = 1e-6
NEG = -1e30
LOW = -3e38
ATT_SCALE = HEAD_DIM ** -0.5
QK_SCALE = float(np.sqrt(ATT_SCALE * np.log2(np.e)))

NSA_GROUPS = 4
NSA_HPG = N_HEADS // NSA_GROUPS
NSA_KV = NSA_GROUPS * HEAD_DIM
NSA_CMP_LEN = 32
NSA_CMP_STRIDE = 16
NSA_SEL_LEN = 64
NSA_TOP_N = 16
NSA_WINDOW = 512
NSA_BONUS = 1e3
NSA_SEL_PAD = 128

MOBA_BLOCK = 256
MOBA_TOP_K = 3
MOBA_BLK_ALIGN = 16

PEER_HEADS = 8
PEER_KEYS = 128
PEER_TOP_K = 16
PEER_HALF = 64

SB_EXIT = -110.0

LANE = 128
VMEM_LIMIT = 56 << 20


def _cparams(sem, vmem=VMEM_LIMIT):
    return pltpu.CompilerParams(dimension_semantics=sem, vmem_limit_bytes=vmem)


def _dot(a, b):
    return jnp.dot(a, b, preferred_element_type=F32)


def _dot_nt(a, b):
    return lax.dot_general(a, b, (((1,), (1,)), ((), ())), preferred_element_type=F32)


def _split_bf16(x):
    hi = x.astype(BF16)
    lo = (x - hi.astype(F32)).astype(BF16)
    return hi, lo


def _norm_matmul_kernel(*refs, n_rope, tn):
    if n_rope:
        x_ref, g_ref, w_ref, c_ref, sa_ref, sb_ref, o_ref, xn_ref = refs
    else:
        x_ref, g_ref, w_ref, o_ref, xn_ref = refs
    j = pl.program_id(1)

    @pl.when(j == 0)
    def _():
        x = x_ref[...]
        y = x * lax.rsqrt(jnp.mean(x * x, axis=-1, keepdims=True) + NORM_EPS)
        xn_ref[...] = (y * g_ref[...]).astype(BF16)

    acc = _dot(xn_ref[...], w_ref[...])
    if n_rope:
        @pl.when(j < n_rope)
        def _():
            wide = lambda t_ref: jnp.concatenate([t_ref[...]] * (tn // LANE), axis=1)
            r = (acc * wide(c_ref) + pltpu.roll(acc, ROPE_HALF, 1) * wide(sb_ref)
                 + pltpu.roll(acc, tn - ROPE_HALF, 1) * wide(sa_ref))
            o_ref[...] = r.astype(o_ref.dtype)

        @pl.when(j >= n_rope)
        def _():
            o_ref[...] = acc.astype(o_ref.dtype)
    else:
        o_ref[...] = acc.astype(o_ref.dtype)


def _rope_tables(pos, width):
    inv = ROPE_THETA ** (-jnp.arange(0, ROPE_DIM, 2, dtype=F32) / ROPE_DIM)
    ang = pos.astype(F32)[:, None] * inv[None, :]
    cos, sin = jnp.cos(ang) * QK_SCALE, jnp.sin(ang) * QK_SCALE
    n = pos.shape[0]
    rest = HEAD_DIM - ROPE_DIM
    c = jnp.concatenate([cos, cos, jnp.full((n, rest), QK_SCALE, F32)], axis=1)
    sa = jnp.concatenate([-sin, jnp.zeros((n, ROPE_HALF + rest), F32)], axis=1)
    sb = jnp.concatenate([jnp.zeros((n, ROPE_HALF), F32), sin, jnp.zeros((n, rest), F32)], axis=1)
    reps = width // HEAD_DIM
    return jnp.tile(c, (1, reps)), jnp.tile(sa, (1, reps)), jnp.tile(sb, (1, reps))


def _norm_matmul(x2, g, w, seq, *, tn, n_rope=0, out_dtype=BF16, tt=1024):
    t, d = x2.shape
    n = w.shape[1]
    tt = min(tt, seq)
    grid = (t // tt, n // tn)
    in_specs = [pl.BlockSpec((tt, d), lambda i, j: (i, 0)),
                pl.BlockSpec((1, d), lambda i, j: (0, 0)),
                pl.BlockSpec((d, tn), lambda i, j: (0, j))]
    args = [x2, g.reshape(1, d).astype(F32), w.astype(BF16)]
    if n_rope:
        nper = seq // tt
        tabs = _rope_tables(jnp.arange(seq), LANE)
        in_specs += [pl.BlockSpec((tt, LANE), lambda i, j: (i % nper, 0))] * 3
        args += list(tabs)
    return pl.pallas_call(
        functools.partial(_norm_matmul_kernel, n_rope=n_rope, tn=tn),
        grid=grid, in_specs=in_specs,
        out_specs=pl.BlockSpec((tt, tn), lambda i, j: (i, j)),
        out_shape=jax.ShapeDtypeStruct((t, n), out_dtype),
        scratch_shapes=[pltpu.VMEM((tt, d), BF16)],
        compiler_params=_cparams(("parallel", "arbitrary")),
        name="norm_matmul",
    )(*args)


def _norm_matmul3_kernel(x_ref, g_ref, wh_ref, wl_ref, o_ref, xh_ref, xl_ref):
    @pl.when(pl.program_id(1) == 0)
    def _():
        x = x_ref[...]
        y = x * lax.rsqrt(jnp.mean(x * x, axis=-1, keepdims=True) + NORM_EPS) * g_ref[...]
        xh_ref[...], xl_ref[...] = _split_bf16(y)

    o_ref[...] = (_dot(xh_ref[...], wh_ref[...]) + _dot(xl_ref[...], wh_ref[...])
                  + _dot(xh_ref[...], wl_ref[...]))


def _norm_matmul3(x2, g, w, *, tn=512, tt=1024):
    t, d = x2.shape
    n = w.shape[1]
    tt = min(tt, t)
    w_hi, w_lo = _split_bf16(w.astype(F32))
    w_spec = pl.BlockSpec((d, tn), lambda i, j: (0, j))
    return pl.pallas_call(
        _norm_matmul3_kernel,
        grid=(t // tt, n // tn),
        in_specs=[pl.BlockSpec((tt, d), lambda i, j: (i, 0)),
                  pl.BlockSpec((1, d), lambda i, j: (0, 0)), w_spec, w_spec],
        out_specs=pl.BlockSpec((tt, tn), lambda i, j: (i, j)),
        out_shape=jax.ShapeDtypeStruct((t, n), F32),
        scratch_shapes=[pltpu.VMEM((tt, d), BF16), pltpu.VMEM((tt, d), BF16)],
        compiler_params=_cparams(("parallel", "arbitrary")),
        name="norm_matmul3",
    )(x2, g.reshape(1, d).astype(F32), w_hi, w_lo)


def _proj_resid_kernel(a_ref, w_ref, r_ref, o_ref):
    o_ref[...] = r_ref[...] + _dot(a_ref[...], w_ref[...])


def _proj_resid(a, w, resid, *, tt=1024, tn=1024):
    t, d = a.shape
    n = w.shape[1]
    tt = min(tt, t)
    return pl.pallas_call(
        _proj_resid_kernel,
        grid=(t // tt, n // tn),
        in_specs=[pl.BlockSpec((tt, d), lambda i, j: (i, 0)),
                  pl.BlockSpec((d, tn), lambda i, j: (0, j)),
                  pl.BlockSpec((tt, tn), lambda i, j: (i, j))],
        out_specs=pl.BlockSpec((tt, tn), lambda i, j: (i, j)),
        out_shape=jax.ShapeDtypeStruct((t, n), F32),
        compiler_params=_cparams(("parallel", "parallel")),
        name="proj_resid",
    )(a, w.astype(BF16), resid)


def _rmsnorm_kernel(x_ref, g_ref, o_ref):
    x = x_ref[...]
    y = x * lax.rsqrt(jnp.mean(x * x, axis=-1, keepdims=True) + NORM_EPS)
    o_ref[...] = y * g_ref[...]


def _rmsnorm(x2, g, *, tt=512):
    t, d = x2.shape
    tt = min(tt, t)
    return pl.pallas_call(
        _rmsnorm_kernel,
        grid=(t // tt,),
        in_specs=[pl.BlockSpec((tt, d), lambda i: (i, 0)), pl.BlockSpec((1, d), lambda i: (0, 0))],
        out_specs=pl.BlockSpec((tt, d), lambda i: (i, 0)),
        out_shape=jax.ShapeDtypeStruct((t, d), F32),
        compiler_params=_cparams(("parallel",)),
        name="final_rmsnorm",
    )(x2, g.reshape(1, d).astype(F32))


SB_TILE = 256
SB_FIRST = 2 * SB_TILE
SB_HEADS = 4


def _sb_kernel(q_ref, k_ref, v_ref, tri_ref, o_ref):
    tile = SB_TILE
    wide = SB_FIRST
    qi = pl.program_id(2)
    t0 = qi * tile

    nh = SB_HEADS
    head_lanes = [slice(hh * HEAD_DIM, (hh + 1) * HEAD_DIM) for hh in range(nh)]
    qs = [q_ref[:, lanes] * jnp.asarray(ATT_SCALE, BF16) for lanes in head_lanes]

    def step(start, width, r_sums, accs, past):
        tri = tri_ref[0:width, 0:width]
        zs = [_dot_nt(qs[hh], k_ref[pl.ds(start, width), lanes]) for hh, lanes in enumerate(head_lanes)]
        sps, lbs, splits = [], [], []
        for z in zs:
            sp = jnp.maximum(z, 0.0) + jnp.log(1.0 + jnp.exp(-jnp.abs(z)))
            lb = -sp if past is None else jnp.where(past, -sp, 0.0)
            sps.append(sp)
            lbs.append(lb)
            splits.append(_split_bf16(lb))
        suffixes = [_dot(hi, tri) + _dot(lo, tri) for hi, lo in splits]
        weights = []
        for hh in range(nh):
            a = jnp.exp((zs[hh] - sps[hh]) + suffixes[hh] + r_sums[hh])
            if past is not None:
                a = jnp.where(past, a, 0.0)
            weights.append(a.astype(BF16))
        accs = tuple(accs[hh] + _dot(weights[hh], v_ref[pl.ds(start, width), lanes])
                     for hh, lanes in enumerate(head_lanes))
        r_sums = tuple(r_sums[hh] + jnp.sum(lbs[hh], axis=-1, keepdims=True) for hh in range(nh))
        return r_sums, accs

    start0 = pl.multiple_of(jnp.maximum(t0 + tile - wide, 0), tile)
    kpos = start0 + lax.broadcasted_iota(jnp.int32, (tile, wide), 1)
    tpos = t0 + lax.broadcasted_iota(jnp.int32, (tile, wide), 0)
    zero_r = tuple(jnp.zeros((tile, 1), F32) for _ in range(nh))
    zero_acc = tuple(jnp.zeros((tile, HEAD_DIM), F32) for _ in range(nh))
    r1, acc1 = step(start0, wide, zero_r, zero_acc, kpos < tpos)

    def older(j, r_sums):
        alive = jnp.max(functools.reduce(jnp.maximum, r_sums)) > SB_EXIT
        return jnp.where(alive, j - 1, -1)

    def cond(c):
        return c[0] >= 0

    def body(c):
        j, r_sums, accs = c
        r_sums, accs = step(pl.multiple_of(j * tile, tile), tile, r_sums, accs, None)
        return older(j, r_sums), r_sums, accs

    _, _, accs = lax.while_loop(cond, body, (older(start0 // tile, r1), r1, acc1))
    for hh, lanes in enumerate(head_lanes):
        o_ref[:, lanes] = accs[hh].astype(o_ref.dtype)


def _sb_attention(qkv, batch, seq):
    tile = SB_TILE
    nq = seq // tile
    d_model = N_HEADS * HEAD_DIM
    width = SB_HEADS * HEAD_DIM
    ncol = d_model // width
    ii = np.arange(SB_FIRST)
    tri = jnp.asarray(ii[:, None] > ii[None, :], dtype=BF16)
    return pl.pallas_call(
        _sb_kernel,
        grid=(batch, ncol, nq),
        in_specs=[pl.BlockSpec((tile, width), lambda b, h, i: (b * nq + i, h)),
                  pl.BlockSpec((seq, width), lambda b, h, i: (b, ncol + h)),
                  pl.BlockSpec((seq, width), lambda b, h, i: (b, 2 * ncol + h)),
                  pl.BlockSpec((SB_FIRST, SB_FIRST), lambda b, h, i: (0, 0))],
        out_specs=pl.BlockSpec((tile, width), lambda b, h, i: (b * nq + i, h)),
        out_shape=jax.ShapeDtypeStruct((batch * seq, d_model), BF16),
        compiler_params=_cparams(("parallel", "parallel", "arbitrary")),
        name="sb_attention",
    )(qkv, qkv, qkv, tri)


def _flash_t_first(q, k, vt, bias_t):
    s = _dot_nt(k, q) + bias_t
    m = jnp.max(s, axis=0, keepdims=True)
    p = jnp.exp2(s - m)
    return m, jnp.sum(p, axis=0, keepdims=True), _dot(vt, p.astype(BF16))


def _topk_mask_rows(vals, k):
    rowi = lax.broadcasted_iota(jnp.int32, vals.shape, 0)
    work = vals
    sel = jnp.zeros(vals.shape, F32)
    for _ in range(k):
        m = jnp.max(work, axis=0, keepdims=True)
        first = jnp.min(jnp.where(work == m, rowi, vals.shape[0]), axis=0, keepdims=True)
        hit = rowi == first
        sel = jnp.where(hit, 1.0, sel)
        work = jnp.where(hit, LOW, work)
    return sel


MOBA_WALK = 4
MOBA_HEADS = 8


def _moba_kernel(q_ref, k_ref, vt_ref, avg_ref, o_ref, kmean_ref, bias_ref):
    blk = MOBA_BLOCK
    wide = MOBA_WALK * blk
    qi = pl.program_id(2)

    @pl.when(qi == 0)
    def _():
        kmean_ref[...] = _dot(avg_ref[...], k_ref[...])

    key_i = lax.broadcasted_iota(jnp.int32, (blk, blk), 0)
    qry_i = lax.broadcasted_iota(jnp.int32, (blk, blk), 1)
    causal_bias = jnp.where(key_i <= qry_i, 0.0, NEG)
    blk_i = lax.broadcasted_iota(jnp.int32, (kmean_ref.shape[0], blk), 0)
    start = pl.multiple_of(qi * blk, blk)
    head_rows = [slice(hh * HEAD_DIM, (hh + 1) * HEAD_DIM) for hh in range(MOBA_HEADS)]

    qs, states = [], []
    for hh, lanes in enumerate(head_rows):
        q_raw = q_ref[:, lanes]
        km_hi, km_lo = _split_bf16(kmean_ref[:, lanes])
        gate = _dot_nt(km_hi, q_raw) + _dot_nt(km_lo, q_raw)
        gate = jnp.where(blk_i < qi, gate, NEG)
        picked = (_topk_mask_rows(gate, MOBA_TOP_K) > 0.5) & (blk_i < qi)
        bias_ref[hh] = jnp.where(picked, 0.0, NEG)
        qs.append(q_raw)
        states.append(_flash_t_first(qs[-1], k_ref[pl.ds(start, blk), lanes],
                                     vt_ref[lanes, pl.ds(start, blk)], causal_bias))

    def body(p, sts):
        s0 = pl.multiple_of(p * wide, wide)
        scores = [_dot_nt(k_ref[pl.ds(s0, wide), lanes], qs[hh]) for hh, lanes in enumerate(head_rows)]
        probs, nxt = [], []
        for hh in range(MOBA_HEADS):
            m, l, acc_t = sts[hh]
            bias = jnp.concatenate(
                [jnp.broadcast_to(bias_ref[hh, pl.ds(p * MOBA_WALK + w, 1), :], (blk, blk))
                 for w in range(MOBA_WALK)], axis=0)
            s = scores[hh] + bias
            m_new = jnp.maximum(m, jnp.max(s, axis=0, keepdims=True))
            pr = jnp.exp2(s - m_new)
            alpha = jnp.exp2(m - m_new)
            probs.append(pr.astype(BF16))
            nxt.append((m_new, alpha * l + jnp.sum(pr, axis=0, keepdims=True), alpha * acc_t))
        return tuple((m_new, l, acc_s + _dot(vt_ref[lanes, pl.ds(s0, wide)], probs[hh]))
                     for hh, (lanes, (m_new, l, acc_s)) in enumerate(zip(head_rows, nxt)))

    states = lax.fori_loop(0, (qi + MOBA_WALK - 1) // MOBA_WALK, body, tuple(states))
    for hh, lanes in enumerate(head_rows):
        _, l, acc_t = states[hh]
        o_ref[:, lanes] = (acc_t / l).T.astype(o_ref.dtype)


def _moba_attention(qkv, batch, seq):
    blk = MOBA_BLOCK
    wide = MOBA_WALK * blk
    nq = seq // blk
    d_model = N_HEADS * HEAD_DIM
    width = MOBA_HEADS * HEAD_DIM
    ncol = d_model // width
    nb = -(-nq // MOBA_BLK_ALIGN) * MOBA_BLK_ALIGN
    avg = np.zeros((nb, seq), np.float32)
    for n in range(nq):
        avg[n, n * blk:(n + 1) * blk] = 1.0 / blk
    avg = jnp.asarray(avg, dtype=BF16)
    v_t = qkv[:, 2 * d_model:].reshape(batch, seq, d_model).transpose(0, 2, 1)
    return pl.pallas_call(
        _moba_kernel,
        grid=(batch, ncol, nq),
        in_specs=[pl.BlockSpec((blk, width), lambda b, h, i: (b * nq + i, h)),
                  pl.BlockSpec((seq, width), lambda b, h, i: (b, ncol + h)),
                  pl.BlockSpec((None, width, seq), lambda b, h, i: (b, h, 0)),
                  pl.BlockSpec((nb, seq), lambda b, h, i: (0, 0))],
        out_specs=pl.BlockSpec((blk, width), lambda b, h, i: (b * nq + i, h)),
        out_shape=jax.ShapeDtypeStruct((batch * seq, d_model), BF16),
        scratch_shapes=[pltpu.VMEM((nb, width), F32),
                        pltpu.VMEM((MOBA_HEADS, nb, blk), F32)],
        compiler_params=_cparams(("parallel", "parallel", "arbitrary")),
        name="moba_attention",
    )(qkv, qkv, v_t, avg)


def _nsa_compress_kernel(kr_ref, vr_ref, pk_ref, pv_ref, wk_ref, wv_ref, c_ref, sa_ref, sb_ref, kc_ref, vc_ref):
    nc = kr_ref.shape[0]
    rowi = lax.broadcasted_iota(jnp.int32, (nc, 1), 0)

    def windows(x_ref, pos_ref, w_ref):
        x = x_ref[...].astype(F32)
        lo = _dot((x + pos_ref[0:1, :]).astype(BF16), w_ref[0])
        hi = _dot((x + pos_ref[1:2, :]).astype(BF16), w_ref[1])
        hi_next = jnp.where(rowi < nc - 1, pltpu.roll(hi, nc - 1, 0), 0.0)
        return lo + hi_next

    kk = windows(kr_ref, pk_ref, wk_ref)
    d = HEAD_DIM
    kc = kk[:, 0:d] * c_ref[...] + kk[:, d:2 * d] * sb_ref[...] + kk[:, 2 * d:3 * d] * sa_ref[...]
    kc_ref[...] = kc.astype(kc_ref.dtype)
    vv = windows(vr_ref, pv_ref, wv_ref)
    vc_ref[...] = vv[:, 0:d].astype(vc_ref.dtype)


def _nsa_compress(k_cmp, v_cmp, pos_k, pos_v, w_k, w_v, batch, seq):
    g, d, st = NSA_GROUPS, HEAD_DIM, NSA_CMP_STRIDE
    nc = seq // st

    def chunks(x):
        return x.reshape(batch, nc, st, g, d).transpose(0, 3, 1, 2, 4).reshape(batch, g, nc, st * d)

    def weights(w):
        w3 = w.reshape(NSA_CMP_LEN, d, d)
        cat = jnp.concatenate([w3, jnp.roll(w3, ROPE_HALF, axis=2), jnp.roll(w3, -ROPE_HALF, axis=2)], axis=2)
        return cat.reshape(2, st * d, 3 * d).astype(BF16)

    def positions(p):
        return p.reshape(2, st * d).astype(F32)

    cmp_end = jnp.arange(nc) * st + NSA_CMP_LEN - 1
    tabs = _rope_tables(cmp_end, d)
    blk4 = pl.BlockSpec((None, None, nc, st * d), lambda b, gi: (b, gi, 0, 0))
    out4 = pl.BlockSpec((None, None, nc, d), lambda b, gi: (b, gi, 0, 0))
    const2 = lambda shape: pl.BlockSpec(shape, lambda b, gi: (0,) * len(shape))
    return pl.pallas_call(
        _nsa_compress_kernel,
        grid=(batch, g),
        in_specs=[blk4, blk4, const2((2, st * d)), const2((2, st * d)),
                  const2((2, st * d, 3 * d)), const2((2, st * d, 3 * d)),
                  const2((nc, d)), const2((nc, d)), const2((nc, d))],
        out_specs=[out4, out4],
        out_shape=[jax.ShapeDtypeStruct((batch, g, nc, d), BF16)] * 2,
        compiler_params=_cparams(("parallel", "parallel")),
        name="nsa_compress",
    )(chunks(k_cmp), chunks(v_cmp), positions(pos_k), positions(pos_v), weights(w_k), weights(w_v), *tabs)


NSA_TQ = 256
NSA_TK = 1024
NSA_CHAINS = 4


def _nsa_kernel(q_ref, kc_ref, vct_ref, ks_ref, vst_ref, kw_ref, vwt_ref, gate_ref, c2st_ref, o_ref, bias_ref):
    tq, tk, d, nh = NSA_TQ, NSA_TK, HEAD_DIM, NSA_HPG
    nql = nh * tq
    half = nql // NSA_CHAINS
    qi = pl.program_id(2)
    t0 = qi * tq
    nc = kc_ref.shape[0]

    def lanes4(x):
        return jnp.concatenate([x] * nh, axis=1)

    tpos1 = t0 + lax.broadcasted_iota(jnp.int32, (1, tq), 1)
    tpos = lanes4(tpos1)
    gates_t = jax.nn.sigmoid(gate_ref[...]).T

    def gate(branch):
        return jnp.concatenate([gates_t[3 * h + branch:3 * h + branch + 1, :] for h in range(nh)], axis=1)

    q = jnp.concatenate([q_ref[:, h * d:(h + 1) * d] for h in range(nh)], axis=0)

    kd = t0 // tk
    dstart = pl.multiple_of(kd * tk, tk)
    span = NSA_WINDOW + tq
    wstart = pl.multiple_of(jnp.maximum(t0 - NSA_WINDOW, 0), tq)
    s_cmp = _dot_nt(kc_ref[...], q)
    s_win = _dot_nt(kw_ref[pl.ds(wstart, span), :], q)
    s_diag = _dot_nt(ks_ref[pl.ds(dstart, tk), :], q)

    cmp_end = lax.broadcasted_iota(jnp.int32, (nc, nql), 0) * NSA_CMP_STRIDE + (NSA_CMP_LEN - 1)
    vis_c = cmp_end <= tpos
    s = jnp.where(vis_c, s_cmp, NEG)
    e = jnp.where(vis_c, jnp.exp2(s - jnp.max(s, axis=0, keepdims=True)), 0.0)
    l = jnp.sum(e, axis=0, keepdims=True)
    p = e * (1.0 / jnp.where(l > 0.0, l, 1.0))
    out_t = gate(0) * _dot(vct_ref[...], p.astype(BF16))
    psum = functools.reduce(lambda a, b: a + b, [p[:, h * tq:(h + 1) * tq] for h in range(nh)])
    p_hi, p_lo = _split_bf16(psum)
    imp = _dot(c2st_ref[...], p_hi) + _dot(c2st_ref[...], p_lo)
    blk = lax.broadcasted_iota(jnp.int32, (NSA_SEL_PAD, tq), 0)
    own = tpos1 // NSA_SEL_LEN
    forced = (blk == 0) | (blk == own) | (blk == own - 1)
    imp = jnp.where(blk <= own, imp + jnp.where(forced, NSA_BONUS, 0.0), NEG)
    picked = _topk_mask_rows(imp, NSA_TOP_N) > 0.5
    bias_ref[...] = lanes4(jnp.where(picked, 0.0, NEG))

    per_tile = tk // NSA_SEL_LEN

    def tile_bias(kb):
        return jnp.concatenate(
            [jnp.broadcast_to(bias_ref[pl.ds(kb * per_tile + w, 1), :], (NSA_SEL_LEN, nql))
             for w in range(per_tile)], axis=0)

    wpos = wstart + lax.broadcasted_iota(jnp.int32, (span, nql), 0)
    s = jnp.where((wpos <= tpos) & (wpos > tpos - NSA_WINDOW), s_win, NEG)
    e = jnp.exp2(s - jnp.max(s, axis=0, keepdims=True))
    out_t = out_t + (gate(2) / jnp.sum(e, axis=0, keepdims=True)) * _dot(vwt_ref[:, pl.ds(wstart, span)],
                                                                         e.astype(BF16))

    kpos = dstart + lax.broadcasted_iota(jnp.int32, (tk, nql), 0)
    s = jnp.where(kpos <= tpos, s_diag + tile_bias(kd), NEG)
    m = jnp.max(s, axis=0, keepdims=True)
    e = jnp.exp2(s - m)
    l = jnp.sum(e, axis=0, keepdims=True)
    acc = _dot(vst_ref[:, pl.ds(dstart, tk)], e.astype(BF16))
    halves = [slice(i * half, (i + 1) * half) for i in range(NSA_CHAINS)]
    q_half = [q[hs] for hs in halves]
    states = tuple((m[:, hs], l[:, hs], acc[:, hs]) for hs in halves)

    def sel_body(kb, sts):
        start = pl.multiple_of(kb * tk, tk)
        k = ks_ref[pl.ds(start, tk), :]
        vt = vst_ref[:, pl.ds(start, tk)]
        bias = tile_bias(kb)
        scores = [_dot_nt(k, q_half[i]) for i in range(NSA_CHAINS)]
        probs, nxt = [], []
        for i, hs in enumerate(halves):
            m, l, acc = sts[i]
            s = scores[i] + bias[:, hs]
            m_new = jnp.maximum(m, jnp.max(s, axis=0, keepdims=True))
            pr = jnp.exp2(s - m_new)
            alpha = jnp.exp2(m - m_new)
            probs.append(pr.astype(BF16))
            nxt.append((m_new, alpha * l + jnp.sum(pr, axis=0, keepdims=True), alpha * acc))
        return tuple((m_new, l, acc + _dot(vt, probs[i])) for i, (m_new, l, acc) in enumerate(nxt))

    states = lax.fori_loop(0, kd, sel_body, states)
    sel_t = jnp.concatenate([st[2] / st[1] for st in states], axis=1)
    out_t = out_t + gate(1) * sel_t
    for h in range(nh):
        o_ref[:, h * d:(h + 1) * d] = out_t[:, h * tq:(h + 1) * tq].T.astype(o_ref.dtype)


def _nsa_attention(q_all, kc, vct, ks, vst, kw, vwt, gates, batch, seq):
    tq, tk, d, g = NSA_TQ, NSA_TK, HEAD_DIM, NSA_GROUPS
    nq = seq // tq
    nc = seq // NSA_CMP_STRIDE
    n_sel = seq // NSA_SEL_LEN
    d_model = N_HEADS * HEAD_DIM
    cs = np.arange(nc) * NSA_CMP_STRIDE
    ss = np.arange(n_sel) * NSA_SEL_LEN
    ov = np.clip(np.minimum(cs[:, None] + NSA_CMP_LEN, ss[None, :] + NSA_SEL_LEN)
                 - np.maximum(cs[:, None], ss[None, :]), 0, None) / NSA_CMP_LEN
    c2s = np.zeros((nc, NSA_SEL_PAD), np.float32)
    c2s[:, :n_sel] = ov
    c2s[nc - 1, :] = 0.0
    k_spec = lambda n: pl.BlockSpec((None, None, n, d), lambda b, gi, i: (b, gi, 0, 0))
    vt_spec = lambda n: pl.BlockSpec((None, None, d, n), lambda b, gi, i: (b, gi, 0, 0))
    return pl.pallas_call(
        _nsa_kernel,
        grid=(batch, g, nq),
        in_specs=[pl.BlockSpec((tq, NSA_HPG * d), lambda b, gi, i: (b * nq + i, gi)),
                  k_spec(nc), vt_spec(nc), k_spec(seq), vt_spec(seq), k_spec(seq), vt_spec(seq),
                  pl.BlockSpec((tq, LANE), lambda b, gi, i: (b * nq + i, gi)),
                  pl.BlockSpec((NSA_SEL_PAD, nc), lambda b, gi, i: (0, 0))],
        out_specs=pl.BlockSpec((tq, NSA_HPG * d), lambda b, gi, i: (b * nq + i, gi)),
        out_shape=jax.ShapeDtypeStruct((batch * seq, d_model), BF16),
        scratch_shapes=[pltpu.VMEM((NSA_SEL_PAD, NSA_HPG * tq), F32)],
        compiler_params=_cparams(("parallel", "parallel", "arbitrary")),
        name="nsa_attention",
    )(q_all, kc, vct, ks, vst, kw, vwt, gates, jnp.asarray(c2s.T, dtype=BF16))


def _nsa_mixer(x2, g_norm, w_in, pos_k, pos_v, w_k, w_v, w_out, batch, seq):
    d_model, kv = N_HEADS * HEAD_DIM, NSA_KV
    sec = lambda i: w_in[:, d_model + i * kv: d_model + (i + 1) * kv]
    w_main = jnp.concatenate([w_in[:, :d_model], sec(2), sec(4), sec(0), sec(1), sec(3), sec(5)], axis=1)
    proj = _norm_matmul(x2, g_norm, w_main, seq, tn=512, n_rope=3)
    w_gate = w_in[:, d_model + 6 * kv:].reshape(d_model, NSA_GROUPS, 3 * NSA_HPG)
    w_gate = jnp.pad(w_gate, ((0, 0), (0, 0), (0, LANE - 3 * NSA_HPG))).reshape(d_model, NSA_GROUPS * LANE)
    gates = _norm_matmul(x2, g_norm, w_gate, seq, tn=NSA_GROUPS * LANE, out_dtype=F32)

    col = lambda i: proj[:, d_model + i * kv: d_model + (i + 1) * kv]
    heads = lambda x: x.reshape(batch, seq, NSA_GROUPS, HEAD_DIM).transpose(0, 2, 1, 3)
    heads_t = lambda x: x.reshape(batch, seq, NSA_GROUPS, HEAD_DIM).transpose(0, 2, 3, 1)
    kc, vc = _nsa_compress(col(2), col(3), pos_k, pos_v, w_k, w_v, batch, seq)
    o = _nsa_attention(proj, kc, vc.transpose(0, 1, 3, 2), heads(col(0)), heads_t(col(4)), heads(col(1)),
                       heads_t(col(5)), gates, batch, seq)
    return _proj_resid(o, w_out, x2)


PEER_SEL_TT = 256
def _batcher_network(n):
    def merge(lo, hi, r):
        step = r * 2
        if step < hi - lo:
            yield from merge(lo, hi, step)
            yield from merge(lo + r, hi, step)
            yield from ((i, i + r) for i in range(lo + r, hi - r, step))
        else:
            yield (lo, lo + r)

    def sort(lo, hi):
        if hi - lo >= 1:
            mid = lo + (hi - lo) // 2
            yield from sort(lo, mid)
            yield from sort(mid + 1, hi)
            yield from merge(lo, hi, 1)

    return list(sort(0, n - 1))


SORT16_NETWORK = _batcher_network(PEER_KEYS // 8)

PEER_PAIRS = [(i, j) for i in range(PEER_TOP_K) for j in range(PEER_TOP_K) if (i + 1) * (j + 1) <= PEER_TOP_K]


def _peer_select_kernel(q_ref, k1_ref, k2_ref, cnt_ref, e1_ref, rank_ref, e2_ref):
    q = q_ref[...]
    q_hi, q_lo = _split_bf16(q)
    nh, nr = PEER_HEADS, PEER_TOP_K

    def scores(k_ref, off):
        k_hi, k_lo = _split_bf16(k_ref[...])
        out = []
        for h in range(nh):
            lanes = slice(off + h * PEER_HALF, off + (h + 1) * PEER_HALF)
            out.append(_dot_nt(k_hi, q_hi[:, lanes]) + _dot_nt(k_hi, q_lo[:, lanes])
                       + _dot_nt(k_lo, q_hi[:, lanes]))
        return out

    def top_values(s):
        slabs = [s[8 * k:8 * (k + 1), :] for k in range(PEER_KEYS // 8)]
        for a, b in SORT16_NETWORK:
            slabs[a], slabs[b] = jnp.maximum(slabs[a], slabs[b]), jnp.minimum(slabs[a], slabs[b])
        vals = []
        for r in range(nr):
            m = jnp.max(slabs[0], axis=0, keepdims=True)
            vals.append(m)
            hit = slabs[0] == m
            for k in range(nr - 1 - r):
                slabs[k] = jnp.where(hit, slabs[k + 1], slabs[k])
        return vals

    s1 = scores(k1_ref, 0)
    s2 = scores(k2_ref, nh * PEER_HALF)
    t1 = [top_values(s) for s in s1]
    t2 = [top_values(s) for s in s2]
    r1 = [jnp.concatenate([t1[h][r] for h in range(nh)], axis=0) for r in range(nr)]
    r2 = [jnp.concatenate([t2[h][r] for h in range(nh)], axis=0) for r in range(nr)]
    cands = [r1[i] + r2[j] for (i, j) in PEER_PAIRS]
    work, tops = list(cands), []
    for _ in range(nr):
        m = functools.reduce(jnp.maximum, work)
        tops.append(m)
        work = [jnp.where(c == m, LOW, c) for c in work]
    thr = tops[PEER_TOP_K - 1]
    cmax = tops[0]
    z = functools.reduce(lambda a, b: a + b, [jnp.where(c >= thr, jnp.exp(c - cmax), 0.0) for c in cands])
    inv_z = 1.0 / z
    for h in range(nh):
        thr_h = thr[h:h + 1, :]
        cnt = jnp.zeros_like(s1[h])
        rank = jnp.zeros_like(s2[h])
        for r in range(nr):
            cnt = jnp.where(s1[h] + t2[h][r] >= thr_h, float(r + 1), cnt)
            rank = jnp.where(t2[h][r] > s2[h], float(r + 1), rank)
        cnt_ref[h] = cnt.astype(BF16)
        e1_ref[h] = jnp.exp(s1[h] - r1[0][h:h + 1, :]).astype(BF16)
        rank_ref[h] = rank.astype(BF16)
        e2_ref[h] = (jnp.exp(s2[h] - r2[0][h:h + 1, :]) * inv_z[h:h + 1, :]).astype(BF16)


def _peer_select(q, sub_keys):
    t = q.shape[0]
    tt = min(PEER_SEL_TT, t)
    out_spec = pl.BlockSpec((PEER_HEADS, PEER_KEYS, tt), lambda i: (0, 0, i))
    return pl.pallas_call(
        _peer_select_kernel,
        grid=(t // tt,),
        in_specs=[pl.BlockSpec((tt, q.shape[1]), lambda i: (i, 0)),
                  pl.BlockSpec((PEER_KEYS, PEER_HALF), lambda i: (0, 0)),
                  pl.BlockSpec((PEER_KEYS, PEER_HALF), lambda i: (0, 0))],
        out_specs=[out_spec] * 4,
        out_shape=[jax.ShapeDtypeStruct((PEER_HEADS, PEER_KEYS, t), BF16)] * 4,
        compiler_params=_cparams(("parallel",)),
        name="peer_select",
    )(q, sub_keys[0].astype(F32), sub_keys[1].astype(F32))


PEER_TT = 512
PEER_NE = 2048
PEER_SUB = 512


def _gelu(x):
    return 0.5 * x * (1.0 + lax.erf(x * np.float32(np.sqrt(0.5))))


def _peer_main_kernel(x_ref, g_ref, u_ref, vt_ref, cnt_ref, e1_ref, rank_ref, e2_ref, o_ref, xnt_ref, acc_ref):
    e = pl.program_id(1)
    ne = u_ref.shape[0]
    tt = x_ref.shape[0]
    sub = PEER_SUB
    n_sub = ne // sub

    @pl.when(e == 0)
    def _():
        x = x_ref[...]
        y = x * lax.rsqrt(jnp.mean(x * x, axis=-1, keepdims=True) + NORM_EPS)
        xnt_ref[...] = (y * g_ref[...]).T.astype(BF16)
        acc_ref[...] = jnp.zeros_like(acc_ref)

    def hidden(c):
        return _dot(u_ref[c * sub:(c + 1) * sub, :], xnt_ref[...])

    def weighted(c, h):
        act = _gelu(h).astype(BF16)
        parts = []
        for k in range(sub // PEER_KEYS):
            i1 = c * (sub // PEER_KEYS) + k
            gsum = jnp.zeros((PEER_KEYS, tt), BF16)
            for hd in range(PEER_HEADS):
                cnt = cnt_ref[hd, i1:i1 + 1, :]
                e1 = e1_ref[hd, i1:i1 + 1, :]
                gsum = gsum + jnp.where(rank_ref[hd] < cnt, e2_ref[hd], jnp.zeros((), BF16)) * e1
            parts.append(gsum * act[k * PEER_KEYS:(k + 1) * PEER_KEYS, :])
        return jnp.concatenate(parts, axis=0) if len(parts) > 1 else parts[0]

    def values(c, w):
        return _dot(vt_ref[:, c * sub:(c + 1) * sub], w)

    hs = {0: hidden(0)}
    if n_sub > 1:
        hs[1] = hidden(1)
    total = None
    for c in range(n_sub):
        w = weighted(c, hs.pop(c))
        if c + 2 < n_sub:
            hs[c + 2] = hidden(c + 2)
        pv = values(c, w)
        total = pv if total is None else total + pv
    acc_ref[...] += total

    @pl.when(e == pl.num_programs(1) - 1)
    def _():
        o_ref[...] = x_ref[...] + acc_ref[...].T


def _peer_mixer(x2, g_norm, w_q, sub_keys, u, v, seq):
    t, d = x2.shape
    w_q2 = w_q.reshape(d, PEER_HEADS, 2, PEER_HALF).transpose(0, 2, 1, 3).reshape(d, 2 * PEER_HEADS * PEER_HALF)
    q = _norm_matmul3(x2, g_norm, w_q2)
    cnt, e1, rank, e2 = _peer_select(q, sub_keys)
    tt = min(PEER_TT, t)
    ne = PEER_NE
    n_exp = u.shape[0]
    tok_spec = pl.BlockSpec((PEER_HEADS, PEER_KEYS, tt), lambda i, e: (0, 0, i))
    row_spec = pl.BlockSpec((PEER_HEADS, ne // PEER_KEYS, tt), lambda i, e: (0, e, i))
    return pl.pallas_call(
        _peer_main_kernel,
        grid=(t // tt, n_exp // ne),
        in_specs=[pl.BlockSpec((tt, d), lambda i, e: (i, 0)),
                  pl.BlockSpec((1, d), lambda i, e: (0, 0)),
                  pl.BlockSpec((ne, d), lambda i, e: (e, 0)),
                  pl.BlockSpec((d, ne), lambda i, e: (0, e)),
                  row_spec, row_spec, tok_spec, tok_spec],
        out_specs=pl.BlockSpec((tt, d), lambda i, e: (i, 0)),
        out_shape=jax.ShapeDtypeStruct((t, d), F32),
        scratch_shapes=[pltpu.VMEM((d, tt), BF16), pltpu.VMEM((d, tt), F32)],
        compiler_params=_cparams(("parallel", "arbitrary")),
        name="peer_main",
    )(x2, g_norm.reshape(1, d).astype(F32), u.astype(BF16), v.T.astype(BF16), cnt, e1, rank, e2)


def _sb_mixer(x2, g_norm, w_in, w_out, batch, seq):
    qkv = _norm_matmul(x2, g_norm, w_in, seq, tn=1024)
    return _proj_resid(_sb_attention(qkv, batch, seq), w_out, x2)


def _moba_mixer(x2, g_norm, w_in, w_out, batch, seq):
    d_model = N_HEADS * HEAD_DIM
    qkv = _norm_matmul(x2, g_norm, w_in, seq, tn=1024, n_rope=2 * d_model // 1024)
    return _proj_resid(_moba_attention(qkv, batch, seq), w_out, x2)


def kernel(x, l0_norm_mix, l0_sb_w_in, l0_sb_w_out, l0_norm_ffn, l0_peer_w_q, l0_peer_sub_keys, l0_peer_u, l0_peer_v, l1_norm_mix, l1_nsa_w_in, l1_nsa_cmp_pos_k, l1_nsa_cmp_pos_v, l1_nsa_cmp_w_k, l1_nsa_cmp_w_v, l1_nsa_w_out, l1_norm_ffn, l1_peer_w_q, l1_peer_sub_keys, l1_peer_u, l1_peer_v, l2_norm_mix, l2_moba_w_in, l2_moba_w_out, l2_norm_ffn, l2_peer_w_q, l2_peer_sub_keys, l2_peer_u, l2_peer_v, l3_norm_mix, l3_sb_w_in, l3_sb_w_out, l3_norm_ffn, l3_peer_w_q, l3_peer_sub_keys, l3_peer_u, l3_peer_v, final_norm):
    batch, seq, d = x.shape
    x2 = x.reshape(batch * seq, d)
    x2 = _sb_mixer(x2, l0_norm_mix, l0_sb_w_in, l0_sb_w_out, batch, seq)
    x2 = _peer_mixer(x2, l0_norm_ffn, l0_peer_w_q, l0_peer_sub_keys, l0_peer_u, l0_peer_v, seq)
    x2 = _nsa_mixer(x2, l1_norm_mix, l1_nsa_w_in, l1_nsa_cmp_pos_k, l1_nsa_cmp_pos_v, l1_nsa_cmp_w_k,
                    l1_nsa_cmp_w_v, l1_nsa_w_out, batch, seq)
    x2 = _peer_mixer(x2, l1_norm_ffn, l1_peer_w_q, l1_peer_sub_keys, l1_peer_u, l1_peer_v, seq)
    x2 = _moba_mixer(x2, l2_norm_mix, l2_moba_w_in, l2_moba_w_out, batch, seq)
    x2 = _peer_mixer(x2, l2_norm_ffn, l2_peer_w_q, l2_peer_sub_keys, l2_peer_u, l2_peer_v, seq)
    x2 = _sb_mixer(x2, l3_norm_mix, l3_sb_w_in, l3_sb_w_out, batch, seq)
    x2 = _peer_mixer(x2, l3_norm_ffn, l3_peer_w_q, l3_peer_sub_keys, l3_peer_u, l3_peer_v, seq)
    return _rmsnorm(x2, final_norm).reshape(batch, seq, d)
```

```python
import functools

import numpy as np
import jax
import jax.numpy as jnp
from jax import lax
from jax.experimental import pallas as pl
from jax.experimental.pallas import tpu as pltpu

F32 = jnp.float32
BF16 = jnp.bfloat16

N_HEADS = 16
HEAD_DIM = 64
ROPE_DIM = 16
ROPE_HALF = ROPE_DIM // 2
ROPE_THETA = 500000.0
NORM_EPS = 1e-6
NEG = -1e30
LOW = -3e38
ATT_SCALE = HEAD_DIM ** -0.5
QK_SCALE = float(np.sqrt(ATT_SCALE * np.log2(np.e)))

NSA_GROUPS = 4
NSA_HPG = N_HEADS // NSA_GROUPS
NSA_KV = NSA_GROUPS * HEAD_DIM
NSA_CMP_LEN = 32
NSA_CMP_STRIDE = 16
NSA_SEL_LEN = 64
NSA_TOP_N = 16
NSA_WINDOW = 512
NSA_BONUS = 1e3
NSA_SEL_PAD = 128

MOBA_BLOCK = 256
MOBA_TOP_K = 3
MOBA_BLK_ALIGN = 16

PEER_HEADS = 8
PEER_KEYS = 128
PEER_TOP_K = 16
PEER_HALF = 64

SB_EXIT = -110.0

LANE = 128
VMEM_LIMIT = 56 << 20


def _cparams(sem, vmem=VMEM_LIMIT):
    return pltpu.CompilerParams(dimension_semantics=sem, vmem_limit_bytes=vmem)


def _dot(a, b):
    return jnp.dot(a, b, preferred_element_type=F32)


def _dot_nt(a, b):
    return lax.dot_general(a, b, (((1,), (1,)), ((), ())), preferred_element_type=F32)


def _split_bf16(x):
    hi = x.astype(BF16)
    lo = (x - hi.astype(F32)).astype(BF16)
    return hi, lo


def _norm_matmul_kernel(*refs, n_rope, tn):
    if n_rope:
        x_ref, g_ref, w_ref, c_ref, sa_ref, sb_ref, o_ref, xn_ref = refs
    else:
        x_ref, g_ref, w_ref, o_ref, xn_ref = refs
    j = pl.program_id(1)

    @pl.when(j == 0)
    def _():
        x = x_ref[...]
        y = x * lax.rsqrt(jnp.mean(x * x, axis=-1, keepdims=True) + NORM_EPS)
        xn_ref[...] = (y * g_ref[...]).astype(BF16)

    acc = _dot(xn_ref[...], w_ref[...])
    if n_rope:
        @pl.when(j < n_rope)
        def _():
            wide = lambda t_ref: jnp.concatenate([t_ref[...]] * (tn // LANE), axis=1)
            r = (acc * wide(c_ref) + pltpu.roll(acc, ROPE_HALF, 1) * wide(sb_ref)
                 + pltpu.roll(acc, tn - ROPE_HALF, 1) * wide(sa_ref))
            o_ref[...] = r.astype(o_ref.dtype)

        @pl.when(j >= n_rope)
        def _():
            o_ref[...] = acc.astype(o_ref.dtype)
    else:
        o_ref[...] = acc.astype(o_ref.dtype)


def _rope_tables(pos, width):
    inv = ROPE_THETA ** (-jnp.arange(0, ROPE_DIM, 2, dtype=F32) / ROPE_DIM)
    ang = pos.astype(F32)[:, None] * inv[None, :]
    cos, sin = jnp.cos(ang) * QK_SCALE, jnp.sin(ang) * QK_SCALE
    n = pos.shape[0]
    rest = HEAD_DIM - ROPE_DIM
    c = jnp.concatenate([cos, cos, jnp.full((n, rest), QK_SCALE, F32)], axis=1)
    sa = jnp.concatenate([-sin, jnp.zeros((n, ROPE_HALF + rest), F32)], axis=1)
    sb = jnp.concatenate([jnp.zeros((n, ROPE_HALF), F32), sin, jnp.zeros((n, rest), F32)], axis=1)
    reps = width // HEAD_DIM
    return jnp.tile(c, (1, reps)), jnp.tile(sa, (1, reps)), jnp.tile(sb, (1, reps))


def _norm_matmul(x2, g, w, seq, *, tn, n_rope=0, out_dtype=BF16, tt=1024):
    t, d = x2.shape
    n = w.shape[1]
    tt = min(tt, seq)
    grid = (t // tt, n // tn)
    in_specs = [pl.BlockSpec((tt, d), lambda i, j: (i, 0)),
                pl.BlockSpec((1, d), lambda i, j: (0, 0)),
                pl.BlockSpec((d, tn), lambda i, j: (0, j))]
    args = [x2, g.reshape(1, d).astype(F32), w.astype(BF16)]
    if n_rope:
        nper = seq // tt
        tabs = _rope_tables(jnp.arange(seq), LANE)
        in_specs += [pl.BlockSpec((tt, LANE), lambda i, j: (i % nper, 0))] * 3
        args += list(tabs)
    return pl.pallas_call(
        functools.partial(_norm_matmul_kernel, n_rope=n_rope, tn=tn),
        grid=grid, in_specs=in_specs,
        out_specs=pl.BlockSpec((tt, tn), lambda i, j: (i, j)),
        out_shape=jax.ShapeDtypeStruct((t, n), out_dtype),
        scratch_shapes=[pltpu.VMEM((tt, d), BF16)],
        compiler_params=_cparams(("parallel", "arbitrary")),
        name="norm_matmul",
    )(*args)


def _norm_matmul3_kernel(x_ref, g_ref, wh_ref, wl_ref, o_ref, xh_ref, xl_ref):
    @pl.when(pl.program_id(1) == 0)
    def _():
        x = x_ref[...]
        y = x * lax.rsqrt(jnp.mean(x * x, axis=-1, keepdims=True) + NORM_EPS) * g_ref[...]
        xh_ref[...], xl_ref[...] = _split_bf16(y)

    o_ref[...] = (_dot(xh_ref[...], wh_ref[...]) + _dot(xl_ref[...], wh_ref[...])
                  + _dot(xh_ref[...], wl_ref[...]))


def _norm_matmul3(x2, g, w, *, tn=512, tt=1024):
    t, d = x2.shape
    n = w.shape[1]
    tt = min(tt, t)
    w_hi, w_lo = _split_bf16(w.astype(F32))
    w_spec = pl.BlockSpec((d, tn), lambda i, j: (0, j))
    return pl.pallas_call(
        _norm_matmul3_kernel,
        grid=(t // tt, n // tn),
        in_specs=[pl.BlockSpec((tt, d), lambda i, j: (i, 0)),
                  pl.BlockSpec((1, d), lambda i, j: (0, 0)), w_spec, w_spec],
        out_specs=pl.BlockSpec((tt, tn), lambda i, j: (i, j)),
        out_shape=jax.ShapeDtypeStruct((t, n), F32),
        scratch_shapes=[pltpu.VMEM((tt, d), BF16), pltpu.VMEM((tt, d), BF16)],
        compiler_params=_cparams(("parallel", "arbitrary")),
        name="norm_matmul3",
    )(x2, g.reshape(1, d).astype(F32), w_hi, w_lo)


def _proj_resid_kernel(a_ref, w_ref, r_ref, o_ref):
    o_ref[...] = r_ref[...] + _dot(a_ref[...], w_ref[...])


def _proj_resid(a, w, resid, *, tt=1024, tn=1024):
    t, d = a.shape
    n = w.shape[1]
    tt = min(tt, t)
    return pl.pallas_call(
        _proj_resid_kernel,
        grid=(t // tt, n // tn),
        in_specs=[pl.BlockSpec((tt, d), lambda i, j: (i, 0)),
                  pl.BlockSpec((d, tn), lambda i, j: (0, j)),
                  pl.BlockSpec((tt, tn), lambda i, j: (i, j))],
        out_specs=pl.BlockSpec((tt, tn), lambda i, j: (i, j)),
        out_shape=jax.ShapeDtypeStruct((t, n), F32),
        compiler_params=_cparams(("parallel", "parallel")),
        name="proj_resid",
    )(a, w.astype(BF16), resid)


def _rmsnorm_kernel(x_ref, g_ref, o_ref):
    x = x_ref[...]
    y = x * lax.rsqrt(jnp.mean(x * x, axis=-1, keepdims=True) + NORM_EPS)
    o_ref[...] = y * g_ref[...]


def _rmsnorm(x2, g, *, tt=512):
    t, d = x2.shape
    tt = min(tt, t)
    return pl.pallas_call(
        _rmsnorm_kernel,
        grid=(t // tt,),
        in_specs=[pl.BlockSpec((tt, d), lambda i: (i, 0)), pl.BlockSpec((1, d), lambda i: (0, 0))],
        out_specs=pl.BlockSpec((tt, d), lambda i: (i, 0)),
        out_shape=jax.ShapeDtypeStruct((t, d), F32),
        compiler_params=_cparams(("parallel",)),
        name="final_rmsnorm",
    )(x2, g.reshape(1, d).astype(F32))


SB_TILE = 256
SB_FIRST = 2 * SB_TILE
SB_HEADS = 4


def _sb_kernel(q_ref, k_ref, v_ref, tri_ref, o_ref):
    tile = SB_TILE
    wide = SB_FIRST
    qi = pl.program_id(2)
    t0 = qi * tile

    nh = SB_HEADS
    head_lanes = [slice(hh * HEAD_DIM, (hh + 1) * HEAD_DIM) for hh in range(nh)]
    qs = [q_ref[:, lanes] * jnp.asarray(ATT_SCALE, BF16) for lanes in head_lanes]

    def step(start, width, r_sums, accs, past):
        tri = tri_ref[0:width, 0:width]
        zs = [_dot_nt(qs[hh], k_ref[pl.ds(start, width), lanes]) for hh, lanes in enumerate(head_lanes)]
        sps, lbs, splits = [], [], []
        for z in zs:
            sp = jnp.maximum(z, 0.0) + jnp.log(1.0 + jnp.exp(-jnp.abs(z)))
            lb = -sp if past is None else jnp.where(past, -sp, 0.0)
            sps.append(sp)
            lbs.append(lb)
            splits.append(_split_bf16(lb))
        suffixes = [_dot(hi, tri) + _dot(lo, tri) for hi, lo in splits]
        weights = []
        for hh in range(nh):
            a = jnp.exp((zs[hh] - sps[hh]) + suffixes[hh] + r_sums[hh])
            if past is not None:
                a = jnp.where(past, a, 0.0)
            weights.append(a.astype(BF16))
        accs = tuple(accs[hh] + _dot(weights[hh], v_ref[pl.ds(start, width), lanes])
                     for hh, lanes in enumerate(head_lanes))
        r_sums = tuple(r_sums[hh] + jnp.sum(lbs[hh], axis=-1, keepdims=True) for hh in range(nh))
        return r_sums, accs

    start0 = pl.multiple_of(jnp.maximum(t0 + tile - wide, 0), tile)
    kpos = start0 + lax.broadcasted_iota(jnp.int32, (tile, wide), 1)
    tpos = t0 + lax.broadcasted_iota(jnp.int32, (tile, wide), 0)
    zero_r = tuple(jnp.zeros((tile, 1), F32) for _ in range(nh))
    zero_acc = tuple(jnp.zeros((tile, HEAD_DIM), F32) for _ in range(nh))
    r1, acc1 = step(start0, wide, zero_r, zero_acc, kpos < tpos)

    def older(j, r_sums):
        alive = jnp.max(functools.reduce(jnp.maximum, r_sums)) > SB_EXIT
        return jnp.where(alive, j - 1, -1)

    def cond(c):
        return c[0] >= 0

    def body(c):
        j, r_sums, accs = c
        r_sums, accs = step(pl.multiple_of(j * tile, tile), tile, r_sums, accs, None)
        return older(j, r_sums), r_sums, accs

    _, _, accs = lax.while_loop(cond, body, (older(start0 // tile, r1), r1, acc1))
    for hh, lanes in enumerate(head_lanes):
        o_ref[:, lanes] = accs[hh].astype(o_ref.dtype)


def _sb_attention(qkv, batch, seq):
    tile = SB_TILE
    nq = seq // tile
    d_model = N_HEADS * HEAD_DIM
    width = SB_HEADS * HEAD_DIM
    ncol = d_model // width
    ii = np.arange(SB_FIRST)
    tri = jnp.asarray(ii[:, None] > ii[None, :], dtype=BF16)
    return pl.pallas_call(
        _sb_kernel,
        grid=(batch, ncol, nq),
        in_specs=[pl.BlockSpec((tile, width), lambda b, h, i: (b * nq + i, h)),
                  pl.BlockSpec((seq, width), lambda b, h, i: (b, ncol + h)),
                  pl.BlockSpec((seq, width), lambda b, h, i: (b, 2 * ncol + h)),
                  pl.BlockSpec((SB_FIRST, SB_FIRST), lambda b, h, i: (0, 0))],
        out_specs=pl.BlockSpec((tile, width), lambda b, h, i: (b * nq + i, h)),
        out_shape=jax.ShapeDtypeStruct((batch * seq, d_model), BF16),
        compiler_params=_cparams(("parallel", "parallel", "arbitrary")),
        name="sb_attention",
    )(qkv, qkv, qkv, tri)


def _flash_t_first(q, k, vt, bias_t):
    s = _dot_nt(k, q) + bias_t
    m = jnp.max(s, axis=0, keepdims=True)
    p = jnp.exp2(s - m)
    return m, jnp.sum(p, axis=0, keepdims=True), _dot(vt, p.astype(BF16))


def _topk_mask_rows(vals, k):
    rowi = lax.broadcasted_iota(jnp.int32, vals.shape, 0)
    work = vals
    sel = jnp.zeros(vals.shape, F32)
    for _ in range(k):
        m = jnp.max(work, axis=0, keepdims=True)
        first = jnp.min(jnp.where(work == m, rowi, vals.shape[0]), axis=0, keepdims=True)
        hit = rowi == first
        sel = jnp.where(hit, 1.0, sel)
        work = jnp.where(hit, LOW, work)
    return sel


MOBA_WALK = 4
MOBA_HEADS = 8


def _moba_kernel(q_ref, k_ref, vt_ref, avg_ref, o_ref, kmean_ref, bias_ref):
    blk = MOBA_BLOCK
    wide = MOBA_WALK * blk
    qi = pl.program_id(2)

    @pl.when(qi == 0)
    def _():
        kmean_ref[...] = _dot(avg_ref[...], k_ref[...])

    key_i = lax.broadcasted_iota(jnp.int32, (blk, blk), 0)
    qry_i = lax.broadcasted_iota(jnp.int32, (blk, blk), 1)
    causal_bias = jnp.where(key_i <= qry_i, 0.0, NEG)
    blk_i = lax.broadcasted_iota(jnp.int32, (kmean_ref.shape[0], blk), 0)
    start = pl.multiple_of(qi * blk, blk)
    head_rows = [slice(hh * HEAD_DIM, (hh + 1) * HEAD_DIM) for hh in range(MOBA_HEADS)]

    qs, states = [], []
    for hh, lanes in enumerate(head_rows):
        q_raw = q_ref[:, lanes]
        km_hi, km_lo = _split_bf16(kmean_ref[:, lanes])
        gate = _dot_nt(km_hi, q_raw) + _dot_nt(km_lo, q_raw)
        gate = jnp.where(blk_i < qi, gate, NEG)
        picked = (_topk_mask_rows(gate, MOBA_TOP_K) > 0.5) & (blk_i < qi)
        bias_ref[hh] = jnp.where(picked, 0.0, NEG)
        qs.append(q_raw)
        states.append(_flash_t_first(qs[-1], k_ref[pl.ds(start, blk), lanes],
                                     vt_ref[lanes, pl.ds(start, blk)], causal_bias))

    def body(p, sts):
        s0 = pl.multiple_of(p * wide, wide)
        scores = [_dot_nt(k_ref[pl.ds(s0, wide), lanes], qs[hh]) for hh, lanes in enumerate(head_rows)]
        probs, nxt = [], []
        for hh in range(MOBA_HEADS):
            m, l, acc_t = sts[hh]
            bias = jnp.concatenate(
                [jnp.broadcast_to(bias_ref[hh, pl.ds(p * MOBA_WALK + w, 1), :], (blk, blk))
                 for w in range(MOBA_WALK)], axis=0)
            s = scores[hh] + bias
            m_new = jnp.maximum(m, jnp.max(s, axis=0, keepdims=True))
            pr = jnp.exp2(s - m_new)
            alpha = jnp.exp2(m - m_new)
            probs.append(pr.astype(BF16))
            nxt.append((m_new, alpha * l + jnp.sum(pr, axis=0, keepdims=True), alpha * acc_t))
        return tuple((m_new, l, acc_s + _dot(vt_ref[lanes, pl.ds(s0, wide)], probs[hh]))
                     for hh, (lanes, (m_new, l, acc_s)) in enumerate(zip(head_rows, nxt)))

    states = lax.fori_loop(0, (qi + MOBA_WALK - 1) // MOBA_WALK, body, tuple(states))
    for hh, lanes in enumerate(head_rows):
        _, l, acc_t = states[hh]
        o_ref[:, lanes] = (acc_t / l).T.astype(o_ref.dtype)


def _moba_attention(qkv, batch, seq):
    blk = MOBA_BLOCK
    wide = MOBA_WALK * blk
    nq = seq // blk
    d_model = N_HEADS * HEAD_DIM
    width = MOBA_HEADS * HEAD_DIM
    ncol = d_model // width
    nb = -(-nq // MOBA_BLK_ALIGN) * MOBA_BLK_ALIGN
    avg = np.zeros((nb, seq), np.float32)
    for n in range(nq):
        avg[n, n * blk:(n + 1) * blk] = 1.0 / blk
    avg = jnp.asarray(avg, dtype=BF16)
    v_t = qkv[:, 2 * d_model:].reshape(batch, seq, d_model).transpose(0, 2, 1)
    return pl.pallas_call(
        _moba_kernel,
        grid=(batch, ncol, nq),
        in_specs=[pl.BlockSpec((blk, width), lambda b, h, i: (b * nq + i, h)),
                  pl.BlockSpec((seq, width), lambda b, h, i: (b, ncol + h)),
                  pl.BlockSpec((None, width, seq), lambda b, h, i: (b, h, 0)),
                  pl.BlockSpec((nb, seq), lambda b, h, i: (0, 0))],
        out_specs=pl.BlockSpec((blk, width), lambda b, h, i: (b * nq + i, h)),
        out_shape=jax.ShapeDtypeStruct((batch * seq, d_model), BF16),
        scratch_shapes=[pltpu.VMEM((nb, width), F32),
                        pltpu.VMEM((MOBA_HEADS, nb, blk), F32)],
        compiler_params=_cparams(("parallel", "parallel", "arbitrary")),
        name="moba_attention",
    )(qkv, qkv, v_t, avg)


def _nsa_compress_kernel(kr_ref, vr_ref, pk_ref, pv_ref, wk_ref, wv_ref, c_ref, sa_ref, sb_ref, kc_ref, vc_ref):
    nc = kr_ref.shape[0]
    rowi = lax.broadcasted_iota(jnp.int32, (nc, 1), 0)

    def windows(x_ref, pos_ref, w_ref):
        x = x_ref[...].astype(F32)
        lo = _dot((x + pos_ref[0:1, :]).astype(BF16), w_ref[0])
        hi = _dot((x + pos_ref[1:2, :]).astype(BF16), w_ref[1])
        hi_next = jnp.where(rowi < nc - 1, pltpu.roll(hi, nc - 1, 0), 0.0)
        return lo + hi_next

    kk = windows(kr_ref, pk_ref, wk_ref)
    d = HEAD_DIM
    kc = kk[:, 0:d] * c_ref[...] + kk[:, d:2 * d] * sb_ref[...] + kk[:, 2 * d:3 * d] * sa_ref[...]
    kc_ref[...] = kc.astype(kc_ref.dtype)
    vv = windows(vr_ref, pv_ref, wv_ref)
    vc_ref[...] = vv[:, 0:d].astype(vc_ref.dtype)


def _nsa_compress(k_cmp, v_cmp, pos_k, pos_v, w_k, w_v, batch, seq):
    g, d, st = NSA_GROUPS, HEAD_DIM, NSA_CMP_STRIDE
    nc = seq // st

    def chunks(x):
        return x.reshape(batch, nc, st, g, d).transpose(0, 3, 1, 2, 4).reshape(batch, g, nc, st * d)

    def weights(w):
        w3 = w.reshape(NSA_CMP_LEN, d, d)
        cat = jnp.concatenate([w3, jnp.roll(w3, ROPE_HALF, axis=2), jnp.roll(w3, -ROPE_HALF, axis=2)], axis=2)
        return cat.reshape(2, st * d, 3 * d).astype(BF16)

    def positions(p):
        return p.reshape(2, st * d).astype(F32)

    cmp_end = jnp.arange(nc) * st + NSA_CMP_LEN - 1
    tabs = _rope_tables(cmp_end, d)
    blk4 = pl.BlockSpec((None, None, nc, st * d), lambda b, gi: (b, gi, 0, 0))
    out4 = pl.BlockSpec((None, None, nc, d), lambda b, gi: (b, gi, 0, 0))
    const2 = lambda shape: pl.BlockSpec(shape, lambda b, gi: (0,) * len(shape))
    return pl.pallas_call(
        _nsa_compress_kernel,
        grid=(batch, g),
        in_specs=[blk4, blk4, const2((2, st * d)), const2((2, st * d)),
                  const2((2, st * d, 3 * d)), const2((2, st * d, 3 * d)),
                  const2((nc, d)), const2((nc, d)), const2((nc, d))],
        out_specs=[out4, out4],
        out_shape=[jax.ShapeDtypeStruct((batch, g, nc, d), BF16)] * 2,
        compiler_params=_cparams(("parallel", "parallel")),
        name="nsa_compress",
    )(chunks(k_cmp), chunks(v_cmp), positions(pos_k), positions(pos_v), weights(w_k), weights(w_v), *tabs)


NSA_TQ = 256
NSA_TK = 1024
NSA_CHAINS = 4


def _nsa_kernel(q_ref, kc_ref, vct_ref, ks_ref, vst_ref, kw_ref, vwt_ref, gate_ref, c2st_ref, o_ref, bias_ref):
    tq, tk, d, nh = NSA_TQ, NSA_TK, HEAD_DIM, NSA_HPG
    nql = nh * tq
    half = nql // NSA_CHAINS
    qi = pl.program_id(2)
    t0 = qi * tq
    nc = kc_ref.shape[0]

    def lanes4(x):
        return jnp.concatenate([x] * nh, axis=1)

    tpos1 = t0 + lax.broadcasted_iota(jnp.int32, (1, tq), 1)
    tpos = lanes4(tpos1)
    gates_t = jax.nn.sigmoid(gate_ref[...]).T

    def gate(branch):
        return jnp.concatenate([gates_t[3 * h + branch:3 * h + branch + 1, :] for h in range(nh)], axis=1)

    q = jnp.concatenate([q_ref[:, h * d:(h + 1) * d] for h in range(nh)], axis=0)

    kd = t0 // tk
    dstart = pl.multiple_of(kd * tk, tk)
    span = NSA_WINDOW + tq
    wstart = pl.multiple_of(jnp.maximum(t0 - NSA_WINDOW, 0), tq)
    s_cmp = _dot_nt(kc_ref[...], q)
    s_win = _dot_nt(kw_ref[pl.ds(wstart, span), :], q)
    s_diag = _dot_nt(ks_ref[pl.ds(dstart, tk), :], q)

    cmp_end = lax.broadcasted_iota(jnp.int32, (nc, nql), 0) * NSA_CMP_STRIDE + (NSA_CMP_LEN - 1)
    vis_c = cmp_end <= tpos
    s = jnp.where(vis_c, s_cmp, NEG)
    e = jnp.where(vis_c, jnp.exp2(s - jnp.max(s, axis=0, keepdims=True)), 0.0)
    l = jnp.sum(e, axis=0, keepdims=True)
    p = e * (1.0 / jnp.where(l > 0.0, l, 1.0))
    out_t = gate(0) * _dot(vct_ref[...], p.astype(BF16))
    psum = functools.reduce(lambda a, b: a + b, [p[:, h * tq:(h + 1) * tq] for h in range(nh)])
    p_hi, p_lo = _split_bf16(psum)
    imp = _dot(c2st_ref[...], p_hi) + _dot(c2st_ref[...], p_lo)
    blk = lax.broadcasted_iota(jnp.int32, (NSA_SEL_PAD, tq), 0)
    own = tpos1 // NSA_SEL_LEN
    forced = (blk == 0) | (blk == own) | (blk == own - 1)
    imp = jnp.where(blk <= own, imp + jnp.where(forced, NSA_BONUS, 0.0), NEG)
    picked = _topk_mask_rows(imp, NSA_TOP_N) > 0.5
    bias_ref[...] = lanes4(jnp.where(picked, 0.0, NEG))

    per_tile = tk // NSA_SEL_LEN

    def tile_bias(kb):
        return jnp.concatenate(
            [jnp.broadcast_to(bias_ref[pl.ds(kb * per_tile + w, 1), :], (NSA_SEL_LEN, nql))
             for w in range(per_tile)], axis=0)

    wpos = wstart + lax.broadcasted_iota(jnp.int32, (span, nql), 0)
    s = jnp.where((wpos <= tpos) & (wpos > tpos - NSA_WINDOW), s_win, NEG)
    e = jnp.exp2(s - jnp.max(s, axis=0, keepdims=True))
    out_t = out_t + (gate(2) / jnp.sum(e, axis=0, keepdims=True)) * _dot(vwt_ref[:, pl.ds(wstart, span)],
                                                                         e.astype(BF16))

    kpos = dstart + lax.broadcasted_iota(jnp.int32, (tk, nql), 0)
    s = jnp.where(kpos <= tpos, s_diag + tile_bias(kd), NEG)
    m = jnp.max(s, axis=0, keepdims=True)
    e = jnp.exp2(s - m)
    l = jnp.sum(e, axis=0, keepdims=True)
    acc = _dot(vst_ref[:, pl.ds(dstart, tk)], e.astype(BF16))
    halves = [slice(i * half, (i + 1) * half) for i in range(NSA_CHAINS)]
    q_half = [q[hs] for hs in halves]
    states = tuple((m[:, hs], l[:, hs], acc[:, hs]) for hs in halves)

    def sel_body(kb, sts):
        start = pl.multiple_of(kb * tk, tk)
        k = ks_ref[pl.ds(start, tk), :]
        vt = vst_ref[:, pl.ds(start, tk)]
        bias = tile_bias(kb)
        scores = [_dot_nt(k, q_half[i]) for i in range(NSA_CHAINS)]
        probs, nxt = [], []
        for i, hs in enumerate(halves):
            m, l, acc = sts[i]
            s = scores[i] + bias[:, hs]
            m_new = jnp.maximum(m, jnp.max(s, axis=0, keepdims=True))
            pr = jnp.exp2(s - m_new)
            alpha = jnp.exp2(m - m_new)
            probs.append(pr.astype(BF16))
            nxt.append((m_new, alpha * l + jnp.sum(pr, axis=0, keepdims=True), alpha * acc))
        return tuple((m_new, l, acc + _dot(vt, probs[i])) for i, (m_new, l, acc) in enumerate(nxt))

    states = lax.fori_loop(0, kd, sel_body, states)
    sel_t = jnp.concatenate([st[2] / st[1] for st in states], axis=1)
    out_t = out_t + gate(1) * sel_t
    for h in range(nh):
        o_ref[:, h * d:(h + 1) * d] = out_t[:, h * tq:(h + 1) * tq].T.astype(o_ref.dtype)


def _nsa_attention(q_all, kc, vct, ks, vst, kw, vwt, gates, batch, seq):
    tq, tk, d, g = NSA_TQ, NSA_TK, HEAD_DIM, NSA_GROUPS
    nq = seq // tq
    nc = seq // NSA_CMP_STRIDE
    n_sel = seq // NSA_SEL_LEN
    d_model = N_HEADS * HEAD_DIM
    cs = np.arange(nc) * NSA_CMP_STRIDE
    ss = np.arange(n_sel) * NSA_SEL_LEN
    ov = np.clip(np.minimum(cs[:, None] + NSA_CMP_LEN, ss[None, :] + NSA_SEL_LEN)
                 - np.maximum(cs[:, None], ss[None, :]), 0, None) / NSA_CMP_LEN
    c2s = np.zeros((nc, NSA_SEL_PAD), np.float32)
    c2s[:, :n_sel] = ov
    c2s[nc - 1, :] = 0.0
    k_spec = lambda n: pl.BlockSpec((None, None, n, d), lambda b, gi, i: (b, gi, 0, 0))
    vt_spec = lambda n: pl.BlockSpec((None, None, d, n), lambda b, gi, i: (b, gi, 0, 0))
    return pl.pallas_call(
        _nsa_kernel,
        grid=(batch, g, nq),
        in_specs=[pl.BlockSpec((tq, NSA_HPG * d), lambda b, gi, i: (b * nq + i, gi)),
                  k_spec(nc), vt_spec(nc), k_spec(seq), vt_spec(seq), k_spec(seq), vt_spec(seq),
                  pl.BlockSpec((tq, LANE), lambda b, gi, i: (b * nq + i, gi)),
                  pl.BlockSpec((NSA_SEL_PAD, nc), lambda b, gi, i: (0, 0))],
        out_specs=pl.BlockSpec((tq, NSA_HPG * d), lambda b, gi, i: (b * nq + i, gi)),
        out_shape=jax.ShapeDtypeStruct((batch * seq, d_model), BF16),
        scratch_shapes=[pltpu.VMEM((NSA_SEL_PAD, NSA_HPG * tq), F32)],
        compiler_params=_cparams(("parallel", "parallel", "arbitrary")),
        name="nsa_attention",
    )(q_all, kc, vct, ks, vst, kw, vwt, gates, jnp.asarray(c2s.T, dtype=BF16))


def _nsa_mixer(x2, g_norm, w_in, pos_k, pos_v, w_k, w_v, w_out, batch, seq):
    d_model, kv = N_HEADS * HEAD_DIM, NSA_KV
    sec = lambda i: w_in[:, d_model + i * kv: d_model + (i + 1) * kv]
    w_main = jnp.concatenate([w_in[:, :d_model], sec(2), sec(4), sec(0), sec(1), sec(3), sec(5)], axis=1)
    proj = _norm_matmul(x2, g_norm, w_main, seq, tn=512, n_rope=3)
    w_gate = w_in[:, d_model + 6 * kv:].reshape(d_model, NSA_GROUPS, 3 * NSA_HPG)
    w_gate = jnp.pad(w_gate, ((0, 0), (0, 0), (0, LANE - 3 * NSA_HPG))).reshape(d_model, NSA_GROUPS * LANE)
    gates = _norm_matmul(x2, g_norm, w_gate, seq, tn=NSA_GROUPS * LANE, out_dtype=F32)

    col = lambda i: proj[:, d_model + i * kv: d_model + (i + 1) * kv]
    heads = lambda x: x.reshape(batch, seq, NSA_GROUPS, HEAD_DIM).transpose(0, 2, 1, 3)
    heads_t = lambda x: x.reshape(batch, seq, NSA_GROUPS, HEAD_DIM).transpose(0, 2, 3, 1)
    kc, vc = _nsa_compress(col(2), col(3), pos_k, pos_v, w_k, w_v, batch, seq)
    o = _nsa_attention(proj, kc, vc.transpose(0, 1, 3, 2), heads(col(0)), heads_t(col(4)), heads(col(1)),
                       heads_t(col(5)), gates, batch, seq)
    return _proj_resid(o, w_out, x2)


PEER_SEL_TT = 256
def _batcher_network(n):
    def merge(lo, hi, r):
        step = r * 2
        if step < hi - lo:
            yield from merge(lo, hi, step)
            yield from merge(lo + r, hi, step)
            yield from ((i, i + r) for i in range(lo + r, hi - r, step))
        else:
            yield (lo, lo + r)

    def sort(lo, hi):
        if hi - lo >= 1:
            mid = lo + (hi - lo) // 2
            yield from sort(lo, mid)
            yield from sort(mid + 1, hi)
            yield from merge(lo, hi, 1)

    return list(sort(0, n - 1))


SORT16_NETWORK = _batcher_network(PEER_KEYS // 8)

PEER_PAIRS = [(i, j) for i in range(PEER_TOP_K) for j in range(PEER_TOP_K) if (i + 1) * (j + 1) <= PEER_TOP_K]


def _peer_select_kernel(q_ref, k1_ref, k2_ref, cnt_ref, e1_ref, rank_ref, e2_ref):
    q = q_ref[...]
    q_hi, q_lo = _split_bf16(q)
    nh, nr = PEER_HEADS, PEER_TOP_K

    def scores(k_ref, off):
        k_hi, k_lo = _split_bf16(k_ref[...])
        out = []
        for h in range(nh):
            lanes = slice(off + h * PEER_HALF, off + (h + 1) * PEER_HALF)
            out.append(_dot_nt(k_hi, q_hi[:, lanes]) + _dot_nt(k_hi, q_lo[:, lanes])
                       + _dot_nt(k_lo, q_hi[:, lanes]))
        return out

    def top_values(s):
        slabs = [s[8 * k:8 * (k + 1), :] for k in range(PEER_KEYS // 8)]
        for a, b in SORT16_NETWORK:
            slabs[a], slabs[b] = jnp.maximum(slabs[a], slabs[b]), jnp.minimum(slabs[a], slabs[b])
        vals = []
        for r in range(nr):
            m = jnp.max(slabs[0], axis=0, keepdims=True)
            vals.append(m)
            hit = slabs[0] == m
            for k in range(nr - 1 - r):
                slabs[k] = jnp.where(hit, slabs[k + 1], slabs[k])
        return vals

    s1 = scores(k1_ref, 0)
    s2 = scores(k2_ref, nh * PEER_HALF)
    t1 = [top_values(s) for s in s1]
    t2 = [top_values(s) for s in s2]
    r1 = [jnp.concatenate([t1[h][r] for h in range(nh)], axis=0) for r in range(nr)]
    r2 = [jnp.concatenate([t2[h][r] for h in range(nh)], axis=0) for r in range(nr)]
    cands = [r1[i] + r2[j] for (i, j) in PEER_PAIRS]
    work, tops = list(cands), []
    for _ in range(nr):
        m = functools.reduce(jnp.maximum, work)
        tops.append(m)
        work = [jnp.where(c == m, LOW, c) for c in work]
    thr = tops[PEER_TOP_K - 1]
    cmax = tops[0]
    z = functools.reduce(lambda a, b: a + b, [jnp.where(c >= thr, jnp.exp(c - cmax), 0.0) for c in cands])
    inv_z = 1.0 / z
    for h in range(nh):
        thr_h = thr[h:h + 1, :]
        cnt = jnp.zeros_like(s1[h])
        rank = jnp.zeros_like(s2[h])
        for r in range(nr):
            cnt = jnp.where(s1[h] + t2[h][r] >= thr_h, float(r + 1), cnt)
            rank = jnp.where(t2[h][r] > s2[h], float(r + 1), rank)
        cnt_ref[h] = cnt.astype(BF16)
        e1_ref[h] = jnp.exp(s1[h] - r1[0][h:h + 1, :]).astype(BF16)
        rank_ref[h] = rank.astype(BF16)
        e2_ref[h] = (jnp.exp(s2[h] - r2[0][h:h + 1, :]) * inv_z[h:h + 1, :]).astype(BF16)


def _peer_select(q, sub_keys):
    t = q.shape[0]
    tt = min(PEER_SEL_TT, t)
    out_spec = pl.BlockSpec((PEER_HEADS, PEER_KEYS, tt), lambda i: (0, 0, i))
    return pl.pallas_call(
        _peer_select_kernel,
        grid=(t // tt,),
        in_specs=[pl.BlockSpec((tt, q.shape[1]), lambda i: (i, 0)),
                  pl.BlockSpec((PEER_KEYS, PEER_HALF), lambda i: (0, 0)),
                  pl.BlockSpec((PEER_KEYS, PEER_HALF), lambda i: (0, 0))],
        out_specs=[out_spec] * 4,
        out_shape=[jax.ShapeDtypeStruct((PEER_HEADS, PEER_KEYS, t), BF16)] * 4,
        compiler_params=_cparams(("parallel",)),
        name="peer_select",
    )(q, sub_keys[0].astype(F32), sub_keys[1].astype(F32))


PEER_TT = 512
PEER_NE = 2048
PEER_SUB = 512


SQRT_HALF = float(np.sqrt(0.5))


def _gelu_of_scaled(y):
    return y * (1.0 + lax.erf(y))


def _peer_main_kernel(x_ref, g_ref, u_ref, vt_ref, cnt_ref, e1_ref, rank_ref, e2_ref, o_ref, xnt_ref, acc_ref):
    e = pl.program_id(1)
    ne = u_ref.shape[0]
    tt = x_ref.shape[0]
    sub = PEER_SUB
    n_sub = ne // sub

    @pl.when(e == 0)
    def _():
        x = x_ref[...]
        y = x * lax.rsqrt(jnp.mean(x * x, axis=-1, keepdims=True) + NORM_EPS)
        xnt_ref[...] = (y * g_ref[...]).T.astype(BF16)
        acc_ref[...] = jnp.zeros_like(acc_ref)

    def hidden(c):
        return _dot(u_ref[c * sub:(c + 1) * sub, :], xnt_ref[...])

    def weighted(c, h):
        act = _gelu_of_scaled(h).astype(BF16)
        parts = []
        for k in range(sub // PEER_KEYS):
            i1 = c * (sub // PEER_KEYS) + k
            gsum = None
            for hd in range(PEER_HEADS):
                cnt = cnt_ref[hd, i1:i1 + 1, :]
                e1 = e1_ref[hd, i1:i1 + 1, :]
                term = jnp.where(rank_ref[hd] < cnt, e2_ref[hd], jnp.zeros((), BF16)) * e1
                gsum = term if gsum is None else gsum + term
            parts.append(gsum * act[k * PEER_KEYS:(k + 1) * PEER_KEYS, :])
        return jnp.concatenate(parts, axis=0) if len(parts) > 1 else parts[0]

    def values(c, w):
        return _dot(vt_ref[:, c * sub:(c + 1) * sub], w)

    hs = {0: hidden(0)}
    if n_sub > 1:
        hs[1] = hidden(1)
    total = None
    for c in range(n_sub):
        w = weighted(c, hs.pop(c))
        if c + 2 < n_sub:
            hs[c + 2] = hidden(c + 2)
        pv = values(c, w)
        total = pv if total is None else total + pv
    acc_ref[...] += total

    @pl.when(e == pl.num_programs(1) - 1)
    def _():
        o_ref[...] = x_ref[...] + acc_ref[...].T


def _peer_mixer(x2, g_norm, w_q, sub_keys, u, v, seq):
    t, d = x2.shape
    w_q2 = w_q.reshape(d, PEER_HEADS, 2, PEER_HALF).transpose(0, 2, 1, 3).reshape(d, 2 * PEER_HEADS * PEER_HALF)
    q = _norm_matmul3(x2, g_norm, w_q2)
    cnt, e1, rank, e2 = _peer_select(q, sub_keys)
    tt = min(PEER_TT, t)
    ne = PEER_NE
    n_exp = u.shape[0]
    tok_spec = pl.BlockSpec((PEER_HEADS, PEER_KEYS, tt), lambda i, e: (0, 0, i))
    row_spec = pl.BlockSpec((PEER_HEADS, ne // PEER_KEYS, tt), lambda i, e: (0, e, i))
    return pl.pallas_call(
        _peer_main_kernel,
        grid=(t // tt, n_exp // ne),
        in_specs=[pl.BlockSpec((tt, d), lambda i, e: (i, 0)),
                  pl.BlockSpec((1, d), lambda i, e: (0, 0)),
                  pl.BlockSpec((ne, d), lambda i, e: (e, 0)),
                  pl.BlockSpec((d, ne), lambda i, e: (0, e)),
                  row_spec, row_spec, tok_spec, tok_spec],
        out_specs=pl.BlockSpec((tt, d), lambda i, e: (i, 0)),
        out_shape=jax.ShapeDtypeStruct((t, d), F32),
        scratch_shapes=[pltpu.VMEM((d, tt), BF16), pltpu.VMEM((d, tt), F32)],
        compiler_params=_cparams(("parallel", "arbitrary")),
        name="peer_main",
    )(x2, (g_norm.reshape(1, d).astype(F32) * SQRT_HALF), u.astype(BF16), (v.T * SQRT_HALF).astype(BF16),
      cnt, e1, rank, e2)


def _sb_mixer(x2, g_norm, w_in, w_out, batch, seq):
    qkv = _norm_matmul(x2, g_norm, w_in, seq, tn=1024)
    return _proj_resid(_sb_attention(qkv, batch, seq), w_out, x2)


def _moba_mixer(x2, g_norm, w_in, w_out, batch, seq):
    d_model = N_HEADS * HEAD_DIM
    qkv = _norm_matmul(x2, g_norm, w_in, seq, tn=1024, n_rope=2 * d_model // 1024)
    return _proj_resid(_moba_attention(qkv, batch, seq), w_out, x2)


def kernel(x, l0_norm_mix, l0_sb_w_in, l0_sb_w_out, l0_norm_ffn, l0_peer_w_q, l0_peer_sub_keys, l0_peer_u, l0_peer_v, l1_norm_mix, l1_nsa_w_in, l1_nsa_cmp_pos_k, l1_nsa_cmp_pos_v, l1_nsa_cmp_w_k, l1_nsa_cmp_w_v, l1_nsa_w_out, l1_norm_ffn, l1_peer_w_q, l1_peer_sub_keys, l1_peer_u, l1_peer_v, l2_norm_mix, l2_moba_w_in, l2_moba_w_out, l2_norm_ffn, l2_peer_w_q, l2_peer_sub_keys, l2_peer_u, l2_peer_v, l3_norm_mix, l3_sb_w_in, l3_sb_w_out, l3_norm_ffn, l3_peer_w_q, l3_peer_sub_keys, l3_peer_u, l3_peer_v, final_norm):
    batch, seq, d = x.shape
    x2 = x.reshape(batch * seq, d)
    x2 = _sb_mixer(x2, l0_norm_mix, l0_sb_w_in, l0_sb_w_out, batch, seq)
    x2 = _peer_mixer(x2, l0_norm_ffn, l0_peer_w_q, l0_peer_sub_keys, l0_peer_u, l0_peer_v, seq)
    x2 = _nsa_mixer(x2, l1_norm_mix, l1_nsa_w_in, l1_nsa_cmp_pos_k, l1_nsa_cmp_pos_v, l1_nsa_cmp_w_k,
                    l1_nsa_cmp_w_v, l1_nsa_w_out, batch, seq)
    x2 = _peer_mixer(x2, l1_norm_ffn, l1_peer_w_q, l1_peer_sub_keys, l1_peer_u, l1_peer_v, seq)
    x2 = _moba_mixer(x2, l2_norm_mix, l2_moba_w_in, l2_moba_w_out, batch, seq)
    x2 = _peer_mixer(x2, l2_norm_ffn, l2_peer_w_q, l2_peer_sub_keys, l2_peer_u, l2_peer_v, seq)
    x2 = _sb_mixer(x2, l3_norm_mix, l3_sb_w_in, l3_sb_w_out, batch, seq)
    x2 = _peer_mixer(x2, l3_norm_ffn, l3_peer_w_q, l3_peer_sub_keys, l3_peer_u, l3_peer_v, seq)
    return _rmsnorm(x2, final_norm).reshape(batch, seq, d)
```

```python
import functools

import numpy as np
import jax
import jax.numpy as jnp
from jax import lax
from jax.experimental import pallas as pl
from jax.experimental.pallas import tpu as pltpu

F32 = jnp.float32
BF16 = jnp.bfloat16

N_HEADS = 16
HEAD_DIM = 64
ROPE_DIM = 16
ROPE_HALF = ROPE_DIM // 2
ROPE_THETA = 500000.0
NORM_EPS = 1e-6
NEG = -1e30
LOW = -3e38
ATT_SCALE = HEAD_DIM ** -0.5
QK_SCALE = float(np.sqrt(ATT_SCALE * np.log2(np.e)))

NSA_GROUPS = 4
NSA_HPG = N_HEADS // NSA_GROUPS
NSA_KV = NSA_GROUPS * HEAD_DIM
NSA_CMP_LEN = 32
NSA_CMP_STRIDE = 16
NSA_SEL_LEN = 64
NSA_TOP_N = 16
NSA_WINDOW = 512
NSA_BONUS = 1e3
NSA_SEL_PAD = 128

MOBA_BLOCK = 256
MOBA_TOP_K = 3
MOBA_BLK_ALIGN = 16

PEER_HEADS = 8
PEER_KEYS = 128
PEER_TOP_K = 16
PEER_HALF = 64

SB_EXIT = -110.0

LANE = 128
VMEM_LIMIT = 56 << 20


def _cparams(sem, vmem=VMEM_LIMIT):
    return pltpu.CompilerParams(dimension_semantics=sem, vmem_limit_bytes=vmem)


def _dot(a, b):
    return jnp.dot(a, b, preferred_element_type=F32)


def _dot_nt(a, b):
    return lax.dot_general(a, b, (((1,), (1,)), ((), ())), preferred_element_type=F32)


def _split_bf16(x):
    hi = x.astype(BF16)
    lo = (x - hi.astype(F32)).astype(BF16)
    return hi, lo


def _norm_matmul_kernel(*refs, n_rope, tn):
    if n_rope:
        x_ref, g_ref, w_ref, c_ref, sa_ref, sb_ref, o_ref, xn_ref = refs
    else:
        x_ref, g_ref, w_ref, o_ref, xn_ref = refs
    j = pl.program_id(1)

    @pl.when(j == 0)
    def _():
        x = x_ref[...]
        y = x * lax.rsqrt(jnp.mean(x * x, axis=-1, keepdims=True) + NORM_EPS)
        xn_ref[...] = (y * g_ref[...]).astype(BF16)

    acc = _dot(xn_ref[...], w_ref[...])
    if n_rope:
        @pl.when(j < n_rope)
        def _():
            wide = lambda t_ref: jnp.concatenate([t_ref[...]] * (tn // LANE), axis=1)
            r = (acc * wide(c_ref) + pltpu.roll(acc, ROPE_HALF, 1) * wide(sb_ref)
                 + pltpu.roll(acc, tn - ROPE_HALF, 1) * wide(sa_ref))
            o_ref[...] = r.astype(o_ref.dtype)

        @pl.when(j >= n_rope)
        def _():
            o_ref[...] = acc.astype(o_ref.dtype)
    else:
        o_ref[...] = acc.astype(o_ref.dtype)


def _rope_tables(pos, width):
    inv = ROPE_THETA ** (-jnp.arange(0, ROPE_DIM, 2, dtype=F32) / ROPE_DIM)
    ang = pos.astype(F32)[:, None] * inv[None, :]
    cos, sin = jnp.cos(ang) * QK_SCALE, jnp.sin(ang) * QK_SCALE
    n = pos.shape[0]
    rest = HEAD_DIM - ROPE_DIM
    c = jnp.concatenate([cos, cos, jnp.full((n, rest), QK_SCALE, F32)], axis=1)
    sa = jnp.concatenate([-sin, jnp.zeros((n, ROPE_HALF + rest), F32)], axis=1)
    sb = jnp.concatenate([jnp.zeros((n, ROPE_HALF), F32), sin, jnp.zeros((n, rest), F32)], axis=1)
    reps = width // HEAD_DIM
    return jnp.tile(c, (1, reps)), jnp.tile(sa, (1, reps)), jnp.tile(sb, (1, reps))


def _norm_matmul(x2, g, w, seq, *, tn, n_rope=0, out_dtype=BF16, tt=1024):
    t, d = x2.shape
    n = w.shape[1]
    tt = min(tt, seq)
    grid = (t // tt, n // tn)
    in_specs = [pl.BlockSpec((tt, d), lambda i, j: (i, 0)),
                pl.BlockSpec((1, d), lambda i, j: (0, 0)),
                pl.BlockSpec((d, tn), lambda i, j: (0, j))]
    args = [x2, g.reshape(1, d).astype(F32), w.astype(BF16)]
    if n_rope:
        nper = seq // tt
        tabs = _rope_tables(jnp.arange(seq), LANE)
        in_specs += [pl.BlockSpec((tt, LANE), lambda i, j: (i % nper, 0))] * 3
        args += list(tabs)
    return pl.pallas_call(
        functools.partial(_norm_matmul_kernel, n_rope=n_rope, tn=tn),
        grid=grid, in_specs=in_specs,
        out_specs=pl.BlockSpec((tt, tn), lambda i, j: (i, j)),
        out_shape=jax.ShapeDtypeStruct((t, n), out_dtype),
        scratch_shapes=[pltpu.VMEM((tt, d), BF16)],
        compiler_params=_cparams(("parallel", "arbitrary")),
        name="norm_matmul",
    )(*args)


def _norm_matmul3_kernel(x_ref, g_ref, wh_ref, wl_ref, o_ref, xh_ref, xl_ref):
    @pl.when(pl.program_id(1) == 0)
    def _():
        x = x_ref[...]
        y = x * lax.rsqrt(jnp.mean(x * x, axis=-1, keepdims=True) + NORM_EPS) * g_ref[...]
        xh_ref[...], xl_ref[...] = _split_bf16(y)

    o_ref[...] = (_dot(xh_ref[...], wh_ref[...]) + _dot(xl_ref[...], wh_ref[...])
                  + _dot(xh_ref[...], wl_ref[...]))


def _norm_matmul3(x2, g, w, *, tn=512, tt=1024):
    t, d = x2.shape
    n = w.shape[1]
    tt = min(tt, t)
    w_hi, w_lo = _split_bf16(w.astype(F32))
    w_spec = pl.BlockSpec((d, tn), lambda i, j: (0, j))
    return pl.pallas_call(
        _norm_matmul3_kernel,
        grid=(t // tt, n // tn),
        in_specs=[pl.BlockSpec((tt, d), lambda i, j: (i, 0)),
                  pl.BlockSpec((1, d), lambda i, j: (0, 0)), w_spec, w_spec],
        out_specs=pl.BlockSpec((tt, tn), lambda i, j: (i, j)),
        out_shape=jax.ShapeDtypeStruct((t, n), F32),
        scratch_shapes=[pltpu.VMEM((tt, d), BF16), pltpu.VMEM((tt, d), BF16)],
        compiler_params=_cparams(("parallel", "arbitrary")),
        name="norm_matmul3",
    )(x2, g.reshape(1, d).astype(F32), w_hi, w_lo)


def _proj_resid_kernel(a_ref, w_ref, r_ref, o_ref):
    o_ref[...] = r_ref[...] + _dot(a_ref[...], w_ref[...])


def _proj_resid(a, w, resid, *, tt=1024, tn=1024):
    t, d = a.shape
    n = w.shape[1]
    tt = min(tt, t)
    return pl.pallas_call(
        _proj_resid_kernel,
        grid=(t // tt, n // tn),
        in_specs=[pl.BlockSpec((tt, d), lambda i, j: (i, 0)),
                  pl.BlockSpec((d, tn), lambda i, j: (0, j)),
                  pl.BlockSpec((tt, tn), lambda i, j: (i, j))],
        out_specs=pl.BlockSpec((tt, tn), lambda i, j: (i, j)),
        out_shape=jax.ShapeDtypeStruct((t, n), F32),
        compiler_params=_cparams(("parallel", "parallel")),
        name="proj_resid",
    )(a, w.astype(BF16), resid)


def _rmsnorm_kernel(x_ref, g_ref, o_ref):
    x = x_ref[...]
    y = x * lax.rsqrt(jnp.mean(x * x, axis=-1, keepdims=True) + NORM_EPS)
    o_ref[...] = y * g_ref[...]


def _rmsnorm(x2, g, *, tt=512):
    t, d = x2.shape
    tt = min(tt, t)
    return pl.pallas_call(
        _rmsnorm_kernel,
        grid=(t // tt,),
        in_specs=[pl.BlockSpec((tt, d), lambda i: (i, 0)), pl.BlockSpec((1, d), lambda i: (0, 0))],
        out_specs=pl.BlockSpec((tt, d), lambda i: (i, 0)),
        out_shape=jax.ShapeDtypeStruct((t, d), F32),
        compiler_params=_cparams(("parallel",)),
        name="final_rmsnorm",
    )(x2, g.reshape(1, d).astype(F32))


SB_TILE = 256
SB_FIRST = 2 * SB_TILE
SB_HEADS = 4


def _sb_kernel(q_ref, k_ref, v_ref, tri_ref, o_ref):
    tile = SB_TILE
    wide = SB_FIRST
    qi = pl.program_id(2)
    t0 = qi * tile

    nh = SB_HEADS
    head_lanes = [slice(hh * HEAD_DIM, (hh + 1) * HEAD_DIM) for hh in range(nh)]
    qs = [q_ref[:, lanes] * jnp.asarray(ATT_SCALE, BF16) for lanes in head_lanes]

    def step(start, width, r_sums, accs, past):
        tri = tri_ref[0:width, 0:width]
        zs = [_dot_nt(qs[hh], k_ref[pl.ds(start, width), lanes]) for hh, lanes in enumerate(head_lanes)]
        sps, lbs, splits = [], [], []
        for z in zs:
            sp = jnp.maximum(z, 0.0) + jnp.log(1.0 + jnp.exp(-jnp.abs(z)))
            lb = -sp if past is None else jnp.where(past, -sp, 0.0)
            sps.append(sp)
            lbs.append(lb)
            splits.append(_split_bf16(lb))
        suffixes = [_dot(hi, tri) + _dot(lo, tri) for hi, lo in splits]
        weights = []
        for hh in range(nh):
            a = jnp.exp((zs[hh] - sps[hh]) + suffixes[hh] + r_sums[hh])
            if past is not None:
                a = jnp.where(past, a, 0.0)
            weights.append(a.astype(BF16))
        accs = tuple(accs[hh] + _dot(weights[hh], v_ref[pl.ds(start, width), lanes])
                     for hh, lanes in enumerate(head_lanes))
        r_sums = tuple(r_sums[hh] + jnp.sum(lbs[hh], axis=-1, keepdims=True) for hh in range(nh))
        return r_sums, accs

    start0 = pl.multiple_of(jnp.maximum(t0 + tile - wide, 0), tile)
    kpos = start0 + lax.broadcasted_iota(jnp.int32, (tile, wide), 1)
    tpos = t0 + lax.broadcasted_iota(jnp.int32, (tile, wide), 0)
    zero_r = tuple(jnp.zeros((tile, 1), F32) for _ in range(nh))
    zero_acc = tuple(jnp.zeros((tile, HEAD_DIM), F32) for _ in range(nh))
    r1, acc1 = step(start0, wide, zero_r, zero_acc, kpos < tpos)

    def older(j, r_sums):
        alive = jnp.max(functools.reduce(jnp.maximum, r_sums)) > SB_EXIT
        return jnp.where(alive, j - 1, -1)

    def cond(c):
        return c[0] >= 0

    def body(c):
        j, r_sums, accs = c
        r_sums, accs = step(pl.multiple_of(j * tile, tile), tile, r_sums, accs, None)
        return older(j, r_sums), r_sums, accs

    _, _, accs = lax.while_loop(cond, body, (older(start0 // tile, r1), r1, acc1))
    for hh, lanes in enumerate(head_lanes):
        o_ref[:, lanes] = accs[hh].astype(o_ref.dtype)


def _sb_attention(qkv, batch, seq):
    tile = SB_TILE
    nq = seq // tile
    d_model = N_HEADS * HEAD_DIM
    width = SB_HEADS * HEAD_DIM
    ncol = d_model // width
    ii = np.arange(SB_FIRST)
    tri = jnp.asarray(ii[:, None] > ii[None, :], dtype=BF16)
    return pl.pallas_call(
        _sb_kernel,
        grid=(batch, ncol, nq),
        in_specs=[pl.BlockSpec((tile, width), lambda b, h, i: (b * nq + i, h)),
                  pl.BlockSpec((seq, width), lambda b, h, i: (b, ncol + h)),
                  pl.BlockSpec((seq, width), lambda b, h, i: (b, 2 * ncol + h)),
                  pl.BlockSpec((SB_FIRST, SB_FIRST), lambda b, h, i: (0, 0))],
        out_specs=pl.BlockSpec((tile, width), lambda b, h, i: (b * nq + i, h)),
        out_shape=jax.ShapeDtypeStruct((batch * seq, d_model), BF16),
        compiler_params=_cparams(("parallel", "parallel", "arbitrary")),
        name="sb_attention",
    )(qkv, qkv, qkv, tri)


def _flash_t_first(q, k, vt, bias_t):
    s = _dot_nt(k, q) + bias_t
    m = jnp.max(s, axis=0, keepdims=True)
    p = jnp.exp2(s - m)
    return m, jnp.sum(p, axis=0, keepdims=True), _dot(vt, p.astype(BF16))


def _topk_mask_rows(vals, k):
    rowi = lax.broadcasted_iota(jnp.int32, vals.shape, 0)
    work = vals
    sel = jnp.zeros(vals.shape, F32)
    for _ in range(k):
        m = jnp.max(work, axis=0, keepdims=True)
        first = jnp.min(jnp.where(work == m, rowi, vals.shape[0]), axis=0, keepdims=True)
        hit = rowi == first
        sel = jnp.where(hit, 1.0, sel)
        work = jnp.where(hit, LOW, work)
    return sel


MOBA_WALK = 4
MOBA_HEADS = 8


def _moba_kernel(q_ref, k_ref, vt_ref, avg_ref, o_ref, kmean_ref, bias_ref):
    blk = MOBA_BLOCK
    wide = MOBA_WALK * blk
    qi = pl.program_id(2)

    @pl.when(qi == 0)
    def _():
        kmean_ref[...] = _dot(avg_ref[...], k_ref[...])

    key_i = lax.broadcasted_iota(jnp.int32, (blk, blk), 0)
    qry_i = lax.broadcasted_iota(jnp.int32, (blk, blk), 1)
    causal_bias = jnp.where(key_i <= qry_i, 0.0, NEG)
    blk_i = lax.broadcasted_iota(jnp.int32, (kmean_ref.shape[0], blk), 0)
    start = pl.multiple_of(qi * blk, blk)
    head_rows = [slice(hh * HEAD_DIM, (hh + 1) * HEAD_DIM) for hh in range(MOBA_HEADS)]

    qs, states = [], []
    for hh, lanes in enumerate(head_rows):
        q_raw = q_ref[:, lanes]
        km_hi, km_lo = _split_bf16(kmean_ref[:, lanes])
        gate = _dot_nt(km_hi, q_raw) + _dot_nt(km_lo, q_raw)
        gate = jnp.where(blk_i < qi, gate, NEG)
        picked = (_topk_mask_rows(gate, MOBA_TOP_K) > 0.5) & (blk_i < qi)
        bias_ref[hh] = jnp.where(picked, 0.0, NEG)
        qs.append(q_raw)
        states.append(_flash_t_first(qs[-1], k_ref[pl.ds(start, blk), lanes],
                                     vt_ref[lanes, pl.ds(start, blk)], causal_bias))

    def body(p, sts):
        s0 = pl.multiple_of(p * wide, wide)
        scores = [_dot_nt(k_ref[pl.ds(s0, wide), lanes], qs[hh]) for hh, lanes in enumerate(head_rows)]
        probs, nxt = [], []
        for hh in range(MOBA_HEADS):
            m, l, acc_t = sts[hh]
            rows = [bias_ref[hh, pl.ds(p * MOBA_WALK + w, 1), :] for w in range(MOBA_WALK)]
            parts = [scores[hh][w * blk:(w + 1) * blk, :] for w in range(MOBA_WALK)]
            m_new = functools.reduce(
                jnp.maximum, [jnp.max(x, axis=0, keepdims=True) + r for x, r in zip(parts, rows)], m)
            pr = jnp.concatenate([jnp.exp2(x - (m_new - r)) for x, r in zip(parts, rows)], axis=0)
            alpha = jnp.exp2(m - m_new)
            probs.append(pr.astype(BF16))
            nxt.append((m_new, alpha * l + jnp.sum(pr, axis=0, keepdims=True), alpha * acc_t))
        return tuple((m_new, l, acc_s + _dot(vt_ref[lanes, pl.ds(s0, wide)], probs[hh]))
                     for hh, (lanes, (m_new, l, acc_s)) in enumerate(zip(head_rows, nxt)))

    states = lax.fori_loop(0, (qi + MOBA_WALK - 1) // MOBA_WALK, body, tuple(states))
    for hh, lanes in enumerate(head_rows):
        _, l, acc_t = states[hh]
        o_ref[:, lanes] = (acc_t / l).T.astype(o_ref.dtype)


def _moba_attention(qkv, batch, seq):
    blk = MOBA_BLOCK
    wide = MOBA_WALK * blk
    nq = seq // blk
    d_model = N_HEADS * HEAD_DIM
    width = MOBA_HEADS * HEAD_DIM
    ncol = d_model // width
    nb = -(-nq // MOBA_BLK_ALIGN) * MOBA_BLK_ALIGN
    avg = np.zeros((nb, seq), np.float32)
    for n in range(nq):
        avg[n, n * blk:(n + 1) * blk] = 1.0 / blk
    avg = jnp.asarray(avg, dtype=BF16)
    v_t = qkv[:, 2 * d_model:].reshape(batch, seq, d_model).transpose(0, 2, 1)
    return pl.pallas_call(
        _moba_kernel,
        grid=(batch, ncol, nq),
        in_specs=[pl.BlockSpec((blk, width), lambda b, h, i: (b * nq + i, h)),
                  pl.BlockSpec((seq, width), lambda b, h, i: (b, ncol + h)),
                  pl.BlockSpec((None, width, seq), lambda b, h, i: (b, h, 0)),
                  pl.BlockSpec((nb, seq), lambda b, h, i: (0, 0))],
        out_specs=pl.BlockSpec((blk, width), lambda b, h, i: (b * nq + i, h)),
        out_shape=jax.ShapeDtypeStruct((batch * seq, d_model), BF16),
        scratch_shapes=[pltpu.VMEM((nb, width), F32),
                        pltpu.VMEM((MOBA_HEADS, nb, blk), F32)],
        compiler_params=_cparams(("parallel", "parallel", "arbitrary")),
        name="moba_attention",
    )(qkv, qkv, v_t, avg)


def _nsa_compress_kernel(kr_ref, vr_ref, pk_ref, pv_ref, wk_ref, wv_ref, c_ref, sa_ref, sb_ref, kc_ref, vc_ref):
    nc = kr_ref.shape[0]
    rowi = lax.broadcasted_iota(jnp.int32, (nc, 1), 0)

    def windows(x_ref, pos_ref, w_ref):
        x = x_ref[...].astype(F32)
        lo = _dot((x + pos_ref[0:1, :]).astype(BF16), w_ref[0])
        hi = _dot((x + pos_ref[1:2, :]).astype(BF16), w_ref[1])
        hi_next = jnp.where(rowi < nc - 1, pltpu.roll(hi, nc - 1, 0), 0.0)
        return lo + hi_next

    kk = windows(kr_ref, pk_ref, wk_ref)
    d = HEAD_DIM
    kc = kk[:, 0:d] * c_ref[...] + kk[:, d:2 * d] * sb_ref[...] + kk[:, 2 * d:3 * d] * sa_ref[...]
    kc_ref[...] = kc.astype(kc_ref.dtype)
    vv = windows(vr_ref, pv_ref, wv_ref)
    vc_ref[...] = vv[:, 0:d].astype(vc_ref.dtype)


def _nsa_compress(k_cmp, v_cmp, pos_k, pos_v, w_k, w_v, batch, seq):
    g, d, st = NSA_GROUPS, HEAD_DIM, NSA_CMP_STRIDE
    nc = seq // st

    def chunks(x):
        return x.reshape(batch, nc, st, g, d).transpose(0, 3, 1, 2, 4).reshape(batch, g, nc, st * d)

    def weights(w):
        w3 = w.reshape(NSA_CMP_LEN, d, d)
        cat = jnp.concatenate([w3, jnp.roll(w3, ROPE_HALF, axis=2), jnp.roll(w3, -ROPE_HALF, axis=2)], axis=2)
        return cat.reshape(2, st * d, 3 * d).astype(BF16)

    def positions(p):
        return p.reshape(2, st * d).astype(F32)

    cmp_end = jnp.arange(nc) * st + NSA_CMP_LEN - 1
    tabs = _rope_tables(cmp_end, d)
    blk4 = pl.BlockSpec((None, None, nc, st * d), lambda b, gi: (b, gi, 0, 0))
    out4 = pl.BlockSpec((None, None, nc, d), lambda b, gi: (b, gi, 0, 0))
    const2 = lambda shape: pl.BlockSpec(shape, lambda b, gi: (0,) * len(shape))
    return pl.pallas_call(
        _nsa_compress_kernel,
        grid=(batch, g),
        in_specs=[blk4, blk4, const2((2, st * d)), const2((2, st * d)),
                  const2((2, st * d, 3 * d)), const2((2, st * d, 3 * d)),
                  const2((nc, d)), const2((nc, d)), const2((nc, d))],
        out_specs=[out4, out4],
        out_shape=[jax.ShapeDtypeStruct((batch, g, nc, d), BF16)] * 2,
        compiler_params=_cparams(("parallel", "parallel")),
        name="nsa_compress",
    )(chunks(k_cmp), chunks(v_cmp), positions(pos_k), positions(pos_v), weights(w_k), weights(w_v), *tabs)


NSA_TQ = 256
NSA_TK = 1024
NSA_CHAINS = 4


def _nsa_kernel(q_ref, kc_ref, vct_ref, ks_ref, vst_ref, kw_ref, vwt_ref, gate_ref, c2st_ref, o_ref, bias_ref):
    tq, tk, d, nh = NSA_TQ, NSA_TK, HEAD_DIM, NSA_HPG
    nql = nh * tq
    half = nql // NSA_CHAINS
    qi = pl.program_id(2)
    t0 = qi * tq
    nc = kc_ref.shape[0]

    def lanes4(x):
        return jnp.concatenate([x] * nh, axis=1)

    tpos1 = t0 + lax.broadcasted_iota(jnp.int32, (1, tq), 1)
    tpos = lanes4(tpos1)
    gates_t = jax.nn.sigmoid(gate_ref[...]).T

    def gate(branch):
        return jnp.concatenate([gates_t[3 * h + branch:3 * h + branch + 1, :] for h in range(nh)], axis=1)

    q = jnp.concatenate([q_ref[:, h * d:(h + 1) * d] for h in range(nh)], axis=0)

    kd = t0 // tk
    dstart = pl.multiple_of(kd * tk, tk)
    span = NSA_WINDOW + tq
    wstart = pl.multiple_of(jnp.maximum(t0 - NSA_WINDOW, 0), tq)
    s_cmp = _dot_nt(kc_ref[...], q)
    s_win = _dot_nt(kw_ref[pl.ds(wstart, span), :], q)
    s_diag = _dot_nt(ks_ref[pl.ds(dstart, tk), :], q)

    cmp_end = lax.broadcasted_iota(jnp.int32, (nc, nql), 0) * NSA_CMP_STRIDE + (NSA_CMP_LEN - 1)
    vis_c = cmp_end <= tpos
    s = jnp.where(vis_c, s_cmp, NEG)
    e = jnp.where(vis_c, jnp.exp2(s - jnp.max(s, axis=0, keepdims=True)), 0.0)
    l = jnp.sum(e, axis=0, keepdims=True)
    p = e * (1.0 / jnp.where(l > 0.0, l, 1.0))
    out_t = gate(0) * _dot(vct_ref[...], p.astype(BF16))
    psum = functools.reduce(lambda a, b: a + b, [p[:, h * tq:(h + 1) * tq] for h in range(nh)])
    p_hi, p_lo = _split_bf16(psum)
    imp = _dot(c2st_ref[...], p_hi) + _dot(c2st_ref[...], p_lo)
    blk = lax.broadcasted_iota(jnp.int32, (NSA_SEL_PAD, tq), 0)
    own = tpos1 // NSA_SEL_LEN
    forced = (blk == 0) | (blk == own) | (blk == own - 1)
    imp = jnp.where(blk <= own, imp + jnp.where(forced, NSA_BONUS, 0.0), NEG)
    picked = _topk_mask_rows(imp, NSA_TOP_N) > 0.5
    bias_ref[...] = lanes4(jnp.where(picked, 0.0, NEG))

    per_tile = tk // NSA_SEL_LEN

    def tile_bias(kb):
        return jnp.concatenate(
            [jnp.broadcast_to(bias_ref[pl.ds(kb * per_tile + w, 1), :], (NSA_SEL_LEN, nql))
             for w in range(per_tile)], axis=0)

    wpos = wstart + lax.broadcasted_iota(jnp.int32, (span, nql), 0)
    s = jnp.where((wpos <= tpos) & (wpos > tpos - NSA_WINDOW), s_win, NEG)
    e = jnp.exp2(s - jnp.max(s, axis=0, keepdims=True))
    out_t = out_t + (gate(2) / jnp.sum(e, axis=0, keepdims=True)) * _dot(vwt_ref[:, pl.ds(wstart, span)],
                                                                         e.astype(BF16))

    kpos = dstart + lax.broadcasted_iota(jnp.int32, (tk, nql), 0)
    s = jnp.where(kpos <= tpos, s_diag + tile_bias(kd), NEG)
    m = jnp.max(s, axis=0, keepdims=True)
    e = jnp.exp2(s - m)
    l = jnp.sum(e, axis=0, keepdims=True)
    acc = _dot(vst_ref[:, pl.ds(dstart, tk)], e.astype(BF16))
    halves = [slice(i * half, (i + 1) * half) for i in range(NSA_CHAINS)]
    q_half = [q[hs] for hs in halves]
    states = tuple((m[:, hs], l[:, hs], acc[:, hs]) for hs in halves)

    def sel_body(kb, sts):
        start = pl.multiple_of(kb * tk, tk)
        k = ks_ref[pl.ds(start, tk), :]
        vt = vst_ref[:, pl.ds(start, tk)]
        rows = [bias_ref[pl.ds(kb * per_tile + w, 1), :] for w in range(per_tile)]
        scores = [_dot_nt(k, q_half[i]) for i in range(NSA_CHAINS)]
        probs, nxt = [], []
        for i, hs in enumerate(halves):
            m, l, acc = sts[i]
            parts = [scores[i][w * NSA_SEL_LEN:(w + 1) * NSA_SEL_LEN, :] for w in range(per_tile)]
            m_new = functools.reduce(
                jnp.maximum, [jnp.max(x, axis=0, keepdims=True) + r[:, hs] for x, r in zip(parts, rows)], m)
            pr = jnp.concatenate([jnp.exp2(x - (m_new - r[:, hs])) for x, r in zip(parts, rows)], axis=0)
            alpha = jnp.exp2(m - m_new)
            probs.append(pr.astype(BF16))
            nxt.append((m_new, alpha * l + jnp.sum(pr, axis=0, keepdims=True), alpha * acc))
        return tuple((m_new, l, acc + _dot(vt, probs[i])) for i, (m_new, l, acc) in enumerate(nxt))

    states = lax.fori_loop(0, kd, sel_body, states)
    sel_t = jnp.concatenate([st[2] / st[1] for st in states], axis=1)
    out_t = out_t + gate(1) * sel_t
    for h in range(nh):
        o_ref[:, h * d:(h + 1) * d] = out_t[:, h * tq:(h + 1) * tq].T.astype(o_ref.dtype)


def _nsa_attention(q_all, kc, vct, ks, vst, kw, vwt, gates, batch, seq):
    tq, tk, d, g = NSA_TQ, NSA_TK, HEAD_DIM, NSA_GROUPS
    nq = seq // tq
    nc = seq // NSA_CMP_STRIDE
    n_sel = seq // NSA_SEL_LEN
    d_model = N_HEADS * HEAD_DIM
    cs = np.arange(nc) * NSA_CMP_STRIDE
    ss = np.arange(n_sel) * NSA_SEL_LEN
    ov = np.clip(np.minimum(cs[:, None] + NSA_CMP_LEN, ss[None, :] + NSA_SEL_LEN)
                 - np.maximum(cs[:, None], ss[None, :]), 0, None) / NSA_CMP_LEN
    c2s = np.zeros((nc, NSA_SEL_PAD), np.float32)
    c2s[:, :n_sel] = ov
    c2s[nc - 1, :] = 0.0
    k_spec = lambda n: pl.BlockSpec((None, None, n, d), lambda b, gi, i: (b, gi, 0, 0))
    vt_spec = lambda n: pl.BlockSpec((None, None, d, n), lambda b, gi, i: (b, gi, 0, 0))
    return pl.pallas_call(
        _nsa_kernel,
        grid=(batch, g, nq),
        in_specs=[pl.BlockSpec((tq, NSA_HPG * d), lambda b, gi, i: (b * nq + i, gi)),
                  k_spec(nc), vt_spec(nc), k_spec(seq), vt_spec(seq), k_spec(seq), vt_spec(seq),
                  pl.BlockSpec((tq, LANE), lambda b, gi, i: (b * nq + i, gi)),
                  pl.BlockSpec((NSA_SEL_PAD, nc), lambda b, gi, i: (0, 0))],
        out_specs=pl.BlockSpec((tq, NSA_HPG * d), lambda b, gi, i: (b * nq + i, gi)),
        out_shape=jax.ShapeDtypeStruct((batch * seq, d_model), BF16),
        scratch_shapes=[pltpu.VMEM((NSA_SEL_PAD, NSA_HPG * tq), F32)],
        compiler_params=_cparams(("parallel", "parallel", "arbitrary")),
        name="nsa_attention",
    )(q_all, kc, vct, ks, vst, kw, vwt, gates, jnp.asarray(c2s.T, dtype=BF16))


def _nsa_mixer(x2, g_norm, w_in, pos_k, pos_v, w_k, w_v, w_out, batch, seq):
    d_model, kv = N_HEADS * HEAD_DIM, NSA_KV
    sec = lambda i: w_in[:, d_model + i * kv: d_model + (i + 1) * kv]
    w_main = jnp.concatenate([w_in[:, :d_model], sec(2), sec(4), sec(0), sec(1), sec(3), sec(5)], axis=1)
    proj = _norm_matmul(x2, g_norm, w_main, seq, tn=512, n_rope=3)
    w_gate = w_in[:, d_model + 6 * kv:].reshape(d_model, NSA_GROUPS, 3 * NSA_HPG)
    w_gate = jnp.pad(w_gate, ((0, 0), (0, 0), (0, LANE - 3 * NSA_HPG))).reshape(d_model, NSA_GROUPS * LANE)
    gates = _norm_matmul(x2, g_norm, w_gate, seq, tn=NSA_GROUPS * LANE, out_dtype=F32)

    col = lambda i: proj[:, d_model + i * kv: d_model + (i + 1) * kv]
    heads = lambda x: x.reshape(batch, seq, NSA_GROUPS, HEAD_DIM).transpose(0, 2, 1, 3)
    heads_t = lambda x: x.reshape(batch, seq, NSA_GROUPS, HEAD_DIM).transpose(0, 2, 3, 1)
    kc, vc = _nsa_compress(col(2), col(3), pos_k, pos_v, w_k, w_v, batch, seq)
    o = _nsa_attention(proj, kc, vc.transpose(0, 1, 3, 2), heads(col(0)), heads_t(col(4)), heads(col(1)),
                       heads_t(col(5)), gates, batch, seq)
    return _proj_resid(o, w_out, x2)


PEER_SEL_TT = 256
def _batcher_network(n):
    def merge(lo, hi, r):
        step = r * 2
        if step < hi - lo:
            yield from merge(lo, hi, step)
            yield from merge(lo + r, hi, step)
            yield from ((i, i + r) for i in range(lo + r, hi - r, step))
        else:
            yield (lo, lo + r)

    def sort(lo, hi):
        if hi - lo >= 1:
            mid = lo + (hi - lo) // 2
            yield from sort(lo, mid)
            yield from sort(mid + 1, hi)
            yield from merge(lo, hi, 1)

    return list(sort(0, n - 1))


SORT16_NETWORK = _batcher_network(PEER_KEYS // 8)

PEER_PAIRS = [(i, j) for i in range(PEER_TOP_K) for j in range(PEER_TOP_K) if (i + 1) * (j + 1) <= PEER_TOP_K]


def _peer_select_kernel(q_ref, k1_ref, k2_ref, cnt_ref, e1_ref, rank_ref, e2_ref):
    q = q_ref[...]
    q_hi, q_lo = _split_bf16(q)
    nh, nr = PEER_HEADS, PEER_TOP_K

    def scores(k_ref, off):
        k_hi, k_lo = _split_bf16(k_ref[...])
        out = []
        for h in range(nh):
            lanes = slice(off + h * PEER_HALF, off + (h + 1) * PEER_HALF)
            out.append(_dot_nt(k_hi, q_hi[:, lanes]) + _dot_nt(k_hi, q_lo[:, lanes])
                       + _dot_nt(k_lo, q_hi[:, lanes]))
        return out

    def top_values(s):
        slabs = [s[8 * k:8 * (k + 1), :] for k in range(PEER_KEYS // 8)]
        for a, b in SORT16_NETWORK:
            slabs[a], slabs[b] = jnp.maximum(slabs[a], slabs[b]), jnp.minimum(slabs[a], slabs[b])
        vals = []
        for r in range(nr):
            m = jnp.max(slabs[0], axis=0, keepdims=True)
            vals.append(m)
            hit = slabs[0] == m
            for k in range(nr - 1 - r):
                slabs[k] = jnp.where(hit, slabs[k + 1], slabs[k])
        return vals

    s1 = scores(k1_ref, 0)
    s2 = scores(k2_ref, nh * PEER_HALF)
    t1 = [top_values(s) for s in s1]
    t2 = [top_values(s) for s in s2]
    r1 = [jnp.concatenate([t1[h][r] for h in range(nh)], axis=0) for r in range(nr)]
    r2 = [jnp.concatenate([t2[h][r] for h in range(nh)], axis=0) for r in range(nr)]
    cands = [r1[i] + r2[j] for (i, j) in PEER_PAIRS]
    work, tops = list(cands), []
    for _ in range(nr):
        m = functools.reduce(jnp.maximum, work)
        tops.append(m)
        work = [jnp.where(c == m, LOW, c) for c in work]
    thr = tops[PEER_TOP_K - 1]
    cmax = tops[0]
    z = functools.reduce(lambda a, b: a + b, [jnp.where(c >= thr, jnp.exp(c - cmax), 0.0) for c in cands])
    inv_z = 1.0 / z
    for h in range(nh):
        thr_h = thr[h:h + 1, :]
        cnt = jnp.zeros_like(s1[h])
        rank = jnp.zeros_like(s2[h])
        for r in range(nr):
            cnt = jnp.where(s1[h] + t2[h][r] >= thr_h, float(r + 1), cnt)
            rank = jnp.where(t2[h][r] > s2[h], float(r + 1), rank)
        cnt_ref[h] = cnt.astype(BF16)
        e1_ref[h] = jnp.exp(s1[h] - r1[0][h:h + 1, :]).astype(BF16)
        rank_ref[h] = rank.astype(BF16)
        e2_ref[h] = (jnp.exp(s2[h] - r2[0][h:h + 1, :]) * inv_z[h:h + 1, :]).astype(BF16)


def _peer_select(q, sub_keys):
    t = q.shape[0]
    tt = min(PEER_SEL_TT, t)
    out_spec = pl.BlockSpec((PEER_HEADS, PEER_KEYS, tt), lambda i: (0, 0, i))
    return pl.pallas_call(
        _peer_select_kernel,
        grid=(t // tt,),
        in_specs=[pl.BlockSpec((tt, q.shape[1]), lambda i: (i, 0)),
                  pl.BlockSpec((PEER_KEYS, PEER_HALF), lambda i: (0, 0)),
                  pl.BlockSpec((PEER_KEYS, PEER_HALF), lambda i: (0, 0))],
        out_specs=[out_spec] * 4,
        out_shape=[jax.ShapeDtypeStruct((PEER_HEADS, PEER_KEYS, t), BF16)] * 4,
        compiler_params=_cparams(("parallel",)),
        name="peer_select",
    )(q, sub_keys[0].astype(F32), sub_keys[1].astype(F32))


PEER_TT = 512
PEER_NE = 2048
PEER_SUB = 512


SQRT_HALF = float(np.sqrt(0.5))


def _gelu_of_scaled(y):
    return y * (1.0 + lax.erf(y))


def _peer_main_kernel(x_ref, g_ref, u_ref, vt_ref, cnt_ref, e1_ref, rank_ref, e2_ref, o_ref, xnt_ref, acc_ref):
    e = pl.program_id(1)
    ne = u_ref.shape[0]
    tt = x_ref.shape[0]
    sub = PEER_SUB
    n_sub = ne // sub

    @pl.when(e == 0)
    def _():
        x = x_ref[...]
        y = x * lax.rsqrt(jnp.mean(x * x, axis=-1, keepdims=True) + NORM_EPS)
        xnt_ref[...] = (y * g_ref[...]).T.astype(BF16)
        acc_ref[...] = jnp.zeros_like(acc_ref)

    def hidden(c):
        return _dot(u_ref[c * sub:(c + 1) * sub, :], xnt_ref[...])

    def weighted(c, h):
        act = _gelu_of_scaled(h).astype(BF16)
        parts = []
        for k in range(sub // PEER_KEYS):
            i1 = c * (sub // PEER_KEYS) + k
            gsum = None
            for hd in range(PEER_HEADS):
                cnt = cnt_ref[hd, i1:i1 + 1, :]
                e1 = e1_ref[hd, i1:i1 + 1, :]
                term = jnp.where(rank_ref[hd] < cnt, e2_ref[hd], jnp.zeros((), BF16)) * e1
                gsum = term if gsum is None else gsum + term
            parts.append(gsum * act[k * PEER_KEYS:(k + 1) * PEER_KEYS, :])
        return jnp.concatenate(parts, axis=0) if len(parts) > 1 else parts[0]

    def values(c, w):
        return _dot(vt_ref[:, c * sub:(c + 1) * sub], w)

    hs = {0: hidden(0)}
    if n_sub > 1:
        hs[1] = hidden(1)
    total = None
    for c in range(n_sub):
        w = weighted(c, hs.pop(c))
        if c + 2 < n_sub:
            hs[c + 2] = hidden(c + 2)
        pv = values(c, w)
        total = pv if total is None else total + pv
    acc_ref[...] += total

    @pl.when(e == pl.num_programs(1) - 1)
    def _():
        o_ref[...] = x_ref[...] + acc_ref[...].T


def _peer_mixer(x2, g_norm, w_q, sub_keys, u, v, seq):
    t, d = x2.shape
    w_q2 = w_q.reshape(d, PEER_HEADS, 2, PEER_HALF).transpose(0, 2, 1, 3).reshape(d, 2 * PEER_HEADS * PEER_HALF)
    q = _norm_matmul3(x2, g_norm, w_q2)
    cnt, e1, rank, e2 = _peer_select(q, sub_keys)
    tt = min(PEER_TT, t)
    ne = PEER_NE
    n_exp = u.shape[0]
    tok_spec = pl.BlockSpec((PEER_HEADS, PEER_KEYS, tt), lambda i, e: (0, 0, i))
    row_spec = pl.BlockSpec((PEER_HEADS, ne // PEER_KEYS, tt), lambda i, e: (0, e, i))
    return pl.pallas_call(
        _peer_main_kernel,
        grid=(t // tt, n_exp // ne),
        in_specs=[pl.BlockSpec((tt, d), lambda i, e: (i, 0)),
                  pl.BlockSpec((1, d), lambda i, e: (0, 0)),
                  pl.BlockSpec((ne, d), lambda i, e: (e, 0)),
                  pl.BlockSpec((d, ne), lambda i, e: (0, e)),
                  row_spec, row_spec, tok_spec, tok_spec],
        out_specs=pl.BlockSpec((tt, d), lambda i, e: (i, 0)),
        out_shape=jax.ShapeDtypeStruct((t, d), F32),
        scratch_shapes=[pltpu.VMEM((d, tt), BF16), pltpu.VMEM((d, tt), F32)],
        compiler_params=_cparams(("parallel", "arbitrary")),
        name="peer_main",
    )(x2, (g_norm.reshape(1, d).astype(F32) * SQRT_HALF), u.astype(BF16), (v.T * SQRT_HALF).astype(BF16),
      cnt, e1, rank, e2)


def _sb_mixer(x2, g_norm, w_in, w_out, batch, seq):
    qkv = _norm_matmul(x2, g_norm, w_in, seq, tn=1024)
    return _proj_resid(_sb_attention(qkv, batch, seq), w_out, x2)


def _moba_mixer(x2, g_norm, w_in, w_out, batch, seq):
    d_model = N_HEADS * HEAD_DIM
    qkv = _norm_matmul(x2, g_norm, w_in, seq, tn=1024, n_rope=2 * d_model // 1024)
    return _proj_resid(_moba_attention(qkv, batch, seq), w_out, x2)


def kernel(x, l0_norm_mix, l0_sb_w_in, l0_sb_w_out, l0_norm_ffn, l0_peer_w_q, l0_peer_sub_keys, l0_peer_u, l0_peer_v, l1_norm_mix, l1_nsa_w_in, l1_nsa_cmp_pos_k, l1_nsa_cmp_pos_v, l1_nsa_cmp_w_k, l1_nsa_cmp_w_v, l1_nsa_w_out, l1_norm_ffn, l1_peer_w_q, l1_peer_sub_keys, l1_peer_u, l1_peer_v, l2_norm_mix, l2_moba_w_in, l2_moba_w_out, l2_norm_ffn, l2_peer_w_q, l2_peer_sub_keys, l2_peer_u, l2_peer_v, l3_norm_mix, l3_sb_w_in, l3_sb_w_out, l3_norm_ffn, l3_peer_w_q, l3_peer_sub_keys, l3_peer_u, l3_peer_v, final_norm):
    batch, seq, d = x.shape
    x2 = x.reshape(batch * seq, d)
    x2 = _sb_mixer(x2, l0_norm_mix, l0_sb_w_in, l0_sb_w_out, batch, seq)
    x2 = _peer_mixer(x2, l0_norm_ffn, l0_peer_w_q, l0_peer_sub_keys, l0_peer_u, l0_peer_v, seq)
    x2 = _nsa_mixer(x2, l1_norm_mix, l1_nsa_w_in, l1_nsa_cmp_pos_k, l1_nsa_cmp_pos_v, l1_nsa_cmp_w_k,
                    l1_nsa_cmp_w_v, l1_nsa_w_out, batch, seq)
    x2 = _peer_mixer(x2, l1_norm_ffn, l1_peer_w_q, l1_peer_sub_keys, l1_peer_u, l1_peer_v, seq)
    x2 = _moba_mixer(x2, l2_norm_mix, l2_moba_w_in, l2_moba_w_out, batch, seq)
    x2 = _peer_mixer(x2, l2_norm_ffn, l2_peer_w_q, l2_peer_sub_keys, l2_peer_u, l2_peer_v, seq)
    x2 = _sb_mixer(x2, l3_norm_mix, l3_sb_w_in, l3_sb_w_out, batch, seq)
    x2 = _peer_mixer(x2, l3_norm_ffn, l3_peer_w_q, l3_peer_sub_keys, l3_peer_u, l3_peer_v, seq)
    return _rmsnorm(x2, final_norm).reshape(batch, seq, d)
```

```python
import functools

import numpy as np
import jax
import jax.numpy as jnp
from jax import lax
from jax.experimental import pallas as pl
from jax.experimental.pallas import tpu as pltpu

F32 = jnp.float32
BF16 = jnp.bfloat16

N_HEADS = 16
HEAD_DIM = 64
ROPE_DIM = 16
ROPE_HALF = ROPE_DIM // 2
ROPE_THETA = 500000.0
NORM_EPS = 1e-6
NEG = -1e30
LOW = -3e38
ATT_SCALE = HEAD_DIM ** -0.5
QK_SCALE = float(np.sqrt(ATT_SCALE * np.log2(np.e)))

NSA_GROUPS = 4
NSA_HPG = N_HEADS // NSA_GROUPS
NSA_KV = NSA_GROUPS * HEAD_DIM
NSA_CMP_LEN = 32
NSA_CMP_STRIDE = 16
NSA_SEL_LEN = 64
NSA_TOP_N = 16
NSA_WINDOW = 512
NSA_BONUS = 1e3
NSA_SEL_PAD = 128

MOBA_BLOCK = 256
MOBA_TOP_K = 3
MOBA_BLK_ALIGN = 16

PEER_HEADS = 8
PEER_KEYS = 128
PEER_TOP_K = 16
PEER_HALF = 64

SB_EXIT = -110.0

LANE = 128
VMEM_LIMIT = 56 << 20


def _cparams(sem, vmem=VMEM_LIMIT):
    return pltpu.CompilerParams(dimension_semantics=sem, vmem_limit_bytes=vmem)


def _dot(a, b):
    return jnp.dot(a, b, preferred_element_type=F32)


def _dot_nt(a, b):
    return lax.dot_general(a, b, (((1,), (1,)), ((), ())), preferred_element_type=F32)


def _split_bf16(x):
    hi = x.astype(BF16)
    lo = (x - hi.astype(F32)).astype(BF16)
    return hi, lo


def _norm_matmul_kernel(*refs, n_rope, tn):
    if n_rope:
        x_ref, g_ref, w_ref, c_ref, sa_ref, sb_ref, o_ref, xn_ref = refs
    else:
        x_ref, g_ref, w_ref, o_ref, xn_ref = refs
    j = pl.program_id(1)

    @pl.when(j == 0)
    def _():
        x = x_ref[...]
        y = x * lax.rsqrt(jnp.mean(x * x, axis=-1, keepdims=True) + NORM_EPS)
        xn_ref[...] = (y * g_ref[...]).astype(BF16)

    acc = _dot(xn_ref[...], w_ref[...])
    if n_rope:
        @pl.when(j < n_rope)
        def _():
            wide = lambda t_ref: jnp.concatenate([t_ref[...]] * (tn // LANE), axis=1)
            r = (acc * wide(c_ref) + pltpu.roll(acc, ROPE_HALF, 1) * wide(sb_ref)
                 + pltpu.roll(acc, tn - ROPE_HALF, 1) * wide(sa_ref))
            o_ref[...] = r.astype(o_ref.dtype)

        @pl.when(j >= n_rope)
        def _():
            o_ref[...] = acc.astype(o_ref.dtype)
    else:
        o_ref[...] = acc.astype(o_ref.dtype)


def _rope_tables(pos, width):
    inv = ROPE_THETA ** (-jnp.arange(0, ROPE_DIM, 2, dtype=F32) / ROPE_DIM)
    ang = pos.astype(F32)[:, None] * inv[None, :]
    cos, sin = jnp.cos(ang) * QK_SCALE, jnp.sin(ang) * QK_SCALE
    n = pos.shape[0]
    rest = HEAD_DIM - ROPE_DIM
    c = jnp.concatenate([cos, cos, jnp.full((n, rest), QK_SCALE, F32)], axis=1)
    sa = jnp.concatenate([-sin, jnp.zeros((n, ROPE_HALF + rest), F32)], axis=1)
    sb = jnp.concatenate([jnp.zeros((n, ROPE_HALF), F32), sin, jnp.zeros((n, rest), F32)], axis=1)
    reps = width // HEAD_DIM
    return jnp.tile(c, (1, reps)), jnp.tile(sa, (1, reps)), jnp.tile(sb, (1, reps))


def _norm_matmul(x2, g, w, seq, *, tn, n_rope=0, out_dtype=BF16, tt=1024):
    t, d = x2.shape
    n = w.shape[1]
    tt = min(tt, seq)
    grid = (t // tt, n // tn)
    in_specs = [pl.BlockSpec((tt, d), lambda i, j: (i, 0)),
                pl.BlockSpec((1, d), lambda i, j: (0, 0)),
                pl.BlockSpec((d, tn), lambda i, j: (0, j))]
    args = [x2, g.reshape(1, d).astype(F32), w.astype(BF16)]
    if n_rope:
        nper = seq // tt
        tabs = _rope_tables(jnp.arange(seq), LANE)
        in_specs += [pl.BlockSpec((tt, LANE), lambda i, j: (i % nper, 0))] * 3
        args += list(tabs)
    return pl.pallas_call(
        functools.partial(_norm_matmul_kernel, n_rope=n_rope, tn=tn),
        grid=grid, in_specs=in_specs,
        out_specs=pl.BlockSpec((tt, tn), lambda i, j: (i, j)),
        out_shape=jax.ShapeDtypeStruct((t, n), out_dtype),
        scratch_shapes=[pltpu.VMEM((tt, d), BF16)],
        compiler_params=_cparams(("parallel", "arbitrary")),
        name="norm_matmul",
    )(*args)


def _norm_matmul3_kernel(x_ref, g_ref, wh_ref, wl_ref, o_ref, xh_ref, xl_ref):
    @pl.when(pl.program_id(1) == 0)
    def _():
        x = x_ref[...]
        y = x * lax.rsqrt(jnp.mean(x * x, axis=-1, keepdims=True) + NORM_EPS) * g_ref[...]
        xh_ref[...], xl_ref[...] = _split_bf16(y)

    o_ref[...] = (_dot(xh_ref[...], wh_ref[...]) + _dot(xl_ref[...], wh_ref[...])
                  + _dot(xh_ref[...], wl_ref[...]))


def _norm_matmul3(x2, g, w, *, tn=512, tt=1024):
    t, d = x2.shape
    n = w.shape[1]
    tt = min(tt, t)
    w_hi, w_lo = _split_bf16(w.astype(F32))
    w_spec = pl.BlockSpec((d, tn), lambda i, j: (0, j))
    return pl.pallas_call(
        _norm_matmul3_kernel,
        grid=(t // tt, n // tn),
        in_specs=[pl.BlockSpec((tt, d), lambda i, j: (i, 0)),
                  pl.BlockSpec((1, d), lambda i, j: (0, 0)), w_spec, w_spec],
        out_specs=pl.BlockSpec((tt, tn), lambda i, j: (i, j)),
        out_shape=jax.ShapeDtypeStruct((t, n), F32),
        scratch_shapes=[pltpu.VMEM((tt, d), BF16), pltpu.VMEM((tt, d), BF16)],
        compiler_params=_cparams(("parallel", "arbitrary")),
        name="norm_matmul3",
    )(x2, g.reshape(1, d).astype(F32), w_hi, w_lo)


def _proj_resid_kernel(a_ref, w_ref, r_ref, o_ref):
    o_ref[...] = r_ref[...] + _dot(a_ref[...], w_ref[...])


def _proj_resid(a, w, resid, *, tt=1024, tn=1024):
    t, d = a.shape
    n = w.shape[1]
    tt = min(tt, t)
    return pl.pallas_call(
        _proj_resid_kernel,
        grid=(t // tt, n // tn),
        in_specs=[pl.BlockSpec((tt, d), lambda i, j: (i, 0)),
                  pl.BlockSpec((d, tn), lambda i, j: (0, j)),
                  pl.BlockSpec((tt, tn), lambda i, j: (i, j))],
        out_specs=pl.BlockSpec((tt, tn), lambda i, j: (i, j)),
        out_shape=jax.ShapeDtypeStruct((t, n), F32),
        compiler_params=_cparams(("parallel", "parallel")),
        name="proj_resid",
    )(a, w.astype(BF16), resid)


def _rmsnorm_kernel(x_ref, g_ref, o_ref):
    x = x_ref[...]
    y = x * lax.rsqrt(jnp.mean(x * x, axis=-1, keepdims=True) + NORM_EPS)
    o_ref[...] = y * g_ref[...]


def _rmsnorm(x2, g, *, tt=512):
    t, d = x2.shape
    tt = min(tt, t)
    return pl.pallas_call(
        _rmsnorm_kernel,
        grid=(t // tt,),
        in_specs=[pl.BlockSpec((tt, d), lambda i: (i, 0)), pl.BlockSpec((1, d), lambda i: (0, 0))],
        out_specs=pl.BlockSpec((tt, d), lambda i: (i, 0)),
        out_shape=jax.ShapeDtypeStruct((t, d), F32),
        compiler_params=_cparams(("parallel",)),
        name="final_rmsnorm",
    )(x2, g.reshape(1, d).astype(F32))


SB_TILE = 256
SB_FIRST = 2 * SB_TILE
SB_HEADS = 4


def _sb_kernel(q_ref, k_ref, v_ref, tri_ref, o_ref):
    tile = SB_TILE
    wide = SB_FIRST
    qi = pl.program_id(2)
    t0 = qi * tile

    nh = SB_HEADS
    head_lanes = [slice(hh * HEAD_DIM, (hh + 1) * HEAD_DIM) for hh in range(nh)]
    qs = [q_ref[:, lanes] * jnp.asarray(ATT_SCALE, BF16) for lanes in head_lanes]

    def step(start, width, r_sums, accs, past):
        tri = tri_ref[0:width, 0:width]
        zs = [_dot_nt(qs[hh], k_ref[pl.ds(start, width), lanes]) for hh, lanes in enumerate(head_lanes)]
        sps, lbs, splits = [], [], []
        for z in zs:
            sp = jnp.maximum(z, 0.0) + jnp.log(1.0 + jnp.exp(-jnp.abs(z)))
            lb = -sp if past is None else jnp.where(past, -sp, 0.0)
            sps.append(sp)
            lbs.append(lb)
            splits.append(_split_bf16(lb))
        suffixes = [_dot(hi, tri) + _dot(lo, tri) for hi, lo in splits]
        weights = []
        for hh in range(nh):
            a = jnp.exp((zs[hh] - sps[hh]) + suffixes[hh] + r_sums[hh])
            if past is not None:
                a = jnp.where(past, a, 0.0)
            weights.append(a.astype(BF16))
        accs = tuple(accs[hh] + _dot(weights[hh], v_ref[pl.ds(start, width), lanes])
                     for hh, lanes in enumerate(head_lanes))
        r_sums = tuple(r_sums[hh] + jnp.sum(lbs[hh], axis=-1, keepdims=True) for hh in range(nh))
        return r_sums, accs

    start0 = pl.multiple_of(jnp.maximum(t0 + tile - wide, 0), tile)
    kpos = start0 + lax.broadcasted_iota(jnp.int32, (tile, wide), 1)
    tpos = t0 + lax.broadcasted_iota(jnp.int32, (tile, wide), 0)
    zero_r = tuple(jnp.zeros((tile, 1), F32) for _ in range(nh))
    zero_acc = tuple(jnp.zeros((tile, HEAD_DIM), F32) for _ in range(nh))
    r1, acc1 = step(start0, wide, zero_r, zero_acc, kpos < tpos)

    def older(j, r_sums):
        alive = jnp.max(functools.reduce(jnp.maximum, r_sums)) > SB_EXIT
        return jnp.where(alive, j - 1, -1)

    def cond(c):
        return c[0] >= 0

    def body(c):
        j, r_sums, accs = c
        r_sums, accs = step(pl.multiple_of(j * tile, tile), tile, r_sums, accs, None)
        return older(j, r_sums), r_sums, accs

    _, _, accs = lax.while_loop(cond, body, (older(start0 // tile, r1), r1, acc1))
    for hh, lanes in enumerate(head_lanes):
        o_ref[:, lanes] = accs[hh].astype(o_ref.dtype)


def _sb_attention(qkv, batch, seq):
    tile = SB_TILE
    nq = seq // tile
    d_model = N_HEADS * HEAD_DIM
    width = SB_HEADS * HEAD_DIM
    ncol = d_model // width
    ii = np.arange(SB_FIRST)
    tri = jnp.asarray(ii[:, None] > ii[None, :], dtype=BF16)
    return pl.pallas_call(
        _sb_kernel,
        grid=(batch, ncol, nq),
        in_specs=[pl.BlockSpec((tile, width), lambda b, h, i: (b * nq + i, h)),
                  pl.BlockSpec((seq, width), lambda b, h, i: (b, ncol + h)),
                  pl.BlockSpec((seq, width), lambda b, h, i: (b, 2 * ncol + h)),
                  pl.BlockSpec((SB_FIRST, SB_FIRST), lambda b, h, i: (0, 0))],
        out_specs=pl.BlockSpec((tile, width), lambda b, h, i: (b * nq + i, h)),
        out_shape=jax.ShapeDtypeStruct((batch * seq, d_model), BF16),
        compiler_params=_cparams(("parallel", "parallel", "arbitrary")),
        name="sb_attention",
    )(qkv, qkv, qkv, tri)


def _flash_t_first(q, k, vt, bias_t):
    s = _dot_nt(k, q) + bias_t
    m = jnp.max(s, axis=0, keepdims=True)
    p = jnp.exp2(s - m)
    return m, jnp.sum(p, axis=0, keepdims=True), _dot(vt, p.astype(BF16))


def _topk_mask_rows(vals, k):
    rowi = lax.broadcasted_iota(jnp.int32, vals.shape, 0)
    work = vals
    sel = jnp.zeros(vals.shape, F32)
    for _ in range(k):
        m = jnp.max(work, axis=0, keepdims=True)
        first = jnp.min(jnp.where(work == m, rowi, vals.shape[0]), axis=0, keepdims=True)
        hit = rowi == first
        sel = jnp.where(hit, 1.0, sel)
        work = jnp.where(hit, LOW, work)
    return sel


MOBA_WALK = 4
MOBA_HEADS = 8


def _moba_kernel(q_ref, k_ref, vt_ref, avg_ref, o_ref, kmean_ref, bias_ref):
    blk = MOBA_BLOCK
    wide = MOBA_WALK * blk
    qi = pl.program_id(2)

    @pl.when(qi == 0)
    def _():
        kmean_ref[...] = _dot(avg_ref[...], k_ref[...])

    key_i = lax.broadcasted_iota(jnp.int32, (blk, blk), 0)
    qry_i = lax.broadcasted_iota(jnp.int32, (blk, blk), 1)
    causal_bias = jnp.where(key_i <= qry_i, 0.0, NEG)
    blk_i = lax.broadcasted_iota(jnp.int32, (kmean_ref.shape[0], blk), 0)
    start = pl.multiple_of(qi * blk, blk)
    head_rows = [slice(hh * HEAD_DIM, (hh + 1) * HEAD_DIM) for hh in range(MOBA_HEADS)]

    qs, states = [], []
    for hh, lanes in enumerate(head_rows):
        q_raw = q_ref[:, lanes]
        km_hi, km_lo = _split_bf16(kmean_ref[:, lanes])
        gate = _dot_nt(km_hi, q_raw) + _dot_nt(km_lo, q_raw)
        gate = jnp.where(blk_i < qi, gate, NEG)
        picked = (_topk_mask_rows(gate, MOBA_TOP_K) > 0.5) & (blk_i < qi)
        bias_ref[hh] = jnp.where(picked, 0.0, NEG)
        qs.append(q_raw)
        states.append(_flash_t_first(qs[-1], k_ref[pl.ds(start, blk), lanes],
                                     vt_ref[lanes, pl.ds(start, blk)], causal_bias))

    def body(p, sts):
        s0 = pl.multiple_of(p * wide, wide)
        scores = [_dot_nt(k_ref[pl.ds(s0, wide), lanes], qs[hh]) for hh, lanes in enumerate(head_rows)]
        probs, nxt = [], []
        for hh in range(MOBA_HEADS):
            m, l, acc_t = sts[hh]
            rows = [bias_ref[hh, pl.ds(p * MOBA_WALK + w, 1), :] for w in range(MOBA_WALK)]
            parts = [scores[hh][w * blk:(w + 1) * blk, :] for w in range(MOBA_WALK)]
            m_new = functools.reduce(
                jnp.maximum, [jnp.max(x, axis=0, keepdims=True) + r for x, r in zip(parts, rows)], m)
            pr = jnp.concatenate([jnp.exp2(x - (m_new - r)) for x, r in zip(parts, rows)], axis=0)
            alpha = jnp.exp2(m - m_new)
            probs.append(pr.astype(BF16))
            nxt.append((m_new, alpha * l + jnp.sum(pr, axis=0, keepdims=True), alpha * acc_t))
        return tuple((m_new, l, acc_s + _dot(vt_ref[lanes, pl.ds(s0, wide)], probs[hh]))
                     for hh, (lanes, (m_new, l, acc_s)) in enumerate(zip(head_rows, nxt)))

    states = lax.fori_loop(0, (qi + MOBA_WALK - 1) // MOBA_WALK, body, tuple(states))
    for hh, lanes in enumerate(head_rows):
        _, l, acc_t = states[hh]
        o_ref[:, lanes] = (acc_t / l).T.astype(o_ref.dtype)


def _moba_attention(qkv, batch, seq):
    blk = MOBA_BLOCK
    wide = MOBA_WALK * blk
    nq = seq // blk
    d_model = N_HEADS * HEAD_DIM
    width = MOBA_HEADS * HEAD_DIM
    ncol = d_model // width
    nb = -(-nq // MOBA_BLK_ALIGN) * MOBA_BLK_ALIGN
    avg = np.zeros((nb, seq), np.float32)
    for n in range(nq):
        avg[n, n * blk:(n + 1) * blk] = 1.0 / blk
    avg = jnp.asarray(avg, dtype=BF16)
    v_t = qkv[:, 2 * d_model:].reshape(batch, seq, d_model).transpose(0, 2, 1)
    return pl.pallas_call(
        _moba_kernel,
        grid=(batch, ncol, nq),
        in_specs=[pl.BlockSpec((blk, width), lambda b, h, i: (b * nq + i, h)),
                  pl.BlockSpec((seq, width), lambda b, h, i: (b, ncol + h)),
                  pl.BlockSpec((None, width, seq), lambda b, h, i: (b, h, 0)),
                  pl.BlockSpec((nb, seq), lambda b, h, i: (0, 0))],
        out_specs=pl.BlockSpec((blk, width), lambda b, h, i: (b * nq + i, h)),
        out_shape=jax.ShapeDtypeStruct((batch * seq, d_model), BF16),
        scratch_shapes=[pltpu.VMEM((nb, width), F32),
                        pltpu.VMEM((MOBA_HEADS, nb, blk), F32)],
        compiler_params=_cparams(("parallel", "parallel", "arbitrary")),
        name="moba_attention",
    )(qkv, qkv, v_t, avg)


def _nsa_compress_kernel(kr_ref, vr_ref, pk_ref, pv_ref, wk_ref, wv_ref, c_ref, sa_ref, sb_ref, kc_ref, vc_ref):
    nc = kr_ref.shape[0]
    rowi = lax.broadcasted_iota(jnp.int32, (nc, 1), 0)

    def windows(x_ref, pos_ref, w_ref):
        x = x_ref[...].astype(F32)
        lo = _dot((x + pos_ref[0:1, :]).astype(BF16), w_ref[0])
        hi = _dot((x + pos_ref[1:2, :]).astype(BF16), w_ref[1])
        hi_next = jnp.where(rowi < nc - 1, pltpu.roll(hi, nc - 1, 0), 0.0)
        return lo + hi_next

    kk = windows(kr_ref, pk_ref, wk_ref)
    d = HEAD_DIM
    kc = kk[:, 0:d] * c_ref[...] + kk[:, d:2 * d] * sb_ref[...] + kk[:, 2 * d:3 * d] * sa_ref[...]
    kc_ref[...] = kc.astype(kc_ref.dtype)
    vv = windows(vr_ref, pv_ref, wv_ref)
    vc_ref[...] = vv[:, 0:d].astype(vc_ref.dtype)


def _nsa_compress(k_cmp, v_cmp, pos_k, pos_v, w_k, w_v, batch, seq):
    g, d, st = NSA_GROUPS, HEAD_DIM, NSA_CMP_STRIDE
    nc = seq // st

    def chunks(x):
        return x.reshape(batch, nc, st, g, d).transpose(0, 3, 1, 2, 4).reshape(batch, g, nc, st * d)

    def weights(w):
        w3 = w.reshape(NSA_CMP_LEN, d, d)
        cat = jnp.concatenate([w3, jnp.roll(w3, ROPE_HALF, axis=2), jnp.roll(w3, -ROPE_HALF, axis=2)], axis=2)
        return cat.reshape(2, st * d, 3 * d).astype(BF16)

    def positions(p):
        return p.reshape(2, st * d).astype(F32)

    cmp_end = jnp.arange(nc) * st + NSA_CMP_LEN - 1
    tabs = _rope_tables(cmp_end, d)
    blk4 = pl.BlockSpec((None, None, nc, st * d), lambda b, gi: (b, gi, 0, 0))
    out4 = pl.BlockSpec((None, None, nc, d), lambda b, gi: (b, gi, 0, 0))
    const2 = lambda shape: pl.BlockSpec(shape, lambda b, gi: (0,) * len(shape))
    return pl.pallas_call(
        _nsa_compress_kernel,
        grid=(batch, g),
        in_specs=[blk4, blk4, const2((2, st * d)), const2((2, st * d)),
                  const2((2, st * d, 3 * d)), const2((2, st * d, 3 * d)),
                  const2((nc, d)), const2((nc, d)), const2((nc, d))],
        out_specs=[out4, out4],
        out_shape=[jax.ShapeDtypeStruct((batch, g, nc, d), BF16)] * 2,
        compiler_params=_cparams(("parallel", "parallel")),
        name="nsa_compress",
    )(chunks(k_cmp), chunks(v_cmp), positions(pos_k), positions(pos_v), weights(w_k), weights(w_v), *tabs)


NSA_TQ = 256
NSA_TK = 1024
NSA_CHAINS = 4


def _nsa_kernel(q_ref, kc_ref, vct_ref, ks_ref, vst_ref, kw_ref, vwt_ref, gate_ref, c2st_ref, wbias_ref, o_ref,
                bias_ref):
    tq, tk, d, nh = NSA_TQ, NSA_TK, HEAD_DIM, NSA_HPG
    nql = nh * tq
    half = nql // NSA_CHAINS
    qi = pl.program_id(2)
    t0 = qi * tq
    nc = kc_ref.shape[0]

    def lanes4(x):
        return jnp.concatenate([x] * nh, axis=1)

    tpos1 = t0 + lax.broadcasted_iota(jnp.int32, (1, tq), 1)
    tpos = lanes4(tpos1)
    gates_t = jax.nn.sigmoid(gate_ref[...]).T

    def gate(branch):
        return jnp.concatenate([gates_t[3 * h + branch:3 * h + branch + 1, :] for h in range(nh)], axis=1)

    q = jnp.concatenate([q_ref[:, h * d:(h + 1) * d] for h in range(nh)], axis=0)

    kd = t0 // tk
    dstart = pl.multiple_of(kd * tk, tk)
    span = NSA_WINDOW + tq
    wstart = pl.multiple_of(jnp.maximum(t0 - NSA_WINDOW, 0), tq)
    s_cmp = _dot_nt(kc_ref[...], q)
    s_win = _dot_nt(kw_ref[pl.ds(wstart, span), :], q)
    s_diag = _dot_nt(ks_ref[pl.ds(dstart, tk), :], q)

    cmp_end = lax.broadcasted_iota(jnp.int32, (nc, nql), 0) * NSA_CMP_STRIDE + (NSA_CMP_LEN - 1)
    vis_c = cmp_end <= tpos
    s = jnp.where(vis_c, s_cmp, NEG)
    e = jnp.where(vis_c, jnp.exp2(s - jnp.max(s, axis=0, keepdims=True)), 0.0)
    l = jnp.sum(e, axis=0, keepdims=True)
    p = e * (1.0 / jnp.where(l > 0.0, l, 1.0))
    out_t = gate(0) * _dot(vct_ref[...], p.astype(BF16))
    psum = functools.reduce(lambda a, b: a + b, [p[:, h * tq:(h + 1) * tq] for h in range(nh)])
    p_hi, p_lo = _split_bf16(psum)
    imp = _dot(c2st_ref[...], p_hi) + _dot(c2st_ref[...], p_lo)
    blk = lax.broadcasted_iota(jnp.int32, (NSA_SEL_PAD, tq), 0)
    own = tpos1 // NSA_SEL_LEN
    forced = (blk == 0) | (blk == own) | (blk == own - 1)
    imp = jnp.where(blk <= own, imp + jnp.where(forced, NSA_BONUS, 0.0), NEG)
    picked = _topk_mask_rows(imp, NSA_TOP_N) > 0.5
    bias_ref[...] = lanes4(jnp.where(picked, 0.0, NEG))

    per_tile = tk // NSA_SEL_LEN

    def tile_bias(kb):
        return jnp.concatenate(
            [jnp.broadcast_to(bias_ref[pl.ds(kb * per_tile + w, 1), :], (NSA_SEL_LEN, nql))
             for w in range(per_tile)], axis=0)

    s = s_win + lanes4(wbias_ref[jnp.minimum(qi, wbias_ref.shape[0] - 1)])
    e = jnp.exp2(s - jnp.max(s, axis=0, keepdims=True))
    out_t = out_t + (gate(2) / jnp.sum(e, axis=0, keepdims=True)) * _dot(vwt_ref[:, pl.ds(wstart, span)],
                                                                         e.astype(BF16))

    kpos = dstart + lax.broadcasted_iota(jnp.int32, (tk, nql), 0)
    s = jnp.where(kpos <= tpos, s_diag + tile_bias(kd), NEG)
    m = jnp.max(s, axis=0, keepdims=True)
    e = jnp.exp2(s - m)
    l = jnp.sum(e, axis=0, keepdims=True)
    acc = _dot(vst_ref[:, pl.ds(dstart, tk)], e.astype(BF16))
    halves = [slice(i * half, (i + 1) * half) for i in range(NSA_CHAINS)]
    q_half = [q[hs] for hs in halves]
    states = tuple((m[:, hs], l[:, hs], acc[:, hs]) for hs in halves)

    def sel_body(kb, sts):
        start = pl.multiple_of(kb * tk, tk)
        k = ks_ref[pl.ds(start, tk), :]
        vt = vst_ref[:, pl.ds(start, tk)]
        rows = [bias_ref[pl.ds(kb * per_tile + w, 1), :] for w in range(per_tile)]
        scores = [_dot_nt(k, q_half[i]) for i in range(NSA_CHAINS)]
        probs, nxt = [], []
        for i, hs in enumerate(halves):
            m, l, acc = sts[i]
            parts = [scores[i][w * NSA_SEL_LEN:(w + 1) * NSA_SEL_LEN, :] for w in range(per_tile)]
            m_new = functools.reduce(
                jnp.maximum, [jnp.max(x, axis=0, keepdims=True) + r[:, hs] for x, r in zip(parts, rows)], m)
            pr = jnp.concatenate([jnp.exp2(x - (m_new - r[:, hs])) for x, r in zip(parts, rows)], axis=0)
            alpha = jnp.exp2(m - m_new)
            probs.append(pr.astype(BF16))
            nxt.append((m_new, alpha * l + jnp.sum(pr, axis=0, keepdims=True), alpha * acc))
        return tuple((m_new, l, acc + _dot(vt, probs[i])) for i, (m_new, l, acc) in enumerate(nxt))

    states = lax.fori_loop(0, kd, sel_body, states)
    sel_t = jnp.concatenate([st[2] / st[1] for st in states], axis=1)
    out_t = out_t + gate(1) * sel_t
    for h in range(nh):
        o_ref[:, h * d:(h + 1) * d] = out_t[:, h * tq:(h + 1) * tq].T.astype(o_ref.dtype)


def _nsa_attention(q_all, kc, vct, ks, vst, kw, vwt, gates, batch, seq):
    tq, tk, d, g = NSA_TQ, NSA_TK, HEAD_DIM, NSA_GROUPS
    nq = seq // tq
    nc = seq // NSA_CMP_STRIDE
    n_sel = seq // NSA_SEL_LEN
    d_model = N_HEADS * HEAD_DIM
    cs = np.arange(nc) * NSA_CMP_STRIDE
    ss = np.arange(n_sel) * NSA_SEL_LEN
    ov = np.clip(np.minimum(cs[:, None] + NSA_CMP_LEN, ss[None, :] + NSA_SEL_LEN)
                 - np.maximum(cs[:, None], ss[None, :]), 0, None) / NSA_CMP_LEN
    c2s = np.zeros((nc, NSA_SEL_PAD), np.float32)
    c2s[:, :n_sel] = ov
    c2s[nc - 1, :] = 0.0
    span = NSA_WINDOW + tq
    cases = []
    for qi in range(NSA_WINDOW // tq + 1):
        t_abs = qi * tq + np.arange(tq)[None, :]
        c_abs = max(qi * tq - NSA_WINDOW, 0) + np.arange(span)[:, None]
        cases.append(np.where((c_abs <= t_abs) & (c_abs > t_abs - NSA_WINDOW), 0.0, NEG))
    wbias = np.stack(cases).astype(np.float32)
    k_spec = lambda n: pl.BlockSpec((None, None, n, d), lambda b, gi, i: (b, gi, 0, 0))
    vt_spec = lambda n: pl.BlockSpec((None, None, d, n), lambda b, gi, i: (b, gi, 0, 0))
    return pl.pallas_call(
        _nsa_kernel,
        grid=(batch, g, nq),
        in_specs=[pl.BlockSpec((tq, NSA_HPG * d), lambda b, gi, i: (b * nq + i, gi)),
                  k_spec(nc), vt_spec(nc), k_spec(seq), vt_spec(seq), k_spec(seq), vt_spec(seq),
                  pl.BlockSpec((tq, LANE), lambda b, gi, i: (b * nq + i, gi)),
                  pl.BlockSpec((NSA_SEL_PAD, nc), lambda b, gi, i: (0, 0)),
                  pl.BlockSpec(wbias.shape, lambda b, gi, i: (0, 0, 0))],
        out_specs=pl.BlockSpec((tq, NSA_HPG * d), lambda b, gi, i: (b * nq + i, gi)),
        out_shape=jax.ShapeDtypeStruct((batch * seq, d_model), BF16),
        scratch_shapes=[pltpu.VMEM((NSA_SEL_PAD, NSA_HPG * tq), F32)],
        compiler_params=_cparams(("parallel", "parallel", "arbitrary")),
        name="nsa_attention",
    )(q_all, kc, vct, ks, vst, kw, vwt, gates, jnp.asarray(c2s.T, dtype=BF16), jnp.asarray(wbias))


def _nsa_mixer(x2, g_norm, w_in, pos_k, pos_v, w_k, w_v, w_out, batch, seq):
    d_model, kv = N_HEADS * HEAD_DIM, NSA_KV
    sec = lambda i: w_in[:, d_model + i * kv: d_model + (i + 1) * kv]
    w_main = jnp.concatenate([w_in[:, :d_model], sec(2), sec(4), sec(0), sec(1), sec(3), sec(5)], axis=1)
    proj = _norm_matmul(x2, g_norm, w_main, seq, tn=512, n_rope=3)
    w_gate = w_in[:, d_model + 6 * kv:].reshape(d_model, NSA_GROUPS, 3 * NSA_HPG)
    w_gate = jnp.pad(w_gate, ((0, 0), (0, 0), (0, LANE - 3 * NSA_HPG))).reshape(d_model, NSA_GROUPS * LANE)
    gates = _norm_matmul(x2, g_norm, w_gate, seq, tn=NSA_GROUPS * LANE, out_dtype=F32)

    col = lambda i: proj[:, d_model + i * kv: d_model + (i + 1) * kv]
    heads = lambda x: x.reshape(batch, seq, NSA_GROUPS, HEAD_DIM).transpose(0, 2, 1, 3)
    heads_t = lambda x: x.reshape(batch, seq, NSA_GROUPS, HEAD_DIM).transpose(0, 2, 3, 1)
    kc, vc = _nsa_compress(col(2), col(3), pos_k, pos_v, w_k, w_v, batch, seq)
    o = _nsa_attention(proj, kc, vc.transpose(0, 1, 3, 2), heads(col(0)), heads_t(col(4)), heads(col(1)),
                       heads_t(col(5)), gates, batch, seq)
    return _proj_resid(o, w_out, x2)


PEER_SEL_TT = 256
def _batcher_network(n):
    def merge(lo, hi, r):
        step = r * 2
        if step < hi - lo:
            yield from merge(lo, hi, step)
            yield from merge(lo + r, hi, step)
            yield from ((i, i + r) for i in range(lo + r, hi - r, step))
        else:
            yield (lo, lo + r)

    def sort(lo, hi):
        if hi - lo >= 1:
            mid = lo + (hi - lo) // 2
            yield from sort(lo, mid)
            yield from sort(mid + 1, hi)
            yield from merge(lo, hi, 1)

    return list(sort(0, n - 1))


SORT16_NETWORK = _batcher_network(PEER_KEYS // 8)

PEER_PAIRS = [(i, j) for i in range(PEER_TOP_K) for j in range(PEER_TOP_K) if (i + 1) * (j + 1) <= PEER_TOP_K]


def _peer_select_kernel(q_ref, k1_ref, k2_ref, cnt_ref, e1_ref, rank_ref, e2_ref):
    q = q_ref[...]
    q_hi, q_lo = _split_bf16(q)
    nh, nr = PEER_HEADS, PEER_TOP_K

    def scores(k_ref, off):
        k_hi, k_lo = _split_bf16(k_ref[...])
        out = []
        for h in range(nh):
            lanes = slice(off + h * PEER_HALF, off + (h + 1) * PEER_HALF)
            out.append(_dot_nt(k_hi, q_hi[:, lanes]) + _dot_nt(k_hi, q_lo[:, lanes])
                       + _dot_nt(k_lo, q_hi[:, lanes]))
        return out

    def top_values(s):
        slabs = [s[8 * k:8 * (k + 1), :] for k in range(PEER_KEYS // 8)]
        for a, b in SORT16_NETWORK:
            slabs[a], slabs[b] = jnp.maximum(slabs[a], slabs[b]), jnp.minimum(slabs[a], slabs[b])
        vals = []
        for r in range(nr):
            m = jnp.max(slabs[0], axis=0, keepdims=True)
            vals.append(m)
            hit = slabs[0] == m
            for k in range(nr - 1 - r):
                slabs[k] = jnp.where(hit, slabs[k + 1], slabs[k])
        return vals

    s1 = scores(k1_ref, 0)
    s2 = scores(k2_ref, nh * PEER_HALF)
    t1 = [top_values(s) for s in s1]
    t2 = [top_values(s) for s in s2]
    r1 = [jnp.concatenate([t1[h][r] for h in range(nh)], axis=0) for r in range(nr)]
    r2 = [jnp.concatenate([t2[h][r] for h in range(nh)], axis=0) for r in range(nr)]
    cands = [r1[i] + r2[j] for (i, j) in PEER_PAIRS]
    work, tops = list(cands), []
    for _ in range(nr):
        m = functools.reduce(jnp.maximum, work)
        tops.append(m)
        work = [jnp.where(c == m, LOW, c) for c in work]
    thr = tops[PEER_TOP_K - 1]
    cmax = tops[0]
    z = functools.reduce(lambda a, b: a + b, [jnp.where(c >= thr, jnp.exp(c - cmax), 0.0) for c in cands])
    inv_z = 1.0 / z
    count_of_rank = []
    for a in range(nr):
        c = jnp.zeros_like(thr)
        for b in range(nr):
            c = jnp.where(r1[a] + r2[b] >= thr, float(b + 1), c)
        count_of_rank.append(c)
    for h in range(nh):
        cnt = jnp.zeros_like(s1[h])
        rank = jnp.zeros_like(s2[h])
        for r in range(nr):
            cnt = jnp.where(s1[h] == t1[h][r], count_of_rank[r][h:h + 1, :], cnt)
            rank = jnp.where(t2[h][r] > s2[h], float(r + 1), rank)
        cnt_ref[h] = cnt.astype(BF16)
        e1_ref[h] = jnp.exp(s1[h] - r1[0][h:h + 1, :]).astype(BF16)
        rank_ref[h] = rank.astype(BF16)
        e2_ref[h] = (jnp.exp(s2[h] - r2[0][h:h + 1, :]) * inv_z[h:h + 1, :]).astype(BF16)


def _peer_select(q, sub_keys):
    t = q.shape[0]
    tt = min(PEER_SEL_TT, t)
    out_spec = pl.BlockSpec((PEER_HEADS, PEER_KEYS, tt), lambda i: (0, 0, i))
    return pl.pallas_call(
        _peer_select_kernel,
        grid=(t // tt,),
        in_specs=[pl.BlockSpec((tt, q.shape[1]), lambda i: (i, 0)),
                  pl.BlockSpec((PEER_KEYS, PEER_HALF), lambda i: (0, 0)),
                  pl.BlockSpec((PEER_KEYS, PEER_HALF), lambda i: (0, 0))],
        out_specs=[out_spec] * 4,
        out_shape=[jax.ShapeDtypeStruct((PEER_HEADS, PEER_KEYS, t), BF16)] * 4,
        compiler_params=_cparams(("parallel",)),
        name="peer_select",
    )(q, sub_keys[0].astype(F32), sub_keys[1].astype(F32))


PEER_TT = 512
PEER_NE = 2048
PEER_SUB = 512


SQRT_HALF = float(np.sqrt(0.5))


def _gelu_of_scaled(y):
    return y * (1.0 + lax.erf(y))


def _peer_main_kernel(x_ref, g_ref, u_ref, vt_ref, cnt_ref, e1_ref, rank_ref, e2_ref, o_ref, xnt_ref, acc_ref):
    e = pl.program_id(1)
    ne = u_ref.shape[0]
    tt = x_ref.shape[0]
    sub = PEER_SUB
    n_sub = ne // sub

    @pl.when(e == 0)
    def _():
        x = x_ref[...]
        y = x * lax.rsqrt(jnp.mean(x * x, axis=-1, keepdims=True) + NORM_EPS)
        xnt_ref[...] = (y * g_ref[...]).T.astype(BF16)
        acc_ref[...] = jnp.zeros_like(acc_ref)

    def hidden(c):
        return _dot(u_ref[c * sub:(c + 1) * sub, :], xnt_ref[...])

    def weighted(c, h):
        act = _gelu_of_scaled(h).astype(BF16)
        parts = []
        for k in range(sub // PEER_KEYS):
            i1 = c * (sub // PEER_KEYS) + k
            gsum = None
            for hd in range(PEER_HEADS):
                cnt = cnt_ref[hd, i1:i1 + 1, :]
                e1 = e1_ref[hd, i1:i1 + 1, :]
                term = jnp.where(rank_ref[hd] < cnt, e2_ref[hd], jnp.zeros((), BF16)) * e1
                gsum = term if gsum is None else gsum + term
            parts.append(gsum * act[k * PEER_KEYS:(k + 1) * PEER_KEYS, :])
        return jnp.concatenate(parts, axis=0) if len(parts) > 1 else parts[0]

    def values(c, w):
        return _dot(vt_ref[:, c * sub:(c + 1) * sub], w)

    hs = {0: hidden(0)}
    if n_sub > 1:
        hs[1] = hidden(1)
    total = None
    for c in range(n_sub):
        w = weighted(c, hs.pop(c))
        if c + 2 < n_sub:
            hs[c + 2] = hidden(c + 2)
        pv = values(c, w)
        total = pv if total is None else total + pv
    acc_ref[...] += total

    @pl.when(e == pl.num_programs(1) - 1)
    def _():
        o_ref[...] = x_ref[...] + acc_ref[...].T


def _peer_mixer(x2, g_norm, w_q, sub_keys, u, v, seq):
    t, d = x2.shape
    w_q2 = w_q.reshape(d, PEER_HEADS, 2, PEER_HALF).transpose(0, 2, 1, 3).reshape(d, 2 * PEER_HEADS * PEER_HALF)
    q = _norm_matmul3(x2, g_norm, w_q2)
    cnt, e1, rank, e2 = _peer_select(q, sub_keys)
    tt = min(PEER_TT, t)
    ne = PEER_NE
    n_exp = u.shape[0]
    tok_spec = pl.BlockSpec((PEER_HEADS, PEER_KEYS, tt), lambda i, e: (0, 0, i))
    row_spec = pl.BlockSpec((PEER_HEADS, ne // PEER_KEYS, tt), lambda i, e: (0, e, i))
    return pl.pallas_call(
        _peer_main_kernel,
        grid=(t // tt, n_exp // ne),
        in_specs=[pl.BlockSpec((tt, d), lambda i, e: (i, 0)),
                  pl.BlockSpec((1, d), lambda i, e: (0, 0)),
                  pl.BlockSpec((ne, d), lambda i, e: (e, 0)),
                  pl.BlockSpec((d, ne), lambda i, e: (0, e)),
                  row_spec, row_spec, tok_spec, tok_spec],
        out_specs=pl.BlockSpec((tt, d), lambda i, e: (i, 0)),
        out_shape=jax.ShapeDtypeStruct((t, d), F32),
        scratch_shapes=[pltpu.VMEM((d, tt), BF16), pltpu.VMEM((d, tt), F32)],
        compiler_params=_cparams(("parallel", "arbitrary")),
        name="peer_main",
    )(x2, (g_norm.reshape(1, d).astype(F32) * SQRT_HALF), u.astype(BF16), (v.T * SQRT_HALF).astype(BF16),
      cnt, e1, rank, e2)


def _sb_mixer(x2, g_norm, w_in, w_out, batch, seq):
    qkv = _norm_matmul(x2, g_norm, w_in, seq, tn=1024)
    return _proj_resid(_sb_attention(qkv, batch, seq), w_out, x2)


def _moba_mixer(x2, g_norm, w_in, w_out, batch, seq):
    d_model = N_HEADS * HEAD_DIM
    qkv = _norm_matmul(x2, g_norm, w_in, seq, tn=1024, n_rope=2 * d_model // 1024)
    return _proj_resid(_moba_attention(qkv, batch, seq), w_out, x2)


def kernel(x, l0_norm_mix, l0_sb_w_in, l0_sb_w_out, l0_norm_ffn, l0_peer_w_q, l0_peer_sub_keys, l0_peer_u, l0_peer_v, l1_norm_mix, l1_nsa_w_in, l1_nsa_cmp_pos_k, l1_nsa_cmp_pos_v, l1_nsa_cmp_w_k, l1_nsa_cmp_w_v, l1_nsa_w_out, l1_norm_ffn, l1_peer_w_q, l1_peer_sub_keys, l1_peer_u, l1_peer_v, l2_norm_mix, l2_moba_w_in, l2_moba_w_out, l2_norm_ffn, l2_peer_w_q, l2_peer_sub_keys, l2_peer_u, l2_peer_v, l3_norm_mix, l3_sb_w_in, l3_sb_w_out, l3_norm_ffn, l3_peer_w_q, l3_peer_sub_keys, l3_peer_u, l3_peer_v, final_norm):
    batch, seq, d = x.shape
    x2 = x.reshape(batch * seq, d)
    x2 = _sb_mixer(x2, l0_norm_mix, l0_sb_w_in, l0_sb_w_out, batch, seq)
    x2 = _peer_mixer(x2, l0_norm_ffn, l0_peer_w_q, l0_peer_sub_keys, l0_peer_u, l0_peer_v, seq)
    x2 = _nsa_mixer(x2, l1_norm_mix, l1_nsa_w_in, l1_nsa_cmp_pos_k, l1_nsa_cmp_pos_v, l1_nsa_cmp_w_k,
                    l1_nsa_cmp_w_v, l1_nsa_w_out, batch, seq)
    x2 = _peer_mixer(x2, l1_norm_ffn, l1_peer_w_q, l1_peer_sub_keys, l1_peer_u, l1_peer_v, seq)
    x2 = _moba_mixer(x2, l2_norm_mix, l2_moba_w_in, l2_moba_w_out, batch, seq)
    x2 = _peer_mixer(x2, l2_norm_ffn, l2_peer_w_q, l2_peer_sub_keys, l2_peer_u, l2_peer_v, seq)
    x2 = _sb_mixer(x2, l3_norm_mix, l3_sb_w_in, l3_sb_w_out, batch, seq)
    x2 = _peer_mixer(x2, l3_norm_ffn, l3_peer_w_q, l3_peer_sub_keys, l3_peer_u, l3_peer_v, seq)
    return _rmsnorm(x2, final_norm).reshape(batch, seq, d)
```

```python
import functools

import numpy as np
import jax
import jax.numpy as jnp
from jax import lax
from jax.experimental import pallas as pl
from jax.experimental.pallas import tpu as pltpu

F32 = jnp.float32
BF16 = jnp.bfloat16

N_HEADS = 16
HEAD_DIM = 64
ROPE_DIM = 16
ROPE_HALF = ROPE_DIM // 2
ROPE_THETA = 500000.0
NORM_EPS = 1e-6
NEG = -1e30
LOW = -3e38
ATT_SCALE = HEAD_DIM ** -0.5
QK_SCALE = float(np.sqrt(ATT_SCALE * np.log2(np.e)))

NSA_GROUPS = 4
NSA_HPG = N_HEADS // NSA_GROUPS
NSA_KV = NSA_GROUPS * HEAD_DIM
NSA_CMP_LEN = 32
NSA_CMP_STRIDE = 16
NSA_SEL_LEN = 64
NSA_TOP_N = 16
NSA_WINDOW = 512
NSA_BONUS = 1e3
NSA_SEL_PAD = 128

MOBA_BLOCK = 256
MOBA_TOP_K = 3
MOBA_BLK_ALIGN = 16

PEER_HEADS = 8
PEER_KEYS = 128
PEER_TOP_K = 16
PEER_HALF = 64

SB_EXIT = -110.0

LANE = 128
VMEM_LIMIT = 56 << 20


def _cparams(sem, vmem=VMEM_LIMIT):
    return pltpu.CompilerParams(dimension_semantics=sem, vmem_limit_bytes=vmem)


def _dot(a, b):
    return jnp.dot(a, b, preferred_element_type=F32)


def _dot_nt(a, b):
    return lax.dot_general(a, b, (((1,), (1,)), ((), ())), preferred_element_type=F32)


def _split_bf16(x):
    hi = x.astype(BF16)
    lo = (x - hi.astype(F32)).astype(BF16)
    return hi, lo


def _norm_matmul_kernel(*refs, n_rope, tn):
    if n_rope:
        x_ref, g_ref, w_ref, c_ref, sa_ref, sb_ref, o_ref, xn_ref = refs
    else:
        x_ref, g_ref, w_ref, o_ref, xn_ref = refs
    j = pl.program_id(1)

    @pl.when(j == 0)
    def _():
        x = x_ref[...]
        y = x * lax.rsqrt(jnp.mean(x * x, axis=-1, keepdims=True) + NORM_EPS)
        xn_ref[...] = (y * g_ref[...]).astype(BF16)

    acc = _dot(xn_ref[...], w_ref[...])
    if n_rope:
        @pl.when(j < n_rope)
        def _():
            wide = lambda t_ref: jnp.concatenate([t_ref[...]] * (tn // LANE), axis=1)
            r = (acc * wide(c_ref) + pltpu.roll(acc, ROPE_HALF, 1) * wide(sb_ref)
                 + pltpu.roll(acc, tn - ROPE_HALF, 1) * wide(sa_ref))
            o_ref[...] = r.astype(o_ref.dtype)

        @pl.when(j >= n_rope)
        def _():
            o_ref[...] = acc.astype(o_ref.dtype)
    else:
        o_ref[...] = acc.astype(o_ref.dtype)


def _rope_tables(pos, width):
    inv = ROPE_THETA ** (-jnp.arange(0, ROPE_DIM, 2, dtype=F32) / ROPE_DIM)
    ang = pos.astype(F32)[:, None] * inv[None, :]
    cos, sin = jnp.cos(ang) * QK_SCALE, jnp.sin(ang) * QK_SCALE
    n = pos.shape[0]
    rest = HEAD_DIM - ROPE_DIM
    c = jnp.concatenate([cos, cos, jnp.full((n, rest), QK_SCALE, F32)], axis=1)
    sa = jnp.concatenate([-sin, jnp.zeros((n, ROPE_HALF + rest), F32)], axis=1)
    sb = jnp.concatenate([jnp.zeros((n, ROPE_HALF), F32), sin, jnp.zeros((n, rest), F32)], axis=1)
    reps = width // HEAD_DIM
    return jnp.tile(c, (1, reps)), jnp.tile(sa, (1, reps)), jnp.tile(sb, (1, reps))


def _norm_matmul(x2, g, w, seq, *, tn, n_rope=0, out_dtype=BF16, tt=1024):
    t, d = x2.shape
    n = w.shape[1]
    tt = min(tt, seq)
    grid = (t // tt, n // tn)
    in_specs = [pl.BlockSpec((tt, d), lambda i, j: (i, 0)),
                pl.BlockSpec((1, d), lambda i, j: (0, 0)),
                pl.BlockSpec((d, tn), lambda i, j: (0, j))]
    args = [x2, g.reshape(1, d).astype(F32), w.astype(BF16)]
    if n_rope:
        nper = seq // tt
        tabs = _rope_tables(jnp.arange(seq), LANE)
        in_specs += [pl.BlockSpec((tt, LANE), lambda i, j: (i % nper, 0))] * 3
        args += list(tabs)
    return pl.pallas_call(
        functools.partial(_norm_matmul_kernel, n_rope=n_rope, tn=tn),
        grid=grid, in_specs=in_specs,
        out_specs=pl.BlockSpec((tt, tn), lambda i, j: (i, j)),
        out_shape=jax.ShapeDtypeStruct((t, n), out_dtype),
        scratch_shapes=[pltpu.VMEM((tt, d), BF16)],
        compiler_params=_cparams(("parallel", "arbitrary")),
        name="norm_matmul",
    )(*args)


def _norm_matmul3_kernel(x_ref, g_ref, wh_ref, wl_ref, o_ref, xh_ref, xl_ref):
    @pl.when(pl.program_id(1) == 0)
    def _():
        x = x_ref[...]
        y = x * lax.rsqrt(jnp.mean(x * x, axis=-1, keepdims=True) + NORM_EPS) * g_ref[...]
        xh_ref[...], xl_ref[...] = _split_bf16(y)

    o_ref[...] = (_dot(xh_ref[...], wh_ref[...]) + _dot(xl_ref[...], wh_ref[...])
                  + _dot(xh_ref[...], wl_ref[...]))


def _norm_matmul3(x2, g, w, *, tn=512, tt=1024):
    t, d = x2.shape
    n = w.shape[1]
    tt = min(tt, t)
    w_hi, w_lo = _split_bf16(w.astype(F32))
    w_spec = pl.BlockSpec((d, tn), lambda i, j: (0, j))
    return pl.pallas_call(
        _norm_matmul3_kernel,
        grid=(t // tt, n // tn),
        in_specs=[pl.BlockSpec((tt, d), lambda i, j: (i, 0)),
                  pl.BlockSpec((1, d), lambda i, j: (0, 0)), w_spec, w_spec],
        out_specs=pl.BlockSpec((tt, tn), lambda i, j: (i, j)),
        out_shape=jax.ShapeDtypeStruct((t, n), F32),
        scratch_shapes=[pltpu.VMEM((tt, d), BF16), pltpu.VMEM((tt, d), BF16)],
        compiler_params=_cparams(("parallel", "arbitrary")),
        name="norm_matmul3",
    )(x2, g.reshape(1, d).astype(F32), w_hi, w_lo)


def _proj_resid_kernel(a_ref, w_ref, r_ref, o_ref):
    o_ref[...] = r_ref[...] + _dot(a_ref[...], w_ref[...])


def _proj_resid(a, w, resid, *, tt=1024, tn=1024):
    t, d = a.shape
    n = w.shape[1]
    tt = min(tt, t)
    return pl.pallas_call(
        _proj_resid_kernel,
        grid=(t // tt, n // tn),
        in_specs=[pl.BlockSpec((tt, d), lambda i, j: (i, 0)),
                  pl.BlockSpec((d, tn), lambda i, j: (0, j)),
                  pl.BlockSpec((tt, tn), lambda i, j: (i, j))],
        out_specs=pl.BlockSpec((tt, tn), lambda i, j: (i, j)),
        out_shape=jax.ShapeDtypeStruct((t, n), F32),
        compiler_params=_cparams(("parallel", "parallel")),
        name="proj_resid",
    )(a, w.astype(BF16), resid)


def _rmsnorm_kernel(x_ref, g_ref, o_ref):
    x = x_ref[...]
    y = x * lax.rsqrt(jnp.mean(x * x, axis=-1, keepdims=True) + NORM_EPS)
    o_ref[...] = y * g_ref[...]


def _rmsnorm(x2, g, *, tt=512):
    t, d = x2.shape
    tt = min(tt, t)
    return pl.pallas_call(
        _rmsnorm_kernel,
        grid=(t // tt,),
        in_specs=[pl.BlockSpec((tt, d), lambda i: (i, 0)), pl.BlockSpec((1, d), lambda i: (0, 0))],
        out_specs=pl.BlockSpec((tt, d), lambda i: (i, 0)),
        out_shape=jax.ShapeDtypeStruct((t, d), F32),
        compiler_params=_cparams(("parallel",)),
        name="final_rmsnorm",
    )(x2, g.reshape(1, d).astype(F32))


SB_TILE = 256
SB_FIRST = 2 * SB_TILE
SB_HEADS = 4


def _sb_kernel(q_ref, k_ref, v_ref, tri_ref, o_ref):
    tile = SB_TILE
    wide = SB_FIRST
    qi = pl.program_id(2)
    t0 = qi * tile

    nh = SB_HEADS
    head_lanes = [slice(hh * HEAD_DIM, (hh + 1) * HEAD_DIM) for hh in range(nh)]
    qs = [q_ref[:, lanes] * jnp.asarray(ATT_SCALE, BF16) for lanes in head_lanes]

    def step(start, width, r_sums, accs, past):
        tri = tri_ref[0:width, 0:width]
        zs = [_dot_nt(qs[hh], k_ref[pl.ds(start, width), lanes]) for hh, lanes in enumerate(head_lanes)]
        sps, lbs, splits = [], [], []
        for z in zs:
            sp = jnp.maximum(z, 0.0) + jnp.log(1.0 + jnp.exp(-jnp.abs(z)))
            lb = -sp if past is None else jnp.where(past, -sp, 0.0)
            sps.append(sp)
            lbs.append(lb)
            splits.append(_split_bf16(lb))
        suffixes = [_dot(hi, tri) + _dot(lo, tri) for hi, lo in splits]
        weights = []
        for hh in range(nh):
            a = jnp.exp((zs[hh] - sps[hh]) + suffixes[hh] + r_sums[hh])
            if past is not None:
                a = jnp.where(past, a, 0.0)
            weights.append(a.astype(BF16))
        accs = tuple(accs[hh] + _dot(weights[hh], v_ref[pl.ds(start, width), lanes])
                     for hh, lanes in enumerate(head_lanes))
        r_sums = tuple(r_sums[hh] + jnp.sum(lbs[hh], axis=-1, keepdims=True) for hh in range(nh))
        return r_sums, accs

    start0 = pl.multiple_of(jnp.maximum(t0 + tile - wide, 0), tile)
    kpos = start0 + lax.broadcasted_iota(jnp.int32, (tile, wide), 1)
    tpos = t0 + lax.broadcasted_iota(jnp.int32, (tile, wide), 0)
    zero_r = tuple(jnp.zeros((tile, 1), F32) for _ in range(nh))
    zero_acc = tuple(jnp.zeros((tile, HEAD_DIM), F32) for _ in range(nh))
    r1, acc1 = step(start0, wide, zero_r, zero_acc, kpos < tpos)

    def older(j, r_sums):
        alive = jnp.max(functools.reduce(jnp.maximum, r_sums)) > SB_EXIT
        return jnp.where(alive, j - 1, -1)

    def cond(c):
        return c[0] >= 0

    def body(c):
        j, r_sums, accs = c
        r_sums, accs = step(pl.multiple_of(j * tile, tile), tile, r_sums, accs, None)
        return older(j, r_sums), r_sums, accs

    _, _, accs = lax.while_loop(cond, body, (older(start0 // tile, r1), r1, acc1))
    for hh, lanes in enumerate(head_lanes):
        o_ref[:, lanes] = accs[hh].astype(o_ref.dtype)


def _sb_attention(qkv, batch, seq):
    tile = SB_TILE
    nq = seq // tile
    d_model = N_HEADS * HEAD_DIM
    width = SB_HEADS * HEAD_DIM
    ncol = d_model // width
    ii = np.arange(SB_FIRST)
    tri = jnp.asarray(ii[:, None] > ii[None, :], dtype=BF16)
    return pl.pallas_call(
        _sb_kernel,
        grid=(batch, ncol, nq),
        in_specs=[pl.BlockSpec((tile, width), lambda b, h, i: (b * nq + i, h)),
                  pl.BlockSpec((seq, width), lambda b, h, i: (b, ncol + h)),
                  pl.BlockSpec((seq, width), lambda b, h, i: (b, 2 * ncol + h)),
                  pl.BlockSpec((SB_FIRST, SB_FIRST), lambda b, h, i: (0, 0))],
        out_specs=pl.BlockSpec((tile, width), lambda b, h, i: (b * nq + i, h)),
        out_shape=jax.ShapeDtypeStruct((batch * seq, d_model), BF16),
        compiler_params=_cparams(("parallel", "parallel", "arbitrary")),
        name="sb_attention",
    )(qkv, qkv, qkv, tri)


def _flash_t_first(q, k, vt, bias_t):
    s = _dot_nt(k, q) + bias_t
    m = jnp.max(s, axis=0, keepdims=True)
    p = jnp.exp2(s - m)
    return m, jnp.sum(p, axis=0, keepdims=True), _dot(vt, p.astype(BF16))


def _topk_mask_rows(vals, k):
    rowi = lax.broadcasted_iota(jnp.int32, vals.shape, 0)
    work = vals
    sel = jnp.zeros(vals.shape, F32)
    for _ in range(k):
        m = jnp.max(work, axis=0, keepdims=True)
        first = jnp.min(jnp.where(work == m, rowi, vals.shape[0]), axis=0, keepdims=True)
        hit = rowi == first
        sel = jnp.where(hit, 1.0, sel)
        work = jnp.where(hit, LOW, work)
    return sel


MOBA_WALK = 4
MOBA_HEADS = 8


def _moba_kernel(q_ref, k_ref, vt_ref, avg_ref, o_ref, kmean_ref, bias_ref):
    blk = MOBA_BLOCK
    wide = MOBA_WALK * blk
    qi = pl.program_id(2)

    @pl.when(qi == 0)
    def _():
        kmean_ref[...] = _dot(avg_ref[...], k_ref[...])

    key_i = lax.broadcasted_iota(jnp.int32, (blk, blk), 0)
    qry_i = lax.broadcasted_iota(jnp.int32, (blk, blk), 1)
    causal_bias = jnp.where(key_i <= qry_i, 0.0, NEG)
    blk_i = lax.broadcasted_iota(jnp.int32, (kmean_ref.shape[0], blk), 0)
    start = pl.multiple_of(qi * blk, blk)
    head_rows = [slice(hh * HEAD_DIM, (hh + 1) * HEAD_DIM) for hh in range(MOBA_HEADS)]

    qs, states = [], []
    for hh, lanes in enumerate(head_rows):
        q_raw = q_ref[:, lanes]
        km_hi, km_lo = _split_bf16(kmean_ref[:, lanes])
        gate = _dot_nt(km_hi, q_raw) + _dot_nt(km_lo, q_raw)
        gate = jnp.where(blk_i < qi, gate, NEG)
        picked = (_topk_mask_rows(gate, MOBA_TOP_K) > 0.5) & (blk_i < qi)
        bias_ref[hh] = jnp.where(picked, 0.0, NEG)
        qs.append(q_raw)
        states.append(_flash_t_first(qs[-1], k_ref[pl.ds(start, blk), lanes],
                                     vt_ref[lanes, pl.ds(start, blk)], causal_bias))

    def body(p, sts):
        s0 = pl.multiple_of(p * wide, wide)
        scores = [_dot_nt(k_ref[pl.ds(s0, wide), lanes], qs[hh]) for hh, lanes in enumerate(head_rows)]
        probs, nxt = [], []
        for hh in range(MOBA_HEADS):
            m, l, acc_t = sts[hh]
            rows = [bias_ref[hh, pl.ds(p * MOBA_WALK + w, 1), :] for w in range(MOBA_WALK)]
            parts = [scores[hh][w * blk:(w + 1) * blk, :] for w in range(MOBA_WALK)]
            m_new = functools.reduce(
                jnp.maximum, [jnp.max(x, axis=0, keepdims=True) + r for x, r in zip(parts, rows)], m)
            pr = jnp.concatenate([jnp.exp2(x - (m_new - r)) for x, r in zip(parts, rows)], axis=0)
            alpha = jnp.exp2(m - m_new)
            probs.append(pr.astype(BF16))
            nxt.append((m_new, alpha * l + jnp.sum(pr, axis=0, keepdims=True), alpha * acc_t))
        return tuple((m_new, l, acc_s + _dot(vt_ref[lanes, pl.ds(s0, wide)], probs[hh]))
                     for hh, (lanes, (m_new, l, acc_s)) in enumerate(zip(head_rows, nxt)))

    states = lax.fori_loop(0, (qi + MOBA_WALK - 1) // MOBA_WALK, body, tuple(states))
    for hh, lanes in enumerate(head_rows):
        _, l, acc_t = states[hh]
        o_ref[:, lanes] = (acc_t / l).T.astype(o_ref.dtype)


def _moba_attention(qkv, batch, seq):
    blk = MOBA_BLOCK
    wide = MOBA_WALK * blk
    nq = seq // blk
    d_model = N_HEADS * HEAD_DIM
    width = MOBA_HEADS * HEAD_DIM
    ncol = d_model // width
    nb = -(-nq // MOBA_BLK_ALIGN) * MOBA_BLK_ALIGN
    avg = np.zeros((nb, seq), np.float32)
    for n in range(nq):
        avg[n, n * blk:(n + 1) * blk] = 1.0 / blk
    avg = jnp.asarray(avg, dtype=BF16)
    v_t = qkv[:, 2 * d_model:].reshape(batch, seq, d_model).transpose(0, 2, 1)
    return pl.pallas_call(
        _moba_kernel,
        grid=(batch, ncol, nq),
        in_specs=[pl.BlockSpec((blk, width), lambda b, h, i: (b * nq + i, h)),
                  pl.BlockSpec((seq, width), lambda b, h, i: (b, ncol + h)),
                  pl.BlockSpec((None, width, seq), lambda b, h, i: (b, h, 0)),
                  pl.BlockSpec((nb, seq), lambda b, h, i: (0, 0))],
        out_specs=pl.BlockSpec((blk, width), lambda b, h, i: (b * nq + i, h)),
        out_shape=jax.ShapeDtypeStruct((batch * seq, d_model), BF16),
        scratch_shapes=[pltpu.VMEM((nb, width), F32),
                        pltpu.VMEM((MOBA_HEADS, nb, blk), F32)],
        compiler_params=_cparams(("parallel", "parallel", "arbitrary")),
        name="moba_attention",
    )(qkv, qkv, v_t, avg)


def _nsa_compress_kernel(kr_ref, vr_ref, pk_ref, pv_ref, wk_ref, wv_ref, c_ref, sa_ref, sb_ref, kc_ref, vc_ref):
    nc = kr_ref.shape[0]
    rowi = lax.broadcasted_iota(jnp.int32, (nc, 1), 0)

    def windows(x_ref, pos_ref, w_ref):
        x = x_ref[...].astype(F32)
        lo = _dot((x + pos_ref[0:1, :]).astype(BF16), w_ref[0])
        hi = _dot((x + pos_ref[1:2, :]).astype(BF16), w_ref[1])
        hi_next = jnp.where(rowi < nc - 1, pltpu.roll(hi, nc - 1, 0), 0.0)
        return lo + hi_next

    kk = windows(kr_ref, pk_ref, wk_ref)
    d = HEAD_DIM
    kc = kk[:, 0:d] * c_ref[...] + kk[:, d:2 * d] * sb_ref[...] + kk[:, 2 * d:3 * d] * sa_ref[...]
    kc_ref[...] = kc.astype(kc_ref.dtype)
    vv = windows(vr_ref, pv_ref, wv_ref)
    vc_ref[...] = vv[:, 0:d].astype(vc_ref.dtype)


def _nsa_compress(k_cmp, v_cmp, pos_k, pos_v, w_k, w_v, batch, seq):
    g, d, st = NSA_GROUPS, HEAD_DIM, NSA_CMP_STRIDE
    nc = seq // st

    def chunks(x):
        return x.reshape(batch, nc, st, g, d).transpose(0, 3, 1, 2, 4).reshape(batch, g, nc, st * d)

    def weights(w):
        w3 = w.reshape(NSA_CMP_LEN, d, d)
        cat = jnp.concatenate([w3, jnp.roll(w3, ROPE_HALF, axis=2), jnp.roll(w3, -ROPE_HALF, axis=2)], axis=2)
        return cat.reshape(2, st * d, 3 * d).astype(BF16)

    def positions(p):
        return p.reshape(2, st * d).astype(F32)

    cmp_end = jnp.arange(nc) * st + NSA_CMP_LEN - 1
    tabs = _rope_tables(cmp_end, d)
    blk4 = pl.BlockSpec((None, None, nc, st * d), lambda b, gi: (b, gi, 0, 0))
    out4 = pl.BlockSpec((None, None, nc, d), lambda b, gi: (b, gi, 0, 0))
    const2 = lambda shape: pl.BlockSpec(shape, lambda b, gi: (0,) * len(shape))
    return pl.pallas_call(
        _nsa_compress_kernel,
        grid=(batch, g),
        in_specs=[blk4, blk4, const2((2, st * d)), const2((2, st * d)),
                  const2((2, st * d, 3 * d)), const2((2, st * d, 3 * d)),
                  const2((nc, d)), const2((nc, d)), const2((nc, d))],
        out_specs=[out4, out4],
        out_shape=[jax.ShapeDtypeStruct((batch, g, nc, d), BF16)] * 2,
        compiler_params=_cparams(("parallel", "parallel")),
        name="nsa_compress",
    )(chunks(k_cmp), chunks(v_cmp), positions(pos_k), positions(pos_v), weights(w_k), weights(w_v), *tabs)


NSA_TQ = 256
NSA_TK = 1024
NSA_CHAINS = 4


def _nsa_kernel(q_ref, kc_ref, vct_ref, ks_ref, vst_ref, kw_ref, vwt_ref, gate_ref, c2st_ref, wbias_ref, cbias_ref,
                o_ref, bias_ref):
    tq, tk, d, nh = NSA_TQ, NSA_TK, HEAD_DIM, NSA_HPG
    nql = nh * tq
    half = nql // NSA_CHAINS
    qi = pl.program_id(2)
    t0 = qi * tq
    nc = kc_ref.shape[0]

    def lanes4(x):
        return jnp.concatenate([x] * nh, axis=1)

    tpos1 = t0 + lax.broadcasted_iota(jnp.int32, (1, tq), 1)
    tpos = lanes4(tpos1)
    gates_t = jax.nn.sigmoid(gate_ref[...]).T

    def gate(branch):
        return jnp.concatenate([gates_t[3 * h + branch:3 * h + branch + 1, :] for h in range(nh)], axis=1)

    q = jnp.concatenate([q_ref[:, h * d:(h + 1) * d] for h in range(nh)], axis=0)

    kd = t0 // tk
    dstart = pl.multiple_of(kd * tk, tk)
    span = NSA_WINDOW + tq
    wstart = pl.multiple_of(jnp.maximum(t0 - NSA_WINDOW, 0), tq)
    s_cmp = _dot_nt(kc_ref[...], q)
    s_win = _dot_nt(kw_ref[pl.ds(wstart, span), :], q)
    s_diag = _dot_nt(ks_ref[pl.ds(dstart, tk), :], q)

    cmp_end = lax.broadcasted_iota(jnp.int32, (nc, nql), 0) * NSA_CMP_STRIDE + (NSA_CMP_LEN - 1)
    vis_c = cmp_end <= tpos
    s = jnp.where(vis_c, s_cmp, NEG)
    e = jnp.where(vis_c, jnp.exp2(s - jnp.max(s, axis=0, keepdims=True)), 0.0)
    l = jnp.sum(e, axis=0, keepdims=True)
    p = e * (1.0 / jnp.where(l > 0.0, l, 1.0))
    out_t = gate(0) * _dot(vct_ref[...], p.astype(BF16))
    psum = functools.reduce(lambda a, b: a + b, [p[:, h * tq:(h + 1) * tq] for h in range(nh)])
    p_hi, p_lo = _split_bf16(psum)
    imp = _dot(c2st_ref[...], p_hi) + _dot(c2st_ref[...], p_lo)
    blk = lax.broadcasted_iota(jnp.int32, (NSA_SEL_PAD, tq), 0)
    own = tpos1 // NSA_SEL_LEN
    forced = (blk == 0) | (blk == own) | (blk == own - 1)
    imp = jnp.where(blk <= own, imp + jnp.where(forced, NSA_BONUS, 0.0), NEG)
    picked = _topk_mask_rows(imp, NSA_TOP_N) > 0.5
    bias_ref[...] = lanes4(jnp.where(picked, 0.0, NEG))

    per_tile = tk // NSA_SEL_LEN

    def tile_bias(kb):
        return jnp.concatenate(
            [jnp.broadcast_to(bias_ref[pl.ds(kb * per_tile + w, 1), :], (NSA_SEL_LEN, nql))
             for w in range(per_tile)], axis=0)

    s = s_win + lanes4(wbias_ref[jnp.minimum(qi, wbias_ref.shape[0] - 1)])
    e = jnp.exp2(s - jnp.max(s, axis=0, keepdims=True))
    out_t = out_t + (gate(2) / jnp.sum(e, axis=0, keepdims=True)) * _dot(vwt_ref[:, pl.ds(wstart, span)],
                                                                         e.astype(BF16))

    s = s_diag + tile_bias(kd) + lanes4(cbias_ref[(t0 - dstart) // tq])
    m = jnp.max(s, axis=0, keepdims=True)
    e = jnp.exp2(s - m)
    l = jnp.sum(e, axis=0, keepdims=True)
    acc = _dot(vst_ref[:, pl.ds(dstart, tk)], e.astype(BF16))
    halves = [slice(i * half, (i + 1) * half) for i in range(NSA_CHAINS)]
    q_half = [q[hs] for hs in halves]
    states = tuple((m[:, hs], l[:, hs], acc[:, hs]) for hs in halves)

    def sel_body(kb, sts):
        start = pl.multiple_of(kb * tk, tk)
        k = ks_ref[pl.ds(start, tk), :]
        vt = vst_ref[:, pl.ds(start, tk)]
        rows = [bias_ref[pl.ds(kb * per_tile + w, 1), :] for w in range(per_tile)]
        scores = [_dot_nt(k, q_half[i]) for i in range(NSA_CHAINS)]
        probs, nxt = [], []
        for i, hs in enumerate(halves):
            m, l, acc = sts[i]
            parts = [scores[i][w * NSA_SEL_LEN:(w + 1) * NSA_SEL_LEN, :] for w in range(per_tile)]
            m_new = functools.reduce(
                jnp.maximum, [jnp.max(x, axis=0, keepdims=True) + r[:, hs] for x, r in zip(parts, rows)], m)
            pr = jnp.concatenate([jnp.exp2(x - (m_new - r[:, hs])) for x, r in zip(parts, rows)], axis=0)
            alpha = jnp.exp2(m - m_new)
            probs.append(pr.astype(BF16))
            nxt.append((m_new, alpha * l + jnp.sum(pr, axis=0, keepdims=True), alpha * acc))
        return tuple((m_new, l, acc + _dot(vt, probs[i])) for i, (m_new, l, acc) in enumerate(nxt))

    states = lax.fori_loop(0, kd, sel_body, states)
    sel_t = jnp.concatenate([st[2] / st[1] for st in states], axis=1)
    out_t = out_t + gate(1) * sel_t
    for h in range(nh):
        o_ref[:, h * d:(h + 1) * d] = out_t[:, h * tq:(h + 1) * tq].T.astype(o_ref.dtype)


def _nsa_attention(q_all, kc, vct, ks, vst, kw, vwt, gates, batch, seq):
    tq, tk, d, g = NSA_TQ, NSA_TK, HEAD_DIM, NSA_GROUPS
    nq = seq // tq
    nc = seq // NSA_CMP_STRIDE
    n_sel = seq // NSA_SEL_LEN
    d_model = N_HEADS * HEAD_DIM
    cs = np.arange(nc) * NSA_CMP_STRIDE
    ss = np.arange(n_sel) * NSA_SEL_LEN
    ov = np.clip(np.minimum(cs[:, None] + NSA_CMP_LEN, ss[None, :] + NSA_SEL_LEN)
                 - np.maximum(cs[:, None], ss[None, :]), 0, None) / NSA_CMP_LEN
    c2s = np.zeros((nc, NSA_SEL_PAD), np.float32)
    c2s[:, :n_sel] = ov
    c2s[nc - 1, :] = 0.0
    span = NSA_WINDOW + tq
    cases = []
    for qi in range(NSA_WINDOW // tq + 1):
        t_abs = qi * tq + np.arange(tq)[None, :]
        c_abs = max(qi * tq - NSA_WINDOW, 0) + np.arange(span)[:, None]
        cases.append(np.where((c_abs <= t_abs) & (c_abs > t_abs - NSA_WINDOW), 0.0, NEG))
    wbias = np.stack(cases).astype(np.float32)
    cbias = np.stack([np.where(np.arange(tk)[:, None] <= r * tq + np.arange(tq)[None, :], 0.0, NEG)
                      for r in range(tk // tq)]).astype(np.float32)
    k_spec = lambda n: pl.BlockSpec((None, None, n, d), lambda b, gi, i: (b, gi, 0, 0))
    vt_spec = lambda n: pl.BlockSpec((None, None, d, n), lambda b, gi, i: (b, gi, 0, 0))
    return pl.pallas_call(
        _nsa_kernel,
        grid=(batch, g, nq),
        in_specs=[pl.BlockSpec((tq, NSA_HPG * d), lambda b, gi, i: (b * nq + i, gi)),
                  k_spec(nc), vt_spec(nc), k_spec(seq), vt_spec(seq), k_spec(seq), vt_spec(seq),
                  pl.BlockSpec((tq, LANE), lambda b, gi, i: (b * nq + i, gi)),
                  pl.BlockSpec((NSA_SEL_PAD, nc), lambda b, gi, i: (0, 0)),
                  pl.BlockSpec(wbias.shape, lambda b, gi, i: (0, 0, 0)),
                  pl.BlockSpec(cbias.shape, lambda b, gi, i: (0, 0, 0))],
        out_specs=pl.BlockSpec((tq, NSA_HPG * d), lambda b, gi, i: (b * nq + i, gi)),
        out_shape=jax.ShapeDtypeStruct((batch * seq, d_model), BF16),
        scratch_shapes=[pltpu.VMEM((NSA_SEL_PAD, NSA_HPG * tq), F32)],
        compiler_params=_cparams(("parallel", "parallel", "arbitrary")),
        name="nsa_attention",
    )(q_all, kc, vct, ks, vst, kw, vwt, gates, jnp.asarray(c2s.T, dtype=BF16), jnp.asarray(wbias),
      jnp.asarray(cbias))


def _nsa_mixer(x2, g_norm, w_in, pos_k, pos_v, w_k, w_v, w_out, batch, seq):
    d_model, kv = N_HEADS * HEAD_DIM, NSA_KV
    sec = lambda i: w_in[:, d_model + i * kv: d_model + (i + 1) * kv]
    w_main = jnp.concatenate([w_in[:, :d_model], sec(2), sec(4), sec(0), sec(1), sec(3), sec(5)], axis=1)
    proj = _norm_matmul(x2, g_norm, w_main, seq, tn=512, n_rope=3)
    w_gate = w_in[:, d_model + 6 * kv:].reshape(d_model, NSA_GROUPS, 3 * NSA_HPG)
    w_gate = jnp.pad(w_gate, ((0, 0), (0, 0), (0, LANE - 3 * NSA_HPG))).reshape(d_model, NSA_GROUPS * LANE)
    gates = _norm_matmul(x2, g_norm, w_gate, seq, tn=NSA_GROUPS * LANE, out_dtype=F32)

    col = lambda i: proj[:, d_model + i * kv: d_model + (i + 1) * kv]
    heads = lambda x: x.reshape(batch, seq, NSA_GROUPS, HEAD_DIM).transpose(0, 2, 1, 3)
    heads_t = lambda x: x.reshape(batch, seq, NSA_GROUPS, HEAD_DIM).transpose(0, 2, 3, 1)
    kc, vc = _nsa_compress(col(2), col(3), pos_k, pos_v, w_k, w_v, batch, seq)
    o = _nsa_attention(proj, kc, vc.transpose(0, 1, 3, 2), heads(col(0)), heads_t(col(4)), heads(col(1)),
                       heads_t(col(5)), gates, batch, seq)
    return _proj_resid(o, w_out, x2)


PEER_SEL_TT = 256
def _batcher_network(n):
    def merge(lo, hi, r):
        step = r * 2
        if step < hi - lo:
            yield from merge(lo, hi, step)
            yield from merge(lo + r, hi, step)
            yield from ((i, i + r) for i in range(lo + r, hi - r, step))
        else:
            yield (lo, lo + r)

    def sort(lo, hi):
        if hi - lo >= 1:
            mid = lo + (hi - lo) // 2
            yield from sort(lo, mid)
            yield from sort(mid + 1, hi)
            yield from merge(lo, hi, 1)

    return list(sort(0, n - 1))


SORT16_NETWORK = _batcher_network(PEER_KEYS // 8)

PEER_PAIRS = [(i, j) for i in range(PEER_TOP_K) for j in range(PEER_TOP_K) if (i + 1) * (j + 1) <= PEER_TOP_K]


def _peer_select_kernel(q_ref, k1_ref, k2_ref, cnt_ref, e1_ref, rank_ref, e2_ref):
    q = q_ref[...]
    q_hi, q_lo = _split_bf16(q)
    nh, nr = PEER_HEADS, PEER_TOP_K

    def scores(k_ref, off):
        k_hi, k_lo = _split_bf16(k_ref[...])
        out = []
        for h in range(nh):
            lanes = slice(off + h * PEER_HALF, off + (h + 1) * PEER_HALF)
            out.append(_dot_nt(k_hi, q_hi[:, lanes]) + _dot_nt(k_hi, q_lo[:, lanes])
                       + _dot_nt(k_lo, q_hi[:, lanes]))
        return out

    def top_values(s):
        slabs = [s[8 * k:8 * (k + 1), :] for k in range(PEER_KEYS // 8)]
        for a, b in SORT16_NETWORK:
            slabs[a], slabs[b] = jnp.maximum(slabs[a], slabs[b]), jnp.minimum(slabs[a], slabs[b])
        vals = []
        for r in range(nr):
            m = jnp.max(slabs[0], axis=0, keepdims=True)
            vals.append(m)
            hit = slabs[0] == m
            for k in range(nr - 1 - r):
                slabs[k] = jnp.where(hit, slabs[k + 1], slabs[k])
        return vals

    s1 = scores(k1_ref, 0)
    s2 = scores(k2_ref, nh * PEER_HALF)
    t1 = [top_values(s) for s in s1]
    t2 = [top_values(s) for s in s2]
    r1 = [jnp.concatenate([t1[h][r] for h in range(nh)], axis=0) for r in range(nr)]
    r2 = [jnp.concatenate([t2[h][r] for h in range(nh)], axis=0) for r in range(nr)]
    cands = [r1[i] + r2[j] for (i, j) in PEER_PAIRS]
    work, tops = list(cands), []
    for _ in range(nr):
        m = functools.reduce(jnp.maximum, work)
        tops.append(m)
        work = [jnp.where(c == m, LOW, c) for c in work]
    thr = tops[PEER_TOP_K - 1]
    cmax = tops[0]
    z = functools.reduce(lambda a, b: a + b, [jnp.where(c >= thr, jnp.exp(c - cmax), 0.0) for c in cands])
    inv_z = 1.0 / z
    count_of_rank = []
    for a in range(nr):
        c = jnp.zeros_like(thr)
        for b in range(nr):
            c = jnp.where(r1[a] + r2[b] >= thr, float(b + 1), c)
        count_of_rank.append(c)
    for h in range(nh):
        cnt = jnp.zeros_like(s1[h])
        rank = jnp.zeros_like(s2[h])
        for r in range(nr):
            cnt = jnp.where(s1[h] == t1[h][r], count_of_rank[r][h:h + 1, :], cnt)
            rank = jnp.where(t2[h][r] > s2[h], float(r + 1), rank)
        cnt_ref[h] = cnt.astype(BF16)
        e1_ref[h] = jnp.exp(s1[h] - r1[0][h:h + 1, :]).astype(BF16)
        rank_ref[h] = rank.astype(BF16)
        e2_ref[h] = (jnp.exp(s2[h] - r2[0][h:h + 1, :]) * inv_z[h:h + 1, :]).astype(BF16)


def _peer_select(q, sub_keys):
    t = q.shape[0]
    tt = min(PEER_SEL_TT, t)
    out_spec = pl.BlockSpec((PEER_HEADS, PEER_KEYS, tt), lambda i: (0, 0, i))
    return pl.pallas_call(
        _peer_select_kernel,
        grid=(t // tt,),
        in_specs=[pl.BlockSpec((tt, q.shape[1]), lambda i: (i, 0)),
                  pl.BlockSpec((PEER_KEYS, PEER_HALF), lambda i: (0, 0)),
                  pl.BlockSpec((PEER_KEYS, PEER_HALF), lambda i: (0, 0))],
        out_specs=[out_spec] * 4,
        out_shape=[jax.ShapeDtypeStruct((PEER_HEADS, PEER_KEYS, t), BF16)] * 4,
        compiler_params=_cparams(("parallel",)),
        name="peer_select",
    )(q, sub_keys[0].astype(F32), sub_keys[1].astype(F32))


PEER_TT = 512
PEER_NE = 2048
PEER_SUB = 512


SQRT_HALF = float(np.sqrt(0.5))


def _gelu_of_scaled(y):
    return y * (1.0 + lax.erf(y))


def _peer_main_kernel(x_ref, g_ref, u_ref, vt_ref, cnt_ref, e1_ref, rank_ref, e2_ref, o_ref, xnt_ref, acc_ref):
    e = pl.program_id(1)
    ne = u_ref.shape[0]
    tt = x_ref.shape[0]
    sub = PEER_SUB
    n_sub = ne // sub

    @pl.when(e == 0)
    def _():
        x = x_ref[...]
        y = x * lax.rsqrt(jnp.mean(x * x, axis=-1, keepdims=True) + NORM_EPS)
        xnt_ref[...] = (y * g_ref[...]).T.astype(BF16)
        acc_ref[...] = jnp.zeros_like(acc_ref)

    def hidden(c):
        return _dot(u_ref[c * sub:(c + 1) * sub, :], xnt_ref[...])

    def weighted(c, h):
        act = _gelu_of_scaled(h).astype(BF16)
        parts = []
        for k in range(sub // PEER_KEYS):
            i1 = c * (sub // PEER_KEYS) + k
            gsum = None
            for hd in range(PEER_HEADS):
                cnt = cnt_ref[hd, i1:i1 + 1, :]
                e1 = e1_ref[hd, i1:i1 + 1, :]
                term = jnp.where(rank_ref[hd] < cnt, e2_ref[hd], jnp.zeros((), BF16)) * e1
                gsum = term if gsum is None else gsum + term
            parts.append(gsum * act[k * PEER_KEYS:(k + 1) * PEER_KEYS, :])
        return jnp.concatenate(parts, axis=0) if len(parts) > 1 else parts[0]

    def values(c, w):
        return _dot(vt_ref[:, c * sub:(c + 1) * sub], w)

    hs = {0: hidden(0)}
    if n_sub > 1:
        hs[1] = hidden(1)
    total = None
    for c in range(n_sub):
        w = weighted(c, hs.pop(c))
        if c + 2 < n_sub:
            hs[c + 2] = hidden(c + 2)
        pv = values(c, w)
        total = pv if total is None else total + pv
    acc_ref[...] += total

    @pl.when(e == pl.num_programs(1) - 1)
    def _():
        o_ref[...] = x_ref[...] + acc_ref[...].T


def _peer_mixer(x2, g_norm, w_q, sub_keys, u, v, seq):
    t, d = x2.shape
    w_q2 = w_q.reshape(d, PEER_HEADS, 2, PEER_HALF).transpose(0, 2, 1, 3).reshape(d, 2 * PEER_HEADS * PEER_HALF)
    q = _norm_matmul3(x2, g_norm, w_q2)
    cnt, e1, rank, e2 = _peer_select(q, sub_keys)
    tt = min(PEER_TT, t)
    ne = PEER_NE
    n_exp = u.shape[0]
    tok_spec = pl.BlockSpec((PEER_HEADS, PEER_KEYS, tt), lambda i, e: (0, 0, i))
    row_spec = pl.BlockSpec((PEER_HEADS, ne // PEER_KEYS, tt), lambda i, e: (0, e, i))
    return pl.pallas_call(
        _peer_main_kernel,
        grid=(t // tt, n_exp // ne),
        in_specs=[pl.BlockSpec((tt, d), lambda i, e: (i, 0)),
                  pl.BlockSpec((1, d), lambda i, e: (0, 0)),
                  pl.BlockSpec((ne, d), lambda i, e: (e, 0)),
                  pl.BlockSpec((d, ne), lambda i, e: (0, e)),
                  row_spec, row_spec, tok_spec, tok_spec],
        out_specs=pl.BlockSpec((tt, d), lambda i, e: (i, 0)),
        out_shape=jax.ShapeDtypeStruct((t, d), F32),
        scratch_shapes=[pltpu.VMEM((d, tt), BF16), pltpu.VMEM((d, tt), F32)],
        compiler_params=_cparams(("parallel", "arbitrary")),
        name="peer_main",
    )(x2, (g_norm.reshape(1, d).astype(F32) * SQRT_HALF), u.astype(BF16), (v.T * SQRT_HALF).astype(BF16),
      cnt, e1, rank, e2)


def _sb_mixer(x2, g_norm, w_in, w_out, batch, seq):
    qkv = _norm_matmul(x2, g_norm, w_in, seq, tn=1024)
    return _proj_resid(_sb_attention(qkv, batch, seq), w_out, x2)


def _moba_mixer(x2, g_norm, w_in, w_out, batch, seq):
    d_model = N_HEADS * HEAD_DIM
    qkv = _norm_matmul(x2, g_norm, w_in, seq, tn=1024, n_rope=2 * d_model // 1024)
    return _proj_resid(_moba_attention(qkv, batch, seq), w_out, x2)


def kernel(x, l0_norm_mix, l0_sb_w_in, l0_sb_w_out, l0_norm_ffn, l0_peer_w_q, l0_peer_sub_keys, l0_peer_u, l0_peer_v, l1_norm_mix, l1_nsa_w_in, l1_nsa_cmp_pos_k, l1_nsa_cmp_pos_v, l1_nsa_cmp_w_k, l1_nsa_cmp_w_v, l1_nsa_w_out, l1_norm_ffn, l1_peer_w_q, l1_peer_sub_keys, l1_peer_u, l1_peer_v, l2_norm_mix, l2_moba_w_in, l2_moba_w_out, l2_norm_ffn, l2_peer_w_q, l2_peer_sub_keys, l2_peer_u, l2_peer_v, l3_norm_mix, l3_sb_w_in, l3_sb_w_out, l3_norm_ffn, l3_peer_w_q, l3_peer_sub_keys, l3_peer_u, l3_peer_v, final_norm):
    batch, seq, d = x.shape
    x2 = x.reshape(batch * seq, d)
    x2 = _sb_mixer(x2, l0_norm_mix, l0_sb_w_in, l0_sb_w_out, batch, seq)
    x2 = _peer_mixer(x2, l0_norm_ffn, l0_peer_w_q, l0_peer_sub_keys, l0_peer_u, l0_peer_v, seq)
    x2 = _nsa_mixer(x2, l1_norm_mix, l1_nsa_w_in, l1_nsa_cmp_pos_k, l1_nsa_cmp_pos_v, l1_nsa_cmp_w_k,
                    l1_nsa_cmp_w_v, l1_nsa_w_out, batch, seq)
    x2 = _peer_mixer(x2, l1_norm_ffn, l1_peer_w_q, l1_peer_sub_keys, l1_peer_u, l1_peer_v, seq)
    x2 = _moba_mixer(x2, l2_norm_mix, l2_moba_w_in, l2_moba_w_out, batch, seq)
    x2 = _peer_mixer(x2, l2_norm_ffn, l2_peer_w_q, l2_peer_sub_keys, l2_peer_u, l2_peer_v, seq)
    x2 = _sb_mixer(x2, l3_norm_mix, l3_sb_w_in, l3_sb_w_out, batch, seq)
    x2 = _peer_mixer(x2, l3_norm_ffn, l3_peer_w_q, l3_peer_sub_keys, l3_peer_u, l3_peer_v, seq)
    return _rmsnorm(x2, final_norm).reshape(batch, seq, d)
```

```python
import functools

import numpy as np
import jax
import jax.numpy as jnp
from jax import lax
from jax.experimental import pallas as pl
from jax.experimental.pallas import tpu as pltpu

F32 = jnp.float32
BF16 = jnp.bfloat16

N_HEADS = 16
HEAD_DIM = 64
ROPE_DIM = 16
ROPE_HALF = ROPE_DIM // 2
ROPE_THETA = 500000.0
NORM_EPS = 1e-6
NEG = -1e30
LOW = -3e38
ATT_SCALE = HEAD_DIM ** -0.5
QK_SCALE = float(np.sqrt(ATT_SCALE * np.log2(np.e)))

NSA_GROUPS = 4
NSA_HPG = N_HEADS // NSA_GROUPS
NSA_KV = NSA_GROUPS * HEAD_DIM
NSA_CMP_LEN = 32
NSA_CMP_STRIDE = 16
NSA_SEL_LEN = 64
NSA_TOP_N = 16
NSA_WINDOW = 512
NSA_BONUS = 1e3
NSA_SEL_PAD = 128

MOBA_BLOCK = 256
MOBA_TOP_K = 3
MOBA_BLK_ALIGN = 16

PEER_HEADS = 8
PEER_KEYS = 128
PEER_TOP_K = 16
PEER_HALF = 64

SB_EXIT = -110.0

LANE = 128
BF16_ROWS = 16
VMEM_LIMIT = 56 << 20


def _cparams(sem, vmem=VMEM_LIMIT):
    return pltpu.CompilerParams(dimension_semantics=sem, vmem_limit_bytes=vmem)


def _dot(a, b):
    return jnp.dot(a, b, preferred_element_type=F32)


def _dot_nt(a, b):
    return lax.dot_general(a, b, (((1,), (1,)), ((), ())), preferred_element_type=F32)


def _split_bf16(x):
    hi = x.astype(BF16)
    lo = (x - hi.astype(F32)).astype(BF16)
    return hi, lo


def _norm_matmul_kernel(*refs, n_rope, tn):
    if n_rope:
        x_ref, g_ref, w_ref, c_ref, sa_ref, sb_ref, o_ref, xn_ref = refs
    else:
        x_ref, g_ref, w_ref, o_ref, xn_ref = refs
    j = pl.program_id(1)

    @pl.when(j == 0)
    def _():
        x = x_ref[...]
        y = x * lax.rsqrt(jnp.mean(x * x, axis=-1, keepdims=True) + NORM_EPS)
        xn_ref[...] = (y * g_ref[...]).astype(BF16)

    acc = _dot(xn_ref[...], w_ref[...])
    if n_rope:
        @pl.when(j < n_rope)
        def _():
            wide = lambda t_ref: jnp.concatenate([t_ref[...]] * (tn // LANE), axis=1)
            r = (acc * wide(c_ref) + pltpu.roll(acc, ROPE_HALF, 1) * wide(sb_ref)
                 + pltpu.roll(acc, tn - ROPE_HALF, 1) * wide(sa_ref))
            o_ref[...] = r.astype(o_ref.dtype)

        @pl.when(j >= n_rope)
        def _():
            o_ref[...] = acc.astype(o_ref.dtype)
    else:
        o_ref[...] = acc.astype(o_ref.dtype)


def _rope_tables(pos, width):
    inv = ROPE_THETA ** (-jnp.arange(0, ROPE_DIM, 2, dtype=F32) / ROPE_DIM)
    ang = pos.astype(F32)[:, None] * inv[None, :]
    cos, sin = jnp.cos(ang) * QK_SCALE, jnp.sin(ang) * QK_SCALE
    n = pos.shape[0]
    rest = HEAD_DIM - ROPE_DIM
    c = jnp.concatenate([cos, cos, jnp.full((n, rest), QK_SCALE, F32)], axis=1)
    sa = jnp.concatenate([-sin, jnp.zeros((n, ROPE_HALF + rest), F32)], axis=1)
    sb = jnp.concatenate([jnp.zeros((n, ROPE_HALF), F32), sin, jnp.zeros((n, rest), F32)], axis=1)
    reps = width // HEAD_DIM
    return jnp.tile(c, (1, reps)), jnp.tile(sa, (1, reps)), jnp.tile(sb, (1, reps))


def _norm_matmul(x2, g, w, seq, *, tn, n_rope=0, out_dtype=BF16, tt=1024):
    t, d = x2.shape
    n = w.shape[1]
    tt = min(tt, seq)
    grid = (t // tt, n // tn)
    in_specs = [pl.BlockSpec((tt, d), lambda i, j: (i, 0)),
                pl.BlockSpec((1, d), lambda i, j: (0, 0)),
                pl.BlockSpec((d, tn), lambda i, j: (0, j))]
    args = [x2, g.reshape(1, d).astype(F32), w.astype(BF16)]
    if n_rope:
        nper = seq // tt
        tabs = _rope_tables(jnp.arange(seq), LANE)
        in_specs += [pl.BlockSpec((tt, LANE), lambda i, j: (i % nper, 0))] * 3
        args += list(tabs)
    return pl.pallas_call(
        functools.partial(_norm_matmul_kernel, n_rope=n_rope, tn=tn),
        grid=grid, in_specs=in_specs,
        out_specs=pl.BlockSpec((tt, tn), lambda i, j: (i, j)),
        out_shape=jax.ShapeDtypeStruct((t, n), out_dtype),
        scratch_shapes=[pltpu.VMEM((tt, d), BF16)],
        compiler_params=_cparams(("parallel", "arbitrary")),
        name="norm_matmul",
    )(*args)


def _norm_matmul3_kernel(x_ref, g_ref, wh_ref, wl_ref, o_ref, xh_ref, xl_ref):
    @pl.when(pl.program_id(1) == 0)
    def _():
        x = x_ref[...]
        y = x * lax.rsqrt(jnp.mean(x * x, axis=-1, keepdims=True) + NORM_EPS) * g_ref[...]
        xh_ref[...], xl_ref[...] = _split_bf16(y)

    o_ref[...] = (_dot(xh_ref[...], wh_ref[...]) + _dot(xl_ref[...], wh_ref[...])
                  + _dot(xh_ref[...], wl_ref[...]))


def _norm_matmul3(x2, g, w, *, tn=512, tt=1024):
    t, d = x2.shape
    n = w.shape[1]
    tt = min(tt, t)
    w_hi, w_lo = _split_bf16(w.astype(F32))
    w_spec = pl.BlockSpec((d, tn), lambda i, j: (0, j))
    return pl.pallas_call(
        _norm_matmul3_kernel,
        grid=(t // tt, n // tn),
        in_specs=[pl.BlockSpec((tt, d), lambda i, j: (i, 0)),
                  pl.BlockSpec((1, d), lambda i, j: (0, 0)), w_spec, w_spec],
        out_specs=pl.BlockSpec((tt, tn), lambda i, j: (i, j)),
        out_shape=jax.ShapeDtypeStruct((t, n), F32),
        scratch_shapes=[pltpu.VMEM((tt, d), BF16), pltpu.VMEM((tt, d), BF16)],
        compiler_params=_cparams(("parallel", "arbitrary")),
        name="norm_matmul3",
    )(x2, g.reshape(1, d).astype(F32), w_hi, w_lo)


def _proj_resid_kernel(a_ref, w_ref, r_ref, o_ref):
    o_ref[...] = r_ref[...] + _dot(a_ref[...], w_ref[...])


def _proj_resid(a, w, resid, *, tt=1024, tn=1024):
    t, d = a.shape
    n = w.shape[1]
    tt = min(tt, t)
    return pl.pallas_call(
        _proj_resid_kernel,
        grid=(t // tt, n // tn),
        in_specs=[pl.BlockSpec((tt, d), lambda i, j: (i, 0)),
                  pl.BlockSpec((d, tn), lambda i, j: (0, j)),
                  pl.BlockSpec((tt, tn), lambda i, j: (i, j))],
        out_specs=pl.BlockSpec((tt, tn), lambda i, j: (i, j)),
        out_shape=jax.ShapeDtypeStruct((t, n), F32),
        compiler_params=_cparams(("parallel", "parallel")),
        name="proj_resid",
    )(a, w.astype(BF16), resid)


def _rmsnorm_kernel(x_ref, g_ref, o_ref):
    x = x_ref[...]
    y = x * lax.rsqrt(jnp.mean(x * x, axis=-1, keepdims=True) + NORM_EPS)
    o_ref[...] = y * g_ref[...]


def _rmsnorm(x2, g, *, tt=512):
    t, d = x2.shape
    tt = min(tt, t)
    return pl.pallas_call(
        _rmsnorm_kernel,
        grid=(t // tt,),
        in_specs=[pl.BlockSpec((tt, d), lambda i: (i, 0)), pl.BlockSpec((1, d), lambda i: (0, 0))],
        out_specs=pl.BlockSpec((tt, d), lambda i: (i, 0)),
        out_shape=jax.ShapeDtypeStruct((t, d), F32),
        compiler_params=_cparams(("parallel",)),
        name="final_rmsnorm",
    )(x2, g.reshape(1, d).astype(F32))


SB_TILE = 256
SB_FIRST = 2 * SB_TILE
SB_HEADS = 4


def _sb_kernel(q_ref, k_ref, v_ref, tri_ref, o_ref):
    tile = SB_TILE
    wide = SB_FIRST
    qi = pl.program_id(2)
    t0 = qi * tile

    nh = SB_HEADS
    head_lanes = [slice(hh * HEAD_DIM, (hh + 1) * HEAD_DIM) for hh in range(nh)]
    qs = [q_ref[:, lanes] * jnp.asarray(ATT_SCALE, BF16) for lanes in head_lanes]

    def step(start, width, r_sums, accs, past):
        tri = tri_ref[0:width, 0:width]
        zs = [_dot_nt(qs[hh], k_ref[pl.ds(start, width), lanes]) for hh, lanes in enumerate(head_lanes)]
        sps, lbs, splits = [], [], []
        for z in zs:
            sp = jnp.maximum(z, 0.0) + jnp.log(1.0 + jnp.exp(-jnp.abs(z)))
            lb = -sp if past is None else jnp.where(past, -sp, 0.0)
            sps.append(sp)
            lbs.append(lb)
            splits.append(_split_bf16(lb))
        suffixes = [_dot(hi, tri) + _dot(lo, tri) for hi, lo in splits]
        weights = []
        for hh in range(nh):
            a = jnp.exp((zs[hh] - sps[hh]) + suffixes[hh] + r_sums[hh])
            if past is not None:
                a = jnp.where(past, a, 0.0)
            weights.append(a.astype(BF16))
        accs = tuple(accs[hh] + _dot(weights[hh], v_ref[pl.ds(start, width), lanes])
                     for hh, lanes in enumerate(head_lanes))
        r_sums = tuple(r_sums[hh] + jnp.sum(lbs[hh], axis=-1, keepdims=True) for hh in range(nh))
        return r_sums, accs

    start0 = pl.multiple_of(jnp.maximum(t0 + tile - wide, 0), tile)
    kpos = start0 + lax.broadcasted_iota(jnp.int32, (tile, wide), 1)
    tpos = t0 + lax.broadcasted_iota(jnp.int32, (tile, wide), 0)
    zero_r = tuple(jnp.zeros((tile, 1), F32) for _ in range(nh))
    zero_acc = tuple(jnp.zeros((tile, HEAD_DIM), F32) for _ in range(nh))
    r1, acc1 = step(start0, wide, zero_r, zero_acc, kpos < tpos)

    def older(j, r_sums):
        alive = jnp.max(functools.reduce(jnp.maximum, r_sums)) > SB_EXIT
        return jnp.where(alive, j - 1, -1)

    def cond(c):
        return c[0] >= 0

    def body(c):
        j, r_sums, accs = c
        r_sums, accs = step(pl.multiple_of(j * tile, tile), tile, r_sums, accs, None)
        return older(j, r_sums), r_sums, accs

    _, _, accs = lax.while_loop(cond, body, (older(start0 // tile, r1), r1, acc1))
    for hh, lanes in enumerate(head_lanes):
        o_ref[:, lanes] = accs[hh].astype(o_ref.dtype)


def _sb_attention(qkv, batch, seq):
    tile = SB_TILE
    nq = seq // tile
    d_model = N_HEADS * HEAD_DIM
    width = SB_HEADS * HEAD_DIM
    ncol = d_model // width
    ii = np.arange(SB_FIRST)
    tri = jnp.asarray(ii[:, None] > ii[None, :], dtype=BF16)
    return pl.pallas_call(
        _sb_kernel,
        grid=(batch, ncol, nq),
        in_specs=[pl.BlockSpec((tile, width), lambda b, h, i: (b * nq + i, h)),
                  pl.BlockSpec((seq, width), lambda b, h, i: (b, ncol + h)),
                  pl.BlockSpec((seq, width), lambda b, h, i: (b, 2 * ncol + h)),
                  pl.BlockSpec((SB_FIRST, SB_FIRST), lambda b, h, i: (0, 0))],
        out_specs=pl.BlockSpec((tile, width), lambda b, h, i: (b * nq + i, h)),
        out_shape=jax.ShapeDtypeStruct((batch * seq, d_model), BF16),
        compiler_params=_cparams(("parallel", "parallel", "arbitrary")),
        name="sb_attention",
    )(qkv, qkv, qkv, tri)


def _flash_t_first(q, k, vt, bias_t):
    s = _dot_nt(k, q) + bias_t
    m = jnp.max(s, axis=0, keepdims=True)
    p = jnp.exp2(s - m)
    return m, jnp.sum(p, axis=0, keepdims=True), _dot(vt, p.astype(BF16))


def _topk_mask_rows(vals, k):
    rowi = lax.broadcasted_iota(jnp.int32, vals.shape, 0)
    work = vals
    sel = jnp.zeros(vals.shape, F32)
    for _ in range(k):
        m = jnp.max(work, axis=0, keepdims=True)
        first = jnp.min(jnp.where(work == m, rowi, vals.shape[0]), axis=0, keepdims=True)
        hit = rowi == first
        sel = jnp.where(hit, 1.0, sel)
        work = jnp.where(hit, LOW, work)
    return sel


MOBA_WALK = 4
MOBA_HEADS = 8


def _moba_kernel(q_ref, k_ref, vt_ref, avg_ref, o_ref, kmean_ref, bias_ref):
    blk = MOBA_BLOCK
    wide = MOBA_WALK * blk
    qi = pl.program_id(2)

    @pl.when(qi == 0)
    def _():
        kmean_ref[...] = _dot(avg_ref[...], k_ref[...])

    key_i = lax.broadcasted_iota(jnp.int32, (blk, blk), 0)
    qry_i = lax.broadcasted_iota(jnp.int32, (blk, blk), 1)
    causal_bias = jnp.where(key_i <= qry_i, 0.0, NEG)
    blk_i = lax.broadcasted_iota(jnp.int32, (kmean_ref.shape[0], blk), 0)
    start = pl.multiple_of(qi * blk, blk)
    head_rows = [slice(hh * HEAD_DIM, (hh + 1) * HEAD_DIM) for hh in range(MOBA_HEADS)]

    qs, states = [], []
    for hh, lanes in enumerate(head_rows):
        q_raw = q_ref[:, lanes]
        km_hi, km_lo = _split_bf16(kmean_ref[:, lanes])
        gate = _dot_nt(km_hi, q_raw) + _dot_nt(km_lo, q_raw)
        gate = jnp.where(blk_i < qi, gate, NEG)
        picked = (_topk_mask_rows(gate, MOBA_TOP_K) > 0.5) & (blk_i < qi)
        bias_ref[hh] = jnp.where(picked, 0.0, NEG)
        qs.append(q_raw)
        states.append(_flash_t_first(qs[-1], k_ref[pl.ds(start, blk), lanes],
                                     vt_ref[lanes, pl.ds(start, blk)], causal_bias))

    def body(p, sts):
        s0 = pl.multiple_of(p * wide, wide)
        scores = [_dot_nt(k_ref[pl.ds(s0, wide), lanes], qs[hh]) for hh, lanes in enumerate(head_rows)]
        probs, nxt = [], []
        for hh in range(MOBA_HEADS):
            m, l, acc_t = sts[hh]
            rows = [bias_ref[hh, pl.ds(p * MOBA_WALK + w, 1), :] for w in range(MOBA_WALK)]
            parts = [scores[hh][w * blk:(w + 1) * blk, :] for w in range(MOBA_WALK)]
            m_new = functools.reduce(
                jnp.maximum, [jnp.max(x, axis=0, keepdims=True) + r for x, r in zip(parts, rows)], m)
            pr = jnp.concatenate([jnp.exp2(x - (m_new - r)) for x, r in zip(parts, rows)], axis=0)
            alpha = jnp.exp2(m - m_new)
            probs.append(pr.astype(BF16))
            nxt.append((m_new, alpha * l + jnp.sum(pr, axis=0, keepdims=True), alpha * acc_t))
        return tuple((m_new, l, acc_s + _dot(vt_ref[lanes, pl.ds(s0, wide)], probs[hh]))
                     for hh, (lanes, (m_new, l, acc_s)) in enumerate(zip(head_rows, nxt)))

    states = lax.fori_loop(0, (qi + MOBA_WALK - 1) // MOBA_WALK, body, tuple(states))
    for hh, lanes in enumerate(head_rows):
        _, l, acc_t = states[hh]
        o_ref[:, lanes] = (acc_t / l).T.astype(o_ref.dtype)


def _moba_attention(qkv, batch, seq):
    blk = MOBA_BLOCK
    wide = MOBA_WALK * blk
    nq = seq // blk
    d_model = N_HEADS * HEAD_DIM
    width = MOBA_HEADS * HEAD_DIM
    ncol = d_model // width
    nb = -(-nq // MOBA_BLK_ALIGN) * MOBA_BLK_ALIGN
    avg = np.zeros((nb, seq), np.float32)
    for n in range(nq):
        avg[n, n * blk:(n + 1) * blk] = 1.0 / blk
    avg = jnp.asarray(avg, dtype=BF16)
    v_t = qkv[:, 2 * d_model:].reshape(batch, seq, d_model).transpose(0, 2, 1)
    return pl.pallas_call(
        _moba_kernel,
        grid=(batch, ncol, nq),
        in_specs=[pl.BlockSpec((blk, width), lambda b, h, i: (b * nq + i, h)),
                  pl.BlockSpec((seq, width), lambda b, h, i: (b, ncol + h)),
                  pl.BlockSpec((None, width, seq), lambda b, h, i: (b, h, 0)),
                  pl.BlockSpec((nb, seq), lambda b, h, i: (0, 0))],
        out_specs=pl.BlockSpec((blk, width), lambda b, h, i: (b * nq + i, h)),
        out_shape=jax.ShapeDtypeStruct((batch * seq, d_model), BF16),
        scratch_shapes=[pltpu.VMEM((nb, width), F32),
                        pltpu.VMEM((MOBA_HEADS, nb, blk), F32)],
        compiler_params=_cparams(("parallel", "parallel", "arbitrary")),
        name="moba_attention",
    )(qkv, qkv, v_t, avg)


def _nsa_compress_kernel(kr_ref, vr_ref, pk_ref, pv_ref, wk_ref, wv_ref, c_ref, sa_ref, sb_ref, kc_ref, vc_ref):
    nc = kr_ref.shape[0]
    rowi = lax.broadcasted_iota(jnp.int32, (nc, 1), 0)

    def windows(x_ref, pos_ref, w_ref):
        x = x_ref[...].astype(F32)
        lo = _dot((x + pos_ref[0:1, :]).astype(BF16), w_ref[0])
        hi = _dot((x + pos_ref[1:2, :]).astype(BF16), w_ref[1])
        hi_next = jnp.where(rowi < nc - 1, pltpu.roll(hi, nc - 1, 0), 0.0)
        return lo + hi_next

    kk = windows(kr_ref, pk_ref, wk_ref)
    d = HEAD_DIM
    kc = kk[:, 0:d] * c_ref[...] + kk[:, d:2 * d] * sb_ref[...] + kk[:, 2 * d:3 * d] * sa_ref[...]
    kc_ref[...] = kc.astype(kc_ref.dtype)
    vv = windows(vr_ref, pv_ref, wv_ref)
    vc_ref[...] = vv[:, 0:d].astype(vc_ref.dtype)


def _nsa_compress(k_cmp, v_cmp, pos_k, pos_v, w_k, w_v, batch, seq):
    g, d, st = NSA_GROUPS, HEAD_DIM, NSA_CMP_STRIDE
    nc = seq // st

    def chunks(x):
        return x.reshape(batch, nc, st, g, d).transpose(0, 3, 1, 2, 4).reshape(batch, g, nc, st * d)

    def weights(w):
        w3 = w.reshape(NSA_CMP_LEN, d, d)
        cat = jnp.concatenate([w3, jnp.roll(w3, ROPE_HALF, axis=2), jnp.roll(w3, -ROPE_HALF, axis=2)], axis=2)
        return cat.reshape(2, st * d, 3 * d).astype(BF16)

    def positions(p):
        return p.reshape(2, st * d).astype(F32)

    cmp_end = jnp.arange(nc) * st + NSA_CMP_LEN - 1
    tabs = _rope_tables(cmp_end, d)
    blk4 = pl.BlockSpec((None, None, nc, st * d), lambda b, gi: (b, gi, 0, 0))
    out4 = pl.BlockSpec((None, None, nc, d), lambda b, gi: (b, gi, 0, 0))
    const2 = lambda shape: pl.BlockSpec(shape, lambda b, gi: (0,) * len(shape))
    return pl.pallas_call(
        _nsa_compress_kernel,
        grid=(batch, g),
        in_specs=[blk4, blk4, const2((2, st * d)), const2((2, st * d)),
                  const2((2, st * d, 3 * d)), const2((2, st * d, 3 * d)),
                  const2((nc, d)), const2((nc, d)), const2((nc, d))],
        out_specs=[out4, out4],
        out_shape=[jax.ShapeDtypeStruct((batch, g, nc, d), BF16)] * 2,
        compiler_params=_cparams(("parallel", "parallel")),
        name="nsa_compress",
    )(chunks(k_cmp), chunks(v_cmp), positions(pos_k), positions(pos_v), weights(w_k), weights(w_v), *tabs)


NSA_TQ = 256
NSA_TK = 1024
NSA_CHAINS = 4


def _nsa_kernel(q_ref, kc_ref, vct_ref, ks_ref, vst_ref, kw_ref, vwt_ref, gate_ref, c2st_ref, wbias_ref, o_ref,
                bias_ref):
    tq, tk, d, nh = NSA_TQ, NSA_TK, HEAD_DIM, NSA_HPG
    nql = nh * tq
    half = nql // NSA_CHAINS
    qi = pl.program_id(2)
    t0 = qi * tq
    nc = kc_ref.shape[0]

    def lanes4(x):
        return jnp.concatenate([x] * nh, axis=1)

    tpos1 = t0 + lax.broadcasted_iota(jnp.int32, (1, tq), 1)
    tpos = lanes4(tpos1)
    gates_t = jax.nn.sigmoid(gate_ref[...]).T

    def gate(branch):
        return jnp.concatenate([gates_t[3 * h + branch:3 * h + branch + 1, :] for h in range(nh)], axis=1)

    q = jnp.concatenate([q_ref[:, h * d:(h + 1) * d] for h in range(nh)], axis=0)

    kd = t0 // tk
    dstart = pl.multiple_of(kd * tk, tk)
    span = NSA_WINDOW + tq
    wstart = pl.multiple_of(jnp.maximum(t0 - NSA_WINDOW, 0), tq)
    s_cmp = _dot_nt(kc_ref[...], q)
    s_win = _dot_nt(kw_ref[pl.ds(wstart, span), :], q)
    s_diag = _dot_nt(ks_ref[pl.ds(dstart, tk), :], q)

    cmp_end = lax.broadcasted_iota(jnp.int32, (nc, nql), 0) * NSA_CMP_STRIDE + (NSA_CMP_LEN - 1)
    vis_c = cmp_end <= tpos
    s = jnp.where(vis_c, s_cmp, NEG)
    e = jnp.where(vis_c, jnp.exp2(s - jnp.max(s, axis=0, keepdims=True)), 0.0)
    l = jnp.sum(e, axis=0, keepdims=True)
    p = e * (1.0 / jnp.where(l > 0.0, l, 1.0))
    out_t = gate(0) * _dot(vct_ref[...], p.astype(BF16))
    psum = functools.reduce(lambda a, b: a + b, [p[:, h * tq:(h + 1) * tq] for h in range(nh)])
    p_hi, p_lo = _split_bf16(psum)
    imp = _dot(c2st_ref[...], p_hi) + _dot(c2st_ref[...], p_lo)
    blk = lax.broadcasted_iota(jnp.int32, (NSA_SEL_PAD, tq), 0)
    own = tpos1 // NSA_SEL_LEN
    forced = (blk == 0) | (blk == own) | (blk == own - 1)
    imp = jnp.where(blk <= own, imp + jnp.where(forced, NSA_BONUS, 0.0), NEG)
    picked = _topk_mask_rows(imp, NSA_TOP_N) > 0.5
    bias_ref[...] = lanes4(jnp.where(picked, 0.0, NEG))

    per_tile = tk // NSA_SEL_LEN

    def tile_bias(kb):
        return jnp.concatenate(
            [jnp.broadcast_to(bias_ref[pl.ds(kb * per_tile + w, 1), :], (NSA_SEL_LEN, nql))
             for w in range(per_tile)], axis=0)

    s = s_win + lanes4(wbias_ref[jnp.minimum(qi, wbias_ref.shape[0] - 1)])
    e = jnp.exp2(s - jnp.max(s, axis=0, keepdims=True))
    out_t = out_t + (gate(2) / jnp.sum(e, axis=0, keepdims=True)) * _dot(vwt_ref[:, pl.ds(wstart, span)],
                                                                         e.astype(BF16))

    kpos = dstart + lax.broadcasted_iota(jnp.int32, (tk, nql), 0)
    s = jnp.where(kpos <= tpos, s_diag + tile_bias(kd), NEG)
    m = jnp.max(s, axis=0, keepdims=True)
    e = jnp.exp2(s - m)
    l = jnp.sum(e, axis=0, keepdims=True)
    acc = _dot(vst_ref[:, pl.ds(dstart, tk)], e.astype(BF16))
    halves = [slice(i * half, (i + 1) * half) for i in range(NSA_CHAINS)]
    q_half = [q[hs] for hs in halves]
    states = tuple((m[:, hs], l[:, hs], acc[:, hs]) for hs in halves)

    def sel_body(kb, sts):
        start = pl.multiple_of(kb * tk, tk)
        k = ks_ref[pl.ds(start, tk), :]
        vt = vst_ref[:, pl.ds(start, tk)]
        rows = [bias_ref[pl.ds(kb * per_tile + w, 1), :] for w in range(per_tile)]
        scores = [_dot_nt(k, q_half[i]) for i in range(NSA_CHAINS)]
        probs, nxt = [], []
        for i, hs in enumerate(halves):
            m, l, acc = sts[i]
            parts = [scores[i][w * NSA_SEL_LEN:(w + 1) * NSA_SEL_LEN, :] for w in range(per_tile)]
            m_new = functools.reduce(
                jnp.maximum, [jnp.max(x, axis=0, keepdims=True) + r[:, hs] for x, r in zip(parts, rows)], m)
            pr = jnp.concatenate([jnp.exp2(x - (m_new - r[:, hs])) for x, r in zip(parts, rows)], axis=0)
            alpha = jnp.exp2(m - m_new)
            probs.append(pr.astype(BF16))
            nxt.append((m_new, alpha * l + jnp.sum(pr, axis=0, keepdims=True), alpha * acc))
        return tuple((m_new, l, acc + _dot(vt, probs[i])) for i, (m_new, l, acc) in enumerate(nxt))

    states = lax.fori_loop(0, kd, sel_body, states)
    sel_t = jnp.concatenate([st[2] / st[1] for st in states], axis=1)
    out_t = out_t + gate(1) * sel_t
    for h in range(nh):
        o_ref[:, h * d:(h + 1) * d] = out_t[:, h * tq:(h + 1) * tq].T.astype(o_ref.dtype)


def _nsa_attention(q_all, kc, vct, ks, vst, kw, vwt, gates, batch, seq):
    tq, tk, d, g = NSA_TQ, NSA_TK, HEAD_DIM, NSA_GROUPS
    nq = seq // tq
    nc = seq // NSA_CMP_STRIDE
    n_sel = seq // NSA_SEL_LEN
    d_model = N_HEADS * HEAD_DIM
    cs = np.arange(nc) * NSA_CMP_STRIDE
    ss = np.arange(n_sel) * NSA_SEL_LEN
    ov = np.clip(np.minimum(cs[:, None] + NSA_CMP_LEN, ss[None, :] + NSA_SEL_LEN)
                 - np.maximum(cs[:, None], ss[None, :]), 0, None) / NSA_CMP_LEN
    c2s = np.zeros((nc, NSA_SEL_PAD), np.float32)
    c2s[:, :n_sel] = ov
    c2s[nc - 1, :] = 0.0
    span = NSA_WINDOW + tq
    cases = []
    for qi in range(NSA_WINDOW // tq + 1):
        t_abs = qi * tq + np.arange(tq)[None, :]
        c_abs = max(qi * tq - NSA_WINDOW, 0) + np.arange(span)[:, None]
        cases.append(np.where((c_abs <= t_abs) & (c_abs > t_abs - NSA_WINDOW), 0.0, NEG))
    wbias = np.stack(cases).astype(np.float32)
    k_spec = lambda n: pl.BlockSpec((None, None, n, d), lambda b, gi, i: (b, gi, 0, 0))
    vt_spec = lambda n: pl.BlockSpec((None, None, d, n), lambda b, gi, i: (b, gi, 0, 0))
    return pl.pallas_call(
        _nsa_kernel,
        grid=(batch, g, nq),
        in_specs=[pl.BlockSpec((tq, NSA_HPG * d), lambda b, gi, i: (b * nq + i, gi)),
                  k_spec(nc), vt_spec(nc), k_spec(seq), vt_spec(seq), k_spec(seq), vt_spec(seq),
                  pl.BlockSpec((tq, LANE), lambda b, gi, i: (b * nq + i, gi)),
                  pl.BlockSpec((NSA_SEL_PAD, nc), lambda b, gi, i: (0, 0)),
                  pl.BlockSpec(wbias.shape, lambda b, gi, i: (0, 0, 0))],
        out_specs=pl.BlockSpec((tq, NSA_HPG * d), lambda b, gi, i: (b * nq + i, gi)),
        out_shape=jax.ShapeDtypeStruct((batch * seq, d_model), BF16),
        scratch_shapes=[pltpu.VMEM((NSA_SEL_PAD, NSA_HPG * tq), F32)],
        compiler_params=_cparams(("parallel", "parallel", "arbitrary")),
        name="nsa_attention",
    )(q_all, kc, vct, ks, vst, kw, vwt, gates, jnp.asarray(c2s.T, dtype=BF16), jnp.asarray(wbias))


def _nsa_mixer(x2, g_norm, w_in, pos_k, pos_v, w_k, w_v, w_out, batch, seq):
    d_model, kv = N_HEADS * HEAD_DIM, NSA_KV
    sec = lambda i: w_in[:, d_model + i * kv: d_model + (i + 1) * kv]
    w_main = jnp.concatenate([w_in[:, :d_model], sec(2), sec(4), sec(0), sec(1), sec(3), sec(5)], axis=1)
    proj = _norm_matmul(x2, g_norm, w_main, seq, tn=512, n_rope=3)
    w_gate = w_in[:, d_model + 6 * kv:].reshape(d_model, NSA_GROUPS, 3 * NSA_HPG)
    w_gate = jnp.pad(w_gate, ((0, 0), (0, 0), (0, LANE - 3 * NSA_HPG))).reshape(d_model, NSA_GROUPS * LANE)
    gates = _norm_matmul(x2, g_norm, w_gate, seq, tn=NSA_GROUPS * LANE, out_dtype=F32)

    col = lambda i: proj[:, d_model + i * kv: d_model + (i + 1) * kv]
    heads = lambda x: x.reshape(batch, seq, NSA_GROUPS, HEAD_DIM).transpose(0, 2, 1, 3)
    heads_t = lambda x: x.reshape(batch, seq, NSA_GROUPS, HEAD_DIM).transpose(0, 2, 3, 1)
    kc, vc = _nsa_compress(col(2), col(3), pos_k, pos_v, w_k, w_v, batch, seq)
    o = _nsa_attention(proj, kc, vc.transpose(0, 1, 3, 2), heads(col(0)), heads_t(col(4)), heads(col(1)),
                       heads_t(col(5)), gates, batch, seq)
    return _proj_resid(o, w_out, x2)


PEER_SEL_TT = 256
def _batcher_network(n):
    def merge(lo, hi, r):
        step = r * 2
        if step < hi - lo:
            yield from merge(lo, hi, step)
            yield from merge(lo + r, hi, step)
            yield from ((i, i + r) for i in range(lo + r, hi - r, step))
        else:
            yield (lo, lo + r)

    def sort(lo, hi):
        if hi - lo >= 1:
            mid = lo + (hi - lo) // 2
            yield from sort(lo, mid)
            yield from sort(mid + 1, hi)
            yield from merge(lo, hi, 1)

    return list(sort(0, n - 1))


SORT16_NETWORK = _batcher_network(PEER_KEYS // 8)

PEER_PAIRS = [(i, j) for i in range(PEER_TOP_K) for j in range(PEER_TOP_K) if (i + 1) * (j + 1) <= PEER_TOP_K]


def _peer_select_kernel(q_ref, k1_ref, k2_ref, cnt_ref, e1_ref, rank_ref, e2_ref):
    q = q_ref[...]
    q_hi, q_lo = _split_bf16(q)
    nh, nr = PEER_HEADS, PEER_TOP_K

    def scores(k_ref, off):
        k_hi, k_lo = _split_bf16(k_ref[...])
        out = []
        for h in range(nh):
            lanes = slice(off + h * PEER_HALF, off + (h + 1) * PEER_HALF)
            out.append(_dot_nt(k_hi, q_hi[:, lanes]) + _dot_nt(k_hi, q_lo[:, lanes])
                       + _dot_nt(k_lo, q_hi[:, lanes]))
        return out

    def top_values(s):
        slabs = [s[8 * k:8 * (k + 1), :] for k in range(PEER_KEYS // 8)]
        for a, b in SORT16_NETWORK:
            slabs[a], slabs[b] = jnp.maximum(slabs[a], slabs[b]), jnp.minimum(slabs[a], slabs[b])
        vals = []
        for r in range(nr):
            m = jnp.max(slabs[0], axis=0, keepdims=True)
            vals.append(m)
            hit = slabs[0] == m
            for k in range(nr - 1 - r):
                slabs[k] = jnp.where(hit, slabs[k + 1], slabs[k])
        return vals

    s1 = scores(k1_ref, 0)
    s2 = scores(k2_ref, nh * PEER_HALF)
    t1 = [top_values(s) for s in s1]
    t2 = [top_values(s) for s in s2]
    r1 = [jnp.concatenate([t1[h][r] for h in range(nh)], axis=0) for r in range(nr)]
    r2 = [jnp.concatenate([t2[h][r] for h in range(nh)], axis=0) for r in range(nr)]
    cands = [r1[i] + r2[j] for (i, j) in PEER_PAIRS]
    work, tops = list(cands), []
    for _ in range(nr):
        m = functools.reduce(jnp.maximum, work)
        tops.append(m)
        work = [jnp.where(c == m, LOW, c) for c in work]
    thr = tops[PEER_TOP_K - 1]
    cmax = tops[0]
    z = functools.reduce(lambda a, b: a + b, [jnp.where(c >= thr, jnp.exp(c - cmax), 0.0) for c in cands])
    inv_z = 1.0 / z
    count_of_rank = []
    for a in range(nr):
        c = jnp.zeros_like(thr)
        for b in range(nr):
            c = jnp.where(r1[a] + r2[b] >= thr, float(b + 1), c)
        count_of_rank.append(c)
    for h in range(nh):
        cnt = jnp.zeros_like(s1[h])
        rank = jnp.zeros_like(s2[h])
        for r in range(nr):
            cnt = jnp.where(s1[h] == t1[h][r], count_of_rank[r][h:h + 1, :], cnt)
            rank = jnp.where(t2[h][r] > s2[h], float(r + 1), rank)
        cnt_ref[h] = cnt
        e1_ref[h] = jnp.exp(s1[h] - r1[0][h:h + 1, :])
        rank_ref[h] = rank.astype(BF16)
        e2_ref[h] = (jnp.exp(s2[h] - r2[0][h:h + 1, :]) * inv_z[h:h + 1, :]).astype(BF16)


def _peer_select(q, sub_keys):
    t = q.shape[0]
    tt = min(PEER_SEL_TT, t)
    out_spec = pl.BlockSpec((PEER_HEADS, PEER_KEYS, tt), lambda i: (0, 0, i))
    return pl.pallas_call(
        _peer_select_kernel,
        grid=(t // tt,),
        in_specs=[pl.BlockSpec((tt, q.shape[1]), lambda i: (i, 0)),
                  pl.BlockSpec((PEER_KEYS, PEER_HALF), lambda i: (0, 0)),
                  pl.BlockSpec((PEER_KEYS, PEER_HALF), lambda i: (0, 0))],
        out_specs=[out_spec] * 4,
        out_shape=[jax.ShapeDtypeStruct((PEER_HEADS, PEER_KEYS, t), dt) for dt in (F32, F32, BF16, BF16)],
        compiler_params=_cparams(("parallel",)),
        name="peer_select",
    )(q, sub_keys[0].astype(F32), sub_keys[1].astype(F32))


PEER_TT = 512
PEER_NE = 2048
PEER_SUB = 512


SQRT_HALF = float(np.sqrt(0.5))


def _gelu_of_scaled(y):
    return y * (1.0 + lax.erf(y))


def _peer_main_kernel(x_ref, g_ref, u_ref, vt_ref, cnt_ref, e1_ref, rank_ref, e2_ref, o_ref, xnt_ref, acc_ref):
    e = pl.program_id(1)
    ne = u_ref.shape[0]
    tt = x_ref.shape[0]
    sub = PEER_SUB
    n_sub = ne // sub

    @pl.when(e == 0)
    def _():
        x = x_ref[...]
        y = x * lax.rsqrt(jnp.mean(x * x, axis=-1, keepdims=True) + NORM_EPS)
        xnt_ref[...] = (y * g_ref[...]).T.astype(BF16)
        acc_ref[...] = jnp.zeros_like(acc_ref)

    def hidden(c):
        return _dot(u_ref[c * sub:(c + 1) * sub, :], xnt_ref[...])

    def weighted(c, h):
        act = _gelu_of_scaled(h).astype(BF16)
        parts = []
        for k in range(sub // PEER_KEYS):
            i1 = c * (sub // PEER_KEYS) + k
            gsum = None
            groups = PEER_KEYS // BF16_ROWS
            for hd in range(PEER_HEADS):
                cnt = jnp.broadcast_to(cnt_ref[hd, i1:i1 + 1, :], (BF16_ROWS, tt)).astype(BF16)
                e1 = jnp.broadcast_to(e1_ref[hd, i1:i1 + 1, :], (BF16_ROWS, tt)).astype(BF16)
                rank = rank_ref[hd].reshape(groups, BF16_ROWS, tt)
                e2 = e2_ref[hd].reshape(groups, BF16_ROWS, tt)
                term = jnp.where(rank < cnt[None], e2, jnp.zeros((), BF16)) * e1[None]
                gsum = term if gsum is None else gsum + term
            gsum = gsum.reshape(PEER_KEYS, tt)
            parts.append(gsum * act[k * PEER_KEYS:(k + 1) * PEER_KEYS, :])
        return jnp.concatenate(parts, axis=0) if len(parts) > 1 else parts[0]

    def values(c, w):
        return _dot(vt_ref[:, c * sub:(c + 1) * sub], w)

    hs = {0: hidden(0)}
    if n_sub > 1:
        hs[1] = hidden(1)
    total = None
    for c in range(n_sub):
        w = weighted(c, hs.pop(c))
        if c + 2 < n_sub:
            hs[c + 2] = hidden(c + 2)
        pv = values(c, w)
        total = pv if total is None else total + pv
    acc_ref[...] += total

    @pl.when(e == pl.num_programs(1) - 1)
    def _():
        o_ref[...] = x_ref[...] + acc_ref[...].T


def _peer_mixer(x2, g_norm, w_q, sub_keys, u, v, seq):
    t, d = x2.shape
    w_q2 = w_q.reshape(d, PEER_HEADS, 2, PEER_HALF).transpose(0, 2, 1, 3).reshape(d, 2 * PEER_HEADS * PEER_HALF)
    q = _norm_matmul3(x2, g_norm, w_q2)
    cnt, e1, rank, e2 = _peer_select(q, sub_keys)
    tt = min(PEER_TT, t)
    ne = PEER_NE
    n_exp = u.shape[0]
    tok_spec = pl.BlockSpec((PEER_HEADS, PEER_KEYS, tt), lambda i, e: (0, 0, i))
    row_spec = pl.BlockSpec((PEER_HEADS, ne // PEER_KEYS, tt), lambda i, e: (0, e, i))
    return pl.pallas_call(
        _peer_main_kernel,
        grid=(t // tt, n_exp // ne),
        in_specs=[pl.BlockSpec((tt, d), lambda i, e: (i, 0)),
                  pl.BlockSpec((1, d), lambda i, e: (0, 0)),
                  pl.BlockSpec((ne, d), lambda i, e: (e, 0)),
                  pl.BlockSpec((d, ne), lambda i, e: (0, e)),
                  row_spec, row_spec, tok_spec, tok_spec],
        out_specs=pl.BlockSpec((tt, d), lambda i, e: (i, 0)),
        out_shape=jax.ShapeDtypeStruct((t, d), F32),
        scratch_shapes=[pltpu.VMEM((d, tt), BF16), pltpu.VMEM((d, tt), F32)],
        compiler_params=_cparams(("parallel", "arbitrary")),
        name="peer_main",
    )(x2, (g_norm.reshape(1, d).astype(F32) * SQRT_HALF), u.astype(BF16), (v.T * SQRT_HALF).astype(BF16),
      cnt, e1, rank, e2)


def _sb_mixer(x2, g_norm, w_in, w_out, batch, seq):
    qkv = _norm_matmul(x2, g_norm, w_in, seq, tn=1024)
    return _proj_resid(_sb_attention(qkv, batch, seq), w_out, x2)


def _moba_mixer(x2, g_norm, w_in, w_out, batch, seq):
    d_model = N_HEADS * HEAD_DIM
    qkv = _norm_matmul(x2, g_norm, w_in, seq, tn=1024, n_rope=2 * d_model // 1024)
    return _proj_resid(_moba_attention(qkv, batch, seq), w_out, x2)


def kernel(x, l0_norm_mix, l0_sb_w_in, l0_sb_w_out, l0_norm_ffn, l0_peer_w_q, l0_peer_sub_keys, l0_peer_u, l0_peer_v, l1_norm_mix, l1_nsa_w_in, l1_nsa_cmp_pos_k, l1_nsa_cmp_pos_v, l1_nsa_cmp_w_k, l1_nsa_cmp_w_v, l1_nsa_w_out, l1_norm_ffn, l1_peer_w_q, l1_peer_sub_keys, l1_peer_u, l1_peer_v, l2_norm_mix, l2_moba_w_in, l2_moba_w_out, l2_norm_ffn, l2_peer_w_q, l2_peer_sub_keys, l2_peer_u, l2_peer_v, l3_norm_mix, l3_sb_w_in, l3_sb_w_out, l3_norm_ffn, l3_peer_w_q, l3_peer_sub_keys, l3_peer_u, l3_peer_v, final_norm):
    batch, seq, d = x.shape
    x2 = x.reshape(batch * seq, d)
    x2 = _sb_mixer(x2, l0_norm_mix, l0_sb_w_in, l0_sb_w_out, batch, seq)
    x2 = _peer_mixer(x2, l0_norm_ffn, l0_peer_w_q, l0_peer_sub_keys, l0_peer_u, l0_peer_v, seq)
    x2 = _nsa_mixer(x2, l1_norm_mix, l1_nsa_w_in, l1_nsa_cmp_pos_k, l1_nsa_cmp_pos_v, l1_nsa_cmp_w_k,
                    l1_nsa_cmp_w_v, l1_nsa_w_out, batch, seq)
    x2 = _peer_mixer(x2, l1_norm_ffn, l1_peer_w_q, l1_peer_sub_keys, l1_peer_u, l1_peer_v, seq)
    x2 = _moba_mixer(x2, l2_norm_mix, l2_moba_w_in, l2_moba_w_out, batch, seq)
    x2 = _peer_mixer(x2, l2_norm_ffn, l2_peer_w_q, l2_peer_sub_keys, l2_peer_u, l2_peer_v, seq)
    x2 = _sb_mixer(x2, l3_norm_mix, l3_sb_w_in, l3_sb_w_out, batch, seq)
    x2 = _peer_mixer(x2, l3_norm_ffn, l3_peer_w_q, l3_peer_sub_keys, l3_peer_u, l3_peer_v, seq)
    return _rmsnorm(x2, final_norm).reshape(batch, seq, d)
```
